```python
import math
import jax, jax.numpy as jnp
from jax import lax
import numpy as np

D_MODEL = 1024
BATCH = 1
SEQ = 16384
DEPTH = 1

ATTN_HEADS = 8
HEAD_DIM = 128
ATTN_W = ATTN_HEADS * HEAD_DIM
MOBA_BLOCK = 256
MOBA_TOPK = 3
Q_CHUNK = 32
ROPE_DIM = HEAD_DIM // 4
ROPE_THETA = 500000.0
DN_HEADS = 8
DN_DK = 128
DN_DV = 128
DN_QK_W = DN_HEADS * DN_DK
DN_V_W = DN_HEADS * DN_DV
DN_CONV = 4
DN_CHUNK = 64
D_FF = int(math.ceil(8 * D_MODEL / 3 / 256) * 256)
EPS = 1e-6

IN_SIZES = [ATTN_W, ATTN_W, ATTN_W,
            DN_QK_W, DN_QK_W, DN_V_W,
            DN_V_W,
            DN_HEADS, DN_HEADS,
            D_MODEL, D_MODEL]
IN_TOTAL = int(sum(IN_SIZES))
IN_SPLITS = [int(s) for s in np.cumsum(IN_SIZES)[:-1]]

kernel_name = "hybrid_moba_gdn_gated_merge_block"


def rms_norm(x, w):
    xf = x.astype(jnp.float32)
    y = xf * lax.rsqrt(jnp.mean(xf * xf, axis=-1, keepdims=True) + EPS)
    return (y * w.astype(jnp.float32)).astype(x.dtype)


def l2_norm(x):
    return x * lax.rsqrt(jnp.sum(x * x, axis=-1, keepdims=True) + EPS)


def partial_rope(x, pos):
    half = ROPE_DIM // 2
    inv = ROPE_THETA ** (-jnp.arange(half, dtype=jnp.float32) * 2.0 / ROPE_DIM)
    ang = pos.astype(jnp.float32)[:, None] * inv[None, :]
    cos = jnp.cos(ang)[None, :, None, :]
    sin = jnp.sin(ang)[None, :, None, :]
    xr = x[..., :ROPE_DIM].astype(jnp.float32)
    x1, x2 = xr[..., :half], xr[..., half:]
    rot = jnp.concatenate([x1 * cos - x2 * sin, x2 * cos + x1 * sin], axis=-1)
    return jnp.concatenate([rot.astype(x.dtype), x[..., ROPE_DIM:]], axis=-1)


def moba_attention(q, k, v):
    B, S, H, D = q.shape
    nb = -(-S // MOBA_BLOCK)
    s_pad = nb * MOBA_BLOCK
    pad = [(0, 0), (0, s_pad - S), (0, 0), (0, 0)]
    q = jnp.pad(q, pad).transpose(0, 2, 1, 3)
    k = jnp.pad(k, pad).transpose(0, 2, 1, 3)
    v = jnp.pad(v, pad).transpose(0, 2, 1, 3)
    kb = k.reshape(B, H, nb, MOBA_BLOCK, D)
    vb = v.reshape(B, H, nb, MOBA_BLOCK, D)
    k_mean = jnp.mean(kb.astype(jnp.float32), axis=3)
    gate = jnp.einsum('bhsd,bhnd->bhsn', q.astype(jnp.float32), k_mean)
    pos = jnp.arange(s_pad)
    q_blk = pos // MOBA_BLOCK
    past = jnp.arange(nb)[None, :] < q_blk[:, None]
    gate = jnp.where(past[None, None], gate, -jnp.inf)
    k_eff = min(MOBA_TOPK, nb)
    _, top_idx = lax.top_k(gate, k_eff)
    top_valid = jnp.arange(k_eff)[None, :] < q_blk[:, None]
    own = jnp.broadcast_to(q_blk[None, None, :, None], (B, H, s_pad, 1))
    idx = jnp.concatenate([top_idx.astype(jnp.int32), own.astype(jnp.int32)], axis=-1)
    slot_valid = jnp.concatenate([top_valid, jnp.ones((s_pad, 1), bool)], axis=-1)
    ns = k_eff + 1
    nc = s_pad // Q_CHUNK
    qc = q.reshape(B, H, nc, Q_CHUNK, D).transpose(2, 0, 1, 3, 4)
    idxc = idx.reshape(B, H, nc, Q_CHUNK, ns).transpose(2, 0, 1, 3, 4)
    validc = slot_valid.reshape(nc, Q_CHUNK, ns)
    posc = pos.reshape(nc, Q_CHUNK)
    key_off = jnp.arange(MOBA_BLOCK)
    bi = jnp.arange(B)[:, None, None, None]
    hi = jnp.arange(H)[None, :, None, None]
    scale = 1.0 / math.sqrt(D)

    def chunk(args):
        qq, ii, vv, pp = args
        ks = kb[bi, hi, ii]
        vs = vb[bi, hi, ii]
        s = jnp.einsum('bhqd,bhqnkd->bhqnk', qq, ks,
                       preferred_element_type=jnp.float32) * scale
        kpos = ii[..., None] * MOBA_BLOCK + key_off
        mask = vv[None, None, :, :, None] & (kpos <= pp[None, None, :, None, None])
        s = jnp.where(mask, s, -jnp.inf)
        p = jax.nn.softmax(s, axis=(-2, -1))
        return jnp.einsum('bhqnk,bhqnkd->bhqd', p.astype(vs.dtype), vs)

    out = lax.map(chunk, (qc, idxc, validc, posc))
    out = out.transpose(1, 2, 0, 3, 4).reshape(B, H, s_pad, D)[:, :, :S]
    return out.transpose(0, 2, 1, 3)


def causal_depthwise_conv(x, w):
    K, C = w.shape
    return lax.conv_general_dilated(x, w[:, None, :].astype(x.dtype), window_strides=(1,),
                                    padding=[(K - 1, 0)],
                                    dimension_numbers=('NWC', 'WIO', 'NWC'),
                                    feature_group_count=C)


def chunk_gated_delta_rule(q, k, v, beta, g):
    B, S, H, dk = q.shape
    dv = v.shape[-1]
    C = DN_CHUNK
    N = S // C
    q = q * (dk ** -0.5)
    tr = lambda t: t.reshape(B, N, C, H, -1).transpose(0, 3, 1, 2, 4)
    q, k, v = tr(q), tr(k), tr(v)
    beta = beta.reshape(B, N, C, H).transpose(0, 3, 1, 2)
    g = jnp.cumsum(g.reshape(B, N, C, H).transpose(0, 3, 1, 2), axis=-1)
    tril = jnp.tril(jnp.ones((C, C), bool))
    strict = jnp.tril(jnp.ones((C, C), bool), -1)
    eye = jnp.eye(C, dtype=q.dtype)
    gdiff = g[..., :, None] - g[..., None, :]
    decay = jnp.where(tril, jnp.exp(jnp.where(tril, gdiff, 0.0)), 0.0)
    k_beta = k * beta[..., None]
    v_beta = v * beta[..., None]
    L = jnp.where(strict, jnp.einsum('bhncd,bhnjd->bhncj', k_beta, k) * decay, 0.0)
    A = L + eye
    T = lax.linalg.triangular_solve(A, jnp.broadcast_to(eye, A.shape), left_side=True,
                                    lower=True, unit_diagonal=True)
    u = jnp.einsum('bhncj,bhnjd->bhncd', T, v_beta)
    w = jnp.einsum('bhncj,bhnjd->bhncd', T, k_beta * jnp.exp(g)[..., None])
    attn = jnp.where(tril, jnp.einsum('bhncd,bhnjd->bhncj', q, k) * decay, 0.0)
    mv = lambda t: jnp.moveaxis(t, 2, 0)

    def step(state, xs):
        qn, kn, un, wn, gn, an = xs
        v_new = un - jnp.einsum('bhcd,bhde->bhce', wn, state)
        o = (jnp.einsum('bhcd,bhde->bhce', qn * jnp.exp(gn)[..., None], state)
             + jnp.einsum('bhcj,bhje->bhce', an, v_new))
        g_last = gn[..., -1]
        state = (state * jnp.exp(g_last)[..., None, None]
                 + jnp.einsum('bhcd,bhce->bhde', kn * jnp.exp(g_last[..., None] - gn)[..., None], v_new))
        return state, o

    s0 = jnp.zeros((B, H, dk, dv), q.dtype)
    _, o = lax.scan(step, s0, (mv(q), mv(k), mv(u), mv(w), mv(g), mv(attn)))
    return o.transpose(1, 0, 3, 2, 4).reshape(B, S, H, dv)


def gated_deltanet(q, k, v, z, b, a, conv_w, a_log, dt_bias, o_norm_w):
    B, S, _ = q.shape
    dt = q.dtype
    qkv = jax.nn.silu(causal_depthwise_conv(jnp.concatenate([q, k, v], axis=-1), conv_w))
    qkv = qkv.astype(jnp.float32)
    q, k, v = jnp.split(qkv, [DN_QK_W, 2 * DN_QK_W], axis=-1)
    q = l2_norm(q.reshape(B, S, DN_HEADS, DN_DK))
    k = l2_norm(k.reshape(B, S, DN_HEADS, DN_DK))
    v = v.reshape(B, S, DN_HEADS, DN_DV)
    beta = jax.nn.sigmoid(b.astype(jnp.float32))
    g = -jnp.exp(a_log.astype(jnp.float32)) * jax.nn.softplus(
        a.astype(jnp.float32) + dt_bias.astype(jnp.float32))
    o = chunk_gated_delta_rule(q, k, v, beta, g)
    o = rms_norm(o, o_norm_w) * jax.nn.silu(z.astype(jnp.float32).reshape(B, S, DN_HEADS, DN_DV))
    return o.reshape(B, S, DN_V_W).astype(dt)


def setup_inputs(seed: int = 0) -> dict:
    key = jax.random.key(seed)
    ks = jax.random.split(key, 20)
    f32 = jnp.float32
    nrm = lambda k_, shape, fan_in: jax.random.normal(k_, shape, f32) * (fan_in ** -0.5)
    gain = lambda k_, shape: 1.0 + 0.05 * jax.random.normal(k_, shape, f32)
    dt = jnp.exp(jax.random.uniform(ks[14], (DEPTH, DN_HEADS), f32, math.log(1e-3), math.log(1e-1)))
    return {
        "x": jax.random.normal(ks[0], (BATCH, SEQ, D_MODEL), f32),
        "norm_mix_pre": gain(ks[1], (DEPTH, D_MODEL)),
        "w_in": nrm(ks[2], (DEPTH, D_MODEL, IN_TOTAL), D_MODEL),
        "conv_w": nrm(ks[3], (DEPTH, DN_CONV, 2 * DN_QK_W + DN_V_W), DN_CONV),
        "a_log": jnp.log(jax.random.uniform(ks[4], (DEPTH, DN_HEADS), f32, 1.0, 16.0)),
        "dt_bias": jnp.log(jnp.expm1(dt)),
        "o_norm_w": gain(ks[5], (DEPTH, DN_DV)),
        "w_o_attn": nrm(ks[6], (DEPTH, ATTN_W, D_MODEL), ATTN_W),
        "w_o_delta": nrm(ks[7], (DEPTH, DN_V_W, D_MODEL), DN_V_W),
        "w_out": nrm(ks[8], (DEPTH, D_MODEL, D_MODEL), D_MODEL),
        "norm_mix_post": gain(ks[9], (DEPTH, D_MODEL)),
        "norm_ffn_pre": gain(ks[10], (DEPTH, D_MODEL)),
        "w_gate": nrm(ks[11], (DEPTH, D_MODEL, D_FF), D_MODEL),
        "w_up": nrm(ks[12], (DEPTH, D_MODEL, D_FF), D_MODEL),
        "w_down": nrm(ks[13], (DEPTH, D_FF, D_MODEL), D_FF),
        "norm_ffn_post": gain(ks[15], (DEPTH, D_MODEL)),
    }


def reference(x, norm_mix_pre, w_in, conv_w, a_log, dt_bias, o_norm_w, w_o_attn, w_o_delta,
              w_out, norm_mix_post, norm_ffn_pre, w_gate, w_up, w_down, norm_ffn_post):
    B, S, _ = x.shape
    pos = jnp.arange(S)
    for l in range(DEPTH):
        h = rms_norm(x, norm_mix_pre[l])
        proj = h @ w_in[l]
        qa, ka, va, qd, kd, vd, zd, bd, ad, ga, gd = jnp.split(proj, IN_SPLITS, axis=-1)
        qa = partial_rope(qa.reshape(B, S, ATTN_HEADS, HEAD_DIM), pos)
        ka = partial_rope(ka.reshape(B, S, ATTN_HEADS, HEAD_DIM), pos)
        va = va.reshape(B, S, ATTN_HEADS, HEAD_DIM)
        ya = moba_attention(qa, ka, va).reshape(B, S, ATTN_W)
        yd = gated_deltanet(qd, kd, vd, zd, bd, ad, conv_w[l], a_log[l], dt_bias[l], o_norm_w[l])
        merged = (jax.nn.sigmoid(ga) * (ya @ w_o_attn[l])
                  + jax.nn.sigmoid(gd) * (yd @ w_o_delta[l]))
        x = x + rms_norm(merged @ w_out[l], norm_mix_post[l])
        h = rms_norm(x, norm_ffn_pre[l])
        f = (jax.nn.silu(h @ w_gate[l]) * (h @ w_up[l])) @ w_down[l]
        x = x + rms_norm(f, norm_ffn_post[l])
    return x
```

```python
import functools
import math

import jax
import jax.numpy as jnp
from jax import lax
from jax.experimental import pallas as pl
from jax.experimental.pallas import tpu as pltpu

D_MODEL = 1024
HEADS = 8
HEAD_DIM = 128
MOBA_BLOCK = 256
MOBA_TOPK = 3
ROPE_DIM = HEAD_DIM // 4
ROPE_THETA = 500000.0
DN_CONV = 4
DN_CHUNK = 256
D_FF = 2816
FF_CHUNK = 256
EPS = 1e-6
HALO = 16

F32 = jnp.float32
BF16 = jnp.bfloat16
NT = (((1,), (1,)), ((), ()))

VMEM_LIMIT = 56 * 1024 * 1024


def _bdot(a, b):
    return jnp.dot(a.astype(BF16), b.astype(BF16), preferred_element_type=F32)


def _sigmoid(x):
    return 1.0 / (1.0 + jnp.exp(-x))


def _proj_kernel(x_ref, nw_ref, w_ref, ws_ref, cos_ref, s1_ref, s2_ref,
                 out_ref, small_ref, kmean_ref, h_scr, *, tm, q_scale):
    j = pl.program_id(1)

    @pl.when(j == 0)
    def _():
        x = x_ref[...]
        ms = jnp.mean(x * x, axis=-1, keepdims=True)
        h = x * lax.rsqrt(ms + EPS) * nw_ref[...]
        h_scr[...] = h.astype(BF16)
        small_ref[...] = jnp.dot(h_scr[...], ws_ref[...], preferred_element_type=F32)

    acc = jnp.dot(h_scr[...], w_ref[...], preferred_element_type=F32)

    @pl.when(j >= 2)
    def _():
        out_ref[...] = acc.astype(BF16)

    def roped(hh):
        a = acc[:, hh * HEAD_DIM:(hh + 1) * HEAD_DIM]
        return (a * cos_ref[...] + pltpu.roll(a, HEAD_DIM - ROPE_DIM // 2, 1) * s1_ref[...]
                + pltpu.roll(a, ROPE_DIM // 2, 1) * s2_ref[...])

    @pl.when(j == 0)
    def _():
        for hh in range(HEADS):
            out_ref[:, hh * HEAD_DIM:(hh + 1) * HEAD_DIM] = (roped(hh) * q_scale).astype(BF16)

    @pl.when(j == 1)
    def _():
        for hh in range(HEADS):
            r = roped(hh)
            out_ref[:, hh * HEAD_DIM:(hh + 1) * HEAD_DIM] = r.astype(BF16)
            for g in range(tm // MOBA_BLOCK):
                blk = r[g * MOBA_BLOCK:(g + 1) * MOBA_BLOCK]
                kmean_ref[0, g:g + 1, hh * HEAD_DIM:(hh + 1) * HEAD_DIM] = (
                    jnp.sum(blk, axis=0, keepdims=True) * (1.0 / MOBA_BLOCK))


def _project(x2, norm_w, w_main, w_small, cos_t, s1_t, s2_t, *, tm=512):
    s = x2.shape[0]
    ncol = w_main.shape[1] // D_MODEL
    nblk = tm // MOBA_BLOCK
    kern = functools.partial(_proj_kernel, tm=tm, q_scale=1.0 / math.sqrt(HEAD_DIM))
    return pl.pallas_call(
        kern,
        grid=(s // tm, ncol),
        in_specs=[
            pl.BlockSpec((tm, D_MODEL), lambda i, j: (i, 0)),
            pl.BlockSpec((1, D_MODEL), lambda i, j: (0, 0)),
            pl.BlockSpec((D_MODEL, D_MODEL), lambda i, j: (0, j)),
            pl.BlockSpec((D_MODEL, 128), lambda i, j: (0, 0)),
            pl.BlockSpec((tm, HEAD_DIM), lambda i, j: (i, 0)),
            pl.BlockSpec((tm, HEAD_DIM), lambda i, j: (i, 0)),
            pl.BlockSpec((tm, HEAD_DIM), lambda i, j: (i, 0)),
        ],
        out_specs=[
            pl.BlockSpec((tm, D_MODEL), lambda i, j: (i, j)),
            pl.BlockSpec((tm, 128), lambda i, j: (i, 0)),
            pl.BlockSpec((1, nblk, D_MODEL), lambda i, j: (i, 0, 0)),
        ],
        out_shape=[
            jax.ShapeDtypeStruct((s, ncol * D_MODEL), BF16),
            jax.ShapeDtypeStruct((s, 128), F32),
            jax.ShapeDtypeStruct((s // tm, nblk, D_MODEL), F32),
        ],
        scratch_shapes=[pltpu.VMEM((tm, D_MODEL), BF16)],
        compiler_params=pltpu.CompilerParams(
            dimension_semantics=("parallel", "arbitrary"), vmem_limit_bytes=VMEM_LIMIT),
        name="proj",
    )(x2, norm_w, w_main, w_small, cos_t, s1_t, s2_t)


def _attn_kernel(q_ref, k_ref, vt_ref, km_ref, o_ref, bias_scr, *, nb):
    j = pl.program_id(1)
    q = q_ref[...]
    blk = lax.broadcasted_iota(jnp.int32, (nb, MOBA_BLOCK), 0)
    blk_f = blk.astype(F32)

    gate = lax.dot_general(km_ref[...], q.astype(F32), NT,
                           precision=lax.Precision.HIGHEST, preferred_element_type=F32)
    gate = jnp.where(blk < j, gate, -jnp.inf)
    bias = jnp.full((nb, MOBA_BLOCK), -jnp.inf, F32)
    for _ in range(MOBA_TOPK):
        m = jnp.max(gate, axis=0, keepdims=True)
        first = jnp.min(jnp.where(gate == m, blk_f, float(nb)), axis=0, keepdims=True)
        pick = (blk_f == first) & (m > -jnp.inf)
        bias = jnp.where(pick, 0.0, bias)
        gate = jnp.where(pick, -jnp.inf, gate)
    bias_scr[...] = bias

    kj = k_ref[pl.ds(pl.multiple_of(j * MOBA_BLOCK, MOBA_BLOCK), MOBA_BLOCK), :]
    s = lax.dot_general(kj, q, NT, preferred_element_type=F32)
    krow = lax.broadcasted_iota(jnp.int32, (MOBA_BLOCK, MOBA_BLOCK), 0)
    qcol = lax.broadcasted_iota(jnp.int32, (MOBA_BLOCK, MOBA_BLOCK), 1)
    s = jnp.where(krow <= qcol, s, -jnp.inf)
    m0 = jnp.max(s, axis=0, keepdims=True)
    p = jnp.exp(s - m0)
    l0 = jnp.sum(p, axis=0, keepdims=True)
    acc0 = jnp.dot(vt_ref[0, j], p.astype(BF16), preferred_element_type=F32)

    def body(n, carry):
        m, l, acc = carry
        kn = k_ref[pl.ds(pl.multiple_of(n * MOBA_BLOCK, MOBA_BLOCK), MOBA_BLOCK), :]
        sn = lax.dot_general(kn, q, NT, preferred_element_type=F32) + bias_scr[pl.ds(n, 1), :]
        m_new = jnp.maximum(m, jnp.max(sn, axis=0, keepdims=True))
        alpha = jnp.exp(m - m_new)
        pn = jnp.exp(sn - m_new)
        l = alpha * l + jnp.sum(pn, axis=0, keepdims=True)
        acc = alpha * acc + jnp.dot(vt_ref[0, n], pn.astype(BF16), preferred_element_type=F32)
        return m_new, l, acc

    _, l, acc = lax.fori_loop(0, j, body, (m0, l0, acc0))
    o_ref[...] = (acc / l).T.astype(BF16)


def _moba_attention(proj, v_t, kmean):
    s = proj.shape[0]
    nb = s // MOBA_BLOCK
    return pl.pallas_call(
        functools.partial(_attn_kernel, nb=nb),
        grid=(HEADS, nb),
        in_specs=[
            pl.BlockSpec((MOBA_BLOCK, HEAD_DIM), lambda h, j: (j, h)),
            pl.BlockSpec((s, HEAD_DIM), lambda h, j: (0, HEADS + h)),
            pl.BlockSpec((1, nb, HEAD_DIM, MOBA_BLOCK), lambda h, j: (h, 0, 0, 0)),
            pl.BlockSpec((nb, HEAD_DIM), lambda h, j: (0, h)),
        ],
        out_specs=pl.BlockSpec((MOBA_BLOCK, HEAD_DIM), lambda h, j: (j, h)),
        out_shape=jax.ShapeDtypeStruct((s, HEADS * HEAD_DIM), BF16),
        scratch_shapes=[pltpu.VMEM((nb, MOBA_BLOCK), F32)],
        compiler_params=pltpu.CompilerParams(
            dimension_semantics=("parallel", "arbitrary"), vmem_limit_bytes=VMEM_LIMIT),
        name="moba",
    )(proj, proj, v_t, kmean)


def _delta_kernel(q_ref, k_ref, v_ref, qh_ref, kh_ref, vh_ref, z_ref, small_ref,
                  cwq_ref, cwk_ref, cwv_ref, alog_ref, dtb_ref, onw_ref,
                  o_ref, state_scr, xx_scr):
    h = pl.program_id(0)
    c = pl.program_id(1)
    C = DN_CHUNK

    @pl.when(c == 0)
    def _():
        state_scr[...] = jnp.zeros_like(state_scr)

    def conv_silu(idx, x_ref, halo_ref, cw_ref):
        halo = jnp.where(c == 0, 0.0, halo_ref[...].astype(F32))
        xx_scr[idx, 0:HALO, :] = halo
        xx_scr[idx, HALO:HALO + C, :] = x_ref[...].astype(F32)
        y = jnp.zeros((C, HEAD_DIM), F32)
        for t in range(DN_CONV):
            off = HALO - (DN_CONV - 1) + t
            y = y + cw_ref[t:t + 1, :] * xx_scr[idx, off:off + C, :]
        return y * _sigmoid(y)

    q = conv_silu(0, q_ref, qh_ref, cwq_ref)
    k = conv_silu(1, k_ref, kh_ref, cwk_ref)
    v = conv_silu(2, v_ref, vh_ref, cwv_ref)
    q = q * lax.rsqrt(jnp.sum(q * q, axis=-1, keepdims=True) + EPS) * (HEAD_DIM ** -0.5)
    k = k * lax.rsqrt(jnp.sum(k * k, axis=-1, keepdims=True) + EPS)

    small = small_ref[...]
    lane = lax.broadcasted_iota(jnp.int32, (C, 128), 1)
    beta = jnp.sum(jnp.where(lane == h, _sigmoid(small), 0.0), axis=-1, keepdims=True)
    xs = small + dtb_ref[...]
    softplus = jnp.maximum(xs, 0.0) + jnp.log(1.0 + jnp.exp(-jnp.abs(xs)))
    g_all = -jnp.exp(alog_ref[...]) * softplus
    g = jnp.sum(jnp.where(lane == h + HEADS, g_all, 0.0), axis=-1, keepdims=True)

    row = lax.broadcasted_iota(jnp.int32, (C, C), 0)
    col = lax.broadcasted_iota(jnp.int32, (C, C), 1)
    tril = row >= col
    strict = row > col
    rxc = row ^ col

    gcum_b = jnp.dot(tril.astype(F32), jnp.broadcast_to(g, (C, 128)),
                     precision=lax.Precision.HIGHEST, preferred_element_type=F32)
    gc = gcum_b[:, 0:1]
    gr = gcum_b.T[0:1, :]
    decay = jnp.where(tril, jnp.exp(jnp.where(tril, gc - gr, 0.0)), 0.0)
    e_g = jnp.exp(gc)
    g_last = gc[C - 1:C, :]
    e_last = jnp.exp(g_last)

    kb = k * beta
    vb = v * beta
    a = lax.dot_general(jnp.concatenate([kb, q], axis=0).astype(BF16), k.astype(BF16), NT,
                        preferred_element_type=F32)
    lmat = jnp.where(strict, a[:C] * decay, 0.0)
    attn = a[C:] * decay

    eye = (row == col).astype(F32)
    d1 = jnp.where(rxc < 16, lmat, 0.0)
    d2 = _bdot(d1, d1)
    d4 = _bdot(d2, d2)
    d8 = _bdot(d4, d4)
    tmat = _bdot(_bdot(eye - d1, eye + d2), _bdot(eye + d4, eye + d8))
    sz = 16
    while sz < C:
        off = jnp.where((rxc >= sz) & (rxc < 2 * sz), lmat, 0.0)
        tmat = tmat - _bdot(_bdot(tmat, off), tmat)
        sz *= 2

    uw = _bdot(tmat, jnp.concatenate([vb, kb * e_g], axis=1))
    u = uw[:, :HEAD_DIM]
    w = uw[:, HEAD_DIM:]

    state = state_scr[...]
    wq = _bdot(jnp.concatenate([w, q * e_g], axis=0), state)
    v_new = u - wq[:C]
    o = wq[C:] + _bdot(attn, v_new)
    kt = k * jnp.exp(g_last - gc)
    state_scr[...] = state * e_last + _bdot(kt.T, v_new)

    y = o * lax.rsqrt(jnp.mean(o * o, axis=-1, keepdims=True) + EPS) * onw_ref[...]
    z = z_ref[...].astype(F32)
    o_ref[...] = (y * (z * _sigmoid(z))).astype(BF16)


def _gated_deltanet(proj, small, conv_w, alog_v, dtb_v, onw):
    s = proj.shape[0]
    nc = s // DN_CHUNK
    rb = DN_CHUNK // HALO

    def col(base):
        return pl.BlockSpec((DN_CHUNK, HEAD_DIM), lambda h, c: (c, base + h))

    def halo(base):
        return pl.BlockSpec((HALO, HEAD_DIM), lambda h, c: (jnp.maximum(c * rb - 1, 0), base + h))

    def cw(base):
        return pl.BlockSpec((DN_CONV, HEAD_DIM), lambda h, c: (0, base + h))

    vec = pl.BlockSpec((1, 128), lambda h, c: (0, 0))
    return pl.pallas_call(
        _delta_kernel,
        grid=(HEADS, nc),
        in_specs=[col(3 * HEADS), col(4 * HEADS), col(5 * HEADS),
                  halo(3 * HEADS), halo(4 * HEADS), halo(5 * HEADS),
                  col(6 * HEADS),
                  pl.BlockSpec((DN_CHUNK, 128), lambda h, c: (c, 0)),
                  cw(0), cw(HEADS), cw(2 * HEADS),
                  vec, vec, vec],
        out_specs=pl.BlockSpec((DN_CHUNK, HEAD_DIM), lambda h, c: (c, h)),
        out_shape=jax.ShapeDtypeStruct((s, HEADS * HEAD_DIM), BF16),
        scratch_shapes=[pltpu.VMEM((HEAD_DIM, HEAD_DIM), F32),
                        pltpu.VMEM((3, HALO + DN_CHUNK, HEAD_DIM), F32)],
        compiler_params=pltpu.CompilerParams(
            dimension_semantics=("parallel", "arbitrary"), vmem_limit_bytes=VMEM_LIMIT),
        name="deltanet",
    )(proj, proj, proj, proj, proj, proj, proj, small, conv_w, conv_w, conv_w, alog_v, dtb_v, onw)


def _mix_out_kernel(ya_ref, yd_ref, ga_ref, gd_ref, x_ref, wa_ref, wd_ref, wo_ref, nw_ref, o_ref):
    pa = jnp.dot(ya_ref[...], wa_ref[...], preferred_element_type=F32)
    pd = jnp.dot(yd_ref[...], wd_ref[...], preferred_element_type=F32)
    merged = _sigmoid(ga_ref[...].astype(F32)) * pa + _sigmoid(gd_ref[...].astype(F32)) * pd
    mo = jnp.dot(merged.astype(BF16), wo_ref[...], preferred_element_type=F32)
    y = mo * lax.rsqrt(jnp.mean(mo * mo, axis=-1, keepdims=True) + EPS) * nw_ref[...]
    o_ref[...] = x_ref[...] + y


def _mix_out(ya, yd, proj, x2, wa, wd, wo, nw, *, tm=512):
    s = x2.shape[0]
    row = lambda i: (i, 0)
    full = lambda i: (0, 0)
    wspec = pl.BlockSpec((D_MODEL, D_MODEL), full)
    return pl.pallas_call(
        _mix_out_kernel,
        grid=(s // tm,),
        in_specs=[pl.BlockSpec((tm, D_MODEL), row), pl.BlockSpec((tm, D_MODEL), row),
                  pl.BlockSpec((tm, D_MODEL), lambda i: (i, 7)),
                  pl.BlockSpec((tm, D_MODEL), lambda i: (i, 8)),
                  pl.BlockSpec((tm, D_MODEL), row),
                  wspec, wspec, wspec, pl.BlockSpec((1, D_MODEL), full)],
        out_specs=pl.BlockSpec((tm, D_MODEL), row),
        out_shape=jax.ShapeDtypeStruct((s, D_MODEL), F32),
        compiler_params=pltpu.CompilerParams(
            dimension_semantics=("parallel",), vmem_limit_bytes=VMEM_LIMIT),
        name="mix_out",
    )(ya, yd, proj, proj, x2, wa, wd, wo, nw)


def _ffn_kernel(x_ref, npre_ref, wg_ref, wu_ref, wd_ref, npost_ref, o_ref):
    x = x_ref[...]
    h = (x * lax.rsqrt(jnp.mean(x * x, axis=-1, keepdims=True) + EPS) * npre_ref[...]).astype(BF16)
    acc = jnp.zeros(x.shape, F32)
    for cc in range(D_FF // FF_CHUNK):
        sl = slice(cc * FF_CHUNK, (cc + 1) * FF_CHUNK)
        g = jnp.dot(h, wg_ref[:, sl], preferred_element_type=F32)
        u = jnp.dot(h, wu_ref[:, sl], preferred_element_type=F32)
        act = (g * _sigmoid(g) * u).astype(BF16)
        acc = acc + jnp.dot(act, wd_ref[sl, :], preferred_element_type=F32)
    y = acc * lax.rsqrt(jnp.mean(acc * acc, axis=-1, keepdims=True) + EPS) * npost_ref[...]
    o_ref[...] = x + y


def _ffn(x1, npre, wg, wu, wd, npost, *, tm=256):
    s = x1.shape[0]
    row = lambda i: (i, 0)
    full = lambda i: (0, 0)
    return pl.pallas_call(
        _ffn_kernel,
        grid=(s // tm,),
        in_specs=[pl.BlockSpec((tm, D_MODEL), row), pl.BlockSpec((1, D_MODEL), full),
                  pl.BlockSpec((D_MODEL, D_FF), full), pl.BlockSpec((D_MODEL, D_FF), full),
                  pl.BlockSpec((D_FF, D_MODEL), full), pl.BlockSpec((1, D_MODEL), full)],
        out_specs=pl.BlockSpec((tm, D_MODEL), row),
        out_shape=jax.ShapeDtypeStruct((s, D_MODEL), F32),
        compiler_params=pltpu.CompilerParams(
            dimension_semantics=("parallel",), vmem_limit_bytes=VMEM_LIMIT),
        name="ffn",
    )(x1, npre, wg, wu, wd, npost)


def _rope_tables(s):
    half = ROPE_DIM // 2
    inv = ROPE_THETA ** (-jnp.arange(half, dtype=F32) * 2.0 / ROPE_DIM)
    ang = jnp.arange(s).astype(F32)[:, None] * inv[None, :]
    cos, sin = jnp.cos(ang), jnp.sin(ang)
    ones = jnp.ones((s, HEAD_DIM - ROPE_DIM), F32)
    zeros_half = jnp.zeros((s, half), F32)
    zeros_rest = jnp.zeros((s, HEAD_DIM - ROPE_DIM), F32)
    cos_t = jnp.concatenate([cos, cos, ones], axis=-1)
    s1_t = jnp.concatenate([-sin, zeros_half, zeros_rest], axis=-1)
    s2_t = jnp.concatenate([zeros_half, sin, zeros_rest], axis=-1)
    return cos_t, s1_t, s2_t


def _layer(x2, l, norm_mix_pre, w_in, conv_w, a_log, dt_bias, o_norm_w, w_o_attn, w_o_delta,
           w_out, norm_mix_post, norm_ffn_pre, w_gate, w_up, w_down, norm_ffn_post):
    s = x2.shape[0]
    nb = s // MOBA_BLOCK
    wide = 7 * D_MODEL
    nsmall = 2 * HEADS
    w = w_in[l]
    w_main = jnp.concatenate([w[:, :wide], w[:, wide + nsmall:]], axis=1).astype(BF16)
    w_small = jnp.pad(w[:, wide:wide + nsmall], ((0, 0), (0, 128 - nsmall))).astype(BF16)
    cos_t, s1_t, s2_t = _rope_tables(s)

    proj, small, kmean = _project(x2, norm_mix_pre[l][None, :], w_main, w_small, cos_t, s1_t, s2_t)
    kmean = kmean.reshape(nb, D_MODEL)

    v_t = proj[:, 2 * D_MODEL:3 * D_MODEL].reshape(nb, MOBA_BLOCK, HEADS, HEAD_DIM)
    v_t = v_t.transpose(2, 0, 3, 1)
    ya = _moba_attention(proj, v_t, kmean)

    pad = lambda vec: jnp.pad(vec.astype(F32), (HEADS, 128 - 2 * HEADS))[None, :]
    yd = _gated_deltanet(proj, small, conv_w[l], pad(a_log[l]), pad(dt_bias[l]), o_norm_w[l][None, :])

    x1 = _mix_out(ya, yd, proj, x2, w_o_attn[l].astype(BF16), w_o_delta[l].astype(BF16),
                  w_out[l].astype(BF16), norm_mix_post[l][None, :])
    return _ffn(x1, norm_ffn_pre[l][None, :], w_gate[l].astype(BF16), w_up[l].astype(BF16),
                w_down[l].astype(BF16), norm_ffn_post[l][None, :])


def kernel(x, norm_mix_pre, w_in, conv_w, a_log, dt_bias, o_norm_w, w_o_attn, w_o_delta, w_out,
           norm_mix_post, norm_ffn_pre, w_gate, w_up, w_down, norm_ffn_post):
    b, s, d = x.shape
    assert d == D_MODEL and s % 512 == 0
    outs = []
    for bi in range(b):
        x2 = x[bi]
        for l in range(w_in.shape[0]):
            x2 = _layer(x2, l, norm_mix_pre, w_in, conv_w, a_log, dt_bias, o_norm_w, w_o_attn,
                        w_o_delta, w_out, norm_mix_post, norm_ffn_pre, w_gate, w_up, w_down,
                        norm_ffn_post)
        outs.append(x2)
    return jnp.stack(outs, axis=0)
```

```python
import functools
import math

import jax
import jax.numpy as jnp
from jax import lax
from jax.experimental import pallas as pl
from jax.experimental.pallas import tpu as pltpu

D_MODEL = 1024
HEADS = 8
HEAD_DIM = 128
MOBA_BLOCK = 256
MOBA_TOPK = 3
ROPE_DIM = HEAD_DIM // 4
ROPE_THETA = 500000.0
DN_CONV = 4
DN_CHUNK = 256
D_FF = 2816
FF_CHUNK = 256
EPS = 1e-6
ATTN_UNROLL = 2
HALO = 16

F32 = jnp.float32
BF16 = jnp.bfloat16
NT = (((1,), (1,)), ((), ()))

VMEM_LIMIT = 56 * 1024 * 1024


def _bdot(a, b):
    return jnp.dot(a.astype(BF16), b.astype(BF16), preferred_element_type=F32)


def _sigmoid(x):
    return 1.0 / (1.0 + jnp.exp(-x))


def _proj_kernel(x_ref, nw_ref, w_ref, ws_ref, cos_ref, s1_ref, s2_ref,
                 out_ref, small_ref, kmean_ref, h_scr, *, tm, q_scale):
    j = pl.program_id(1)

    @pl.when(j == 0)
    def _():
        x = x_ref[...]
        ms = jnp.mean(x * x, axis=-1, keepdims=True)
        h = x * lax.rsqrt(ms + EPS) * nw_ref[...]
        h_scr[...] = h.astype(BF16)
        small_ref[...] = jnp.dot(h_scr[...], ws_ref[...], preferred_element_type=F32)

    acc = jnp.dot(h_scr[...], w_ref[...], preferred_element_type=F32)

    @pl.when(j >= 2)
    def _():
        out_ref[...] = acc.astype(BF16)

    def roped(hh):
        a = acc[:, hh * HEAD_DIM:(hh + 1) * HEAD_DIM]
        return (a * cos_ref[...] + pltpu.roll(a, HEAD_DIM - ROPE_DIM // 2, 1) * s1_ref[...]
                + pltpu.roll(a, ROPE_DIM // 2, 1) * s2_ref[...])

    @pl.when(j == 0)
    def _():
        for hh in range(HEADS):
            out_ref[:, hh * HEAD_DIM:(hh + 1) * HEAD_DIM] = (roped(hh) * q_scale).astype(BF16)

    @pl.when(j == 1)
    def _():
        for hh in range(HEADS):
            r = roped(hh)
            out_ref[:, hh * HEAD_DIM:(hh + 1) * HEAD_DIM] = r.astype(BF16)
            for g in range(tm // MOBA_BLOCK):
                blk = r[g * MOBA_BLOCK:(g + 1) * MOBA_BLOCK]
                kmean_ref[0, g:g + 1, hh * HEAD_DIM:(hh + 1) * HEAD_DIM] = (
                    jnp.sum(blk, axis=0, keepdims=True) * (1.0 / MOBA_BLOCK))


def _project(x2, norm_w, w_main, w_small, cos_t, s1_t, s2_t, *, tm=512):
    s = x2.shape[0]
    ncol = w_main.shape[1] // D_MODEL
    nblk = tm // MOBA_BLOCK
    kern = functools.partial(_proj_kernel, tm=tm, q_scale=math.log2(math.e) / math.sqrt(HEAD_DIM))
    return pl.pallas_call(
        kern,
        grid=(s // tm, ncol),
        in_specs=[
            pl.BlockSpec((tm, D_MODEL), lambda i, j: (i, 0)),
            pl.BlockSpec((1, D_MODEL), lambda i, j: (0, 0)),
            pl.BlockSpec((D_MODEL, D_MODEL), lambda i, j: (0, j)),
            pl.BlockSpec((D_MODEL, 128), lambda i, j: (0, 0)),
            pl.BlockSpec((tm, HEAD_DIM), lambda i, j: (i, 0)),
            pl.BlockSpec((tm, HEAD_DIM), lambda i, j: (i, 0)),
            pl.BlockSpec((tm, HEAD_DIM), lambda i, j: (i, 0)),
        ],
        out_specs=[
            pl.BlockSpec((tm, D_MODEL), lambda i, j: (i, j)),
            pl.BlockSpec((tm, 128), lambda i, j: (i, 0)),
            pl.BlockSpec((1, nblk, D_MODEL), lambda i, j: (i, 0, 0)),
        ],
        out_shape=[
            jax.ShapeDtypeStruct((s, ncol * D_MODEL), BF16),
            jax.ShapeDtypeStruct((s, 128), F32),
            jax.ShapeDtypeStruct((s // tm, nblk, D_MODEL), F32),
        ],
        scratch_shapes=[pltpu.VMEM((tm, D_MODEL), BF16)],
        compiler_params=pltpu.CompilerParams(
            dimension_semantics=("parallel", "arbitrary"), vmem_limit_bytes=VMEM_LIMIT),
        name="proj",
    )(x2, norm_w, w_main, w_small, cos_t, s1_t, s2_t)


def _attn_kernel(q_ref, k_ref, vt_ref, km_ref, o_ref, bias_scr, sa_scr, pb_scr, *, nb):
    j = pl.program_id(1)
    q = q_ref[...]
    blk = lax.broadcasted_iota(jnp.int32, (nb, MOBA_BLOCK), 0)
    blk_f = blk.astype(F32)

    gate = lax.dot_general(km_ref[...], q.astype(F32), NT,
                           precision=lax.Precision.HIGHEST, preferred_element_type=F32)
    gate = jnp.where(blk < j, gate, -jnp.inf)
    bias = jnp.full((nb, MOBA_BLOCK), -jnp.inf, F32)
    for _ in range(MOBA_TOPK):
        m = jnp.max(gate, axis=0, keepdims=True)
        first = jnp.min(jnp.where(gate == m, blk_f, float(nb)), axis=0, keepdims=True)
        pick = (blk_f == first) & (m > -jnp.inf)
        bias = jnp.where(pick, 0.0, bias)
        gate = jnp.where(pick, -jnp.inf, gate)
    bias_scr[...] = bias

    kj = k_ref[pl.ds(pl.multiple_of(j * MOBA_BLOCK, MOBA_BLOCK), MOBA_BLOCK), :]
    s = lax.dot_general(kj, q, NT, preferred_element_type=F32)
    krow = lax.broadcasted_iota(jnp.int32, (MOBA_BLOCK, MOBA_BLOCK), 0)
    qcol = lax.broadcasted_iota(jnp.int32, (MOBA_BLOCK, MOBA_BLOCK), 1)
    s = jnp.where(krow <= qcol, s, -jnp.inf)
    m0 = jnp.max(s, axis=0, keepdims=True)
    p = jnp.exp2(s - m0)
    l0 = jnp.sum(p, axis=0, keepdims=True)
    acc0 = jnp.dot(vt_ref[0, j], p.astype(BF16), preferred_element_type=F32)

    U = ATTN_UNROLL
    B = MOBA_BLOCK

    def scores(g):
        n0 = jnp.minimum(g, nb // U - 1) * U
        kn = k_ref[pl.ds(pl.multiple_of(n0 * B, U * B), U * B), :]
        return lax.dot_general(kn, q, NT, preferred_element_type=F32)

    def values(g, p_of):
        n0 = jnp.maximum(g, 0) * U
        pv = jnp.dot(vt_ref[0, n0], p_of(0), preferred_element_type=F32)
        for u in range(1, U):
            pv = pv + jnp.dot(vt_ref[0, n0 + u], p_of(u), preferred_element_type=F32)
        return pv

    def softmax(g, s_of, m, l):
        sn = [s_of(u) + bias_scr[pl.ds(g * U + u, 1), :] for u in range(U)]
        m_new = m
        for u in range(U):
            m_new = jnp.maximum(m_new, jnp.max(sn[u], axis=0, keepdims=True))
        alpha = jnp.exp2(m - m_new)
        pn = [jnp.exp2(sn[u] - m_new) for u in range(U)]
        l = alpha * l
        for u in range(U):
            l = l + jnp.sum(pn[u], axis=0, keepdims=True)
        return m_new, l, alpha, [x.astype(BF16) for x in pn]

    def body(t, carry):
        m, l, acc, alpha_b = carry
        acc = alpha_b * acc + values(2 * t - 1, lambda u: pb_scr[u * B:(u + 1) * B, :])
        m, l, alpha_a, p_a = softmax(2 * t, lambda u: sa_scr[u * B:(u + 1) * B, :], m, l)
        s_b = scores(2 * t + 1)
        acc = alpha_a * acc + values(2 * t, lambda u: p_a[u])
        m, l, alpha_b, p_b = softmax(2 * t + 1, lambda u: s_b[u * B:(u + 1) * B], m, l)
        for u in range(U):
            pb_scr[u * B:(u + 1) * B, :] = p_b[u]
        sa_scr[...] = scores(2 * t + 2)
        return m, l, acc, alpha_b

    sa_scr[...] = scores(0)
    pb_scr[...] = jnp.zeros_like(pb_scr)
    trips = (j + 2 * U - 1) // (2 * U)
    _, l, acc, alpha_b = lax.fori_loop(0, trips, body, (m0, l0, acc0, jnp.ones_like(m0)))
    acc = alpha_b * acc + values(2 * trips - 1, lambda u: pb_scr[u * B:(u + 1) * B, :])
    o_ref[...] = (acc / l).T.astype(BF16)


def _moba_attention(proj, v_t, kmean):
    s = proj.shape[0]
    nb = s // MOBA_BLOCK
    return pl.pallas_call(
        functools.partial(_attn_kernel, nb=nb),
        grid=(HEADS, nb),
        in_specs=[
            pl.BlockSpec((MOBA_BLOCK, HEAD_DIM), lambda h, j: (j, h)),
            pl.BlockSpec((s, HEAD_DIM), lambda h, j: (0, HEADS + h)),
            pl.BlockSpec((1, nb, HEAD_DIM, MOBA_BLOCK), lambda h, j: (h, 0, 0, 0)),
            pl.BlockSpec((nb, HEAD_DIM), lambda h, j: (0, h)),
        ],
        out_specs=pl.BlockSpec((MOBA_BLOCK, HEAD_DIM), lambda h, j: (j, h)),
        out_shape=jax.ShapeDtypeStruct((s, HEADS * HEAD_DIM), BF16),
        scratch_shapes=[pltpu.VMEM((nb, MOBA_BLOCK), F32),
                        pltpu.VMEM((ATTN_UNROLL * MOBA_BLOCK, MOBA_BLOCK), F32),
                        pltpu.VMEM((ATTN_UNROLL * MOBA_BLOCK, MOBA_BLOCK), BF16)],
        compiler_params=pltpu.CompilerParams(
            dimension_semantics=("parallel", "arbitrary"), vmem_limit_bytes=VMEM_LIMIT),
        name="moba",
    )(proj, proj, v_t, kmean)


def _delta_kernel(q_ref, k_ref, v_ref, qh_ref, kh_ref, vh_ref, z_ref, small_ref,
                  cwq_ref, cwk_ref, cwv_ref, alog_ref, dtb_ref, onw_ref,
                  o_ref, state_scr, xx_scr):
    h = pl.program_id(0)
    c = pl.program_id(1)
    C = DN_CHUNK

    @pl.when(c == 0)
    def _():
        state_scr[...] = jnp.zeros_like(state_scr)

    def conv_silu(idx, x_ref, halo_ref, cw_ref):
        halo = jnp.where(c == 0, 0.0, halo_ref[...].astype(F32))
        xx_scr[idx, 0:HALO, :] = halo
        xx_scr[idx, HALO:HALO + C, :] = x_ref[...].astype(F32)
        y = jnp.zeros((C, HEAD_DIM), F32)
        for t in range(DN_CONV):
            off = HALO - (DN_CONV - 1) + t
            y = y + cw_ref[t:t + 1, :] * xx_scr[idx, off:off + C, :]
        return y * _sigmoid(y)

    q = conv_silu(0, q_ref, qh_ref, cwq_ref)
    k = conv_silu(1, k_ref, kh_ref, cwk_ref)
    v = conv_silu(2, v_ref, vh_ref, cwv_ref)
    q = q * lax.rsqrt(jnp.sum(q * q, axis=-1, keepdims=True) + EPS) * (HEAD_DIM ** -0.5)
    k = k * lax.rsqrt(jnp.sum(k * k, axis=-1, keepdims=True) + EPS)

    small = small_ref[...]
    lane = lax.broadcasted_iota(jnp.int32, (C, 128), 1)
    beta = jnp.sum(jnp.where(lane == h, _sigmoid(small), 0.0), axis=-1, keepdims=True)
    xs = small + dtb_ref[...]
    softplus = jnp.maximum(xs, 0.0) + jnp.log(1.0 + jnp.exp(-jnp.abs(xs)))
    g_all = -jnp.exp(alog_ref[...]) * softplus
    g = jnp.sum(jnp.where(lane == h + HEADS, g_all, 0.0), axis=-1, keepdims=True)

    row = lax.broadcasted_iota(jnp.int32, (C, C), 0)
    col = lax.broadcasted_iota(jnp.int32, (C, C), 1)
    tril = row >= col
    strict = row > col
    rxc = row ^ col

    gcum_b = jnp.dot(tril.astype(F32), jnp.broadcast_to(g, (C, 128)),
                     precision=lax.Precision.HIGHEST, preferred_element_type=F32)
    gc = gcum_b[:, 0:1]
    gr = gcum_b.T[0:1, :]
    decay = jnp.where(tril, jnp.exp(jnp.where(tril, gc - gr, 0.0)), 0.0)
    e_g = jnp.exp(gc)
    g_last = gc[C - 1:C, :]
    e_last = jnp.exp(g_last)

    kb = k * beta
    vb = v * beta
    a = lax.dot_general(jnp.concatenate([kb, q], axis=0).astype(BF16), k.astype(BF16), NT,
                        preferred_element_type=F32)
    lmat = jnp.where(strict, a[:C] * decay, 0.0)
    attn = a[C:] * decay

    eye = (row == col).astype(F32)
    d1 = jnp.where(rxc < 16, lmat, 0.0)
    d2 = _bdot(d1, d1)
    d4 = _bdot(d2, d2)
    d8 = _bdot(d4, d4)
    tmat = _bdot(_bdot(eye - d1, eye + d2), _bdot(eye + d4, eye + d8))
    sz = 16
    while sz < C:
        off = jnp.where((rxc >= sz) & (rxc < 2 * sz), lmat, 0.0)
        tmat = tmat - _bdot(_bdot(tmat, off), tmat)
        sz *= 2

    uw = _bdot(tmat, jnp.concatenate([vb, kb * e_g], axis=1))
    u = uw[:, :HEAD_DIM]
    w = uw[:, HEAD_DIM:]

    state = state_scr[...]
    wq = _bdot(jnp.concatenate([w, q * e_g], axis=0), state)
    v_new = u - wq[:C]
    o = wq[C:] + _bdot(attn, v_new)
    kt = k * jnp.exp(g_last - gc)
    state_scr[...] = state * e_last + _bdot(kt.T, v_new)

    y = o * lax.rsqrt(jnp.mean(o * o, axis=-1, keepdims=True) + EPS) * onw_ref[...]
    z = z_ref[...].astype(F32)
    o_ref[...] = (y * (z * _sigmoid(z))).astype(BF16)


def _gated_deltanet(proj, small, conv_w, alog_v, dtb_v, onw):
    s = proj.shape[0]
    nc = s // DN_CHUNK
    rb = DN_CHUNK // HALO

    def col(base):
        return pl.BlockSpec((DN_CHUNK, HEAD_DIM), lambda h, c: (c, base + h))

    def halo(base):
        return pl.BlockSpec((HALO, HEAD_DIM), lambda h, c: (jnp.maximum(c * rb - 1, 0), base + h))

    def cw(base):
        return pl.BlockSpec((DN_CONV, HEAD_DIM), lambda h, c: (0, base + h))

    vec = pl.BlockSpec((1, 128), lambda h, c: (0, 0))
    return pl.pallas_call(
        _delta_kernel,
        grid=(HEADS, nc),
        in_specs=[col(3 * HEADS), col(4 * HEADS), col(5 * HEADS),
                  halo(3 * HEADS), halo(4 * HEADS), halo(5 * HEADS),
                  col(6 * HEADS),
                  pl.BlockSpec((DN_CHUNK, 128), lambda h, c: (c, 0)),
                  cw(0), cw(HEADS), cw(2 * HEADS),
                  vec, vec, vec],
        out_specs=pl.BlockSpec((DN_CHUNK, HEAD_DIM), lambda h, c: (c, h)),
        out_shape=jax.ShapeDtypeStruct((s, HEADS * HEAD_DIM), BF16),
        scratch_shapes=[pltpu.VMEM((HEAD_DIM, HEAD_DIM), F32),
                        pltpu.VMEM((3, HALO + DN_CHUNK, HEAD_DIM), F32)],
        compiler_params=pltpu.CompilerParams(
            dimension_semantics=("parallel", "arbitrary"), vmem_limit_bytes=VMEM_LIMIT),
        name="deltanet",
    )(proj, proj, proj, proj, proj, proj, proj, small, conv_w, conv_w, conv_w, alog_v, dtb_v, onw)


def _mix_out_kernel(ya_ref, yd_ref, ga_ref, gd_ref, x_ref, wa_ref, wd_ref, wo_ref, nw_ref, o_ref):
    pa = jnp.dot(ya_ref[...], wa_ref[...], preferred_element_type=F32)
    pd = jnp.dot(yd_ref[...], wd_ref[...], preferred_element_type=F32)
    merged = _sigmoid(ga_ref[...].astype(F32)) * pa + _sigmoid(gd_ref[...].astype(F32)) * pd
    mo = jnp.dot(merged.astype(BF16), wo_ref[...], preferred_element_type=F32)
    y = mo * lax.rsqrt(jnp.mean(mo * mo, axis=-1, keepdims=True) + EPS) * nw_ref[...]
    o_ref[...] = x_ref[...] + y


def _mix_out(ya, yd, proj, x2, wa, wd, wo, nw, *, tm=512):
    s = x2.shape[0]
    row = lambda i: (i, 0)
    full = lambda i: (0, 0)
    wspec = pl.BlockSpec((D_MODEL, D_MODEL), full)
    return pl.pallas_call(
        _mix_out_kernel,
        grid=(s // tm,),
        in_specs=[pl.BlockSpec((tm, D_MODEL), row), pl.BlockSpec((tm, D_MODEL), row),
                  pl.BlockSpec((tm, D_MODEL), lambda i: (i, 7)),
                  pl.BlockSpec((tm, D_MODEL), lambda i: (i, 8)),
                  pl.BlockSpec((tm, D_MODEL), row),
                  wspec, wspec, wspec, pl.BlockSpec((1, D_MODEL), full)],
        out_specs=pl.BlockSpec((tm, D_MODEL), row),
        out_shape=jax.ShapeDtypeStruct((s, D_MODEL), F32),
        compiler_params=pltpu.CompilerParams(
            dimension_semantics=("parallel",), vmem_limit_bytes=VMEM_LIMIT),
        name="mix_out",
    )(ya, yd, proj, proj, x2, wa, wd, wo, nw)


def _ffn_kernel(x_ref, npre_ref, wg_ref, wu_ref, wd_ref, npost_ref, o_ref):
    x = x_ref[...]
    h = (x * lax.rsqrt(jnp.mean(x * x, axis=-1, keepdims=True) + EPS) * npre_ref[...]).astype(BF16)
    acc = jnp.zeros(x.shape, F32)
    for cc in range(D_FF // FF_CHUNK):
        sl = slice(cc * FF_CHUNK, (cc + 1) * FF_CHUNK)
        g = jnp.dot(h, wg_ref[:, sl], preferred_element_type=F32)
        u = jnp.dot(h, wu_ref[:, sl], preferred_element_type=F32)
        act = (g * _sigmoid(g) * u).astype(BF16)
        acc = acc + jnp.dot(act, wd_ref[sl, :], preferred_element_type=F32)
    y = acc * lax.rsqrt(jnp.mean(acc * acc, axis=-1, keepdims=True) + EPS) * npost_ref[...]
    o_ref[...] = x + y


def _ffn(x1, npre, wg, wu, wd, npost, *, tm=256):
    s = x1.shape[0]
    row = lambda i: (i, 0)
    full = lambda i: (0, 0)
    return pl.pallas_call(
        _ffn_kernel,
        grid=(s // tm,),
        in_specs=[pl.BlockSpec((tm, D_MODEL), row), pl.BlockSpec((1, D_MODEL), full),
                  pl.BlockSpec((D_MODEL, D_FF), full), pl.BlockSpec((D_MODEL, D_FF), full),
                  pl.BlockSpec((D_FF, D_MODEL), full), pl.BlockSpec((1, D_MODEL), full)],
        out_specs=pl.BlockSpec((tm, D_MODEL), row),
        out_shape=jax.ShapeDtypeStruct((s, D_MODEL), F32),
        compiler_params=pltpu.CompilerParams(
            dimension_semantics=("parallel",), vmem_limit_bytes=VMEM_LIMIT),
        name="ffn",
    )(x1, npre, wg, wu, wd, npost)


def _rope_tables(s):
    half = ROPE_DIM // 2
    inv = ROPE_THETA ** (-jnp.arange(half, dtype=F32) * 2.0 / ROPE_DIM)
    ang = jnp.arange(s).astype(F32)[:, None] * inv[None, :]
    cos, sin = jnp.cos(ang), jnp.sin(ang)
    ones = jnp.ones((s, HEAD_DIM - ROPE_DIM), F32)
    zeros_half = jnp.zeros((s, half), F32)
    zeros_rest = jnp.zeros((s, HEAD_DIM - ROPE_DIM), F32)
    cos_t = jnp.concatenate([cos, cos, ones], axis=-1)
    s1_t = jnp.concatenate([-sin, zeros_half, zeros_rest], axis=-1)
    s2_t = jnp.concatenate([zeros_half, sin, zeros_rest], axis=-1)
    return cos_t, s1_t, s2_t


def _layer(x2, l, norm_mix_pre, w_in, conv_w, a_log, dt_bias, o_norm_w, w_o_attn, w_o_delta,
           w_out, norm_mix_post, norm_ffn_pre, w_gate, w_up, w_down, norm_ffn_post):
    s = x2.shape[0]
    nb = s // MOBA_BLOCK
    wide = 7 * D_MODEL
    nsmall = 2 * HEADS
    w = w_in[l]
    w_main = jnp.concatenate([w[:, :wide], w[:, wide + nsmall:]], axis=1).astype(BF16)
    w_small = jnp.pad(w[:, wide:wide + nsmall], ((0, 0), (0, 128 - nsmall))).astype(BF16)
    cos_t, s1_t, s2_t = _rope_tables(s)

    proj, small, kmean = _project(x2, norm_mix_pre[l][None, :], w_main, w_small, cos_t, s1_t, s2_t)
    kmean = kmean.reshape(nb, D_MODEL)

    v_t = proj[:, 2 * D_MODEL:3 * D_MODEL].reshape(nb, MOBA_BLOCK, HEADS, HEAD_DIM)
    v_t = v_t.transpose(2, 0, 3, 1)
    ya = _moba_attention(proj, v_t, kmean)

    pad = lambda vec: jnp.pad(vec.astype(F32), (HEADS, 128 - 2 * HEADS))[None, :]
    yd = _gated_deltanet(proj, small, conv_w[l], pad(a_log[l]), pad(dt_bias[l]), o_norm_w[l][None, :])

    x1 = _mix_out(ya, yd, proj, x2, w_o_attn[l].astype(BF16), w_o_delta[l].astype(BF16),
                  w_out[l].astype(BF16), norm_mix_post[l][None, :])
    return _ffn(x1, norm_ffn_pre[l][None, :], w_gate[l].astype(BF16), w_up[l].astype(BF16),
                w_down[l].astype(BF16), norm_ffn_post[l][None, :])


def kernel(x, norm_mix_pre, w_in, conv_w, a_log, dt_bias, o_norm_w, w_o_attn, w_o_delta, w_out,
           norm_mix_post, norm_ffn_pre, w_gate, w_up, w_down, norm_ffn_post):
    b, s, d = x.shape
    assert d == D_MODEL and s % 512 == 0
    outs = []
    for bi in range(b):
        x2 = x[bi]
        for l in range(w_in.shape[0]):
            x2 = _layer(x2, l, norm_mix_pre, w_in, conv_w, a_log, dt_bias, o_norm_w, w_o_attn,
                        w_o_delta, w_out, norm_mix_post, norm_ffn_pre, w_gate, w_up, w_down,
                        norm_ffn_post)
        outs.append(x2)
    return jnp.stack(outs, axis=0)
```

```python
import functools
import math

import jax
import jax.numpy as jnp
from jax import lax
from jax.experimental import pallas as pl
from jax.experimental.pallas import tpu as pltpu

D_MODEL = 1024
HEADS = 8
HEAD_DIM = 128
MOBA_BLOCK = 256
MOBA_TOPK = 3
ROPE_DIM = HEAD_DIM // 4
ROPE_THETA = 500000.0
DN_CONV = 4
ONES_ROWS = 16
DN_HEADS_PER_STEP = 4
DN_CHUNK = 256
D_FF = 2816
FF_CHUNK = 256
EPS = 1e-6
ATTN_UNROLL = 2
HALO = 16

F32 = jnp.float32
BF16 = jnp.bfloat16
NT = (((1,), (1,)), ((), ()))

VMEM_LIMIT = 56 * 1024 * 1024


def _bdot(a, b):
    return jnp.dot(a.astype(BF16), b.astype(BF16), preferred_element_type=F32)


def _sigmoid(x):
    return 1.0 / (1.0 + jnp.exp(-x))


def _proj_kernel(x_ref, nw_ref, w_ref, ws_ref, cos_ref, s1_ref, s2_ref,
                 out_ref, small_ref, kmean_ref, h_scr, *, tm, q_scale):
    j = pl.program_id(1)

    @pl.when(j == 0)
    def _():
        x = x_ref[...]
        ms = jnp.mean(x * x, axis=-1, keepdims=True)
        h = x * lax.rsqrt(ms + EPS) * nw_ref[...]
        h_scr[...] = h.astype(BF16)
        small_ref[...] = jnp.dot(h_scr[...], ws_ref[...], preferred_element_type=F32)

    acc = jnp.dot(h_scr[...], w_ref[...], preferred_element_type=F32)

    @pl.when(j >= 2)
    def _():
        out_ref[...] = acc.astype(BF16)

    def roped(hh):
        a = acc[:, hh * HEAD_DIM:(hh + 1) * HEAD_DIM]
        return (a * cos_ref[...] + pltpu.roll(a, HEAD_DIM - ROPE_DIM // 2, 1) * s1_ref[...]
                + pltpu.roll(a, ROPE_DIM // 2, 1) * s2_ref[...])

    @pl.when(j == 0)
    def _():
        for hh in range(HEADS):
            out_ref[:, hh * HEAD_DIM:(hh + 1) * HEAD_DIM] = (roped(hh) * q_scale).astype(BF16)

    @pl.when(j == 1)
    def _():
        for hh in range(HEADS):
            r = roped(hh)
            out_ref[:, hh * HEAD_DIM:(hh + 1) * HEAD_DIM] = r.astype(BF16)
            for g in range(tm // MOBA_BLOCK):
                blk = r[g * MOBA_BLOCK:(g + 1) * MOBA_BLOCK]
                kmean_ref[0, g:g + 1, hh * HEAD_DIM:(hh + 1) * HEAD_DIM] = (
                    jnp.sum(blk, axis=0, keepdims=True) * (1.0 / MOBA_BLOCK))


def _project(x2, norm_w, w_main, w_small, cos_t, s1_t, s2_t, *, tm=512):
    s = x2.shape[0]
    ncol = w_main.shape[1] // D_MODEL
    nblk = tm // MOBA_BLOCK
    kern = functools.partial(_proj_kernel, tm=tm, q_scale=math.log2(math.e) / math.sqrt(HEAD_DIM))
    return pl.pallas_call(
        kern,
        grid=(s // tm, ncol),
        in_specs=[
            pl.BlockSpec((tm, D_MODEL), lambda i, j: (i, 0)),
            pl.BlockSpec((1, D_MODEL), lambda i, j: (0, 0)),
            pl.BlockSpec((D_MODEL, D_MODEL), lambda i, j: (0, j)),
            pl.BlockSpec((D_MODEL, 128), lambda i, j: (0, 0)),
            pl.BlockSpec((tm, HEAD_DIM), lambda i, j: (i, 0)),
            pl.BlockSpec((tm, HEAD_DIM), lambda i, j: (i, 0)),
            pl.BlockSpec((tm, HEAD_DIM), lambda i, j: (i, 0)),
        ],
        out_specs=[
            pl.BlockSpec((tm, D_MODEL), lambda i, j: (i, j)),
            pl.BlockSpec((tm, 128), lambda i, j: (i, 0)),
            pl.BlockSpec((1, nblk, D_MODEL), lambda i, j: (i, 0, 0)),
        ],
        out_shape=[
            jax.ShapeDtypeStruct((s, ncol * D_MODEL), BF16),
            jax.ShapeDtypeStruct((s, 128), F32),
            jax.ShapeDtypeStruct((s // tm, nblk, D_MODEL), F32),
        ],
        scratch_shapes=[pltpu.VMEM((tm, D_MODEL), BF16)],
        compiler_params=pltpu.CompilerParams(
            dimension_semantics=("parallel", "arbitrary"), vmem_limit_bytes=VMEM_LIMIT),
        name="proj",
    )(x2, norm_w, w_main, w_small, cos_t, s1_t, s2_t)


def _attn_kernel(q_ref, k_ref, vt_ref, km_ref, o_ref, bias_scr, sa_scr, pb_scr, *, nb):
    j = pl.program_id(1)
    q = q_ref[...]
    blk = lax.broadcasted_iota(jnp.int32, (nb, MOBA_BLOCK), 0)
    blk_f = blk.astype(F32)

    gate = lax.dot_general(km_ref[...], q.astype(F32), NT,
                           precision=lax.Precision.HIGHEST, preferred_element_type=F32)
    gate = jnp.where(blk < j, gate, -jnp.inf)
    bias = jnp.full((nb, MOBA_BLOCK), -jnp.inf, F32)
    for _ in range(MOBA_TOPK):
        m = jnp.max(gate, axis=0, keepdims=True)
        first = jnp.min(jnp.where(gate == m, blk_f, float(nb)), axis=0, keepdims=True)
        pick = (blk_f == first) & (m > -jnp.inf)
        bias = jnp.where(pick, 0.0, bias)
        gate = jnp.where(pick, -jnp.inf, gate)
    bias_scr[...] = bias

    kj = k_ref[pl.ds(pl.multiple_of(j * MOBA_BLOCK, MOBA_BLOCK), MOBA_BLOCK), :]
    s = lax.dot_general(kj, q, NT, preferred_element_type=F32)
    krow = lax.broadcasted_iota(jnp.int32, (MOBA_BLOCK, MOBA_BLOCK), 0)
    qcol = lax.broadcasted_iota(jnp.int32, (MOBA_BLOCK, MOBA_BLOCK), 1)
    s = jnp.where(krow <= qcol, s, -jnp.inf)
    m0 = jnp.max(s, axis=0, keepdims=True)
    p = jnp.exp2(s - m0)
    acc0 = jnp.dot(vt_ref[0, j], p.astype(BF16), preferred_element_type=F32)

    U = ATTN_UNROLL
    B = MOBA_BLOCK

    def scores(g):
        n0 = jnp.minimum(g, nb // U - 1) * U
        kn = k_ref[pl.ds(pl.multiple_of(n0 * B, U * B), U * B), :]
        return lax.dot_general(kn, q, NT, preferred_element_type=F32)

    def values(g, p_of):
        n0 = jnp.maximum(g, 0) * U
        pv = jnp.dot(vt_ref[0, n0], p_of(0), preferred_element_type=F32)
        for u in range(1, U):
            pv = pv + jnp.dot(vt_ref[0, n0 + u], p_of(u), preferred_element_type=F32)
        return pv

    def softmax(g, s_of, m):
        sn = [s_of(u) + bias_scr[pl.ds(g * U + u, 1), :] for u in range(U)]
        m_new = m
        for u in range(U):
            m_new = jnp.maximum(m_new, jnp.max(sn[u], axis=0, keepdims=True))
        alpha = jnp.exp2(m - m_new)
        return m_new, alpha, [jnp.exp2(sn[u] - m_new).astype(BF16) for u in range(U)]

    def body(t, carry):
        m, acc, alpha_b = carry
        acc = alpha_b * acc + values(2 * t - 1, lambda u: pb_scr[u * B:(u + 1) * B, :])
        m, alpha_a, p_a = softmax(2 * t, lambda u: sa_scr[u * B:(u + 1) * B, :], m)
        s_b = scores(2 * t + 1)
        acc = alpha_a * acc + values(2 * t, lambda u: p_a[u])
        m, alpha_b, p_b = softmax(2 * t + 1, lambda u: s_b[u * B:(u + 1) * B], m)
        for u in range(U):
            pb_scr[u * B:(u + 1) * B, :] = p_b[u]
        sa_scr[...] = scores(2 * t + 2)
        return m, acc, alpha_b

    sa_scr[...] = scores(0)
    pb_scr[...] = jnp.zeros_like(pb_scr)
    trips = (j + 2 * U - 1) // (2 * U)
    _, acc, alpha_b = lax.fori_loop(0, trips, body, (m0, acc0, jnp.ones_like(m0)))
    acc = alpha_b * acc + values(2 * trips - 1, lambda u: pb_scr[u * B:(u + 1) * B, :])
    o_ref[...] = (acc[:HEAD_DIM] / acc[HEAD_DIM:HEAD_DIM + 1]).T.astype(BF16)


def _moba_attention(proj, v_t, kmean):
    s = proj.shape[0]
    nb = s // MOBA_BLOCK
    return pl.pallas_call(
        functools.partial(_attn_kernel, nb=nb),
        grid=(HEADS, nb),
        in_specs=[
            pl.BlockSpec((MOBA_BLOCK, HEAD_DIM), lambda h, j: (j, h)),
            pl.BlockSpec((s, HEAD_DIM), lambda h, j: (0, HEADS + h)),
            pl.BlockSpec((1, nb, HEAD_DIM + ONES_ROWS, MOBA_BLOCK),
                         lambda h, j: (h, 0, 0, 0)),
            pl.BlockSpec((nb, HEAD_DIM), lambda h, j: (0, h)),
        ],
        out_specs=pl.BlockSpec((MOBA_BLOCK, HEAD_DIM), lambda h, j: (j, h)),
        out_shape=jax.ShapeDtypeStruct((s, HEADS * HEAD_DIM), BF16),
        scratch_shapes=[pltpu.VMEM((nb, MOBA_BLOCK), F32),
                        pltpu.VMEM((ATTN_UNROLL * MOBA_BLOCK, MOBA_BLOCK), F32),
                        pltpu.VMEM((ATTN_UNROLL * MOBA_BLOCK, MOBA_BLOCK), BF16)],
        compiler_params=pltpu.CompilerParams(
            dimension_semantics=("parallel", "arbitrary"), vmem_limit_bytes=VMEM_LIMIT),
        name="moba",
    )(proj, proj, v_t, kmean)


def _delta_kernel(q_ref, k_ref, v_ref, qh_ref, kh_ref, vh_ref, z_ref, small_ref,
                  cwq_ref, cwk_ref, cwv_ref, alog_ref, dtb_ref, onw_ref,
                  o_ref, state_scr, xx_scr):
    hg = pl.program_id(0)
    c = pl.program_id(1)
    C = DN_CHUNK

    @pl.when(c == 0)
    def _():
        state_scr[...] = jnp.zeros_like(state_scr)

    def conv_silu(idx, x_ref, halo_ref, cw_ref):
        halo = jnp.where(c == 0, 0.0, halo_ref[...].astype(F32))
        xx_scr[idx, 0:HALO, :] = halo
        xx_scr[idx, HALO:HALO + C, :] = x_ref[...].astype(F32)
        y = jnp.zeros((C, DN_HEADS_PER_STEP * HEAD_DIM), F32)
        for t in range(DN_CONV):
            off = HALO - (DN_CONV - 1) + t
            y = y + cw_ref[t:t + 1, :] * xx_scr[idx, off:off + C, :]
        return y * _sigmoid(y)

    q_all = conv_silu(0, q_ref, qh_ref, cwq_ref)
    k_all = conv_silu(1, k_ref, kh_ref, cwk_ref)
    v_all = conv_silu(2, v_ref, vh_ref, cwv_ref)

    small = small_ref[...]
    lane = lax.broadcasted_iota(jnp.int32, (C, 128), 1)
    beta_all = _sigmoid(small)
    xs = small + dtb_ref[...]
    softplus = jnp.maximum(xs, 0.0) + jnp.log(1.0 + jnp.exp(-jnp.abs(xs)))
    g_all = -jnp.exp(alog_ref[...]) * softplus

    row = lax.broadcasted_iota(jnp.int32, (C, C), 0)
    col = lax.broadcasted_iota(jnp.int32, (C, C), 1)
    tril = row >= col
    strict = row > col
    rxc = row ^ col
    eye = (row == col).astype(F32)
    tril_f = tril.astype(F32)

    heads = range(DN_HEADS_PER_STEP)
    sls = [slice(hh * HEAD_DIM, (hh + 1) * HEAD_DIM) for hh in heads]

    def hmap(f, *lists):
        return [f(*xs) for xs in zip(*lists)]

    def pick(table, lane_idx):
        return jnp.sum(jnp.where(lane == lane_idx, table, 0.0), axis=-1, keepdims=True)

    q = [q_all[:, sl] for sl in sls]
    k = [k_all[:, sl] for sl in sls]
    v = [v_all[:, sl] for sl in sls]
    q = hmap(lambda x: x * lax.rsqrt(jnp.sum(x * x, axis=-1, keepdims=True) + EPS) * (HEAD_DIM ** -0.5), q)
    k = hmap(lambda x: x * lax.rsqrt(jnp.sum(x * x, axis=-1, keepdims=True) + EPS), k)
    beta = [pick(beta_all, hg * DN_HEADS_PER_STEP + hh) for hh in heads]
    g = [pick(g_all, hg * DN_HEADS_PER_STEP + hh + HEADS) for hh in heads]

    gcum_b = hmap(lambda x: jnp.dot(tril_f, jnp.broadcast_to(x, (C, 128)),
                                    precision=lax.Precision.HIGHEST, preferred_element_type=F32), g)
    gc = [x[:, 0:1] for x in gcum_b]
    gr = [x.T[0:1, :] for x in gcum_b]
    decay = hmap(lambda c_, r_: jnp.where(tril, jnp.exp(jnp.where(tril, c_ - r_, 0.0)), 0.0), gc, gr)
    e_g = hmap(jnp.exp, gc)
    g_last = [x[C - 1:C, :] for x in gc]

    kb = hmap(lambda a_, b_: a_ * b_, k, beta)
    vb = hmap(lambda a_, b_: a_ * b_, v, beta)
    a = hmap(lambda kb_, q_, k_: lax.dot_general(
        jnp.concatenate([kb_, q_], axis=0).astype(BF16), k_.astype(BF16), NT,
        preferred_element_type=F32), kb, q, k)
    lmat = hmap(lambda a_, d_: jnp.where(strict, a_[:C] * d_, 0.0), a, decay)
    attn = hmap(lambda a_, d_: a_[C:] * d_, a, decay)

    d1 = hmap(lambda l_: jnp.where(rxc < 16, l_, 0.0), lmat)
    d2 = hmap(lambda x: _bdot(x, x), d1)
    d4 = hmap(lambda x: _bdot(x, x), d2)
    d8 = hmap(lambda x: _bdot(x, x), d4)
    p1 = hmap(lambda x, y: _bdot(eye - x, eye + y), d1, d2)
    p2 = hmap(lambda x, y: _bdot(eye + x, eye + y), d4, d8)
    tmat = hmap(_bdot, p1, p2)
    sz = 16
    while sz < C:
        off = hmap(lambda l_: jnp.where((rxc >= sz) & (rxc < 2 * sz), l_, 0.0), lmat)
        x = hmap(_bdot, tmat, off)
        x = hmap(_bdot, x, tmat)
        tmat = hmap(lambda t_, x_: t_ - x_, tmat, x)
        sz *= 2

    uw = hmap(lambda t_, vb_, kb_, e_: _bdot(t_, jnp.concatenate([vb_, kb_ * e_], axis=1)),
              tmat, vb, kb, e_g)
    state = [state_scr[hh] for hh in heads]
    wq = hmap(lambda uw_, q_, e_, s_: _bdot(jnp.concatenate([uw_[:, HEAD_DIM:], q_ * e_], axis=0), s_),
              uw, q, e_g, state)
    v_new = hmap(lambda uw_, wq_: uw_[:, :HEAD_DIM] - wq_[:C], uw, wq)
    av = hmap(_bdot, attn, v_new)
    kt = hmap(lambda k_, gl_, gc_: (k_ * jnp.exp(gl_ - gc_)).T, k, g_last, gc)
    ds = hmap(_bdot, kt, v_new)
    for hh in heads:
        state_scr[hh] = state[hh] * jnp.exp(g_last[hh]) + ds[hh]
        o = wq[hh][C:] + av[hh]
        y = o * lax.rsqrt(jnp.mean(o * o, axis=-1, keepdims=True) + EPS) * onw_ref[...]
        z = z_ref[:, sls[hh]].astype(F32)
        o_ref[:, sls[hh]] = (y * (z * _sigmoid(z))).astype(BF16)


def _gated_deltanet(proj, small, conv_w, alog_v, dtb_v, onw):
    s = proj.shape[0]
    nc = s // DN_CHUNK
    rb = DN_CHUNK // HALO
    hp = DN_HEADS_PER_STEP
    ng = HEADS // hp
    width = hp * HEAD_DIM

    def col(part):
        return pl.BlockSpec((DN_CHUNK, width), lambda h, c: (c, part * ng + h))

    def halo(part):
        return pl.BlockSpec((HALO, width), lambda h, c: (jnp.maximum(c * rb - 1, 0), part * ng + h))

    def cw(part):
        return pl.BlockSpec((DN_CONV, width), lambda h, c: (0, part * ng + h))

    vec = pl.BlockSpec((1, 128), lambda h, c: (0, 0))
    return pl.pallas_call(
        _delta_kernel,
        grid=(ng, nc),
        in_specs=[col(3), col(4), col(5),
                  halo(3), halo(4), halo(5),
                  col(6),
                  pl.BlockSpec((DN_CHUNK, 128), lambda h, c: (c, 0)),
                  cw(0), cw(1), cw(2),
                  vec, vec, vec],
        out_specs=pl.BlockSpec((DN_CHUNK, width), lambda h, c: (c, h)),
        out_shape=jax.ShapeDtypeStruct((s, HEADS * HEAD_DIM), BF16),
        scratch_shapes=[pltpu.VMEM((hp, HEAD_DIM, HEAD_DIM), F32),
                        pltpu.VMEM((3, HALO + DN_CHUNK, width), F32)],
        compiler_params=pltpu.CompilerParams(
            dimension_semantics=("parallel", "arbitrary"), vmem_limit_bytes=VMEM_LIMIT),
        name="deltanet",
    )(proj, proj, proj, proj, proj, proj, proj, small, conv_w, conv_w, conv_w, alog_v, dtb_v, onw)


def _mix_out_kernel(ya_ref, yd_ref, ga_ref, gd_ref, x_ref, wa_ref, wd_ref, wo_ref, nw_ref, o_ref):
    pa = jnp.dot(ya_ref[...], wa_ref[...], preferred_element_type=F32)
    pd = jnp.dot(yd_ref[...], wd_ref[...], preferred_element_type=F32)
    merged = _sigmoid(ga_ref[...].astype(F32)) * pa + _sigmoid(gd_ref[...].astype(F32)) * pd
    mo = jnp.dot(merged.astype(BF16), wo_ref[...], preferred_element_type=F32)
    y = mo * lax.rsqrt(jnp.mean(mo * mo, axis=-1, keepdims=True) + EPS) * nw_ref[...]
    o_ref[...] = x_ref[...] + y


def _mix_out(ya, yd, proj, x2, wa, wd, wo, nw, *, tm=512):
    s = x2.shape[0]
    row = lambda i: (i, 0)
    full = lambda i: (0, 0)
    wspec = pl.BlockSpec((D_MODEL, D_MODEL), full)
    return pl.pallas_call(
        _mix_out_kernel,
        grid=(s // tm,),
        in_specs=[pl.BlockSpec((tm, D_MODEL), row), pl.BlockSpec((tm, D_MODEL), row),
                  pl.BlockSpec((tm, D_MODEL), lambda i: (i, 7)),
                  pl.BlockSpec((tm, D_MODEL), lambda i: (i, 8)),
                  pl.BlockSpec((tm, D_MODEL), row),
                  wspec, wspec, wspec, pl.BlockSpec((1, D_MODEL), full)],
        out_specs=pl.BlockSpec((tm, D_MODEL), row),
        out_shape=jax.ShapeDtypeStruct((s, D_MODEL), F32),
        compiler_params=pltpu.CompilerParams(
            dimension_semantics=("parallel",), vmem_limit_bytes=VMEM_LIMIT),
        name="mix_out",
    )(ya, yd, proj, proj, x2, wa, wd, wo, nw)


def _ffn_kernel(x_ref, npre_ref, wg_ref, wu_ref, wd_ref, npost_ref, o_ref):
    x = x_ref[...]
    h = (x * lax.rsqrt(jnp.mean(x * x, axis=-1, keepdims=True) + EPS) * npre_ref[...]).astype(BF16)
    acc = jnp.zeros(x.shape, F32)
    for cc in range(D_FF // FF_CHUNK):
        sl = slice(cc * FF_CHUNK, (cc + 1) * FF_CHUNK)
        g = jnp.dot(h, wg_ref[:, sl], preferred_element_type=F32)
        u = jnp.dot(h, wu_ref[:, sl], preferred_element_type=F32)
        act = (g * _sigmoid(g) * u).astype(BF16)
        acc = acc + jnp.dot(act, wd_ref[sl, :], preferred_element_type=F32)
    y = acc * lax.rsqrt(jnp.mean(acc * acc, axis=-1, keepdims=True) + EPS) * npost_ref[...]
    o_ref[...] = x + y


def _ffn(x1, npre, wg, wu, wd, npost, *, tm=256):
    s = x1.shape[0]
    row = lambda i: (i, 0)
    full = lambda i: (0, 0)
    return pl.pallas_call(
        _ffn_kernel,
        grid=(s // tm,),
        in_specs=[pl.BlockSpec((tm, D_MODEL), row), pl.BlockSpec((1, D_MODEL), full),
                  pl.BlockSpec((D_MODEL, D_FF), full), pl.BlockSpec((D_MODEL, D_FF), full),
                  pl.BlockSpec((D_FF, D_MODEL), full), pl.BlockSpec((1, D_MODEL), full)],
        out_specs=pl.BlockSpec((tm, D_MODEL), row),
        out_shape=jax.ShapeDtypeStruct((s, D_MODEL), F32),
        compiler_params=pltpu.CompilerParams(
            dimension_semantics=("parallel",), vmem_limit_bytes=VMEM_LIMIT),
        name="ffn",
    )(x1, npre, wg, wu, wd, npost)


def _rope_tables(s):
    half = ROPE_DIM // 2
    inv = ROPE_THETA ** (-jnp.arange(half, dtype=F32) * 2.0 / ROPE_DIM)
    ang = jnp.arange(s).astype(F32)[:, None] * inv[None, :]
    cos, sin = jnp.cos(ang), jnp.sin(ang)
    ones = jnp.ones((s, HEAD_DIM - ROPE_DIM), F32)
    zeros_half = jnp.zeros((s, half), F32)
    zeros_rest = jnp.zeros((s, HEAD_DIM - ROPE_DIM), F32)
    cos_t = jnp.concatenate([cos, cos, ones], axis=-1)
    s1_t = jnp.concatenate([-sin, zeros_half, zeros_rest], axis=-1)
    s2_t = jnp.concatenate([zeros_half, sin, zeros_rest], axis=-1)
    return cos_t, s1_t, s2_t


def _layer(x2, l, norm_mix_pre, w_in, conv_w, a_log, dt_bias, o_norm_w, w_o_attn, w_o_delta,
           w_out, norm_mix_post, norm_ffn_pre, w_gate, w_up, w_down, norm_ffn_post):
    s = x2.shape[0]
    nb = s // MOBA_BLOCK
    wide = 7 * D_MODEL
    nsmall = 2 * HEADS
    w = w_in[l]
    w_main = jnp.concatenate([w[:, :wide], w[:, wide + nsmall:]], axis=1).astype(BF16)
    w_small = jnp.pad(w[:, wide:wide + nsmall], ((0, 0), (0, 128 - nsmall))).astype(BF16)
    cos_t, s1_t, s2_t = _rope_tables(s)

    proj, small, kmean = _project(x2, norm_mix_pre[l][None, :], w_main, w_small, cos_t, s1_t, s2_t)
    kmean = kmean.reshape(nb, D_MODEL)

    v_t = proj[:, 2 * D_MODEL:3 * D_MODEL].reshape(nb, MOBA_BLOCK, HEADS, HEAD_DIM)
    v_t = v_t.transpose(2, 0, 3, 1)
    v_t = jnp.concatenate([v_t, jnp.ones((HEADS, nb, ONES_ROWS, MOBA_BLOCK), BF16)], axis=2)
    ya = _moba_attention(proj, v_t, kmean)

    pad = lambda vec: jnp.pad(vec.astype(F32), (HEADS, 128 - 2 * HEADS))[None, :]
    yd = _gated_deltanet(proj, small, conv_w[l], pad(a_log[l]), pad(dt_bias[l]), o_norm_w[l][None, :])

    x1 = _mix_out(ya, yd, proj, x2, w_o_attn[l].astype(BF16), w_o_delta[l].astype(BF16),
                  w_out[l].astype(BF16), norm_mix_post[l][None, :])
    return _ffn(x1, norm_ffn_pre[l][None, :], w_gate[l].astype(BF16), w_up[l].astype(BF16),
                w_down[l].astype(BF16), norm_ffn_post[l][None, :])


def kernel(x, norm_mix_pre, w_in, conv_w, a_log, dt_bias, o_norm_w, w_o_attn, w_o_delta, w_out,
           norm_mix_post, norm_ffn_pre, w_gate, w_up, w_down, norm_ffn_post):
    b, s, d = x.shape
    assert d == D_MODEL and s % 512 == 0
    outs = []
    for bi in range(b):
        x2 = x[bi]
        for l in range(w_in.shape[0]):
            x2 = _layer(x2, l, norm_mix_pre, w_in, conv_w, a_log, dt_bias, o_norm_w, w_o_attn,
                        w_o_delta, w_out, norm_mix_post, norm_ffn_pre, w_gate, w_up, w_down,
                        norm_ffn_post)
        outs.append(x2)
    return jnp.stack(outs, axis=0)
```

```python
import functools
import math

import jax
import jax.numpy as jnp
from jax import lax
from jax.experimental import pallas as pl
from jax.experimental.pallas import tpu as pltpu

D_MODEL = 1024
HEADS = 8
HEAD_DIM = 128
MOBA_BLOCK = 256
MOBA_TOPK = 3
ROPE_DIM = HEAD_DIM // 4
ROPE_THETA = 500000.0
DN_CONV = 4
ONES_ROWS = 16
DN_HEADS_PER_STEP = 4
DN_CHUNK = 256
D_FF = 2816
FF_CHUNK = 256
EPS = 1e-6
ATTN_UNROLL = 2
HALO = 16

F32 = jnp.float32
BF16 = jnp.bfloat16
NT = (((1,), (1,)), ((), ()))

VMEM_LIMIT = 56 * 1024 * 1024


def _bdot(a, b):
    return jnp.dot(a.astype(BF16), b.astype(BF16), preferred_element_type=F32)


def _sigmoid(x):
    return 1.0 / (1.0 + jnp.exp(-x))


def _proj_kernel(x_ref, nw_ref, w_ref, ws_ref, cos_ref, s1_ref, s2_ref,
                 out_ref, small_ref, kmean_ref, vt_ref, *, tm, q_scale):
    x = x_ref[...]
    ms = jnp.mean(x * x, axis=-1, keepdims=True)
    h = (x * lax.rsqrt(ms + EPS) * nw_ref[...]).astype(BF16)
    small_ref[...] = jnp.dot(h, ws_ref[...], preferred_element_type=F32)
    heads = [slice(hh * HEAD_DIM, (hh + 1) * HEAD_DIM) for hh in range(HEADS)]
    groups = [slice(g * MOBA_BLOCK, (g + 1) * MOBA_BLOCK) for g in range(tm // MOBA_BLOCK)]

    def roped(a):
        return (a * cos_ref[...] + pltpu.roll(a, HEAD_DIM - ROPE_DIM // 2, 1) * s1_ref[...]
                + pltpu.roll(a, ROPE_DIM // 2, 1) * s2_ref[...])

    for c in range(out_ref.shape[1] // D_MODEL):
        cols = slice(c * D_MODEL, (c + 1) * D_MODEL)
        acc = jnp.dot(h, w_ref[:, cols], preferred_element_type=F32)
        if c == 0:
            for hs in heads:
                out_ref[:, hs] = (roped(acc[:, hs]) * q_scale).astype(BF16)
        elif c == 1:
            for hs in heads:
                r = roped(acc[:, hs])
                out_ref[:, D_MODEL + hs.start:D_MODEL + hs.stop] = r.astype(BF16)
                for g, gs in enumerate(groups):
                    kmean_ref[0, g:g + 1, hs] = jnp.sum(r[gs], axis=0, keepdims=True) * (1.0 / MOBA_BLOCK)
        elif c == 2:
            out_ref[:, cols] = acc.astype(BF16)
            for hh, hs in enumerate(heads):
                for g, gs in enumerate(groups):
                    vt_ref[hh, g, 0:HEAD_DIM, :] = acc[gs, hs].T.astype(BF16)
                    vt_ref[hh, g, HEAD_DIM:, :] = jnp.ones((ONES_ROWS, MOBA_BLOCK), BF16)
        else:
            out_ref[:, cols] = acc.astype(BF16)


def _project(x2, norm_w, w_main, w_small, cos_t, s1_t, s2_t, *, tm=512):
    s = x2.shape[0]
    width = w_main.shape[1]
    nblk = tm // MOBA_BLOCK
    kern = functools.partial(_proj_kernel, tm=tm, q_scale=math.log2(math.e) / math.sqrt(HEAD_DIM))
    row = lambda i: (i, 0)
    full = lambda i: (0, 0)
    return pl.pallas_call(
        kern,
        grid=(s // tm,),
        in_specs=[
            pl.BlockSpec((tm, D_MODEL), row),
            pl.BlockSpec((1, D_MODEL), full),
            pl.BlockSpec((D_MODEL, width), full, pipeline_mode=pl.Buffered(1)),
            pl.BlockSpec((D_MODEL, 128), full),
            pl.BlockSpec((tm, HEAD_DIM), row),
            pl.BlockSpec((tm, HEAD_DIM), row),
            pl.BlockSpec((tm, HEAD_DIM), row),
        ],
        out_specs=[
            pl.BlockSpec((tm, width), row),
            pl.BlockSpec((tm, 128), row),
            pl.BlockSpec((1, nblk, D_MODEL), lambda i: (i, 0, 0)),
            pl.BlockSpec((HEADS, nblk, HEAD_DIM + ONES_ROWS, MOBA_BLOCK), lambda i: (0, i, 0, 0)),
        ],
        out_shape=[
            jax.ShapeDtypeStruct((s, width), BF16),
            jax.ShapeDtypeStruct((s, 128), F32),
            jax.ShapeDtypeStruct((s // tm, nblk, D_MODEL), F32),
            jax.ShapeDtypeStruct((HEADS, s // MOBA_BLOCK, HEAD_DIM + ONES_ROWS, MOBA_BLOCK), BF16),
        ],
        compiler_params=pltpu.CompilerParams(
            dimension_semantics=("parallel",), vmem_limit_bytes=VMEM_LIMIT),
        name="proj",
    )(x2, norm_w, w_main, w_small, cos_t, s1_t, s2_t)


def _attn_kernel(q_ref, k_ref, vt_ref, km_ref, o_ref, bias_scr, sa_scr, pb_scr, *, nb):
    j = pl.program_id(1)
    q = q_ref[...]
    U = ATTN_UNROLL
    B = MOBA_BLOCK
    blk = lax.broadcasted_iota(jnp.int32, (nb, B), 0)
    blk_f = blk.astype(F32)

    def scores(g):
        n0 = jnp.minimum(g, nb // U - 1) * U
        kn = k_ref[pl.ds(pl.multiple_of(n0 * B, U * B), U * B), :]
        return lax.dot_general(kn, q, NT, preferred_element_type=F32)

    km = km_ref[...]
    km_hi = km.astype(BF16)
    km_lo = (km - km_hi.astype(F32)).astype(BF16)
    gate2 = lax.dot_general(jnp.concatenate([km_hi, km_lo], axis=0), q, NT,
                            preferred_element_type=F32)
    kj = k_ref[pl.ds(pl.multiple_of(j * B, B), B), :]
    s = lax.dot_general(kj, q, NT, preferred_element_type=F32)
    sa_scr[...] = scores(0)
    pb_scr[...] = jnp.zeros_like(pb_scr)

    gate = jnp.where(blk < j, gate2[:nb] + gate2[nb:], -jnp.inf)
    bias = jnp.full((nb, B), -jnp.inf, F32)
    for _ in range(MOBA_TOPK):
        m = jnp.max(gate, axis=0, keepdims=True)
        first = jnp.min(jnp.where(gate == m, blk_f, float(nb)), axis=0, keepdims=True)
        pick = (blk_f == first) & (m > -jnp.inf)
        bias = jnp.where(pick, 0.0, bias)
        gate = jnp.where(pick, -jnp.inf, gate)
    bias_scr[...] = bias

    krow = lax.broadcasted_iota(jnp.int32, (B, B), 0)
    qcol = lax.broadcasted_iota(jnp.int32, (B, B), 1)
    s = jnp.where(krow <= qcol, s, -jnp.inf)
    m0 = jnp.max(s, axis=0, keepdims=True)
    p = jnp.exp2(s - m0)
    acc0 = jnp.dot(vt_ref[0, j], p.astype(BF16), preferred_element_type=F32)


    def values(g, p_of):
        n0 = jnp.maximum(g, 0) * U
        pv = jnp.dot(vt_ref[0, n0], p_of(0), preferred_element_type=F32)
        for u in range(1, U):
            pv = pv + jnp.dot(vt_ref[0, n0 + u], p_of(u), preferred_element_type=F32)
        return pv

    def softmax(g, s_of, m):
        sn = [s_of(u) + bias_scr[pl.ds(g * U + u, 1), :] for u in range(U)]
        m_new = m
        for u in range(U):
            m_new = jnp.maximum(m_new, jnp.max(sn[u], axis=0, keepdims=True))
        alpha = jnp.exp2(m - m_new)
        return m_new, alpha, [jnp.exp2(sn[u] - m_new).astype(BF16) for u in range(U)]

    def body(t, carry):
        m, acc, alpha_b = carry
        acc = alpha_b * acc + values(2 * t - 1, lambda u: pb_scr[u * B:(u + 1) * B, :])
        m, alpha_a, p_a = softmax(2 * t, lambda u: sa_scr[u * B:(u + 1) * B, :], m)
        s_b = scores(2 * t + 1)
        acc = alpha_a * acc + values(2 * t, lambda u: p_a[u])
        m, alpha_b, p_b = softmax(2 * t + 1, lambda u: s_b[u * B:(u + 1) * B], m)
        for u in range(U):
            pb_scr[u * B:(u + 1) * B, :] = p_b[u]
        sa_scr[...] = scores(2 * t + 2)
        return m, acc, alpha_b

    trips = (j + 2 * U - 1) // (2 * U)
    _, acc, alpha_b = lax.fori_loop(0, trips, body, (m0, acc0, jnp.ones_like(m0)))
    acc = alpha_b * acc + values(2 * trips - 1, lambda u: pb_scr[u * B:(u + 1) * B, :])
    o_ref[...] = (acc[:HEAD_DIM] / acc[HEAD_DIM:HEAD_DIM + 1]).T.astype(BF16)


def _moba_attention(proj, v_t, kmean):
    s = proj.shape[0]
    nb = s // MOBA_BLOCK
    return pl.pallas_call(
        functools.partial(_attn_kernel, nb=nb),
        grid=(HEADS, nb),
        in_specs=[
            pl.BlockSpec((MOBA_BLOCK, HEAD_DIM), lambda h, j: (j, h)),
            pl.BlockSpec((s, HEAD_DIM), lambda h, j: (0, HEADS + h)),
            pl.BlockSpec((1, nb, HEAD_DIM + ONES_ROWS, MOBA_BLOCK),
                         lambda h, j: (h, 0, 0, 0)),
            pl.BlockSpec((nb, HEAD_DIM), lambda h, j: (0, h)),
        ],
        out_specs=pl.BlockSpec((MOBA_BLOCK, HEAD_DIM), lambda h, j: (j, h)),
        out_shape=jax.ShapeDtypeStruct((s, HEADS * HEAD_DIM), BF16),
        scratch_shapes=[pltpu.VMEM((nb, MOBA_BLOCK), F32),
                        pltpu.VMEM((ATTN_UNROLL * MOBA_BLOCK, MOBA_BLOCK), F32),
                        pltpu.VMEM((ATTN_UNROLL * MOBA_BLOCK, MOBA_BLOCK), BF16)],
        compiler_params=pltpu.CompilerParams(
            dimension_semantics=("parallel", "arbitrary"), vmem_limit_bytes=VMEM_LIMIT),
        name="moba",
    )(proj, proj, v_t, kmean)


def _delta_kernel(q_ref, k_ref, v_ref, qh_ref, kh_ref, vh_ref, z_ref, small_ref,
                  cwq_ref, cwk_ref, cwv_ref, alog_ref, dtb_ref, onw_ref,
                  o_ref, state_scr, xx_scr):
    hg = pl.program_id(0)
    c = pl.program_id(1)
    C = DN_CHUNK

    @pl.when(c == 0)
    def _():
        state_scr[...] = jnp.zeros_like(state_scr)

    def conv_silu(idx, x_ref, halo_ref, cw_ref):
        halo = jnp.where(c == 0, 0.0, halo_ref[...].astype(F32))
        xx_scr[idx, 0:HALO, :] = halo
        xx_scr[idx, HALO:HALO + C, :] = x_ref[...].astype(F32)
        y = jnp.zeros((C, DN_HEADS_PER_STEP * HEAD_DIM), F32)
        for t in range(DN_CONV):
            off = HALO - (DN_CONV - 1) + t
            y = y + cw_ref[t:t + 1, :] * xx_scr[idx, off:off + C, :]
        return y * _sigmoid(y)

    q_all = conv_silu(0, q_ref, qh_ref, cwq_ref)
    k_all = conv_silu(1, k_ref, kh_ref, cwk_ref)
    v_all = conv_silu(2, v_ref, vh_ref, cwv_ref)

    small = small_ref[...]
    lane = lax.broadcasted_iota(jnp.int32, (C, 128), 1)
    beta_all = _sigmoid(small)
    xs = small + dtb_ref[...]
    softplus = jnp.maximum(xs, 0.0) + jnp.log(1.0 + jnp.exp(-jnp.abs(xs)))
    g_all = -jnp.exp(alog_ref[...]) * softplus

    row = lax.broadcasted_iota(jnp.int32, (C, C), 0)
    col = lax.broadcasted_iota(jnp.int32, (C, C), 1)
    tril = row >= col
    strict = row > col
    rxc = row ^ col
    eye = (row == col).astype(F32)
    tril_f = tril.astype(F32)

    heads = range(DN_HEADS_PER_STEP)
    sls = [slice(hh * HEAD_DIM, (hh + 1) * HEAD_DIM) for hh in heads]

    def hmap(f, *lists):
        return [f(*xs) for xs in zip(*lists)]

    def pick(table, lane_idx):
        return jnp.sum(jnp.where(lane == lane_idx, table, 0.0), axis=-1, keepdims=True)

    q = [q_all[:, sl] for sl in sls]
    k = [k_all[:, sl] for sl in sls]
    v = [v_all[:, sl] for sl in sls]
    q = hmap(lambda x: x * lax.rsqrt(jnp.sum(x * x, axis=-1, keepdims=True) + EPS) * (HEAD_DIM ** -0.5), q)
    k = hmap(lambda x: x * lax.rsqrt(jnp.sum(x * x, axis=-1, keepdims=True) + EPS), k)
    beta = [pick(beta_all, hg * DN_HEADS_PER_STEP + hh) for hh in heads]
    g = [pick(g_all, hg * DN_HEADS_PER_STEP + hh + HEADS) for hh in heads]

    gcum_b = hmap(lambda x: jnp.dot(tril_f, jnp.broadcast_to(x, (C, 128)),
                                    precision=lax.Precision.HIGHEST, preferred_element_type=F32), g)
    gc = [x[:, 0:1] for x in gcum_b]
    gr = [x.T[0:1, :] for x in gcum_b]
    decay = hmap(lambda c_, r_: jnp.where(tril, jnp.exp(jnp.where(tril, c_ - r_, 0.0)), 0.0), gc, gr)
    e_g = hmap(jnp.exp, gc)
    g_last = [x[C - 1:C, :] for x in gc]

    kb = hmap(lambda a_, b_: a_ * b_, k, beta)
    vb = hmap(lambda a_, b_: a_ * b_, v, beta)
    a = hmap(lambda kb_, q_, k_: lax.dot_general(
        jnp.concatenate([kb_, q_], axis=0).astype(BF16), k_.astype(BF16), NT,
        preferred_element_type=F32), kb, q, k)
    lmat = hmap(lambda a_, d_: jnp.where(strict, a_[:C] * d_, 0.0), a, decay)
    attn = hmap(lambda a_, d_: a_[C:] * d_, a, decay)

    d1 = hmap(lambda l_: jnp.where(rxc < 16, l_, 0.0), lmat)
    d2 = hmap(lambda x: _bdot(x, x), d1)
    d4 = hmap(lambda x: _bdot(x, x), d2)
    d8 = hmap(lambda x: _bdot(x, x), d4)
    p1 = hmap(lambda x, y: _bdot(eye - x, eye + y), d1, d2)
    p2 = hmap(lambda x, y: _bdot(eye + x, eye + y), d4, d8)
    tmat = hmap(_bdot, p1, p2)
    sz = 16
    while sz < C:
        off = hmap(lambda l_: jnp.where((rxc >= sz) & (rxc < 2 * sz), l_, 0.0), lmat)
        x = hmap(_bdot, tmat, off)
        x = hmap(_bdot, x, tmat)
        tmat = hmap(lambda t_, x_: t_ - x_, tmat, x)
        sz *= 2

    uw = hmap(lambda t_, vb_, kb_, e_: _bdot(t_, jnp.concatenate([vb_, kb_ * e_], axis=1)),
              tmat, vb, kb, e_g)
    state = [state_scr[hh] for hh in heads]
    wq = hmap(lambda uw_, q_, e_, s_: _bdot(jnp.concatenate([uw_[:, HEAD_DIM:], q_ * e_], axis=0), s_),
              uw, q, e_g, state)
    v_new = hmap(lambda uw_, wq_: uw_[:, :HEAD_DIM] - wq_[:C], uw, wq)
    av = hmap(_bdot, attn, v_new)
    kt = hmap(lambda k_, gl_, gc_: (k_ * jnp.exp(gl_ - gc_)).T, k, g_last, gc)
    ds = hmap(_bdot, kt, v_new)
    for hh in heads:
        state_scr[hh] = state[hh] * jnp.exp(g_last[hh]) + ds[hh]
        o = wq[hh][C:] + av[hh]
        y = o * lax.rsqrt(jnp.mean(o * o, axis=-1, keepdims=True) + EPS) * onw_ref[...]
        z = z_ref[:, sls[hh]].astype(F32)
        o_ref[:, sls[hh]] = (y * (z * _sigmoid(z))).astype(BF16)


def _gated_deltanet(proj, small, conv_w, alog_v, dtb_v, onw):
    s = proj.shape[0]
    nc = s // DN_CHUNK
    rb = DN_CHUNK // HALO
    hp = DN_HEADS_PER_STEP
    ng = HEADS // hp
    width = hp * HEAD_DIM

    def col(part):
        return pl.BlockSpec((DN_CHUNK, width), lambda h, c: (c, part * ng + h))

    def halo(part):
        return pl.BlockSpec((HALO, width), lambda h, c: (jnp.maximum(c * rb - 1, 0), part * ng + h))

    def cw(part):
        return pl.BlockSpec((DN_CONV, width), lambda h, c: (0, part * ng + h))

    vec = pl.BlockSpec((1, 128), lambda h, c: (0, 0))
    return pl.pallas_call(
        _delta_kernel,
        grid=(ng, nc),
        in_specs=[col(3), col(4), col(5),
                  halo(3), halo(4), halo(5),
                  col(6),
                  pl.BlockSpec((DN_CHUNK, 128), lambda h, c: (c, 0)),
                  cw(0), cw(1), cw(2),
                  vec, vec, vec],
        out_specs=pl.BlockSpec((DN_CHUNK, width), lambda h, c: (c, h)),
        out_shape=jax.ShapeDtypeStruct((s, HEADS * HEAD_DIM), BF16),
        scratch_shapes=[pltpu.VMEM((hp, HEAD_DIM, HEAD_DIM), F32),
                        pltpu.VMEM((3, HALO + DN_CHUNK, width), F32)],
        compiler_params=pltpu.CompilerParams(
            dimension_semantics=("parallel", "arbitrary"), vmem_limit_bytes=VMEM_LIMIT),
        name="deltanet",
    )(proj, proj, proj, proj, proj, proj, proj, small, conv_w, conv_w, conv_w, alog_v, dtb_v, onw)


def _mix_out_kernel(ya_ref, yd_ref, ga_ref, gd_ref, x_ref, wa_ref, wd_ref, wo_ref, nw_ref, o_ref):
    pa = jnp.dot(ya_ref[...], wa_ref[...], preferred_element_type=F32)
    pd = jnp.dot(yd_ref[...], wd_ref[...], preferred_element_type=F32)
    merged = _sigmoid(ga_ref[...].astype(F32)) * pa + _sigmoid(gd_ref[...].astype(F32)) * pd
    mo = jnp.dot(merged.astype(BF16), wo_ref[...], preferred_element_type=F32)
    y = mo * lax.rsqrt(jnp.mean(mo * mo, axis=-1, keepdims=True) + EPS) * nw_ref[...]
    o_ref[...] = x_ref[...] + y


def _mix_out(ya, yd, proj, x2, wa, wd, wo, nw, *, tm=512):
    s = x2.shape[0]
    row = lambda i: (i, 0)
    full = lambda i: (0, 0)
    wspec = pl.BlockSpec((D_MODEL, D_MODEL), full)
    return pl.pallas_call(
        _mix_out_kernel,
        grid=(s // tm,),
        in_specs=[pl.BlockSpec((tm, D_MODEL), row), pl.BlockSpec((tm, D_MODEL), row),
                  pl.BlockSpec((tm, D_MODEL), lambda i: (i, 7)),
                  pl.BlockSpec((tm, D_MODEL), lambda i: (i, 8)),
                  pl.BlockSpec((tm, D_MODEL), row),
                  wspec, wspec, wspec, pl.BlockSpec((1, D_MODEL), full)],
        out_specs=pl.BlockSpec((tm, D_MODEL), row),
        out_shape=jax.ShapeDtypeStruct((s, D_MODEL), F32),
        compiler_params=pltpu.CompilerParams(
            dimension_semantics=("parallel",), vmem_limit_bytes=VMEM_LIMIT),
        name="mix_out",
    )(ya, yd, proj, proj, x2, wa, wd, wo, nw)


def _ffn_kernel(x_ref, npre_ref, wg_ref, wu_ref, wd_ref, npost_ref, o_ref):
    x = x_ref[...]
    h = (x * lax.rsqrt(jnp.mean(x * x, axis=-1, keepdims=True) + EPS) * npre_ref[...]).astype(BF16)
    acc = jnp.zeros(x.shape, F32)
    for cc in range(D_FF // FF_CHUNK):
        sl = slice(cc * FF_CHUNK, (cc + 1) * FF_CHUNK)
        g = jnp.dot(h, wg_ref[:, sl], preferred_element_type=F32)
        u = jnp.dot(h, wu_ref[:, sl], preferred_element_type=F32)
        act = (g * _sigmoid(g) * u).astype(BF16)
        acc = acc + jnp.dot(act, wd_ref[sl, :], preferred_element_type=F32)
    y = acc * lax.rsqrt(jnp.mean(acc * acc, axis=-1, keepdims=True) + EPS) * npost_ref[...]
    o_ref[...] = x + y


def _ffn(x1, npre, wg, wu, wd, npost, *, tm=512):
    s = x1.shape[0]
    row = lambda i: (i, 0)
    full = lambda i: (0, 0)
    once = pl.Buffered(1)
    return pl.pallas_call(
        _ffn_kernel,
        grid=(s // tm,),
        in_specs=[pl.BlockSpec((tm, D_MODEL), row), pl.BlockSpec((1, D_MODEL), full),
                  pl.BlockSpec((D_MODEL, D_FF), full, pipeline_mode=once),
                  pl.BlockSpec((D_MODEL, D_FF), full, pipeline_mode=once),
                  pl.BlockSpec((D_FF, D_MODEL), full, pipeline_mode=once),
                  pl.BlockSpec((1, D_MODEL), full)],
        out_specs=pl.BlockSpec((tm, D_MODEL), row),
        out_shape=jax.ShapeDtypeStruct((s, D_MODEL), F32),
        compiler_params=pltpu.CompilerParams(
            dimension_semantics=("parallel",), vmem_limit_bytes=VMEM_LIMIT),
        name="ffn",
    )(x1, npre, wg, wu, wd, npost)


def _rope_tables(s):
    half = ROPE_DIM // 2
    inv = ROPE_THETA ** (-jnp.arange(half, dtype=F32) * 2.0 / ROPE_DIM)
    ang = jnp.arange(s).astype(F32)[:, None] * inv[None, :]
    cos, sin = jnp.cos(ang), jnp.sin(ang)
    ones = jnp.ones((s, HEAD_DIM - ROPE_DIM), F32)
    zeros_half = jnp.zeros((s, half), F32)
    zeros_rest = jnp.zeros((s, HEAD_DIM - ROPE_DIM), F32)
    cos_t = jnp.concatenate([cos, cos, ones], axis=-1)
    s1_t = jnp.concatenate([-sin, zeros_half, zeros_rest], axis=-1)
    s2_t = jnp.concatenate([zeros_half, sin, zeros_rest], axis=-1)
    return cos_t, s1_t, s2_t


def _layer(x2, l, norm_mix_pre, w_in, conv_w, a_log, dt_bias, o_norm_w, w_o_attn, w_o_delta,
           w_out, norm_mix_post, norm_ffn_pre, w_gate, w_up, w_down, norm_ffn_post):
    s = x2.shape[0]
    nb = s // MOBA_BLOCK
    wide = 7 * D_MODEL
    nsmall = 2 * HEADS
    w = w_in[l]
    w_main = jnp.concatenate([w[:, :wide], w[:, wide + nsmall:]], axis=1).astype(BF16)
    w_small = jnp.pad(w[:, wide:wide + nsmall], ((0, 0), (0, 128 - nsmall))).astype(BF16)
    cos_t, s1_t, s2_t = _rope_tables(s)

    proj, small, kmean, v_t = _project(x2, norm_mix_pre[l][None, :], w_main, w_small,
                                       cos_t, s1_t, s2_t)
    ya = _moba_attention(proj, v_t, kmean.reshape(nb, D_MODEL))

    pad = lambda vec: jnp.pad(vec.astype(F32), (HEADS, 128 - 2 * HEADS))[None, :]
    yd = _gated_deltanet(proj, small, conv_w[l], pad(a_log[l]), pad(dt_bias[l]), o_norm_w[l][None, :])

    x1 = _mix_out(ya, yd, proj, x2, w_o_attn[l].astype(BF16), w_o_delta[l].astype(BF16),
                  w_out[l].astype(BF16), norm_mix_post[l][None, :])
    return _ffn(x1, norm_ffn_pre[l][None, :], w_gate[l].astype(BF16), w_up[l].astype(BF16),
                w_down[l].astype(BF16), norm_ffn_post[l][None, :])


def kernel(x, norm_mix_pre, w_in, conv_w, a_log, dt_bias, o_norm_w, w_o_attn, w_o_delta, w_out,
           norm_mix_post, norm_ffn_pre, w_gate, w_up, w_down, norm_ffn_post):
    b, s, d = x.shape
    assert d == D_MODEL and s % 512 == 0
    outs = []
    for bi in range(b):
        x2 = x[bi]
        for l in range(w_in.shape[0]):
            x2 = _layer(x2, l, norm_mix_pre, w_in, conv_w, a_log, dt_bias, o_norm_w, w_o_attn,
                        w_o_delta, w_out, norm_mix_post, norm_ffn_pre, w_gate, w_up, w_down,
                        norm_ffn_post)
        outs.append(x2)
    return jnp.stack(outs, axis=0)
```

```python
import functools
import math

import jax
import jax.numpy as jnp
from jax import lax
from jax.experimental import pallas as pl
from jax.experimental.pallas import tpu as pltpu
from jax.experimental.pallas import tpu_sc as plsc

D_MODEL = 1024
HEADS = 8
HEAD_DIM = 128
MOBA_BLOCK = 256
MOBA_TOPK = 3
ROPE_DIM = HEAD_DIM // 4
ROPE_THETA = 500000.0
DN_CONV = 4
ONES_ROWS = 16
DN_HEADS_PER_STEP = 4
DN_CHUNK = 256
D_FF = 2816
FF_CHUNK = 256
EPS = 1e-6
LOG2_E = math.log2(math.e)
ATTN_UNROLL = 4
ATTN_GROUPS = 2
HALO = 16

F32 = jnp.float32
BF16 = jnp.bfloat16
NT = (((1,), (1,)), ((), ()))

VMEM_LIMIT = 56 * 1024 * 1024


def _bdot(a, b):
    return jnp.dot(a.astype(BF16), b.astype(BF16), preferred_element_type=F32)


def _sigmoid(x):
    return 1.0 / (1.0 + jnp.exp(-x))


def _proj_kernel(x_ref, nw_ref, w_ref, ws_ref, cos_ref, s1_ref, s2_ref,
                 out_ref, small_ref, kmean_ref, vt_ref, qf_ref, *, tm, q_scale):
    x = x_ref[...]
    ms = jnp.mean(x * x, axis=-1, keepdims=True)
    h = (x * lax.rsqrt(ms + EPS) * nw_ref[...]).astype(BF16)
    small_ref[...] = jnp.dot(h, ws_ref[...], preferred_element_type=F32)
    heads = [slice(hh * HEAD_DIM, (hh + 1) * HEAD_DIM) for hh in range(HEADS)]
    groups = [slice(g * MOBA_BLOCK, (g + 1) * MOBA_BLOCK) for g in range(tm // MOBA_BLOCK)]

    def roped(a):
        return (a * cos_ref[...] + pltpu.roll(a, HEAD_DIM - ROPE_DIM // 2, 1) * s1_ref[...]
                + pltpu.roll(a, ROPE_DIM // 2, 1) * s2_ref[...])

    for c in range(out_ref.shape[1] // D_MODEL):
        cols = slice(c * D_MODEL, (c + 1) * D_MODEL)
        acc = jnp.dot(h, w_ref[:, cols], preferred_element_type=F32)
        if c == 0:
            for hh, hs in enumerate(heads):
                r = roped(acc[:, hs]) * q_scale
                out_ref[:, hs] = r.astype(BF16)
                qf_ref[hh] = r
        elif c == 1:
            for hs in heads:
                r = roped(acc[:, hs])
                out_ref[:, D_MODEL + hs.start:D_MODEL + hs.stop] = r.astype(BF16)
                for g, gs in enumerate(groups):
                    kmean_ref[0, g:g + 1, hs] = jnp.sum(r[gs], axis=0, keepdims=True) * (1.0 / MOBA_BLOCK)
        elif c == 2:
            out_ref[:, cols] = acc.astype(BF16)
            for hh, hs in enumerate(heads):
                for g, gs in enumerate(groups):
                    vt_ref[hh, g, 0:HEAD_DIM, :] = acc[gs, hs].T.astype(BF16)
                    vt_ref[hh, g, HEAD_DIM:, :] = jnp.ones((ONES_ROWS, MOBA_BLOCK), BF16)
        else:
            out_ref[:, cols] = acc.astype(BF16)


def _project(x2, norm_w, w_main, w_small, cos_t, s1_t, s2_t, *, tm=512):
    s = x2.shape[0]
    width = w_main.shape[1]
    nblk = tm // MOBA_BLOCK
    kern = functools.partial(_proj_kernel, tm=tm, q_scale=math.log2(math.e) / math.sqrt(HEAD_DIM))
    row = lambda i: (i, 0)
    full = lambda i: (0, 0)
    return pl.pallas_call(
        kern,
        grid=(s // tm,),
        in_specs=[
            pl.BlockSpec((tm, D_MODEL), row),
            pl.BlockSpec((1, D_MODEL), full),
            pl.BlockSpec((D_MODEL, width), full, pipeline_mode=pl.Buffered(1)),
            pl.BlockSpec((D_MODEL, 128), full),
            pl.BlockSpec((tm, HEAD_DIM), row),
            pl.BlockSpec((tm, HEAD_DIM), row),
            pl.BlockSpec((tm, HEAD_DIM), row),
        ],
        out_specs=[
            pl.BlockSpec((tm, width), row),
            pl.BlockSpec((tm, 128), row),
            pl.BlockSpec((1, nblk, D_MODEL), lambda i: (i, 0, 0)),
            pl.BlockSpec((HEADS, nblk, HEAD_DIM + ONES_ROWS, MOBA_BLOCK), lambda i: (0, i, 0, 0)),
            pl.BlockSpec((HEADS, tm, HEAD_DIM), lambda i: (0, i, 0)),
        ],
        out_shape=[
            jax.ShapeDtypeStruct((s, width), BF16),
            jax.ShapeDtypeStruct((s, 128), F32),
            jax.ShapeDtypeStruct((s // tm, nblk, D_MODEL), F32),
            jax.ShapeDtypeStruct((HEADS, s // MOBA_BLOCK, HEAD_DIM + ONES_ROWS, MOBA_BLOCK), BF16),
            jax.ShapeDtypeStruct((HEADS, s, HEAD_DIM), F32),
        ],
        compiler_params=pltpu.CompilerParams(
            dimension_semantics=("parallel",), vmem_limit_bytes=VMEM_LIMIT),
        name="proj",
    )(x2, norm_w, w_main, w_small, cos_t, s1_t, s2_t)


def _attn_kernel(q_ref, k_ref, vt_ref, km_ref, o_ref, bias_scr, sa_scr, pb_scr, *, nb):
    j = pl.program_id(1)
    q = q_ref[...]
    U = ATTN_UNROLL
    B = MOBA_BLOCK
    blk = lax.broadcasted_iota(jnp.int32, (nb, B), 0)
    blk_f = blk.astype(F32)

    def scores(g):
        n0 = jnp.minimum(g, nb // U - 1) * U
        kn = k_ref[pl.ds(pl.multiple_of(n0 * B, U * B), U * B), :]
        return lax.dot_general(kn, q, NT, preferred_element_type=F32)

    km = km_ref[...]
    km_hi = km.astype(BF16)
    km_lo = (km - km_hi.astype(F32)).astype(BF16)
    gate2 = lax.dot_general(jnp.concatenate([km_hi, km_lo], axis=0), q, NT,
                            preferred_element_type=F32)
    kj = k_ref[pl.ds(pl.multiple_of(j * B, B), B), :]
    s = lax.dot_general(kj, q, NT, preferred_element_type=F32)
    sa_scr[...] = scores(0)
    pb_scr[...] = jnp.zeros_like(pb_scr)

    gate = jnp.where(blk < j, gate2[:nb] + gate2[nb:], -jnp.inf)
    bias = jnp.full((nb, B), -jnp.inf, F32)
    for _ in range(MOBA_TOPK):
        m = jnp.max(gate, axis=0, keepdims=True)
        first = jnp.min(jnp.where(gate == m, blk_f, float(nb)), axis=0, keepdims=True)
        pick = (blk_f == first) & (m > -jnp.inf)
        bias = jnp.where(pick, 0.0, bias)
        gate = jnp.where(pick, -jnp.inf, gate)
    bias_scr[...] = bias

    krow = lax.broadcasted_iota(jnp.int32, (B, B), 0)
    qcol = lax.broadcasted_iota(jnp.int32, (B, B), 1)
    s = jnp.where(krow <= qcol, s, -jnp.inf)
    m0 = jnp.max(s, axis=0, keepdims=True)
    p = jnp.exp2(s - m0)
    acc0 = jnp.dot(vt_ref[0, j], p.astype(BF16), preferred_element_type=F32)


    def values(g, p_of):
        n0 = jnp.maximum(g, 0) * U
        pv = jnp.dot(vt_ref[0, n0], p_of(0), preferred_element_type=F32)
        for u in range(1, U):
            pv = pv + jnp.dot(vt_ref[0, n0 + u], p_of(u), preferred_element_type=F32)
        return pv

    def softmax(g, s_of, m):
        sn = [s_of(u) + bias_scr[pl.ds(g * U + u, 1), :] for u in range(U)]
        m_new = m
        for u in range(U):
            m_new = jnp.maximum(m_new, jnp.max(sn[u], axis=0, keepdims=True))
        alpha = jnp.exp2(m - m_new)
        return m_new, alpha, [jnp.exp2(sn[u] - m_new).astype(BF16) for u in range(U)]

    G = ATTN_GROUPS

    def body(t, carry):
        m, acc, alpha_last = carry
        acc = alpha_last * acc + values(G * t - 1, lambda u: pb_scr[u * B:(u + 1) * B, :])
        s_of = lambda u: sa_scr[u * B:(u + 1) * B, :]
        for i in range(G):
            if i + 1 < G:
                s_next = scores(G * t + i + 1)
            m, alpha, p_i = softmax(G * t + i, s_of, m)
            if i + 1 < G:
                acc = alpha * acc + values(G * t + i, lambda u, p_i=p_i: p_i[u])
                s_of = lambda u, s_next=s_next: s_next[u * B:(u + 1) * B]
        for u in range(U):
            pb_scr[u * B:(u + 1) * B, :] = p_i[u]
        sa_scr[...] = scores(G * t + G)
        return m, acc, alpha

    trips = (j + G * U - 1) // (G * U)
    _, acc, alpha_last = lax.fori_loop(0, trips, body, (m0, acc0, jnp.ones_like(m0)))
    acc = alpha_last * acc + values(G * trips - 1, lambda u: pb_scr[u * B:(u + 1) * B, :])
    o_ref[...] = (acc[:HEAD_DIM] / acc[HEAD_DIM:HEAD_DIM + 1]).T.astype(BF16)


def _moba_attention(proj, v_t, kmean):
    s = proj.shape[0]
    nb = s // MOBA_BLOCK
    return pl.pallas_call(
        functools.partial(_attn_kernel, nb=nb),
        grid=(HEADS, nb),
        in_specs=[
            pl.BlockSpec((MOBA_BLOCK, HEAD_DIM), lambda h, j: (j, h)),
            pl.BlockSpec((s, HEAD_DIM), lambda h, j: (0, HEADS + h)),
            pl.BlockSpec((1, nb, HEAD_DIM + ONES_ROWS, MOBA_BLOCK),
                         lambda h, j: (h, 0, 0, 0)),
            pl.BlockSpec((nb, HEAD_DIM), lambda h, j: (0, h)),
        ],
        out_specs=pl.BlockSpec((MOBA_BLOCK, HEAD_DIM), lambda h, j: (j, h)),
        out_shape=jax.ShapeDtypeStruct((s, HEADS * HEAD_DIM), BF16),
        scratch_shapes=[pltpu.VMEM((nb, MOBA_BLOCK), F32),
                        pltpu.VMEM((ATTN_UNROLL * MOBA_BLOCK, MOBA_BLOCK), F32),
                        pltpu.VMEM((ATTN_UNROLL * MOBA_BLOCK, MOBA_BLOCK), BF16)],
        compiler_params=pltpu.CompilerParams(
            dimension_semantics=("parallel", "arbitrary"), vmem_limit_bytes=VMEM_LIMIT),
        name="moba",
    )(proj, proj, v_t, kmean)


ROUTE_QBLOCKS = 4
TILES_PER_STEP = 8
SC_WINDOW = 128


def _top_blocks(gate, blk_f, nb):
    picks = []
    for _ in range(MOBA_TOPK):
        m = jnp.max(gate, axis=0, keepdims=True)
        first = jnp.min(jnp.where(gate == m, blk_f, float(nb)), axis=0, keepdims=True)
        pick = (blk_f == first) & (m > -jnp.inf)
        gate = jnp.where(pick, -jnp.inf, gate)
        picks.append(pick)
    return picks


def _route_kernel(q_ref, km_ref, info_ref, cnt_ref, run_scr, *, nb):
    jb = pl.program_id(1)
    L = ROUTE_QBLOCKS * MOBA_BLOCK

    @pl.when(jb == 0)
    def _():
        run_scr[...] = jnp.zeros_like(run_scr)

    km = km_ref[...]
    km_hi = km.astype(BF16)
    km_lo = (km - km_hi.astype(F32)).astype(BF16)
    gate2 = lax.dot_general(jnp.concatenate([km_hi, km_lo], axis=0), q_ref[...], NT,
                            preferred_element_type=F32)
    blk = lax.broadcasted_iota(jnp.int32, (nb, L), 0)
    blk_f = blk.astype(F32)
    qblk = jb * ROUTE_QBLOCKS + (lax.broadcasted_iota(jnp.int32, (nb, L), 1) >> 8)
    gate = jnp.where(blk < qblk, gate2[:nb] + gate2[nb:], -jnp.inf)
    picks = _top_blocks(gate, blk_f, nb)

    chosen = jnp.where(picks[0] | picks[1] | picks[2], 1.0, 0.0)
    before = jnp.where(lax.broadcasted_iota(jnp.int32, (L, L), 0)
                       < lax.broadcasted_iota(jnp.int32, (L, L), 1), 1.0, 0.0).astype(BF16)
    base = run_scr[:, 0:1] + jnp.dot(chosen.astype(BF16), before, preferred_element_type=F32)

    rows = []
    for pick in picks:
        rows.append(jnp.sum(jnp.where(pick, base, 0.0), axis=0, keepdims=True))
    for pick in picks:
        bid = jnp.sum(jnp.where(pick, blk_f, 0.0), axis=0, keepdims=True)
        valid = jnp.sum(jnp.where(pick, 1.0, 0.0), axis=0, keepdims=True)
        rows.append(jnp.where(valid > 0.0, bid, float(nb)))
    rows += [jnp.zeros((1, L), F32)] * (8 - len(rows))
    info_ref[...] = jnp.concatenate(rows, axis=0).astype(jnp.int32)

    run = run_scr[...] + jnp.sum(chosen, axis=1, keepdims=True)
    run_scr[...] = run
    cnt_ref[0] = run


def _route(proj, kmean):
    s = proj.shape[0]
    nb = s // MOBA_BLOCK
    L = ROUTE_QBLOCKS * MOBA_BLOCK
    steps = s // L
    return pl.pallas_call(
        functools.partial(_route_kernel, nb=nb),
        grid=(HEADS, steps),
        in_specs=[pl.BlockSpec((L, HEAD_DIM), lambda h, j: (j, h)),
                  pl.BlockSpec((nb, HEAD_DIM), lambda h, j: (0, h))],
        out_specs=[pl.BlockSpec((8, L), lambda h, j: (0, h * steps + j)),
                   pl.BlockSpec((1, nb, 128), lambda h, j: (h, 0, 0))],
        out_shape=[jax.ShapeDtypeStruct((8, HEADS * s), jnp.int32),
                   jax.ShapeDtypeStruct((HEADS, nb, 128), F32)],
        scratch_shapes=[pltpu.VMEM((nb, 128), F32)],
        compiler_params=pltpu.CompilerParams(
            dimension_semantics=("parallel", "arbitrary"), vmem_limit_bytes=VMEM_LIMIT),
        name="moba_route",
    )(proj, kmean)


def _dest_kernel(info_ref, cnt_ref, dest_ref, tb_ref, *, nb, s):
    h = pl.program_id(0)
    cap = 4 * s
    ntile = cap // MOBA_BLOCK
    cnt = cnt_ref[0]
    tiles = jnp.floor((cnt + float(MOBA_BLOCK - 1)) * (1.0 / MOBA_BLOCK))
    lower = jnp.where(lax.broadcasted_iota(jnp.int32, (nb, nb), 1)
                      < lax.broadcasted_iota(jnp.int32, (nb, nb), 0), 1.0, 0.0).astype(BF16)
    start = jnp.dot(lower, tiles.astype(BF16), preferred_element_type=F32)
    start1 = start[:, 0:1]
    tiles1 = tiles[:, 0:1]

    t_f = lax.broadcasted_iota(jnp.int32, (nb, ntile), 1).astype(F32)
    n_f = lax.broadcasted_iota(jnp.int32, (nb, ntile), 0).astype(F32)
    inside = (t_f >= start1) & (t_f < start1 + tiles1)
    tb = jnp.sum(jnp.where(inside, n_f, 0.0), axis=0, keepdims=True)
    used = jnp.sum(jnp.where(inside, 1.0, 0.0), axis=0, keepdims=True)
    tb = jnp.where(used > 0.0, tb, -1.0)
    tb_ref[0] = jnp.concatenate([tb, jnp.full((7, ntile), -1.0, F32)], axis=0).astype(jnp.int32)

    off1 = start1 * float(MOBA_BLOCK)
    CH = 2048
    blk_f = lax.broadcasted_iota(jnp.int32, (nb, CH), 0).astype(F32)
    lane = lax.broadcasted_iota(jnp.int32, (1, CH), 1)
    trash = HEADS * cap + (lane & (MOBA_BLOCK - 1))
    for ch in range(s // CH):
        sl = slice(ch * CH, (ch + 1) * CH)
        rows = []
        for r in range(MOBA_TOPK):
            rank = info_ref[r:r + 1, sl]
            bid = info_ref[MOBA_TOPK + r:MOBA_TOPK + r + 1, sl]
            off = jnp.sum(jnp.where(blk_f == bid.astype(F32), off1, 0.0), axis=0, keepdims=True)
            rows.append(jnp.where(bid < nb, h * cap + off.astype(jnp.int32) + rank, trash))
        rows += [jnp.zeros((1, CH), jnp.int32)] * (8 - len(rows))
        dest_ref[:, sl] = jnp.concatenate(rows, axis=0)


def _dest(info, cnt, s):
    nb = s // MOBA_BLOCK
    ntile = 4 * s // MOBA_BLOCK
    return pl.pallas_call(
        functools.partial(_dest_kernel, nb=nb, s=s),
        grid=(HEADS,),
        in_specs=[pl.BlockSpec((8, s), lambda h: (0, h)),
                  pl.BlockSpec((1, nb, 128), lambda h: (h, 0, 0))],
        out_specs=[pl.BlockSpec((8, s), lambda h: (0, h)),
                   pl.BlockSpec((1, 8, ntile), lambda h: (h, 0, 0))],
        out_shape=[jax.ShapeDtypeStruct((8, HEADS * s), jnp.int32),
                   jax.ShapeDtypeStruct((HEADS, 8, ntile), jnp.int32)],
        compiler_params=pltpu.CompilerParams(
            dimension_semantics=("parallel",), vmem_limit_bytes=VMEM_LIMIT),
        name="moba_dest",
    )(info, cnt)


def _sc_scatter_rows(rows, idx, n_out):
    m, c = rows.shape
    k = idx.shape[0]
    nwin = m // SC_WINDOW
    mesh = plsc.VectorSubcoreMesh(core_axis_name="core", subcore_axis_name="subcore")

    @pl.kernel(out_type=jax.ShapeDtypeStruct((n_out, c), rows.dtype), mesh=mesh)
    def kern(x_hbm, i_hbm, o_hbm):
        def body(x_vmem, i_vmem):
            pltpu.sync_copy(x_vmem, o_hbm.at[i_vmem.at[0]])

        pltpu.emit_pipeline(
            body, grid=(k * nwin,),
            in_specs=[pl.BlockSpec((SC_WINDOW, c), lambda i: (i % nwin, 0)),
                      pl.BlockSpec((1, SC_WINDOW), lambda i: (0, i))],
            out_specs=[],
            core_axis_name=("core", "subcore"),
            dimension_semantics=(pltpu.PARALLEL,),
        )(x_hbm, i_hbm)

    return kern(rows, idx.reshape(1, k * m))


def _sc_gather_rows(table, idx):
    m = idx.shape[0]
    c = table.shape[1]
    mesh = plsc.VectorSubcoreMesh(core_axis_name="core", subcore_axis_name="subcore")

    @pl.kernel(out_type=jax.ShapeDtypeStruct((m, c), table.dtype), mesh=mesh)
    def kern(x_hbm, i_hbm, o_hbm):
        def body(i_vmem, o_vmem):
            pltpu.sync_copy(x_hbm.at[i_vmem.at[0]], o_vmem)

        pltpu.emit_pipeline(
            body, grid=(m // SC_WINDOW,),
            in_specs=[pl.BlockSpec((1, SC_WINDOW), lambda i: (0, i))],
            out_specs=[pl.BlockSpec((SC_WINDOW, c), lambda i: (i, 0))],
            core_axis_name=("core", "subcore"),
            dimension_semantics=(pltpu.PARALLEL,),
        )(i_hbm, o_hbm)

    return kern(table, idx.reshape(1, m))


def _pack_partial(o_norm, lse_rows):
    u = lax.bitcast_convert_type(o_norm, jnp.uint32) + jnp.uint32(0x8000)
    word = (u & jnp.uint32(0xFFFF0000)) | (pltpu.roll(u, HEAD_DIM // 2, 1) >> 16)
    lane = lax.broadcasted_iota(jnp.int32, o_norm.shape, 1)
    return jnp.where(lane < HEAD_DIM // 2, word, lax.bitcast_convert_type(lse_rows, jnp.uint32))


def _unpack_partial(word):
    hi = lax.bitcast_convert_type(word & jnp.uint32(0xFFFF0000), F32)
    lo = lax.bitcast_convert_type(word << 16, F32)
    lane = lax.broadcasted_iota(jnp.int32, word.shape, 1)
    o = jnp.where(lane < HEAD_DIM // 2, hi, pltpu.roll(lo, HEAD_DIM // 2, 1))
    lse = lax.bitcast_convert_type(word, F32)[:, HEAD_DIM // 2:HEAD_DIM // 2 + 1]
    return o, lse


def _tiles_kernel(tb_ref, qs_ref, k_ref, vt_ref, o_ref, *, ntile):
    h = pl.program_id(0)
    g = pl.program_id(1)
    B = MOBA_BLOCK
    base = h * ntile + g * TILES_PER_STEP

    @pl.when(tb_ref[base] >= 0)
    def _():
        blocks = [jnp.maximum(tb_ref[base + u], 0) for u in range(TILES_PER_STEP)]
        s_t = [lax.dot_general(k_ref[pl.ds(pl.multiple_of(n * B, B), B), :],
                               qs_ref[u * B:(u + 1) * B, :].astype(BF16), NT,
                               preferred_element_type=F32) for u, n in enumerate(blocks)]
        m = [jnp.max(x, axis=0, keepdims=True) for x in s_t]
        p = [jnp.exp2(x - mm).astype(BF16) for x, mm in zip(s_t, m)]
        acc = [jnp.dot(vt_ref[0, n], pp, preferred_element_type=F32) for n, pp in zip(blocks, p)]
        for u in range(TILES_PER_STEP):
            l = acc[u][HEAD_DIM:HEAD_DIM + 1]
            o_norm = (acc[u][:HEAD_DIM] / l).T
            lse = jnp.broadcast_to(m[u] + jnp.log(l) * LOG2_E, (HEAD_DIM, B)).T
            o_ref[u * B:(u + 1) * B, :] = _pack_partial(o_norm, lse)


def _tiles(tb, qsorted, proj, v_t, s):
    nb = s // MOBA_BLOCK
    ntile = 4 * s // MOBA_BLOCK
    steps = ntile // TILES_PER_STEP
    rows = TILES_PER_STEP * MOBA_BLOCK
    grid_spec = pltpu.PrefetchScalarGridSpec(
        num_scalar_prefetch=1,
        grid=(HEADS, steps),
        in_specs=[pl.BlockSpec((rows, HEAD_DIM), lambda h, g, tb: (h * steps + g, 0)),
                  pl.BlockSpec((s, HEAD_DIM), lambda h, g, tb: (0, HEADS + h)),
                  pl.BlockSpec((1, nb, HEAD_DIM + ONES_ROWS, MOBA_BLOCK), lambda h, g, tb: (h, 0, 0, 0))],
        out_specs=pl.BlockSpec((rows, HEAD_DIM), lambda h, g, tb: (h * steps + g, 0)),
    )
    return pl.pallas_call(
        functools.partial(_tiles_kernel, ntile=ntile),
        grid_spec=grid_spec,
        out_shape=jax.ShapeDtypeStruct(qsorted.shape, jnp.uint32),
        compiler_params=pltpu.CompilerParams(
            dimension_semantics=("parallel", "arbitrary"), vmem_limit_bytes=VMEM_LIMIT),
        name="moba_tiles",
    )(tb, qsorted, proj, v_t)


def _merge_kernel(q_ref, k_ref, v_ref, part_ref, o_ref):
    j = pl.program_id(1)
    B = MOBA_BLOCK
    s = lax.dot_general(q_ref[...], k_ref[...], NT, preferred_element_type=F32)
    qrow = lax.broadcasted_iota(jnp.int32, (B, B), 0)
    kcol = lax.broadcasted_iota(jnp.int32, (B, B), 1)
    s = jnp.where(kcol <= qrow, s, -jnp.inf)
    m_own = jnp.max(s, axis=-1, keepdims=True)
    p = jnp.exp2(s - m_own)
    l_own = jnp.sum(p, axis=-1, keepdims=True)
    o_own = jnp.dot(p.astype(BF16), v_ref[...], preferred_element_type=F32) / l_own
    lse_own = m_own + jnp.log(l_own) * LOG2_E

    parts = []
    for r in range(MOBA_TOPK):
        o_r, lse_r = _unpack_partial(part_ref[r, 0])
        valid = r < j
        parts.append((jnp.where(valid, o_r, 0.0), jnp.where(valid, lse_r, -jnp.inf)))
    m_all = lse_own
    for _, lse_r in parts:
        m_all = jnp.maximum(m_all, lse_r)
    w = jnp.exp2(lse_own - m_all)
    num = w * o_own
    den = w
    for o_r, lse_r in parts:
        w = jnp.exp2(lse_r - m_all)
        num = num + w * o_r
        den = den + w
    o_ref[...] = (num / den).astype(BF16)


def _merge(proj, parts, s):
    nb = s // MOBA_BLOCK
    return pl.pallas_call(
        _merge_kernel,
        grid=(HEADS, nb),
        in_specs=[pl.BlockSpec((MOBA_BLOCK, HEAD_DIM), lambda h, j: (j, h)),
                  pl.BlockSpec((MOBA_BLOCK, HEAD_DIM), lambda h, j: (j, HEADS + h)),
                  pl.BlockSpec((MOBA_BLOCK, HEAD_DIM), lambda h, j: (j, 2 * HEADS + h)),
                  pl.BlockSpec((MOBA_TOPK, 1, MOBA_BLOCK, HEAD_DIM), lambda h, j: (0, h, j, 0))],
        out_specs=pl.BlockSpec((MOBA_BLOCK, HEAD_DIM), lambda h, j: (j, h)),
        out_shape=jax.ShapeDtypeStruct((s, HEADS * HEAD_DIM), BF16),
        compiler_params=pltpu.CompilerParams(
            dimension_semantics=("parallel", "arbitrary"), vmem_limit_bytes=VMEM_LIMIT),
        name="moba_merge",
    )(proj, proj, proj, parts)


def _moba_sparse(proj, v_t, kmean, qf):
    s = proj.shape[0]
    cap = 4 * s
    n_rows = HEADS * cap + MOBA_BLOCK
    info, cnt = _route(proj, kmean)
    dest, tb = _dest(info, cnt, s)
    dest3 = dest[:MOBA_TOPK]
    qsorted = _sc_scatter_rows(qf.reshape(HEADS * s, HEAD_DIM), dest3, n_rows)
    osorted = _tiles(tb[:, 0, :].reshape(-1), qsorted, proj, v_t, s)
    parts = _sc_gather_rows(osorted, dest3.reshape(-1))
    return _merge(proj, parts.reshape(MOBA_TOPK, HEADS, s, HEAD_DIM), s)


def _delta_kernel(q_ref, k_ref, v_ref, qh_ref, kh_ref, vh_ref, z_ref, small_ref,
                  cwq_ref, cwk_ref, cwv_ref, alog_ref, dtb_ref, onw_ref,
                  o_ref, state_scr, xx_scr):
    hg = pl.program_id(0)
    c = pl.program_id(1)
    C = DN_CHUNK

    @pl.when(c == 0)
    def _():
        state_scr[...] = jnp.zeros_like(state_scr)

    def conv_silu(idx, x_ref, halo_ref, cw_ref):
        halo = jnp.where(c == 0, 0.0, halo_ref[...].astype(F32))
        xx_scr[idx, 0:HALO, :] = halo
        xx_scr[idx, HALO:HALO + C, :] = x_ref[...].astype(F32)
        y = jnp.zeros((C, DN_HEADS_PER_STEP * HEAD_DIM), F32)
        for t in range(DN_CONV):
            off = HALO - (DN_CONV - 1) + t
            y = y + cw_ref[t:t + 1, :] * xx_scr[idx, off:off + C, :]
        return y * _sigmoid(y)

    q_all = conv_silu(0, q_ref, qh_ref, cwq_ref)
    k_all = conv_silu(1, k_ref, kh_ref, cwk_ref)
    v_all = conv_silu(2, v_ref, vh_ref, cwv_ref)

    small = small_ref[...]
    lane = lax.broadcasted_iota(jnp.int32, (C, 128), 1)
    beta_all = _sigmoid(small)
    xs = small + dtb_ref[...]
    softplus = jnp.maximum(xs, 0.0) + jnp.log(1.0 + jnp.exp(-jnp.abs(xs)))
    g_all = -jnp.exp(alog_ref[...]) * softplus

    row = lax.broadcasted_iota(jnp.int32, (C, C), 0)
    col = lax.broadcasted_iota(jnp.int32, (C, C), 1)
    tril = row >= col
    strict = row > col
    rxc = row ^ col
    eye = (row == col).astype(F32)
    tril_f = tril.astype(F32)

    heads = range(DN_HEADS_PER_STEP)
    sls = [slice(hh * HEAD_DIM, (hh + 1) * HEAD_DIM) for hh in heads]

    def hmap(f, *lists):
        return [f(*xs) for xs in zip(*lists)]

    def pick(table, lane_idx):
        return jnp.sum(jnp.where(lane == lane_idx, table, 0.0), axis=-1, keepdims=True)

    q = [q_all[:, sl] for sl in sls]
    k = [k_all[:, sl] for sl in sls]
    v = [v_all[:, sl] for sl in sls]
    q = hmap(lambda x: x * lax.rsqrt(jnp.sum(x * x, axis=-1, keepdims=True) + EPS) * (HEAD_DIM ** -0.5), q)
    k = hmap(lambda x: x * lax.rsqrt(jnp.sum(x * x, axis=-1, keepdims=True) + EPS), k)
    beta = [pick(beta_all, hg * DN_HEADS_PER_STEP + hh) for hh in heads]
    g = [pick(g_all, hg * DN_HEADS_PER_STEP + hh + HEADS) for hh in heads]

    gcum_b = hmap(lambda x: jnp.dot(tril_f, jnp.broadcast_to(x, (C, 128)),
                                    precision=lax.Precision.HIGHEST, preferred_element_type=F32), g)
    gc = [x[:, 0:1] for x in gcum_b]
    gr = [x.T[0:1, :] for x in gcum_b]
    decay = hmap(lambda c_, r_: jnp.where(tril, jnp.exp(jnp.where(tril, c_ - r_, 0.0)), 0.0), gc, gr)
    e_g = hmap(jnp.exp, gc)
    g_last = [x[C - 1:C, :] for x in gc]

    kb = hmap(lambda a_, b_: a_ * b_, k, beta)
    vb = hmap(lambda a_, b_: a_ * b_, v, beta)
    a = hmap(lambda kb_, q_, k_: lax.dot_general(
        jnp.concatenate([kb_, q_], axis=0).astype(BF16), k_.astype(BF16), NT,
        preferred_element_type=F32), kb, q, k)
    lmat = hmap(lambda a_, d_: jnp.where(strict, a_[:C] * d_, 0.0), a, decay)
    attn = hmap(lambda a_, d_: a_[C:] * d_, a, decay)

    d1 = hmap(lambda l_: jnp.where(rxc < 16, l_, 0.0), lmat)
    d2 = hmap(lambda x: _bdot(x, x), d1)
    d4 = hmap(lambda x: _bdot(x, x), d2)
    d8 = hmap(lambda x: _bdot(x, x), d4)
    p1 = hmap(lambda x, y: _bdot(eye - x, eye + y), d1, d2)
    p2 = hmap(lambda x, y: _bdot(eye + x, eye + y), d4, d8)
    tmat = hmap(_bdot, p1, p2)
    sz = 16
    while sz < C:
        off = hmap(lambda l_: jnp.where((rxc >= sz) & (rxc < 2 * sz), l_, 0.0), lmat)
        x = hmap(_bdot, tmat, off)
        x = hmap(_bdot, x, tmat)
        tmat = hmap(lambda t_, x_: t_ - x_, tmat, x)
        sz *= 2

    uw = hmap(lambda t_, vb_, kb_, e_: _bdot(t_, jnp.concatenate([vb_, kb_ * e_], axis=1)),
              tmat, vb, kb, e_g)
    state = [state_scr[hh] for hh in heads]
    wq = hmap(lambda uw_, q_, e_, s_: _bdot(jnp.concatenate([uw_[:, HEAD_DIM:], q_ * e_], axis=0), s_),
              uw, q, e_g, state)
    v_new = hmap(lambda uw_, wq_: uw_[:, :HEAD_DIM] - wq_[:C], uw, wq)
    av = hmap(_bdot, attn, v_new)
    kt = hmap(lambda k_, gl_, gc_: (k_ * jnp.exp(gl_ - gc_)).T, k, g_last, gc)
    ds = hmap(_bdot, kt, v_new)
    for hh in heads:
        state_scr[hh] = state[hh] * jnp.exp(g_last[hh]) + ds[hh]
        o = wq[hh][C:] + av[hh]
        y = o * lax.rsqrt(jnp.mean(o * o, axis=-1, keepdims=True) + EPS) * onw_ref[...]
        z = z_ref[:, sls[hh]].astype(F32)
        o_ref[:, sls[hh]] = (y * (z * _sigmoid(z))).astype(BF16)


def _gated_deltanet(proj, small, conv_w, alog_v, dtb_v, onw):
    s = proj.shape[0]
    nc = s // DN_CHUNK
    rb = DN_CHUNK // HALO
    hp = DN_HEADS_PER_STEP
    ng = HEADS // hp
    width = hp * HEAD_DIM

    def col(part):
        return pl.BlockSpec((DN_CHUNK, width), lambda h, c: (c, part * ng + h))

    def halo(part):
        return pl.BlockSpec((HALO, width), lambda h, c: (jnp.maximum(c * rb - 1, 0), part * ng + h))

    def cw(part):
        return pl.BlockSpec((DN_CONV, width), lambda h, c: (0, part * ng + h))

    vec = pl.BlockSpec((1, 128), lambda h, c: (0, 0))
    return pl.pallas_call(
        _delta_kernel,
        grid=(ng, nc),
        in_specs=[col(3), col(4), col(5),
                  halo(3), halo(4), halo(5),
                  col(6),
                  pl.BlockSpec((DN_CHUNK, 128), lambda h, c: (c, 0)),
                  cw(0), cw(1), cw(2),
                  vec, vec, vec],
        out_specs=pl.BlockSpec((DN_CHUNK, width), lambda h, c: (c, h)),
        out_shape=jax.ShapeDtypeStruct((s, HEADS * HEAD_DIM), BF16),
        scratch_shapes=[pltpu.VMEM((hp, HEAD_DIM, HEAD_DIM), F32),
                        pltpu.VMEM((3, HALO + DN_CHUNK, width), F32)],
        compiler_params=pltpu.CompilerParams(
            dimension_semantics=("parallel", "arbitrary"), vmem_limit_bytes=VMEM_LIMIT),
        name="deltanet",
    )(proj, proj, proj, proj, proj, proj, proj, small, conv_w, conv_w, conv_w, alog_v, dtb_v, onw)


def _mix_out_kernel(ya_ref, yd_ref, ga_ref, gd_ref, x_ref, wa_ref, wd_ref, wo_ref, nw_ref, o_ref):
    pa = jnp.dot(ya_ref[...], wa_ref[...], preferred_element_type=F32)
    pd = jnp.dot(yd_ref[...], wd_ref[...], preferred_element_type=F32)
    merged = _sigmoid(ga_ref[...].astype(F32)) * pa + _sigmoid(gd_ref[...].astype(F32)) * pd
    mo = jnp.dot(merged.astype(BF16), wo_ref[...], preferred_element_type=F32)
    y = mo * lax.rsqrt(jnp.mean(mo * mo, axis=-1, keepdims=True) + EPS) * nw_ref[...]
    o_ref[...] = x_ref[...] + y


def _mix_out(ya, yd, proj, x2, wa, wd, wo, nw, *, tm=512):
    s = x2.shape[0]
    row = lambda i: (i, 0)
    full = lambda i: (0, 0)
    wspec = pl.BlockSpec((D_MODEL, D_MODEL), full)
    return pl.pallas_call(
        _mix_out_kernel,
        grid=(s // tm,),
        in_specs=[pl.BlockSpec((tm, D_MODEL), row), pl.BlockSpec((tm, D_MODEL), row),
                  pl.BlockSpec((tm, D_MODEL), lambda i: (i, 7)),
                  pl.BlockSpec((tm, D_MODEL), lambda i: (i, 8)),
                  pl.BlockSpec((tm, D_MODEL), row),
                  wspec, wspec, wspec, pl.BlockSpec((1, D_MODEL), full)],
        out_specs=pl.BlockSpec((tm, D_MODEL), row),
        out_shape=jax.ShapeDtypeStruct((s, D_MODEL), F32),
        compiler_params=pltpu.CompilerParams(
            dimension_semantics=("parallel",), vmem_limit_bytes=VMEM_LIMIT),
        name="mix_out",
    )(ya, yd, proj, proj, x2, wa, wd, wo, nw)


def _ffn_kernel(x_ref, npre_ref, wg_ref, wu_ref, wd_ref, npost_ref, o_ref):
    x = x_ref[...]
    h = (x * lax.rsqrt(jnp.mean(x * x, axis=-1, keepdims=True) + EPS) * npre_ref[...]).astype(BF16)
    acc = jnp.zeros(x.shape, F32)
    for cc in range(D_FF // FF_CHUNK):
        sl = slice(cc * FF_CHUNK, (cc + 1) * FF_CHUNK)
        g = jnp.dot(h, wg_ref[:, sl], preferred_element_type=F32)
        u = jnp.dot(h, wu_ref[:, sl], preferred_element_type=F32)
        act = (g * _sigmoid(g) * u).astype(BF16)
        acc = acc + jnp.dot(act, wd_ref[sl, :], preferred_element_type=F32)
    y = acc * lax.rsqrt(jnp.mean(acc * acc, axis=-1, keepdims=True) + EPS) * npost_ref[...]
    o_ref[...] = x + y


def _ffn(x1, npre, wg, wu, wd, npost, *, tm=512):
    s = x1.shape[0]
    row = lambda i: (i, 0)
    full = lambda i: (0, 0)
    once = pl.Buffered(1)
    return pl.pallas_call(
        _ffn_kernel,
        grid=(s // tm,),
        in_specs=[pl.BlockSpec((tm, D_MODEL), row), pl.BlockSpec((1, D_MODEL), full),
                  pl.BlockSpec((D_MODEL, D_FF), full, pipeline_mode=once),
                  pl.BlockSpec((D_MODEL, D_FF), full, pipeline_mode=once),
                  pl.BlockSpec((D_FF, D_MODEL), full, pipeline_mode=once),
                  pl.BlockSpec((1, D_MODEL), full)],
        out_specs=pl.BlockSpec((tm, D_MODEL), row),
        out_shape=jax.ShapeDtypeStruct((s, D_MODEL), F32),
        compiler_params=pltpu.CompilerParams(
            dimension_semantics=("parallel",), vmem_limit_bytes=VMEM_LIMIT),
        name="ffn",
    )(x1, npre, wg, wu, wd, npost)


def _rope_tables(s):
    half = ROPE_DIM // 2
    inv = ROPE_THETA ** (-jnp.arange(half, dtype=F32) * 2.0 / ROPE_DIM)
    ang = jnp.arange(s).astype(F32)[:, None] * inv[None, :]
    cos, sin = jnp.cos(ang), jnp.sin(ang)
    ones = jnp.ones((s, HEAD_DIM - ROPE_DIM), F32)
    zeros_half = jnp.zeros((s, half), F32)
    zeros_rest = jnp.zeros((s, HEAD_DIM - ROPE_DIM), F32)
    cos_t = jnp.concatenate([cos, cos, ones], axis=-1)
    s1_t = jnp.concatenate([-sin, zeros_half, zeros_rest], axis=-1)
    s2_t = jnp.concatenate([zeros_half, sin, zeros_rest], axis=-1)
    return cos_t, s1_t, s2_t


def _layer(x2, l, norm_mix_pre, w_in, conv_w, a_log, dt_bias, o_norm_w, w_o_attn, w_o_delta,
           w_out, norm_mix_post, norm_ffn_pre, w_gate, w_up, w_down, norm_ffn_post):
    s = x2.shape[0]
    nb = s // MOBA_BLOCK
    wide = 7 * D_MODEL
    nsmall = 2 * HEADS
    w = w_in[l]
    w_main = jnp.concatenate([w[:, :wide], w[:, wide + nsmall:]], axis=1).astype(BF16)
    w_small = jnp.pad(w[:, wide:wide + nsmall], ((0, 0), (0, 128 - nsmall))).astype(BF16)
    cos_t, s1_t, s2_t = _rope_tables(s)

    proj, small, kmean, v_t, qf = _project(x2, norm_mix_pre[l][None, :], w_main, w_small,
                                           cos_t, s1_t, s2_t)
    ya = _moba_sparse(proj, v_t, kmean.reshape(nb, D_MODEL), qf)

    pad = lambda vec: jnp.pad(vec.astype(F32), (HEADS, 128 - 2 * HEADS))[None, :]
    yd = _gated_deltanet(proj, small, conv_w[l], pad(a_log[l]), pad(dt_bias[l]), o_norm_w[l][None, :])

    x1 = _mix_out(ya, yd, proj, x2, w_o_attn[l].astype(BF16), w_o_delta[l].astype(BF16),
                  w_out[l].astype(BF16), norm_mix_post[l][None, :])
    return _ffn(x1, norm_ffn_pre[l][None, :], w_gate[l].astype(BF16), w_up[l].astype(BF16),
                w_down[l].astype(BF16), norm_ffn_post[l][None, :])


def kernel(x, norm_mix_pre, w_in, conv_w, a_log, dt_bias, o_norm_w, w_o_attn, w_o_delta, w_out,
           norm_mix_post, norm_ffn_pre, w_gate, w_up, w_down, norm_ffn_post):
    b, s, d = x.shape
    assert d == D_MODEL and s % 512 == 0
    outs = []
    for bi in range(b):
        x2 = x[bi]
        for l in range(w_in.shape[0]):
            x2 = _layer(x2, l, norm_mix_pre, w_in, conv_w, a_log, dt_bias, o_norm_w, w_o_attn,
                        w_o_delta, w_out, norm_mix_post, norm_ffn_pre, w_gate, w_up, w_down,
                        norm_ffn_post)
        outs.append(x2)
    return jnp.stack(outs, axis=0)
```

```python
import functools
import math

import jax
import jax.numpy as jnp
from jax import lax
from jax.experimental import pallas as pl
from jax.experimental.pallas import tpu as pltpu
from jax.experimental.pallas import tpu_sc as plsc

D_MODEL = 1024
HEADS = 8
HEAD_DIM = 128
MOBA_BLOCK = 256
MOBA_TOPK = 3
ROPE_DIM = HEAD_DIM // 4
ROPE_THETA = 500000.0
DN_CONV = 4
ONES_ROWS = 16
DN_HEADS_PER_STEP = 4
DN_CHUNK = 256
D_FF = 2816
FF_CHUNK = 256
EPS = 1e-6
LOG2_E = math.log2(math.e)
ATTN_UNROLL = 4
ATTN_GROUPS = 2
HALO = 16

F32 = jnp.float32
BF16 = jnp.bfloat16
NT = (((1,), (1,)), ((), ()))

VMEM_LIMIT = 56 * 1024 * 1024


def _bdot(a, b):
    return jnp.dot(a.astype(BF16), b.astype(BF16), preferred_element_type=F32)


def _sigmoid(x):
    return 1.0 / (1.0 + jnp.exp(-x))


def _proj_kernel(x_ref, nw_ref, w_ref, ws_ref, cos_ref, s1_ref, s2_ref,
                 out_ref, small_ref, kmean_ref, vt_ref, qf_ref, *, tm, q_scale):
    x = x_ref[...]
    ms = jnp.mean(x * x, axis=-1, keepdims=True)
    h = (x * lax.rsqrt(ms + EPS) * nw_ref[...]).astype(BF16)
    small_ref[...] = jnp.dot(h, ws_ref[...], preferred_element_type=F32)
    heads = [slice(hh * HEAD_DIM, (hh + 1) * HEAD_DIM) for hh in range(HEADS)]
    groups = [slice(g * MOBA_BLOCK, (g + 1) * MOBA_BLOCK) for g in range(tm // MOBA_BLOCK)]

    def roped(a):
        return (a * cos_ref[...] + pltpu.roll(a, HEAD_DIM - ROPE_DIM // 2, 1) * s1_ref[...]
                + pltpu.roll(a, ROPE_DIM // 2, 1) * s2_ref[...])

    for c in range(out_ref.shape[1] // D_MODEL):
        cols = slice(c * D_MODEL, (c + 1) * D_MODEL)
        acc = jnp.dot(h, w_ref[:, cols], preferred_element_type=F32)
        if c == 0:
            for hh, hs in enumerate(heads):
                r = roped(acc[:, hs]) * q_scale
                out_ref[:, hs] = r.astype(BF16)
                qf_ref[hh] = r
        elif c == 1:
            for hs in heads:
                r = roped(acc[:, hs])
                out_ref[:, D_MODEL + hs.start:D_MODEL + hs.stop] = r.astype(BF16)
                for g, gs in enumerate(groups):
                    kmean_ref[0, g:g + 1, hs] = jnp.sum(r[gs], axis=0, keepdims=True) * (1.0 / MOBA_BLOCK)
        elif c == 2:
            out_ref[:, cols] = acc.astype(BF16)
            for hh, hs in enumerate(heads):
                for g, gs in enumerate(groups):
                    vt_ref[hh, g, 0:HEAD_DIM, :] = acc[gs, hs].T.astype(BF16)
                    vt_ref[hh, g, HEAD_DIM:, :] = jnp.ones((ONES_ROWS, MOBA_BLOCK), BF16)
        else:
            out_ref[:, cols] = acc.astype(BF16)


def _project(x2, norm_w, w_main, w_small, cos_t, s1_t, s2_t, *, tm=512):
    s = x2.shape[0]
    width = w_main.shape[1]
    nblk = tm // MOBA_BLOCK
    kern = functools.partial(_proj_kernel, tm=tm, q_scale=math.log2(math.e) / math.sqrt(HEAD_DIM))
    row = lambda i: (i, 0)
    full = lambda i: (0, 0)
    return pl.pallas_call(
        kern,
        grid=(s // tm,),
        in_specs=[
            pl.BlockSpec((tm, D_MODEL), row),
            pl.BlockSpec((1, D_MODEL), full),
            pl.BlockSpec((D_MODEL, width), full, pipeline_mode=pl.Buffered(1)),
            pl.BlockSpec((D_MODEL, 128), full),
            pl.BlockSpec((tm, HEAD_DIM), row),
            pl.BlockSpec((tm, HEAD_DIM), row),
            pl.BlockSpec((tm, HEAD_DIM), row),
        ],
        out_specs=[
            pl.BlockSpec((tm, width), row),
            pl.BlockSpec((tm, 128), row),
            pl.BlockSpec((1, nblk, D_MODEL), lambda i: (i, 0, 0)),
            pl.BlockSpec((HEADS, nblk, HEAD_DIM + ONES_ROWS, MOBA_BLOCK), lambda i: (0, i, 0, 0)),
            pl.BlockSpec((HEADS, tm, HEAD_DIM), lambda i: (0, i, 0)),
        ],
        out_shape=[
            jax.ShapeDtypeStruct((s, width), BF16),
            jax.ShapeDtypeStruct((s, 128), F32),
            jax.ShapeDtypeStruct((s // tm, nblk, D_MODEL), F32),
            jax.ShapeDtypeStruct((HEADS, s // MOBA_BLOCK, HEAD_DIM + ONES_ROWS, MOBA_BLOCK), BF16),
            jax.ShapeDtypeStruct((HEADS, s, HEAD_DIM), F32),
        ],
        compiler_params=pltpu.CompilerParams(
            dimension_semantics=("parallel",), vmem_limit_bytes=VMEM_LIMIT),
        name="proj",
    )(x2, norm_w, w_main, w_small, cos_t, s1_t, s2_t)


def _attn_kernel(q_ref, k_ref, vt_ref, km_ref, o_ref, bias_scr, sa_scr, pb_scr, *, nb):
    j = pl.program_id(1)
    q = q_ref[...]
    U = ATTN_UNROLL
    B = MOBA_BLOCK
    blk = lax.broadcasted_iota(jnp.int32, (nb, B), 0)
    blk_f = blk.astype(F32)

    def scores(g):
        n0 = jnp.minimum(g, nb // U - 1) * U
        kn = k_ref[pl.ds(pl.multiple_of(n0 * B, U * B), U * B), :]
        return lax.dot_general(kn, q, NT, preferred_element_type=F32)

    km = km_ref[...]
    km_hi = km.astype(BF16)
    km_lo = (km - km_hi.astype(F32)).astype(BF16)
    gate2 = lax.dot_general(jnp.concatenate([km_hi, km_lo], axis=0), q, NT,
                            preferred_element_type=F32)
    kj = k_ref[pl.ds(pl.multiple_of(j * B, B), B), :]
    s = lax.dot_general(kj, q, NT, preferred_element_type=F32)
    sa_scr[...] = scores(0)
    pb_scr[...] = jnp.zeros_like(pb_scr)

    gate = jnp.where(blk < j, gate2[:nb] + gate2[nb:], -jnp.inf)
    bias = jnp.full((nb, B), -jnp.inf, F32)
    for _ in range(MOBA_TOPK):
        m = jnp.max(gate, axis=0, keepdims=True)
        first = jnp.min(jnp.where(gate == m, blk_f, float(nb)), axis=0, keepdims=True)
        pick = (blk_f == first) & (m > -jnp.inf)
        bias = jnp.where(pick, 0.0, bias)
        gate = jnp.where(pick, -jnp.inf, gate)
    bias_scr[...] = bias

    krow = lax.broadcasted_iota(jnp.int32, (B, B), 0)
    qcol = lax.broadcasted_iota(jnp.int32, (B, B), 1)
    s = jnp.where(krow <= qcol, s, -jnp.inf)
    m0 = jnp.max(s, axis=0, keepdims=True)
    p = jnp.exp2(s - m0)
    acc0 = jnp.dot(vt_ref[0, j], p.astype(BF16), preferred_element_type=F32)


    def values(g, p_of):
        n0 = jnp.maximum(g, 0) * U
        pv = jnp.dot(vt_ref[0, n0], p_of(0), preferred_element_type=F32)
        for u in range(1, U):
            pv = pv + jnp.dot(vt_ref[0, n0 + u], p_of(u), preferred_element_type=F32)
        return pv

    def softmax(g, s_of, m):
        sn = [s_of(u) + bias_scr[pl.ds(g * U + u, 1), :] for u in range(U)]
        m_new = m
        for u in range(U):
            m_new = jnp.maximum(m_new, jnp.max(sn[u], axis=0, keepdims=True))
        alpha = jnp.exp2(m - m_new)
        return m_new, alpha, [jnp.exp2(sn[u] - m_new).astype(BF16) for u in range(U)]

    G = ATTN_GROUPS

    def body(t, carry):
        m, acc, alpha_last = carry
        acc = alpha_last * acc + values(G * t - 1, lambda u: pb_scr[u * B:(u + 1) * B, :])
        s_of = lambda u: sa_scr[u * B:(u + 1) * B, :]
        for i in range(G):
            if i + 1 < G:
                s_next = scores(G * t + i + 1)
            m, alpha, p_i = softmax(G * t + i, s_of, m)
            if i + 1 < G:
                acc = alpha * acc + values(G * t + i, lambda u, p_i=p_i: p_i[u])
                s_of = lambda u, s_next=s_next: s_next[u * B:(u + 1) * B]
        for u in range(U):
            pb_scr[u * B:(u + 1) * B, :] = p_i[u]
        sa_scr[...] = scores(G * t + G)
        return m, acc, alpha

    trips = (j + G * U - 1) // (G * U)
    _, acc, alpha_last = lax.fori_loop(0, trips, body, (m0, acc0, jnp.ones_like(m0)))
    acc = alpha_last * acc + values(G * trips - 1, lambda u: pb_scr[u * B:(u + 1) * B, :])
    o_ref[...] = (acc[:HEAD_DIM] / acc[HEAD_DIM:HEAD_DIM + 1]).T.astype(BF16)


def _moba_attention(proj, v_t, kmean):
    s = proj.shape[0]
    nb = s // MOBA_BLOCK
    return pl.pallas_call(
        functools.partial(_attn_kernel, nb=nb),
        grid=(HEADS, nb),
        in_specs=[
            pl.BlockSpec((MOBA_BLOCK, HEAD_DIM), lambda h, j: (j, h)),
            pl.BlockSpec((s, HEAD_DIM), lambda h, j: (0, HEADS + h)),
            pl.BlockSpec((1, nb, HEAD_DIM + ONES_ROWS, MOBA_BLOCK),
                         lambda h, j: (h, 0, 0, 0)),
            pl.BlockSpec((nb, HEAD_DIM), lambda h, j: (0, h)),
        ],
        out_specs=pl.BlockSpec((MOBA_BLOCK, HEAD_DIM), lambda h, j: (j, h)),
        out_shape=jax.ShapeDtypeStruct((s, HEADS * HEAD_DIM), BF16),
        scratch_shapes=[pltpu.VMEM((nb, MOBA_BLOCK), F32),
                        pltpu.VMEM((ATTN_UNROLL * MOBA_BLOCK, MOBA_BLOCK), F32),
                        pltpu.VMEM((ATTN_UNROLL * MOBA_BLOCK, MOBA_BLOCK), BF16)],
        compiler_params=pltpu.CompilerParams(
            dimension_semantics=("parallel", "arbitrary"), vmem_limit_bytes=VMEM_LIMIT),
        name="moba",
    )(proj, proj, v_t, kmean)


ROUTE_QBLOCKS = 4
TILES_PER_STEP = 8
SC_WINDOW = 128


def _top_blocks(gate, blk_f, nb):
    picks = []
    for _ in range(MOBA_TOPK):
        m = jnp.max(gate, axis=0, keepdims=True)
        first = jnp.min(jnp.where(gate == m, blk_f, float(nb)), axis=0, keepdims=True)
        pick = (blk_f == first) & (m > -jnp.inf)
        gate = jnp.where(pick, -jnp.inf, gate)
        picks.append(pick)
    return picks


def _route_kernel(q_ref, km_ref, info_ref, cnt_ref, run_scr, *, nb):
    jb = pl.program_id(1)
    L = ROUTE_QBLOCKS * MOBA_BLOCK

    @pl.when(jb == 0)
    def _():
        run_scr[...] = jnp.zeros_like(run_scr)

    km = km_ref[...]
    km_hi = km.astype(BF16)
    km_lo = (km - km_hi.astype(F32)).astype(BF16)
    gate2 = lax.dot_general(jnp.concatenate([km_hi, km_lo], axis=0), q_ref[...], NT,
                            preferred_element_type=F32)
    blk = lax.broadcasted_iota(jnp.int32, (nb, L), 0)
    blk_f = blk.astype(F32)
    qblk = jb * ROUTE_QBLOCKS + (lax.broadcasted_iota(jnp.int32, (nb, L), 1) >> 8)
    gate = jnp.where(blk < qblk, gate2[:nb] + gate2[nb:], -jnp.inf)
    picks = _top_blocks(gate, blk_f, nb)

    chosen = jnp.where(picks[0] | picks[1] | picks[2], 1.0, 0.0)
    before = jnp.where(lax.broadcasted_iota(jnp.int32, (L, L), 0)
                       < lax.broadcasted_iota(jnp.int32, (L, L), 1), 1.0, 0.0).astype(BF16)
    base = run_scr[:, 0:1] + jnp.dot(chosen.astype(BF16), before, preferred_element_type=F32)

    rows = []
    for pick in picks:
        rows.append(jnp.sum(jnp.where(pick, base, 0.0), axis=0, keepdims=True))
    for pick in picks:
        bid = jnp.sum(jnp.where(pick, blk_f, 0.0), axis=0, keepdims=True)
        valid = jnp.sum(jnp.where(pick, 1.0, 0.0), axis=0, keepdims=True)
        rows.append(jnp.where(valid > 0.0, bid, float(nb)))
    rows += [jnp.zeros((1, L), F32)] * (8 - len(rows))
    info_ref[...] = jnp.concatenate(rows, axis=0).astype(jnp.int32)

    run = run_scr[...] + jnp.sum(chosen, axis=1, keepdims=True)
    run_scr[...] = run
    cnt_ref[0] = run


def _route(proj, kmean):
    s = proj.shape[0]
    nb = s // MOBA_BLOCK
    L = ROUTE_QBLOCKS * MOBA_BLOCK
    steps = s // L
    return pl.pallas_call(
        functools.partial(_route_kernel, nb=nb),
        grid=(HEADS, steps),
        in_specs=[pl.BlockSpec((L, HEAD_DIM), lambda h, j: (j, h)),
                  pl.BlockSpec((nb, HEAD_DIM), lambda h, j: (0, h))],
        out_specs=[pl.BlockSpec((8, L), lambda h, j: (0, h * steps + j)),
                   pl.BlockSpec((1, nb, 128), lambda h, j: (h, 0, 0))],
        out_shape=[jax.ShapeDtypeStruct((8, HEADS * s), jnp.int32),
                   jax.ShapeDtypeStruct((HEADS, nb, 128), F32)],
        scratch_shapes=[pltpu.VMEM((nb, 128), F32)],
        compiler_params=pltpu.CompilerParams(
            dimension_semantics=("parallel", "arbitrary"), vmem_limit_bytes=VMEM_LIMIT),
        name="moba_route",
    )(proj, kmean)


def _dest_kernel(info_ref, cnt_ref, dest_ref, tb_ref, *, nb, s):
    h = pl.program_id(0)
    cap = 4 * s
    ntile = cap // MOBA_BLOCK
    cnt = cnt_ref[0]
    tiles = jnp.floor((cnt + float(MOBA_BLOCK - 1)) * (1.0 / MOBA_BLOCK))
    lower = jnp.where(lax.broadcasted_iota(jnp.int32, (nb, nb), 1)
                      < lax.broadcasted_iota(jnp.int32, (nb, nb), 0), 1.0, 0.0).astype(BF16)
    start = jnp.dot(lower, tiles.astype(BF16), preferred_element_type=F32)
    start1 = start[:, 0:1]
    tiles1 = tiles[:, 0:1]

    t_f = lax.broadcasted_iota(jnp.int32, (nb, ntile), 1).astype(F32)
    n_f = lax.broadcasted_iota(jnp.int32, (nb, ntile), 0).astype(F32)
    inside = (t_f >= start1) & (t_f < start1 + tiles1)
    tb = jnp.sum(jnp.where(inside, n_f, 0.0), axis=0, keepdims=True)
    used = jnp.sum(jnp.where(inside, 1.0, 0.0), axis=0, keepdims=True)
    tb = jnp.where(used > 0.0, tb, -1.0)
    tb_ref[0] = jnp.concatenate([tb, jnp.full((7, ntile), -1.0, F32)], axis=0).astype(jnp.int32)

    off1 = start1 * float(MOBA_BLOCK)
    CH = 2048
    blk_f = lax.broadcasted_iota(jnp.int32, (nb, CH), 0).astype(F32)
    lane = lax.broadcasted_iota(jnp.int32, (1, CH), 1)
    trash = HEADS * cap + (lane & (MOBA_BLOCK - 1))
    for ch in range(s // CH):
        sl = slice(ch * CH, (ch + 1) * CH)
        rows = []
        for r in range(MOBA_TOPK):
            rank = info_ref[r:r + 1, sl]
            bid = info_ref[MOBA_TOPK + r:MOBA_TOPK + r + 1, sl]
            off = jnp.sum(jnp.where(blk_f == bid.astype(F32), off1, 0.0), axis=0, keepdims=True)
            rows.append(jnp.where(bid < nb, h * cap + off.astype(jnp.int32) + rank, trash))
        rows += [jnp.zeros((1, CH), jnp.int32)] * (8 - len(rows))
        dest_ref[:, sl] = jnp.concatenate(rows, axis=0)


def _dest(info, cnt, s):
    nb = s // MOBA_BLOCK
    ntile = 4 * s // MOBA_BLOCK
    return pl.pallas_call(
        functools.partial(_dest_kernel, nb=nb, s=s),
        grid=(HEADS,),
        in_specs=[pl.BlockSpec((8, s), lambda h: (0, h)),
                  pl.BlockSpec((1, nb, 128), lambda h: (h, 0, 0))],
        out_specs=[pl.BlockSpec((8, s), lambda h: (0, h)),
                   pl.BlockSpec((1, 8, ntile), lambda h: (h, 0, 0))],
        out_shape=[jax.ShapeDtypeStruct((8, HEADS * s), jnp.int32),
                   jax.ShapeDtypeStruct((HEADS, 8, ntile), jnp.int32)],
        compiler_params=pltpu.CompilerParams(
            dimension_semantics=("parallel",), vmem_limit_bytes=VMEM_LIMIT),
        name="moba_dest",
    )(info, cnt)


def _sc_scatter_rows(rows, idx, n_out):
    m, c = rows.shape
    k = idx.shape[0]
    nwin = m // SC_WINDOW
    mesh = plsc.VectorSubcoreMesh(core_axis_name="core", subcore_axis_name="subcore")

    @pl.kernel(out_type=jax.ShapeDtypeStruct((n_out, c), rows.dtype), mesh=mesh)
    def kern(x_hbm, i_hbm, o_hbm):
        def body(x_vmem, i_vmem):
            pltpu.sync_copy(x_vmem, o_hbm.at[i_vmem.at[0]])

        pltpu.emit_pipeline(
            body, grid=(k * nwin,),
            in_specs=[pl.BlockSpec((SC_WINDOW, c), lambda i: (i % nwin, 0)),
                      pl.BlockSpec((1, SC_WINDOW), lambda i: (0, i))],
            out_specs=[],
            core_axis_name=("core", "subcore"),
            dimension_semantics=(pltpu.PARALLEL,),
        )(x_hbm, i_hbm)

    return kern(rows, idx.reshape(1, k * m))


def _sc_gather_rows(table, idx):
    m = idx.shape[0]
    c = table.shape[1]
    mesh = plsc.VectorSubcoreMesh(core_axis_name="core", subcore_axis_name="subcore")

    @pl.kernel(out_type=jax.ShapeDtypeStruct((m, c), table.dtype), mesh=mesh)
    def kern(x_hbm, i_hbm, o_hbm):
        def body(i_vmem, o_vmem):
            pltpu.sync_copy(x_hbm.at[i_vmem.at[0]], o_vmem)

        pltpu.emit_pipeline(
            body, grid=(m // SC_WINDOW,),
            in_specs=[pl.BlockSpec((1, SC_WINDOW), lambda i: (0, i))],
            out_specs=[pl.BlockSpec((SC_WINDOW, c), lambda i: (i, 0))],
            core_axis_name=("core", "subcore"),
            dimension_semantics=(pltpu.PARALLEL,),
        )(i_hbm, o_hbm)

    return kern(table, idx.reshape(1, m))


def _pack_partial(o_norm, lse_rows):
    u = lax.bitcast_convert_type(o_norm, jnp.uint32) + jnp.uint32(0x8000)
    word = (u & jnp.uint32(0xFFFF0000)) | (pltpu.roll(u, HEAD_DIM // 2, 1) >> 16)
    lane = lax.broadcasted_iota(jnp.int32, o_norm.shape, 1)
    return jnp.where(lane < HEAD_DIM // 2, word, lax.bitcast_convert_type(lse_rows, jnp.uint32))


def _unpack_partial(word):
    hi = lax.bitcast_convert_type(word & jnp.uint32(0xFFFF0000), F32)
    lo = lax.bitcast_convert_type(word << 16, F32)
    lane = lax.broadcasted_iota(jnp.int32, word.shape, 1)
    o = jnp.where(lane < HEAD_DIM // 2, hi, pltpu.roll(lo, HEAD_DIM // 2, 1))
    lse = lax.bitcast_convert_type(word, F32)[:, HEAD_DIM // 2:HEAD_DIM // 2 + 1]
    return o, lse


def _tiles_kernel(tb_ref, qs_ref, k_ref, vt_ref, o_ref, *, ntile):
    h = pl.program_id(0)
    g = pl.program_id(1)
    B = MOBA_BLOCK
    base = h * ntile + g * TILES_PER_STEP

    @pl.when(tb_ref[base] >= 0)
    def _():
        blocks = [jnp.maximum(tb_ref[base + u], 0) for u in range(TILES_PER_STEP)]
        s_t = [lax.dot_general(k_ref[pl.ds(pl.multiple_of(n * B, B), B), :],
                               qs_ref[u * B:(u + 1) * B, :].astype(BF16), NT,
                               preferred_element_type=F32) for u, n in enumerate(blocks)]
        m = [jnp.max(x, axis=0, keepdims=True) for x in s_t]
        p = [jnp.exp2(x - mm).astype(BF16) for x, mm in zip(s_t, m)]
        acc = [jnp.dot(vt_ref[0, n], pp, preferred_element_type=F32) for n, pp in zip(blocks, p)]
        for u in range(TILES_PER_STEP):
            l = acc[u][HEAD_DIM:HEAD_DIM + 1]
            o_norm = (acc[u][:HEAD_DIM] / l).T
            lse = jnp.broadcast_to(m[u] + jnp.log(l) * LOG2_E, (HEAD_DIM, B)).T
            o_ref[u * B:(u + 1) * B, :] = _pack_partial(o_norm, lse)


def _tiles(tb, qsorted, proj, v_t, s):
    nb = s // MOBA_BLOCK
    ntile = 4 * s // MOBA_BLOCK
    steps = ntile // TILES_PER_STEP
    rows = TILES_PER_STEP * MOBA_BLOCK
    grid_spec = pltpu.PrefetchScalarGridSpec(
        num_scalar_prefetch=1,
        grid=(HEADS, steps),
        in_specs=[pl.BlockSpec((rows, HEAD_DIM), lambda h, g, tb: (h * steps + g, 0)),
                  pl.BlockSpec((s, HEAD_DIM), lambda h, g, tb: (0, HEADS + h)),
                  pl.BlockSpec((1, nb, HEAD_DIM + ONES_ROWS, MOBA_BLOCK), lambda h, g, tb: (h, 0, 0, 0))],
        out_specs=pl.BlockSpec((rows, HEAD_DIM), lambda h, g, tb: (h * steps + g, 0)),
    )
    return pl.pallas_call(
        functools.partial(_tiles_kernel, ntile=ntile),
        grid_spec=grid_spec,
        out_shape=jax.ShapeDtypeStruct(qsorted.shape, jnp.uint32),
        compiler_params=pltpu.CompilerParams(
            dimension_semantics=("parallel", "arbitrary"), vmem_limit_bytes=VMEM_LIMIT),
        name="moba_tiles",
    )(tb, qsorted, proj, v_t)


def _merge_kernel(q_ref, k_ref, v_ref, part_ref, o_ref):
    j = pl.program_id(0)
    B = MOBA_BLOCK
    heads = [slice(hh * HEAD_DIM, (hh + 1) * HEAD_DIM) for hh in range(HEADS)]
    causal = (lax.broadcasted_iota(jnp.int32, (B, B), 1) <= lax.broadcasted_iota(jnp.int32, (B, B), 0))
    s = [lax.dot_general(q_ref[:, hs], k_ref[:, hs], NT, preferred_element_type=F32) for hs in heads]
    s = [jnp.where(causal, x, -jnp.inf) for x in s]
    m_own = [jnp.max(x, axis=-1, keepdims=True) for x in s]
    p = [jnp.exp2(x - m) for x, m in zip(s, m_own)]
    l_own = [jnp.sum(x, axis=-1, keepdims=True) for x in p]
    o_own = [jnp.dot(x.astype(BF16), v_ref[:, hs], preferred_element_type=F32) for x, hs in zip(p, heads)]
    for hh, hs in enumerate(heads):
        lse_own = m_own[hh] + jnp.log(l_own[hh]) * LOG2_E
        parts = []
        for r in range(MOBA_TOPK):
            o_r, lse_r = _unpack_partial(part_ref[r, hh])
            valid = r < j
            parts.append((jnp.where(valid, o_r, 0.0), jnp.where(valid, lse_r, -jnp.inf)))
        m_all = lse_own
        for _, lse_r in parts:
            m_all = jnp.maximum(m_all, lse_r)
        w = jnp.exp2(lse_own - m_all)
        num = (w / l_own[hh]) * o_own[hh]
        den = w
        for o_r, lse_r in parts:
            w = jnp.exp2(lse_r - m_all)
            num = num + w * o_r
            den = den + w
        o_ref[:, hs] = (num / den).astype(BF16)


def _merge(proj, parts, s):
    nb = s // MOBA_BLOCK
    width = HEADS * HEAD_DIM
    return pl.pallas_call(
        _merge_kernel,
        grid=(nb,),
        in_specs=[pl.BlockSpec((MOBA_BLOCK, width), lambda j: (j, 0)),
                  pl.BlockSpec((MOBA_BLOCK, width), lambda j: (j, 1)),
                  pl.BlockSpec((MOBA_BLOCK, width), lambda j: (j, 2)),
                  pl.BlockSpec((MOBA_TOPK, HEADS, MOBA_BLOCK, HEAD_DIM), lambda j: (0, 0, j, 0))],
        out_specs=pl.BlockSpec((MOBA_BLOCK, width), lambda j: (j, 0)),
        out_shape=jax.ShapeDtypeStruct((s, width), BF16),
        compiler_params=pltpu.CompilerParams(
            dimension_semantics=("parallel",), vmem_limit_bytes=VMEM_LIMIT),
        name="moba_merge",
    )(proj, proj, proj, parts)


def _moba_sparse(proj, v_t, kmean, qf):
    s = proj.shape[0]
    cap = 4 * s
    n_rows = HEADS * cap + MOBA_BLOCK
    info, cnt = _route(proj, kmean)
    dest, tb = _dest(info, cnt, s)
    dest3 = dest[:MOBA_TOPK]
    qsorted = _sc_scatter_rows(qf.reshape(HEADS * s, HEAD_DIM), dest3, n_rows)
    osorted = _tiles(tb[:, 0, :].reshape(-1), qsorted, proj, v_t, s)
    parts = _sc_gather_rows(osorted, dest3.reshape(-1))
    return _merge(proj, parts.reshape(MOBA_TOPK, HEADS, s, HEAD_DIM), s)


def _delta_kernel(q_ref, k_ref, v_ref, qh_ref, kh_ref, vh_ref, z_ref, small_ref,
                  cwq_ref, cwk_ref, cwv_ref, alog_ref, dtb_ref, onw_ref,
                  o_ref, state_scr, xx_scr, *, hg):
    c = pl.program_id(0)
    C = DN_CHUNK

    @pl.when(c == 0)
    def _():
        state_scr[...] = jnp.zeros_like(state_scr)

    def conv_silu(idx, x_ref, halo_ref, cw_ref):
        halo = jnp.where(c == 0, 0.0, halo_ref[...].astype(F32))
        xx_scr[idx, 0:HALO, :] = halo
        xx_scr[idx, HALO:HALO + C, :] = x_ref[...].astype(F32)
        y = jnp.zeros((C, DN_HEADS_PER_STEP * HEAD_DIM), F32)
        for t in range(DN_CONV):
            off = HALO - (DN_CONV - 1) + t
            y = y + cw_ref[t:t + 1, :] * xx_scr[idx, off:off + C, :]
        return y * _sigmoid(y)

    q_all = conv_silu(0, q_ref, qh_ref, cwq_ref)
    k_all = conv_silu(1, k_ref, kh_ref, cwk_ref)
    v_all = conv_silu(2, v_ref, vh_ref, cwv_ref)

    small = small_ref[...]
    lane = lax.broadcasted_iota(jnp.int32, (C, 128), 1)
    beta_all = _sigmoid(small)
    xs = small + dtb_ref[...]
    softplus = jnp.maximum(xs, 0.0) + jnp.log(1.0 + jnp.exp(-jnp.abs(xs)))
    g_all = -jnp.exp(alog_ref[...]) * softplus

    row = lax.broadcasted_iota(jnp.int32, (C, C), 0)
    col = lax.broadcasted_iota(jnp.int32, (C, C), 1)
    tril = row >= col
    strict = row > col
    rxc = row ^ col
    eye = (row == col).astype(F32)
    tril_f = tril.astype(F32)

    heads = range(DN_HEADS_PER_STEP)
    sls = [slice(hh * HEAD_DIM, (hh + 1) * HEAD_DIM) for hh in heads]

    def hmap(f, *lists):
        return [f(*xs) for xs in zip(*lists)]

    def pick(table, lane_idx):
        return jnp.sum(jnp.where(lane == lane_idx, table, 0.0), axis=-1, keepdims=True)

    q = [q_all[:, sl] for sl in sls]
    k = [k_all[:, sl] for sl in sls]
    v = [v_all[:, sl] for sl in sls]
    q = hmap(lambda x: x * lax.rsqrt(jnp.sum(x * x, axis=-1, keepdims=True) + EPS) * (HEAD_DIM ** -0.5), q)
    k = hmap(lambda x: x * lax.rsqrt(jnp.sum(x * x, axis=-1, keepdims=True) + EPS), k)
    beta = [pick(beta_all, hg * DN_HEADS_PER_STEP + hh) for hh in heads]
    g = [pick(g_all, hg * DN_HEADS_PER_STEP + hh + HEADS) for hh in heads]

    gcum_b = hmap(lambda x: jnp.dot(tril_f, jnp.broadcast_to(x, (C, 128)),
                                    precision=lax.Precision.HIGHEST, preferred_element_type=F32), g)
    gc = [x[:, 0:1] for x in gcum_b]
    gr = [x.T[0:1, :] for x in gcum_b]
    decay = hmap(lambda c_, r_: jnp.where(tril, jnp.exp(jnp.where(tril, c_ - r_, 0.0)), 0.0), gc, gr)
    e_g = hmap(jnp.exp, gc)
    g_last = [x[C - 1:C, :] for x in gc]

    kb = hmap(lambda a_, b_: a_ * b_, k, beta)
    vb = hmap(lambda a_, b_: a_ * b_, v, beta)
    a = hmap(lambda kb_, q_, k_: lax.dot_general(
        jnp.concatenate([kb_, q_], axis=0).astype(BF16), k_.astype(BF16), NT,
        preferred_element_type=F32), kb, q, k)
    lmat = hmap(lambda a_, d_: jnp.where(strict, a_[:C] * d_, 0.0), a, decay)
    attn = hmap(lambda a_, d_: a_[C:] * d_, a, decay)

    d1 = hmap(lambda l_: jnp.where(rxc < 16, l_, 0.0), lmat)
    d2 = hmap(lambda x: _bdot(x, x), d1)
    d4 = hmap(lambda x: _bdot(x, x), d2)
    d8 = hmap(lambda x: _bdot(x, x), d4)
    p1 = hmap(lambda x, y: _bdot(eye - x, eye + y), d1, d2)
    p2 = hmap(lambda x, y: _bdot(eye + x, eye + y), d4, d8)
    tmat = hmap(_bdot, p1, p2)
    sz = 16
    while sz < C:
        off = hmap(lambda l_: jnp.where((rxc >= sz) & (rxc < 2 * sz), l_, 0.0), lmat)
        x = hmap(_bdot, tmat, off)
        x = hmap(_bdot, x, tmat)
        tmat = hmap(lambda t_, x_: t_ - x_, tmat, x)
        sz *= 2

    uw = hmap(lambda t_, vb_, kb_, e_: _bdot(t_, jnp.concatenate([vb_, kb_ * e_], axis=1)),
              tmat, vb, kb, e_g)
    state = [state_scr[hh] for hh in heads]
    wq = hmap(lambda uw_, q_, e_, s_: _bdot(jnp.concatenate([uw_[:, HEAD_DIM:], q_ * e_], axis=0), s_),
              uw, q, e_g, state)
    v_new = hmap(lambda uw_, wq_: uw_[:, :HEAD_DIM] - wq_[:C], uw, wq)
    av = hmap(_bdot, attn, v_new)
    kt = hmap(lambda k_, gl_, gc_: (k_ * jnp.exp(gl_ - gc_)).T, k, g_last, gc)
    ds = hmap(_bdot, kt, v_new)
    for hh in heads:
        state_scr[hh] = state[hh] * jnp.exp(g_last[hh]) + ds[hh]
        o = wq[hh][C:] + av[hh]
        y = o * lax.rsqrt(jnp.mean(o * o, axis=-1, keepdims=True) + EPS) * onw_ref[...]
        z = z_ref[:, sls[hh]].astype(F32)
        o_ref[:, sls[hh]] = (y * (z * _sigmoid(z))).astype(BF16)


def _gated_deltanet(proj, small, conv_w, alog_v, dtb_v, onw, hg):
    s = proj.shape[0]
    nc = s // DN_CHUNK
    rb = DN_CHUNK // HALO
    hp = DN_HEADS_PER_STEP
    ng = HEADS // hp
    width = hp * HEAD_DIM

    def col(part):
        return pl.BlockSpec((DN_CHUNK, width), lambda c: (c, part * ng + hg))

    def halo(part):
        return pl.BlockSpec((HALO, width), lambda c: (jnp.maximum(c * rb - 1, 0), part * ng + hg))

    def cw(part):
        return pl.BlockSpec((DN_CONV, width), lambda c: (0, part * ng + hg))

    vec = pl.BlockSpec((1, 128), lambda c: (0, 0))
    return pl.pallas_call(
        functools.partial(_delta_kernel, hg=hg),
        grid=(nc,),
        in_specs=[col(3), col(4), col(5),
                  halo(3), halo(4), halo(5),
                  col(6),
                  pl.BlockSpec((DN_CHUNK, 128), lambda c: (c, 0)),
                  cw(0), cw(1), cw(2),
                  vec, vec, vec],
        out_specs=pl.BlockSpec((DN_CHUNK, width), lambda c: (c, 0)),
        out_shape=jax.ShapeDtypeStruct((s, width), BF16),
        scratch_shapes=[pltpu.VMEM((hp, HEAD_DIM, HEAD_DIM), F32),
                        pltpu.VMEM((3, HALO + DN_CHUNK, width), F32)],
        compiler_params=pltpu.CompilerParams(
            dimension_semantics=("arbitrary",), vmem_limit_bytes=VMEM_LIMIT),
        name="deltanet",
    )(proj, proj, proj, proj, proj, proj, proj, small, conv_w, conv_w, conv_w, alog_v, dtb_v, onw)


def _mix_out_kernel(ya_ref, *rest):
    ng = HEADS // DN_HEADS_PER_STEP
    yd_refs = rest[:ng]
    ga_ref, gd_ref, x_ref, wa_ref, wd_ref, wo_ref, nw_ref, o_ref = rest[ng:]
    width = DN_HEADS_PER_STEP * HEAD_DIM
    pa = jnp.dot(ya_ref[...], wa_ref[...], preferred_element_type=F32)
    pd = jnp.dot(yd_refs[0][...], wd_ref[0:width, :], preferred_element_type=F32)
    for g in range(1, ng):
        pd = pd + jnp.dot(yd_refs[g][...], wd_ref[g * width:(g + 1) * width, :],
                          preferred_element_type=F32)
    merged = _sigmoid(ga_ref[...].astype(F32)) * pa + _sigmoid(gd_ref[...].astype(F32)) * pd
    mo = jnp.dot(merged.astype(BF16), wo_ref[...], preferred_element_type=F32)
    y = mo * lax.rsqrt(jnp.mean(mo * mo, axis=-1, keepdims=True) + EPS) * nw_ref[...]
    o_ref[...] = x_ref[...] + y


def _mix_out(ya, yds, proj, x2, wa, wd, wo, nw, *, tm=512):
    s = x2.shape[0]
    row = lambda i: (i, 0)
    full = lambda i: (0, 0)
    wspec = pl.BlockSpec((D_MODEL, D_MODEL), full)
    return pl.pallas_call(
        _mix_out_kernel,
        grid=(s // tm,),
        in_specs=[pl.BlockSpec((tm, D_MODEL), row)]
                 + [pl.BlockSpec((tm, yd.shape[1]), row) for yd in yds]
                 + [pl.BlockSpec((tm, D_MODEL), lambda i: (i, 7)),
                  pl.BlockSpec((tm, D_MODEL), lambda i: (i, 8)),
                  pl.BlockSpec((tm, D_MODEL), row),
                  wspec, wspec, wspec, pl.BlockSpec((1, D_MODEL), full)],
        out_specs=pl.BlockSpec((tm, D_MODEL), row),
        out_shape=jax.ShapeDtypeStruct((s, D_MODEL), F32),
        compiler_params=pltpu.CompilerParams(
            dimension_semantics=("parallel",), vmem_limit_bytes=VMEM_LIMIT),
        name="mix_out",
    )(ya, *yds, proj, proj, x2, wa, wd, wo, nw)


def _ffn_kernel(x_ref, npre_ref, wg_ref, wu_ref, wd_ref, npost_ref, o_ref):
    x = x_ref[...]
    h = (x * lax.rsqrt(jnp.mean(x * x, axis=-1, keepdims=True) + EPS) * npre_ref[...]).astype(BF16)
    acc = jnp.zeros(x.shape, F32)
    for cc in range(D_FF // FF_CHUNK):
        sl = slice(cc * FF_CHUNK, (cc + 1) * FF_CHUNK)
        g = jnp.dot(h, wg_ref[:, sl], preferred_element_type=F32)
        u = jnp.dot(h, wu_ref[:, sl], preferred_element_type=F32)
        act = (g * _sigmoid(g) * u).astype(BF16)
        acc = acc + jnp.dot(act, wd_ref[sl, :], preferred_element_type=F32)
    y = acc * lax.rsqrt(jnp.mean(acc * acc, axis=-1, keepdims=True) + EPS) * npost_ref[...]
    o_ref[...] = x + y


def _ffn(x1, npre, wg, wu, wd, npost, *, tm=512):
    s = x1.shape[0]
    row = lambda i: (i, 0)
    full = lambda i: (0, 0)
    once = pl.Buffered(1)
    return pl.pallas_call(
        _ffn_kernel,
        grid=(s // tm,),
        in_specs=[pl.BlockSpec((tm, D_MODEL), row), pl.BlockSpec((1, D_MODEL), full),
                  pl.BlockSpec((D_MODEL, D_FF), full, pipeline_mode=once),
                  pl.BlockSpec((D_MODEL, D_FF), full, pipeline_mode=once),
                  pl.BlockSpec((D_FF, D_MODEL), full, pipeline_mode=once),
                  pl.BlockSpec((1, D_MODEL), full)],
        out_specs=pl.BlockSpec((tm, D_MODEL), row),
        out_shape=jax.ShapeDtypeStruct((s, D_MODEL), F32),
        compiler_params=pltpu.CompilerParams(
            dimension_semantics=("parallel",), vmem_limit_bytes=VMEM_LIMIT),
        name="ffn",
    )(x1, npre, wg, wu, wd, npost)


def _rope_tables(s):
    half = ROPE_DIM // 2
    inv = ROPE_THETA ** (-jnp.arange(half, dtype=F32) * 2.0 / ROPE_DIM)
    ang = jnp.arange(s).astype(F32)[:, None] * inv[None, :]
    cos, sin = jnp.cos(ang), jnp.sin(ang)
    ones = jnp.ones((s, HEAD_DIM - ROPE_DIM), F32)
    zeros_half = jnp.zeros((s, half), F32)
    zeros_rest = jnp.zeros((s, HEAD_DIM - ROPE_DIM), F32)
    cos_t = jnp.concatenate([cos, cos, ones], axis=-1)
    s1_t = jnp.concatenate([-sin, zeros_half, zeros_rest], axis=-1)
    s2_t = jnp.concatenate([zeros_half, sin, zeros_rest], axis=-1)
    return cos_t, s1_t, s2_t


def _layer(x2, l, norm_mix_pre, w_in, conv_w, a_log, dt_bias, o_norm_w, w_o_attn, w_o_delta,
           w_out, norm_mix_post, norm_ffn_pre, w_gate, w_up, w_down, norm_ffn_post):
    s = x2.shape[0]
    nb = s // MOBA_BLOCK
    wide = 7 * D_MODEL
    nsmall = 2 * HEADS
    w = w_in[l]
    w_main = jnp.concatenate([w[:, :wide], w[:, wide + nsmall:]], axis=1).astype(BF16)
    w_small = jnp.pad(w[:, wide:wide + nsmall], ((0, 0), (0, 128 - nsmall))).astype(BF16)
    cos_t, s1_t, s2_t = _rope_tables(s)

    proj, small, kmean, v_t, qf = _project(x2, norm_mix_pre[l][None, :], w_main, w_small,
                                           cos_t, s1_t, s2_t)
    ya = _moba_sparse(proj, v_t, kmean.reshape(nb, D_MODEL), qf)

    pad = lambda vec: jnp.pad(vec.astype(F32), (HEADS, 128 - 2 * HEADS))[None, :]
    yds = [_gated_deltanet(proj, small, conv_w[l], pad(a_log[l]), pad(dt_bias[l]),
                           o_norm_w[l][None, :], hg) for hg in range(HEADS // DN_HEADS_PER_STEP)]

    x1 = _mix_out(ya, yds, proj, x2, w_o_attn[l].astype(BF16), w_o_delta[l].astype(BF16),
                  w_out[l].astype(BF16), norm_mix_post[l][None, :])
    return _ffn(x1, norm_ffn_pre[l][None, :], w_gate[l].astype(BF16), w_up[l].astype(BF16),
                w_down[l].astype(BF16), norm_ffn_post[l][None, :])


def kernel(x, norm_mix_pre, w_in, conv_w, a_log, dt_bias, o_norm_w, w_o_attn, w_o_delta, w_out,
           norm_mix_post, norm_ffn_pre, w_gate, w_up, w_down, norm_ffn_post):
    b, s, d = x.shape
    assert d == D_MODEL and s % 512 == 0
    outs = []
    for bi in range(b):
        x2 = x[bi]
        for l in range(w_in.shape[0]):
            x2 = _layer(x2, l, norm_mix_pre, w_in, conv_w, a_log, dt_bias, o_norm_w, w_o_attn,
                        w_o_delta, w_out, norm_mix_post, norm_ffn_pre, w_gate, w_up, w_down,
                        norm_ffn_post)
        outs.append(x2)
    return jnp.stack(outs, axis=0)
```

```python
import functools
import math

import jax
import jax.numpy as jnp
from jax import lax
from jax.experimental import pallas as pl
from jax.experimental.pallas import tpu as pltpu
from jax.experimental.pallas import tpu_sc as plsc

D_MODEL = 1024
HEADS = 8
HEAD_DIM = 128
MOBA_BLOCK = 256
MOBA_TOPK = 3
ROPE_DIM = HEAD_DIM // 4
ROPE_THETA = 500000.0
DN_CONV = 4
ONES_ROWS = 16
DN_HEADS_PER_STEP = 4
DN_CHUNK = 256
D_FF = 2816
FF_CHUNK = 256
EPS = 1e-6
LOG2_E = math.log2(math.e)
ATTN_UNROLL = 4
ATTN_GROUPS = 2
HALO = 16

F32 = jnp.float32
BF16 = jnp.bfloat16
NT = (((1,), (1,)), ((), ()))

VMEM_LIMIT = 56 * 1024 * 1024


def _bdot(a, b):
    return jnp.dot(a.astype(BF16), b.astype(BF16), preferred_element_type=F32)


def _sigmoid(x):
    return 1.0 / (1.0 + jnp.exp(-x))


def _proj_kernel(x_ref, nw_ref, wa_ref, wb_ref, ws_ref, rope_ref,
                 out_ref, small_ref, kmean_ref, vt_ref, qf_ref, *, tm, q_scale):
    x = x_ref[...]
    ms = jnp.mean(x * x, axis=-1, keepdims=True)
    h = (x * lax.rsqrt(ms + EPS) * nw_ref[...]).astype(BF16)
    small_ref[...] = jnp.dot(h, ws_ref[...], preferred_element_type=F32)
    heads = [slice(hh * HEAD_DIM, (hh + 1) * HEAD_DIM) for hh in range(HEADS)]
    groups = [slice(g * MOBA_BLOCK, (g + 1) * MOBA_BLOCK) for g in range(tm // MOBA_BLOCK)]

    half = ROPE_DIM // 2
    tab = rope_ref[...]
    lane = lax.broadcasted_iota(jnp.int32, tab.shape, 1)
    cos_t = jnp.where(lane < half, tab, jnp.where(lane < ROPE_DIM, pltpu.roll(tab, half, 1), 1.0))
    s1_t = jnp.where(lane < half, -pltpu.roll(tab, HEAD_DIM - half, 1), 0.0)
    s2_t = jnp.where((lane >= half) & (lane < ROPE_DIM), tab, 0.0)

    def roped(a):
        return (a * cos_t + pltpu.roll(a, HEAD_DIM - half, 1) * s1_t + pltpu.roll(a, half, 1) * s2_t)

    na = wa_ref.shape[1] // D_MODEL
    for c in range(out_ref.shape[1] // D_MODEL):
        cols = slice(c * D_MODEL, (c + 1) * D_MODEL)
        w_c = wa_ref[:, cols] if c < na else wb_ref[:, (c - na) * D_MODEL:(c - na + 1) * D_MODEL]
        acc = jnp.dot(h, w_c, preferred_element_type=F32)
        if c == 0:
            for hh, hs in enumerate(heads):
                r = roped(acc[:, hs]) * q_scale
                out_ref[:, hs] = r.astype(BF16)
                qf_ref[hh] = r
        elif c == 1:
            for hs in heads:
                r = roped(acc[:, hs])
                out_ref[:, D_MODEL + hs.start:D_MODEL + hs.stop] = r.astype(BF16)
                for g, gs in enumerate(groups):
                    kmean_ref[0, g:g + 1, hs] = jnp.sum(r[gs], axis=0, keepdims=True) * (1.0 / MOBA_BLOCK)
        elif c == 2:
            out_ref[:, cols] = acc.astype(BF16)
            for hh, hs in enumerate(heads):
                for g, gs in enumerate(groups):
                    vt_ref[hh, g, 0:HEAD_DIM, :] = acc[gs, hs].T.astype(BF16)
                    vt_ref[hh, g, HEAD_DIM:, :] = jnp.ones((ONES_ROWS, MOBA_BLOCK), BF16)
        else:
            out_ref[:, cols] = acc.astype(BF16)


def _project(x2, norm_w, w_a, w_b, w_small, rope_t, *, tm=512):
    s = x2.shape[0]
    width = w_a.shape[1] + w_b.shape[1]
    once = pl.Buffered(1)
    nblk = tm // MOBA_BLOCK
    kern = functools.partial(_proj_kernel, tm=tm, q_scale=math.log2(math.e) / math.sqrt(HEAD_DIM))
    row = lambda i: (i, 0)
    full = lambda i: (0, 0)
    return pl.pallas_call(
        kern,
        grid=(s // tm,),
        in_specs=[
            pl.BlockSpec((tm, D_MODEL), row),
            pl.BlockSpec((1, D_MODEL), full),
            pl.BlockSpec((D_MODEL, w_a.shape[1]), full, pipeline_mode=once),
            pl.BlockSpec((D_MODEL, w_b.shape[1]), full, pipeline_mode=once),
            pl.BlockSpec((D_MODEL, 128), full),
            pl.BlockSpec((tm, HEAD_DIM), row),
        ],
        out_specs=[
            pl.BlockSpec((tm, width), row),
            pl.BlockSpec((tm, 128), row),
            pl.BlockSpec((1, nblk, D_MODEL), lambda i: (i, 0, 0)),
            pl.BlockSpec((HEADS, nblk, HEAD_DIM + ONES_ROWS, MOBA_BLOCK), lambda i: (0, i, 0, 0)),
            pl.BlockSpec((HEADS, tm, HEAD_DIM), lambda i: (0, i, 0)),
        ],
        out_shape=[
            jax.ShapeDtypeStruct((s, width), BF16),
            jax.ShapeDtypeStruct((s, 128), F32),
            jax.ShapeDtypeStruct((s // tm, nblk, D_MODEL), F32),
            jax.ShapeDtypeStruct((HEADS, s // MOBA_BLOCK, HEAD_DIM + ONES_ROWS, MOBA_BLOCK), BF16),
            jax.ShapeDtypeStruct((HEADS, s, HEAD_DIM), F32),
        ],
        compiler_params=pltpu.CompilerParams(
            dimension_semantics=("parallel",), vmem_limit_bytes=VMEM_LIMIT),
        name="proj",
    )(x2, norm_w, w_a, w_b, w_small, rope_t)


def _attn_kernel(q_ref, k_ref, vt_ref, km_ref, o_ref, bias_scr, sa_scr, pb_scr, *, nb):
    j = pl.program_id(1)
    q = q_ref[...]
    U = ATTN_UNROLL
    B = MOBA_BLOCK
    blk = lax.broadcasted_iota(jnp.int32, (nb, B), 0)
    blk_f = blk.astype(F32)

    def scores(g):
        n0 = jnp.minimum(g, nb // U - 1) * U
        kn = k_ref[pl.ds(pl.multiple_of(n0 * B, U * B), U * B), :]
        return lax.dot_general(kn, q, NT, preferred_element_type=F32)

    km = km_ref[...]
    km_hi = km.astype(BF16)
    km_lo = (km - km_hi.astype(F32)).astype(BF16)
    gate2 = lax.dot_general(jnp.concatenate([km_hi, km_lo], axis=0), q, NT,
                            preferred_element_type=F32)
    kj = k_ref[pl.ds(pl.multiple_of(j * B, B), B), :]
    s = lax.dot_general(kj, q, NT, preferred_element_type=F32)
    sa_scr[...] = scores(0)
    pb_scr[...] = jnp.zeros_like(pb_scr)

    gate = jnp.where(blk < j, gate2[:nb] + gate2[nb:], -jnp.inf)
    bias = jnp.full((nb, B), -jnp.inf, F32)
    for _ in range(MOBA_TOPK):
        m = jnp.max(gate, axis=0, keepdims=True)
        first = jnp.min(jnp.where(gate == m, blk_f, float(nb)), axis=0, keepdims=True)
        pick = (blk_f == first) & (m > -jnp.inf)
        bias = jnp.where(pick, 0.0, bias)
        gate = jnp.where(pick, -jnp.inf, gate)
    bias_scr[...] = bias

    krow = lax.broadcasted_iota(jnp.int32, (B, B), 0)
    qcol = lax.broadcasted_iota(jnp.int32, (B, B), 1)
    s = jnp.where(krow <= qcol, s, -jnp.inf)
    m0 = jnp.max(s, axis=0, keepdims=True)
    p = jnp.exp2(s - m0)
    acc0 = jnp.dot(vt_ref[0, j], p.astype(BF16), preferred_element_type=F32)


    def values(g, p_of):
        n0 = jnp.maximum(g, 0) * U
        pv = jnp.dot(vt_ref[0, n0], p_of(0), preferred_element_type=F32)
        for u in range(1, U):
            pv = pv + jnp.dot(vt_ref[0, n0 + u], p_of(u), preferred_element_type=F32)
        return pv

    def softmax(g, s_of, m):
        sn = [s_of(u) + bias_scr[pl.ds(g * U + u, 1), :] for u in range(U)]
        m_new = m
        for u in range(U):
            m_new = jnp.maximum(m_new, jnp.max(sn[u], axis=0, keepdims=True))
        alpha = jnp.exp2(m - m_new)
        return m_new, alpha, [jnp.exp2(sn[u] - m_new).astype(BF16) for u in range(U)]

    G = ATTN_GROUPS

    def body(t, carry):
        m, acc, alpha_last = carry
        acc = alpha_last * acc + values(G * t - 1, lambda u: pb_scr[u * B:(u + 1) * B, :])
        s_of = lambda u: sa_scr[u * B:(u + 1) * B, :]
        for i in range(G):
            if i + 1 < G:
                s_next = scores(G * t + i + 1)
            m, alpha, p_i = softmax(G * t + i, s_of, m)
            if i + 1 < G:
                acc = alpha * acc + values(G * t + i, lambda u, p_i=p_i: p_i[u])
                s_of = lambda u, s_next=s_next: s_next[u * B:(u + 1) * B]
        for u in range(U):
            pb_scr[u * B:(u + 1) * B, :] = p_i[u]
        sa_scr[...] = scores(G * t + G)
        return m, acc, alpha

    trips = (j + G * U - 1) // (G * U)
    _, acc, alpha_last = lax.fori_loop(0, trips, body, (m0, acc0, jnp.ones_like(m0)))
    acc = alpha_last * acc + values(G * trips - 1, lambda u: pb_scr[u * B:(u + 1) * B, :])
    o_ref[...] = (acc[:HEAD_DIM] / acc[HEAD_DIM:HEAD_DIM + 1]).T.astype(BF16)


def _moba_attention(proj, v_t, kmean):
    s = proj.shape[0]
    nb = s // MOBA_BLOCK
    return pl.pallas_call(
        functools.partial(_attn_kernel, nb=nb),
        grid=(HEADS, nb),
        in_specs=[
            pl.BlockSpec((MOBA_BLOCK, HEAD_DIM), lambda h, j: (j, h)),
            pl.BlockSpec((s, HEAD_DIM), lambda h, j: (0, HEADS + h)),
            pl.BlockSpec((1, nb, HEAD_DIM + ONES_ROWS, MOBA_BLOCK),
                         lambda h, j: (h, 0, 0, 0)),
            pl.BlockSpec((nb, HEAD_DIM), lambda h, j: (0, h)),
        ],
        out_specs=pl.BlockSpec((MOBA_BLOCK, HEAD_DIM), lambda h, j: (j, h)),
        out_shape=jax.ShapeDtypeStruct((s, HEADS * HEAD_DIM), BF16),
        scratch_shapes=[pltpu.VMEM((nb, MOBA_BLOCK), F32),
                        pltpu.VMEM((ATTN_UNROLL * MOBA_BLOCK, MOBA_BLOCK), F32),
                        pltpu.VMEM((ATTN_UNROLL * MOBA_BLOCK, MOBA_BLOCK), BF16)],
        compiler_params=pltpu.CompilerParams(
            dimension_semantics=("parallel", "arbitrary"), vmem_limit_bytes=VMEM_LIMIT),
        name="moba",
    )(proj, proj, v_t, kmean)


ROUTE_QBLOCKS = 4
TILES_PER_STEP = 8
SC_WINDOW = 128


def _top_blocks(gate, blk_f, nb):
    picks = []
    for _ in range(MOBA_TOPK):
        m = jnp.max(gate, axis=0, keepdims=True)
        first = jnp.min(jnp.where(gate == m, blk_f, float(nb)), axis=0, keepdims=True)
        pick = (blk_f == first) & (m > -jnp.inf)
        gate = jnp.where(pick, -jnp.inf, gate)
        picks.append(pick)
    return picks


def _route_kernel(q_ref, km_ref, info_ref, cnt_ref, run_scr, *, nb):
    jb = pl.program_id(1)
    L = ROUTE_QBLOCKS * MOBA_BLOCK

    @pl.when(jb == 0)
    def _():
        run_scr[...] = jnp.zeros_like(run_scr)

    km = km_ref[...]
    km_hi = km.astype(BF16)
    km_lo = (km - km_hi.astype(F32)).astype(BF16)
    gate2 = lax.dot_general(jnp.concatenate([km_hi, km_lo], axis=0), q_ref[...], NT,
                            preferred_element_type=F32)
    blk = lax.broadcasted_iota(jnp.int32, (nb, L), 0)
    blk_f = blk.astype(F32)
    qblk = jb * ROUTE_QBLOCKS + (lax.broadcasted_iota(jnp.int32, (nb, L), 1) >> 8)
    gate = jnp.where(blk < qblk, gate2[:nb] + gate2[nb:], -jnp.inf)
    picks = _top_blocks(gate, blk_f, nb)

    chosen = jnp.where(picks[0] | picks[1] | picks[2], 1.0, 0.0)
    before = jnp.where(lax.broadcasted_iota(jnp.int32, (L, L), 0)
                       < lax.broadcasted_iota(jnp.int32, (L, L), 1), 1.0, 0.0).astype(BF16)
    base = run_scr[:, 0:1] + jnp.dot(chosen.astype(BF16), before, preferred_element_type=F32)

    rows = []
    for pick in picks:
        rows.append(jnp.sum(jnp.where(pick, base, 0.0), axis=0, keepdims=True))
    for pick in picks:
        bid = jnp.sum(jnp.where(pick, blk_f, 0.0), axis=0, keepdims=True)
        valid = jnp.sum(jnp.where(pick, 1.0, 0.0), axis=0, keepdims=True)
        rows.append(jnp.where(valid > 0.0, bid, float(nb)))
    rows += [jnp.zeros((1, L), F32)] * (8 - len(rows))
    info_ref[...] = jnp.concatenate(rows, axis=0).astype(jnp.int32)

    run = run_scr[...] + jnp.sum(chosen, axis=1, keepdims=True)
    run_scr[...] = run
    cnt_ref[0] = run


def _route(proj, kmean):
    s = proj.shape[0]
    nb = s // MOBA_BLOCK
    L = ROUTE_QBLOCKS * MOBA_BLOCK
    steps = s // L
    return pl.pallas_call(
        functools.partial(_route_kernel, nb=nb),
        grid=(HEADS, steps),
        in_specs=[pl.BlockSpec((L, HEAD_DIM), lambda h, j: (j, h)),
                  pl.BlockSpec((nb, HEAD_DIM), lambda h, j: (0, h))],
        out_specs=[pl.BlockSpec((8, L), lambda h, j: (0, h * steps + j)),
                   pl.BlockSpec((1, nb, 128), lambda h, j: (h, 0, 0))],
        out_shape=[jax.ShapeDtypeStruct((8, HEADS * s), jnp.int32),
                   jax.ShapeDtypeStruct((HEADS, nb, 128), F32)],
        scratch_shapes=[pltpu.VMEM((nb, 128), F32)],
        compiler_params=pltpu.CompilerParams(
            dimension_semantics=("parallel", "arbitrary"), vmem_limit_bytes=VMEM_LIMIT),
        name="moba_route",
    )(proj, kmean)


def _dest_kernel(info_ref, cnt_ref, dest_ref, tb_ref, *, nb, s):
    h = pl.program_id(0)
    cap = 4 * s
    ntile = cap // MOBA_BLOCK
    cnt = cnt_ref[0]
    tiles = jnp.floor((cnt + float(MOBA_BLOCK - 1)) * (1.0 / MOBA_BLOCK))
    lower = jnp.where(lax.broadcasted_iota(jnp.int32, (nb, nb), 1)
                      < lax.broadcasted_iota(jnp.int32, (nb, nb), 0), 1.0, 0.0).astype(BF16)
    start = jnp.dot(lower, tiles.astype(BF16), preferred_element_type=F32)
    start1 = start[:, 0:1]
    tiles1 = tiles[:, 0:1]

    t_f = lax.broadcasted_iota(jnp.int32, (nb, ntile), 1).astype(F32)
    n_f = lax.broadcasted_iota(jnp.int32, (nb, ntile), 0).astype(F32)
    inside = (t_f >= start1) & (t_f < start1 + tiles1)
    tb = jnp.sum(jnp.where(inside, n_f, 0.0), axis=0, keepdims=True)
    used = jnp.sum(jnp.where(inside, 1.0, 0.0), axis=0, keepdims=True)
    tb = jnp.where(used > 0.0, tb, -1.0)
    tb_ref[0] = jnp.concatenate([tb, jnp.full((7, ntile), -1.0, F32)], axis=0).astype(jnp.int32)

    off1 = start1 * float(MOBA_BLOCK)
    CH = 2048
    blk_f = lax.broadcasted_iota(jnp.int32, (nb, CH), 0).astype(F32)
    lane = lax.broadcasted_iota(jnp.int32, (1, CH), 1)
    trash = HEADS * cap + (lane & (MOBA_BLOCK - 1))
    for ch in range(s // CH):
        sl = slice(ch * CH, (ch + 1) * CH)
        rows = []
        for r in range(MOBA_TOPK):
            rank = info_ref[r:r + 1, sl]
            bid = info_ref[MOBA_TOPK + r:MOBA_TOPK + r + 1, sl]
            off = jnp.sum(jnp.where(blk_f == bid.astype(F32), off1, 0.0), axis=0, keepdims=True)
            rows.append(jnp.where(bid < nb, h * cap + off.astype(jnp.int32) + rank, trash))
        rows += [jnp.zeros((1, CH), jnp.int32)] * (8 - len(rows))
        dest_ref[:, sl] = jnp.concatenate(rows, axis=0)


def _dest(info, cnt, s):
    nb = s // MOBA_BLOCK
    ntile = 4 * s // MOBA_BLOCK
    return pl.pallas_call(
        functools.partial(_dest_kernel, nb=nb, s=s),
        grid=(HEADS,),
        in_specs=[pl.BlockSpec((8, s), lambda h: (0, h)),
                  pl.BlockSpec((1, nb, 128), lambda h: (h, 0, 0))],
        out_specs=[pl.BlockSpec((8, s), lambda h: (0, h)),
                   pl.BlockSpec((1, 8, ntile), lambda h: (h, 0, 0))],
        out_shape=[jax.ShapeDtypeStruct((8, HEADS * s), jnp.int32),
                   jax.ShapeDtypeStruct((HEADS, 8, ntile), jnp.int32)],
        compiler_params=pltpu.CompilerParams(
            dimension_semantics=("parallel",), vmem_limit_bytes=VMEM_LIMIT),
        name="moba_dest",
    )(info, cnt)


def _sc_scatter_rows(rows, idx, n_out):
    m, c = rows.shape
    k = idx.shape[0]
    nwin = m // SC_WINDOW
    mesh = plsc.VectorSubcoreMesh(core_axis_name="core", subcore_axis_name="subcore")

    @pl.kernel(out_type=jax.ShapeDtypeStruct((n_out, c), rows.dtype), mesh=mesh)
    def kern(x_hbm, i_hbm, o_hbm):
        def body(x_vmem, i_vmem):
            pltpu.sync_copy(x_vmem, o_hbm.at[i_vmem.at[0]])

        pltpu.emit_pipeline(
            body, grid=(k * nwin,),
            in_specs=[pl.BlockSpec((SC_WINDOW, c), lambda i: (i % nwin, 0)),
                      pl.BlockSpec((1, SC_WINDOW), lambda i: (0, i))],
            out_specs=[],
            core_axis_name=("core", "subcore"),
            dimension_semantics=(pltpu.PARALLEL,),
        )(x_hbm, i_hbm)

    return kern(rows, idx.reshape(1, k * m))


def _sc_gather_rows(table, idx):
    m = idx.shape[0]
    c = table.shape[1]
    mesh = plsc.VectorSubcoreMesh(core_axis_name="core", subcore_axis_name="subcore")

    @pl.kernel(out_type=jax.ShapeDtypeStruct((m, c), table.dtype), mesh=mesh)
    def kern(x_hbm, i_hbm, o_hbm):
        def body(i_vmem, o_vmem):
            pltpu.sync_copy(x_hbm.at[i_vmem.at[0]], o_vmem)

        pltpu.emit_pipeline(
            body, grid=(m // SC_WINDOW,),
            in_specs=[pl.BlockSpec((1, SC_WINDOW), lambda i: (0, i))],
            out_specs=[pl.BlockSpec((SC_WINDOW, c), lambda i: (i, 0))],
            core_axis_name=("core", "subcore"),
            dimension_semantics=(pltpu.PARALLEL,),
        )(i_hbm, o_hbm)

    return kern(table, idx.reshape(1, m))


def _pack_partial_t(o_norm_t, lse):
    half = HEAD_DIM // 2
    u = lax.bitcast_convert_type(o_norm_t, jnp.uint32) + jnp.uint32(0x8000)
    word = (u[:half] & jnp.uint32(0xFFFF0000)) | (u[half:] >> 16)
    lse_bits = lax.bitcast_convert_type(jnp.broadcast_to(lse, word.shape), jnp.uint32)
    full = jnp.concatenate([word, lse_bits], axis=0)
    return lax.bitcast_convert_type(full, F32).T


def _unpack_partial(part):
    word = lax.bitcast_convert_type(part, jnp.uint32)
    hi = lax.bitcast_convert_type(word & jnp.uint32(0xFFFF0000), F32)
    lo = lax.bitcast_convert_type(word << 16, F32)
    lane = lax.broadcasted_iota(jnp.int32, word.shape, 1)
    o = jnp.where(lane < HEAD_DIM // 2, hi, pltpu.roll(lo, HEAD_DIM // 2, 1))
    return o, part[:, HEAD_DIM // 2:HEAD_DIM // 2 + 1]


def _tiles_kernel(tb_ref, qs_ref, k_ref, vt_ref, after_ref, o_ref, *, ntile):
    del after_ref
    h = pl.program_id(0)
    g = pl.program_id(1)
    B = MOBA_BLOCK
    base = h * ntile + g * TILES_PER_STEP

    @pl.when(tb_ref[base] >= 0)
    def _():
        blocks = [jnp.maximum(tb_ref[base + u], 0) for u in range(TILES_PER_STEP)]
        s_t = [lax.dot_general(k_ref[pl.ds(pl.multiple_of(n * B, B), B), :],
                               qs_ref[u * B:(u + 1) * B, :].astype(BF16), NT,
                               preferred_element_type=F32) for u, n in enumerate(blocks)]
        m = [jnp.max(x, axis=0, keepdims=True) for x in s_t]
        p = [jnp.exp2(x - mm).astype(BF16) for x, mm in zip(s_t, m)]
        acc = [jnp.dot(vt_ref[0, n], pp, preferred_element_type=F32) for n, pp in zip(blocks, p)]
        for u in range(TILES_PER_STEP):
            l = acc[u][HEAD_DIM:HEAD_DIM + 1]
            o_ref[u * B:(u + 1) * B, :] = _pack_partial_t(acc[u][:HEAD_DIM] / l,
                                                          m[u] + jnp.log(l) * LOG2_E)


def _tiles(tb, qsorted, proj, v_t, after, s):
    nb = s // MOBA_BLOCK
    ntile = 4 * s // MOBA_BLOCK
    steps = ntile // TILES_PER_STEP
    rows = TILES_PER_STEP * MOBA_BLOCK
    grid_spec = pltpu.PrefetchScalarGridSpec(
        num_scalar_prefetch=1,
        grid=(HEADS, steps),
        in_specs=[pl.BlockSpec((rows, HEAD_DIM), lambda h, g, tb: (h * steps + g, 0)),
                  pl.BlockSpec((s, HEAD_DIM), lambda h, g, tb: (0, HEADS + h)),
                  pl.BlockSpec((1, nb, HEAD_DIM + ONES_ROWS, MOBA_BLOCK), lambda h, g, tb: (h, 0, 0, 0)),
                  pl.BlockSpec((16, HEAD_DIM), lambda h, g, tb: (0, 0))],
        out_specs=pl.BlockSpec((rows, HEAD_DIM), lambda h, g, tb: (h * steps + g, 0)),
    )
    return pl.pallas_call(
        functools.partial(_tiles_kernel, ntile=ntile),
        grid_spec=grid_spec,
        out_shape=jax.ShapeDtypeStruct(qsorted.shape, F32),
        compiler_params=pltpu.CompilerParams(
            dimension_semantics=("parallel", "arbitrary"), vmem_limit_bytes=VMEM_LIMIT),
        name="moba_tiles",
    )(tb, qsorted, proj, v_t, after)


def _merge_kernel(q_ref, k_ref, v_ref, part_ref, o_ref):
    j = pl.program_id(0)
    B = MOBA_BLOCK
    heads = [slice(hh * HEAD_DIM, (hh + 1) * HEAD_DIM) for hh in range(HEADS)]
    causal = (lax.broadcasted_iota(jnp.int32, (B, B), 1) <= lax.broadcasted_iota(jnp.int32, (B, B), 0))
    s = [lax.dot_general(q_ref[:, hs], k_ref[:, hs], NT, preferred_element_type=F32) for hs in heads]
    s = [jnp.where(causal, x, -jnp.inf) for x in s]
    m_own = [jnp.max(x, axis=-1, keepdims=True) for x in s]
    p = [jnp.exp2(x - m) for x, m in zip(s, m_own)]
    l_own = [jnp.sum(x, axis=-1, keepdims=True) for x in p]
    o_own = [jnp.dot(x.astype(BF16), v_ref[:, hs], preferred_element_type=F32) for x, hs in zip(p, heads)]
    for hh, hs in enumerate(heads):
        lse_own = m_own[hh] + jnp.log(l_own[hh]) * LOG2_E
        parts = []
        for r in range(MOBA_TOPK):
            o_r, lse_r = _unpack_partial(part_ref[r, hh])
            valid = r < j
            parts.append((jnp.where(valid, o_r, 0.0), jnp.where(valid, lse_r, -jnp.inf)))
        m_all = lse_own
        for _, lse_r in parts:
            m_all = jnp.maximum(m_all, lse_r)
        w = jnp.exp2(lse_own - m_all)
        num = (w / l_own[hh]) * o_own[hh]
        den = w
        for o_r, lse_r in parts:
            w = jnp.exp2(lse_r - m_all)
            num = num + w * o_r
            den = den + w
        o_ref[:, hs] = (num / den).astype(BF16)


def _merge(proj, parts, s):
    nb = s // MOBA_BLOCK
    width = HEADS * HEAD_DIM
    return pl.pallas_call(
        _merge_kernel,
        grid=(nb,),
        in_specs=[pl.BlockSpec((MOBA_BLOCK, width), lambda j: (j, 0)),
                  pl.BlockSpec((MOBA_BLOCK, width), lambda j: (j, 1)),
                  pl.BlockSpec((MOBA_BLOCK, width), lambda j: (j, 2)),
                  pl.BlockSpec((MOBA_TOPK, HEADS, MOBA_BLOCK, HEAD_DIM), lambda j: (0, 0, j, 0))],
        out_specs=pl.BlockSpec((MOBA_BLOCK, width), lambda j: (j, 0)),
        out_shape=jax.ShapeDtypeStruct((s, width), BF16),
        compiler_params=pltpu.CompilerParams(
            dimension_semantics=("parallel",), vmem_limit_bytes=VMEM_LIMIT),
        name="moba_merge",
    )(proj, proj, proj, parts)


def _moba_sparse(proj, v_t, kmean, qf, run_during_scatter):
    s = proj.shape[0]
    cap = 4 * s
    n_rows = HEADS * cap + MOBA_BLOCK
    info, cnt = _route(proj, kmean)
    dest, tb = _dest(info, cnt, s)
    dest3 = dest[:MOBA_TOPK]
    qsorted = _sc_scatter_rows(qf.reshape(HEADS * s, HEAD_DIM), dest3, n_rows)
    after = run_during_scatter()
    osorted = _tiles(tb[:, 0, :].reshape(-1), qsorted, proj, v_t, after, s)
    parts = _sc_gather_rows(osorted, dest3.reshape(-1))
    return _merge(proj, parts.reshape(MOBA_TOPK, HEADS, s, HEAD_DIM), s)


def _delta_kernel(q_ref, k_ref, v_ref, qh_ref, kh_ref, vh_ref, z_ref, small_ref,
                  cwq_ref, cwk_ref, cwv_ref, alog_ref, dtb_ref, onw_ref,
                  o_ref, state_scr, xx_scr, *, hg):
    c = pl.program_id(0)
    C = DN_CHUNK

    @pl.when(c == 0)
    def _():
        state_scr[...] = jnp.zeros_like(state_scr)

    def conv_silu(idx, x_ref, halo_ref, cw_ref):
        halo = jnp.where(c == 0, 0.0, halo_ref[...].astype(F32))
        xx_scr[idx, 0:HALO, :] = halo
        xx_scr[idx, HALO:HALO + C, :] = x_ref[...].astype(F32)
        y = jnp.zeros((C, DN_HEADS_PER_STEP * HEAD_DIM), F32)
        for t in range(DN_CONV):
            off = HALO - (DN_CONV - 1) + t
            y = y + cw_ref[t:t + 1, :] * xx_scr[idx, off:off + C, :]
        return y * _sigmoid(y)

    q_all = conv_silu(0, q_ref, qh_ref, cwq_ref)
    k_all = conv_silu(1, k_ref, kh_ref, cwk_ref)
    v_all = conv_silu(2, v_ref, vh_ref, cwv_ref)

    small = small_ref[...]
    lane = lax.broadcasted_iota(jnp.int32, (C, 128), 1)
    beta_all = _sigmoid(small)
    xs = small + dtb_ref[...]
    softplus = jnp.maximum(xs, 0.0) + jnp.log(1.0 + jnp.exp(-jnp.abs(xs)))
    g_all = -jnp.exp(alog_ref[...]) * softplus

    row = lax.broadcasted_iota(jnp.int32, (C, C), 0)
    col = lax.broadcasted_iota(jnp.int32, (C, C), 1)
    tril = row >= col
    strict = row > col
    rxc = row ^ col
    eye = (row == col).astype(F32)
    tril_f = tril.astype(F32)

    heads = range(DN_HEADS_PER_STEP)
    sls = [slice(hh * HEAD_DIM, (hh + 1) * HEAD_DIM) for hh in heads]

    def hmap(f, *lists):
        return [f(*xs) for xs in zip(*lists)]

    def pick(table, lane_idx):
        return jnp.sum(jnp.where(lane == lane_idx, table, 0.0), axis=-1, keepdims=True)

    q = [q_all[:, sl] for sl in sls]
    k = [k_all[:, sl] for sl in sls]
    v = [v_all[:, sl] for sl in sls]
    q = hmap(lambda x: x * lax.rsqrt(jnp.sum(x * x, axis=-1, keepdims=True) + EPS) * (HEAD_DIM ** -0.5), q)
    k = hmap(lambda x: x * lax.rsqrt(jnp.sum(x * x, axis=-1, keepdims=True) + EPS), k)
    beta = [pick(beta_all, hg * DN_HEADS_PER_STEP + hh) for hh in heads]
    g = [pick(g_all, hg * DN_HEADS_PER_STEP + hh + HEADS) for hh in heads]

    gcum_b = hmap(lambda x: jnp.dot(tril_f, jnp.broadcast_to(x, (C, 128)),
                                    precision=lax.Precision.HIGHEST, preferred_element_type=F32), g)
    gc = [x[:, 0:1] for x in gcum_b]
    gr = [x.T[0:1, :] for x in gcum_b]
    decay = hmap(lambda c_, r_: jnp.where(tril, jnp.exp(jnp.where(tril, c_ - r_, 0.0)), 0.0), gc, gr)
    e_g = hmap(jnp.exp, gc)
    g_last = [x[C - 1:C, :] for x in gc]

    kb = hmap(lambda a_, b_: a_ * b_, k, beta)
    vb = hmap(lambda a_, b_: a_ * b_, v, beta)
    a = hmap(lambda kb_, q_, k_: lax.dot_general(
        jnp.concatenate([kb_, q_], axis=0).astype(BF16), k_.astype(BF16), NT,
        preferred_element_type=F32), kb, q, k)
    lmat = hmap(lambda a_, d_: jnp.where(strict, a_[:C] * d_, 0.0), a, decay)
    attn = hmap(lambda a_, d_: a_[C:] * d_, a, decay)

    d1 = hmap(lambda l_: jnp.where(rxc < 16, l_, 0.0), lmat)
    d2 = hmap(lambda x: _bdot(x, x), d1)
    d4 = hmap(lambda x: _bdot(x, x), d2)
    d8 = hmap(lambda x: _bdot(x, x), d4)
    p1 = hmap(lambda x, y: _bdot(eye - x, eye + y), d1, d2)
    p2 = hmap(lambda x, y: _bdot(eye + x, eye + y), d4, d8)
    tmat = hmap(_bdot, p1, p2)
    sz = 16
    while sz < C:
        off = hmap(lambda l_: jnp.where((rxc >= sz) & (rxc < 2 * sz), l_, 0.0), lmat)
        x = hmap(_bdot, tmat, off)
        x = hmap(_bdot, x, tmat)
        tmat = hmap(lambda t_, x_: t_ - x_, tmat, x)
        sz *= 2

    uw = hmap(lambda t_, vb_, kb_, e_: _bdot(t_, jnp.concatenate([vb_, kb_ * e_], axis=1)),
              tmat, vb, kb, e_g)
    state = [state_scr[hh] for hh in heads]
    wq = hmap(lambda uw_, q_, e_, s_: _bdot(jnp.concatenate([uw_[:, HEAD_DIM:], q_ * e_], axis=0), s_),
              uw, q, e_g, state)
    v_new = hmap(lambda uw_, wq_: uw_[:, :HEAD_DIM] - wq_[:C], uw, wq)
    av = hmap(_bdot, attn, v_new)
    kt = hmap(lambda k_, gl_, gc_: (k_ * jnp.exp(gl_ - gc_)).T, k, g_last, gc)
    ds = hmap(_bdot, kt, v_new)
    for hh in heads:
        state_scr[hh] = state[hh] * jnp.exp(g_last[hh]) + ds[hh]
        o = wq[hh][C:] + av[hh]
        y = o * lax.rsqrt(jnp.mean(o * o, axis=-1, keepdims=True) + EPS) * onw_ref[...]
        z = z_ref[:, sls[hh]].astype(F32)
        o_ref[:, sls[hh]] = (y * (z * _sigmoid(z))).astype(BF16)


def _gated_deltanet(proj, small, conv_w, alog_v, dtb_v, onw, hg):
    s = proj.shape[0]
    nc = s // DN_CHUNK
    rb = DN_CHUNK // HALO
    hp = DN_HEADS_PER_STEP
    ng = HEADS // hp
    width = hp * HEAD_DIM

    def col(part):
        return pl.BlockSpec((DN_CHUNK, width), lambda c: (c, part * ng + hg))

    def halo(part):
        return pl.BlockSpec((HALO, width), lambda c: (jnp.maximum(c * rb - 1, 0), part * ng + hg))

    def cw(part):
        return pl.BlockSpec((DN_CONV, width), lambda c: (0, part * ng + hg))

    vec = pl.BlockSpec((1, 128), lambda c: (0, 0))
    return pl.pallas_call(
        functools.partial(_delta_kernel, hg=hg),
        grid=(nc,),
        in_specs=[col(3), col(4), col(5),
                  halo(3), halo(4), halo(5),
                  col(6),
                  pl.BlockSpec((DN_CHUNK, 128), lambda c: (c, 0)),
                  cw(0), cw(1), cw(2),
                  vec, vec, vec],
        out_specs=pl.BlockSpec((DN_CHUNK, width), lambda c: (c, 0)),
        out_shape=jax.ShapeDtypeStruct((s, width), BF16),
        scratch_shapes=[pltpu.VMEM((hp, HEAD_DIM, HEAD_DIM), F32),
                        pltpu.VMEM((3, HALO + DN_CHUNK, width), F32)],
        compiler_params=pltpu.CompilerParams(
            dimension_semantics=("arbitrary",), vmem_limit_bytes=VMEM_LIMIT),
        name="deltanet",
    )(proj, proj, proj, proj, proj, proj, proj, small, conv_w, conv_w, conv_w, alog_v, dtb_v, onw)


def _mix_out_kernel(ya_ref, *rest):
    ng = HEADS // DN_HEADS_PER_STEP
    yd_refs = rest[:ng]
    ga_ref, gd_ref, x_ref, wa_ref, wd_ref, wo_ref, nw_ref, o_ref = rest[ng:]
    width = DN_HEADS_PER_STEP * HEAD_DIM
    pa = jnp.dot(ya_ref[...], wa_ref[...], preferred_element_type=F32)
    pd = jnp.dot(yd_refs[0][...], wd_ref[0:width, :], preferred_element_type=F32)
    for g in range(1, ng):
        pd = pd + jnp.dot(yd_refs[g][...], wd_ref[g * width:(g + 1) * width, :],
                          preferred_element_type=F32)
    merged = _sigmoid(ga_ref[...].astype(F32)) * pa + _sigmoid(gd_ref[...].astype(F32)) * pd
    mo = jnp.dot(merged.astype(BF16), wo_ref[...], preferred_element_type=F32)
    y = mo * lax.rsqrt(jnp.mean(mo * mo, axis=-1, keepdims=True) + EPS) * nw_ref[...]
    o_ref[...] = x_ref[...] + y


def _mix_out(ya, yds, proj, x2, wa, wd, wo, nw, *, tm=512):
    s = x2.shape[0]
    row = lambda i: (i, 0)
    full = lambda i: (0, 0)
    wspec = pl.BlockSpec((D_MODEL, D_MODEL), full)
    return pl.pallas_call(
        _mix_out_kernel,
        grid=(s // tm,),
        in_specs=[pl.BlockSpec((tm, D_MODEL), row)]
                 + [pl.BlockSpec((tm, yd.shape[1]), row) for yd in yds]
                 + [pl.BlockSpec((tm, D_MODEL), lambda i: (i, 7)),
                  pl.BlockSpec((tm, D_MODEL), lambda i: (i, 8)),
                  pl.BlockSpec((tm, D_MODEL), row),
                  wspec, wspec, wspec, pl.BlockSpec((1, D_MODEL), full)],
        out_specs=pl.BlockSpec((tm, D_MODEL), row),
        out_shape=jax.ShapeDtypeStruct((s, D_MODEL), F32),
        compiler_params=pltpu.CompilerParams(
            dimension_semantics=("parallel",), vmem_limit_bytes=VMEM_LIMIT),
        name="mix_out",
    )(ya, *yds, proj, proj, x2, wa, wd, wo, nw)


def _ffn_kernel(x_ref, npre_ref, wg_ref, wu_ref, wd_ref, npost_ref, o_ref):
    x = x_ref[...]
    h = (x * lax.rsqrt(jnp.mean(x * x, axis=-1, keepdims=True) + EPS) * npre_ref[...]).astype(BF16)
    acc = jnp.zeros(x.shape, F32)
    for cc in range(D_FF // FF_CHUNK):
        sl = slice(cc * FF_CHUNK, (cc + 1) * FF_CHUNK)
        g = jnp.dot(h, wg_ref[:, sl], preferred_element_type=F32)
        u = jnp.dot(h, wu_ref[:, sl], preferred_element_type=F32)
        act = (g * _sigmoid(g) * u).astype(BF16)
        acc = acc + jnp.dot(act, wd_ref[sl, :], preferred_element_type=F32)
    y = acc * lax.rsqrt(jnp.mean(acc * acc, axis=-1, keepdims=True) + EPS) * npost_ref[...]
    o_ref[...] = x + y


def _ffn(x1, npre, wg, wu, wd, npost, *, tm=512):
    s = x1.shape[0]
    row = lambda i: (i, 0)
    full = lambda i: (0, 0)
    once = pl.Buffered(1)
    return pl.pallas_call(
        _ffn_kernel,
        grid=(s // tm,),
        in_specs=[pl.BlockSpec((tm, D_MODEL), row), pl.BlockSpec((1, D_MODEL), full),
                  pl.BlockSpec((D_MODEL, D_FF), full, pipeline_mode=once),
                  pl.BlockSpec((D_MODEL, D_FF), full, pipeline_mode=once),
                  pl.BlockSpec((D_FF, D_MODEL), full, pipeline_mode=once),
                  pl.BlockSpec((1, D_MODEL), full)],
        out_specs=pl.BlockSpec((tm, D_MODEL), row),
        out_shape=jax.ShapeDtypeStruct((s, D_MODEL), F32),
        compiler_params=pltpu.CompilerParams(
            dimension_semantics=("parallel",), vmem_limit_bytes=VMEM_LIMIT),
        name="ffn",
    )(x1, npre, wg, wu, wd, npost)


def _rope_table(s):
    half = ROPE_DIM // 2
    rep = 128 // half
    inv = ROPE_THETA ** (-jnp.arange(half, dtype=F32) * 2.0 / ROPE_DIM)
    pos = (jnp.arange(s // rep)[:, None] * rep + jnp.arange(128)[None, :] // half).astype(F32)
    ang = pos * jnp.tile(inv, rep)[None, :]
    cos = jnp.cos(ang).reshape(s, half)
    sin = jnp.sin(ang).reshape(s, half)
    return jnp.concatenate([cos, sin, jnp.zeros((s, HEAD_DIM - ROPE_DIM), F32)], axis=-1)


def _layer(x2, l, norm_mix_pre, w_in, conv_w, a_log, dt_bias, o_norm_w, w_o_attn, w_o_delta,
           w_out, norm_mix_post, norm_ffn_pre, w_gate, w_up, w_down, norm_ffn_post):
    s = x2.shape[0]
    nb = s // MOBA_BLOCK
    wide = 7 * D_MODEL
    nsmall = 2 * HEADS
    w = w_in[l]
    w_a = w[:, :wide].astype(BF16)
    w_b = w[:, wide + nsmall:].astype(BF16)
    w_small = jnp.pad(w[:, wide:wide + nsmall], ((0, 0), (0, 128 - nsmall))).astype(BF16)

    proj, small, kmean, v_t, qf = _project(x2, norm_mix_pre[l][None, :], w_a, w_b, w_small,
                                           _rope_table(s))
    pad = lambda vec: jnp.pad(vec.astype(F32), (HEADS, 128 - 2 * HEADS))[None, :]
    delta = lambda hg: _gated_deltanet(proj, small, conv_w[l], pad(a_log[l]), pad(dt_bias[l]),
                                       o_norm_w[l][None, :], hg)
    yds = []
    ya = _moba_sparse(proj, v_t, kmean.reshape(nb, D_MODEL), qf,
                      lambda: yds.append(delta(0)) or yds[0])
    yds += [delta(hg) for hg in range(1, HEADS // DN_HEADS_PER_STEP)]

    x1 = _mix_out(ya, yds, proj, x2, w_o_attn[l].astype(BF16), w_o_delta[l].astype(BF16),
                  w_out[l].astype(BF16), norm_mix_post[l][None, :])
    return _ffn(x1, norm_ffn_pre[l][None, :], w_gate[l].astype(BF16), w_up[l].astype(BF16),
                w_down[l].astype(BF16), norm_ffn_post[l][None, :])


def kernel(x, norm_mix_pre, w_in, conv_w, a_log, dt_bias, o_norm_w, w_o_attn, w_o_delta, w_out,
           norm_mix_post, norm_ffn_pre, w_gate, w_up, w_down, norm_ffn_post):
    b, s, d = x.shape
    assert d == D_MODEL and s % 512 == 0
    outs = []
    for bi in range(b):
        x2 = x.reshape(s, d) if b == 1 else x[bi]
        for l in range(w_in.shape[0]):
            x2 = _layer(x2, l, norm_mix_pre, w_in, conv_w, a_log, dt_bias, o_norm_w, w_o_attn,
                        w_o_delta, w_out, norm_mix_post, norm_ffn_pre, w_gate, w_up, w_down,
                        norm_ffn_post)
        outs.append(x2)
    return outs[0].reshape(1, s, d) if b == 1 else jnp.stack(outs, axis=0)
```

```python
import functools
import math

import jax
import jax.numpy as jnp
from jax import lax
from jax.experimental import pallas as pl
from jax.experimental.pallas import tpu as pltpu
from jax.experimental.pallas import tpu_sc as plsc

D_MODEL = 1024
HEADS = 8
HEAD_DIM = 128
MOBA_BLOCK = 256
MOBA_TOPK = 3
ROPE_DIM = HEAD_DIM // 4
ROPE_THETA = 500000.0
DN_CONV = 4
ONES_ROWS = 16
DN_HEADS_PER_STEP = 4
DN_CHUNK = 256
D_FF = 2816
FF_CHUNK = 256
EPS = 1e-6
LOG2_E = math.log2(math.e)
ATTN_UNROLL = 4
ATTN_GROUPS = 2
HALO = 16

F32 = jnp.float32
BF16 = jnp.bfloat16
NT = (((1,), (1,)), ((), ()))

VMEM_LIMIT = 56 * 1024 * 1024


def _bdot(a, b):
    return jnp.dot(a.astype(BF16), b.astype(BF16), preferred_element_type=F32)


def _sigmoid(x):
    return 1.0 / (1.0 + jnp.exp(-x))


def _proj_kernel(x_ref, nw_ref, wa_ref, wb_ref, ws_ref, rope_ref,
                 out_ref, small_ref, kmean_ref, vt_ref, qf_ref, *, tm, q_scale):
    x = x_ref[...]
    ms = jnp.mean(x * x, axis=-1, keepdims=True)
    h = (x * lax.rsqrt(ms + EPS) * nw_ref[...]).astype(BF16)
    small_ref[...] = jnp.dot(h, ws_ref[...], preferred_element_type=F32)
    heads = [slice(hh * HEAD_DIM, (hh + 1) * HEAD_DIM) for hh in range(HEADS)]
    groups = [slice(g * MOBA_BLOCK, (g + 1) * MOBA_BLOCK) for g in range(tm // MOBA_BLOCK)]

    half = ROPE_DIM // 2
    tab = rope_ref[...]
    lane = lax.broadcasted_iota(jnp.int32, tab.shape, 1)
    cos_t = jnp.where(lane < half, tab, jnp.where(lane < ROPE_DIM, pltpu.roll(tab, half, 1), 1.0))
    s1_t = jnp.where(lane < half, -pltpu.roll(tab, HEAD_DIM - half, 1), 0.0)
    s2_t = jnp.where((lane >= half) & (lane < ROPE_DIM), tab, 0.0)

    def roped(a):
        return (a * cos_t + pltpu.roll(a, HEAD_DIM - half, 1) * s1_t + pltpu.roll(a, half, 1) * s2_t)

    na = wa_ref.shape[1] // D_MODEL
    for c in range(out_ref.shape[1] // D_MODEL):
        cols = slice(c * D_MODEL, (c + 1) * D_MODEL)
        w_c = wa_ref[:, cols] if c < na else wb_ref[:, (c - na) * D_MODEL:(c - na + 1) * D_MODEL]
        acc = jnp.dot(h, w_c, preferred_element_type=F32)
        if c == 0:
            for hh, hs in enumerate(heads):
                r = roped(acc[:, hs]) * q_scale
                out_ref[:, hs] = r.astype(BF16)
                qf_ref[hh] = r
        elif c == 1:
            for hs in heads:
                r = roped(acc[:, hs])
                out_ref[:, D_MODEL + hs.start:D_MODEL + hs.stop] = r.astype(BF16)
                for g, gs in enumerate(groups):
                    kmean_ref[0, g:g + 1, hs] = jnp.sum(r[gs], axis=0, keepdims=True) * (1.0 / MOBA_BLOCK)
        elif c == 2:
            out_ref[:, cols] = acc.astype(BF16)
            for hh, hs in enumerate(heads):
                for g, gs in enumerate(groups):
                    vt_ref[hh, g, 0:HEAD_DIM, :] = acc[gs, hs].T.astype(BF16)
                    vt_ref[hh, g, HEAD_DIM:, :] = jnp.ones((ONES_ROWS, MOBA_BLOCK), BF16)
        else:
            out_ref[:, cols] = acc.astype(BF16)


def _project(x2, norm_w, w_a, w_b, w_small, rope_t, *, tm=512):
    s = x2.shape[0]
    width = w_a.shape[1] + w_b.shape[1]
    once = pl.Buffered(1)
    nblk = tm // MOBA_BLOCK
    kern = functools.partial(_proj_kernel, tm=tm, q_scale=math.log2(math.e) / math.sqrt(HEAD_DIM))
    row = lambda i: (i, 0)
    full = lambda i: (0, 0)
    return pl.pallas_call(
        kern,
        grid=(s // tm,),
        in_specs=[
            pl.BlockSpec((tm, D_MODEL), row),
            pl.BlockSpec((1, D_MODEL), full),
            pl.BlockSpec((D_MODEL, w_a.shape[1]), full, pipeline_mode=once),
            pl.BlockSpec((D_MODEL, w_b.shape[1]), full, pipeline_mode=once),
            pl.BlockSpec((D_MODEL, 128), full),
            pl.BlockSpec((tm, HEAD_DIM), row),
        ],
        out_specs=[
            pl.BlockSpec((tm, width), row),
            pl.BlockSpec((tm, 128), row),
            pl.BlockSpec((1, nblk, D_MODEL), lambda i: (i, 0, 0)),
            pl.BlockSpec((HEADS, nblk, HEAD_DIM + ONES_ROWS, MOBA_BLOCK), lambda i: (0, i, 0, 0)),
            pl.BlockSpec((HEADS, tm, HEAD_DIM), lambda i: (0, i, 0)),
        ],
        out_shape=[
            jax.ShapeDtypeStruct((s, width), BF16),
            jax.ShapeDtypeStruct((s, 128), F32),
            jax.ShapeDtypeStruct((s // tm, nblk, D_MODEL), F32),
            jax.ShapeDtypeStruct((HEADS, s // MOBA_BLOCK, HEAD_DIM + ONES_ROWS, MOBA_BLOCK), BF16),
            jax.ShapeDtypeStruct((HEADS, s, HEAD_DIM), F32),
        ],
        compiler_params=pltpu.CompilerParams(
            dimension_semantics=("parallel",), vmem_limit_bytes=VMEM_LIMIT),
        name="proj",
    )(x2, norm_w, w_a, w_b, w_small, rope_t)


def _attn_kernel(q_ref, k_ref, vt_ref, km_ref, o_ref, bias_scr, sa_scr, pb_scr, *, nb):
    j = pl.program_id(1)
    q = q_ref[...]
    U = ATTN_UNROLL
    B = MOBA_BLOCK
    blk = lax.broadcasted_iota(jnp.int32, (nb, B), 0)
    blk_f = blk.astype(F32)

    def scores(g):
        n0 = jnp.minimum(g, nb // U - 1) * U
        kn = k_ref[pl.ds(pl.multiple_of(n0 * B, U * B), U * B), :]
        return lax.dot_general(kn, q, NT, preferred_element_type=F32)

    km = km_ref[...]
    km_hi = km.astype(BF16)
    km_lo = (km - km_hi.astype(F32)).astype(BF16)
    gate2 = lax.dot_general(jnp.concatenate([km_hi, km_lo], axis=0), q, NT,
                            preferred_element_type=F32)
    kj = k_ref[pl.ds(pl.multiple_of(j * B, B), B), :]
    s = lax.dot_general(kj, q, NT, preferred_element_type=F32)
    sa_scr[...] = scores(0)
    pb_scr[...] = jnp.zeros_like(pb_scr)

    gate = jnp.where(blk < j, gate2[:nb] + gate2[nb:], -jnp.inf)
    bias = jnp.full((nb, B), -jnp.inf, F32)
    for _ in range(MOBA_TOPK):
        m = jnp.max(gate, axis=0, keepdims=True)
        first = jnp.min(jnp.where(gate == m, blk_f, float(nb)), axis=0, keepdims=True)
        pick = (blk_f == first) & (m > -jnp.inf)
        bias = jnp.where(pick, 0.0, bias)
        gate = jnp.where(pick, -jnp.inf, gate)
    bias_scr[...] = bias

    krow = lax.broadcasted_iota(jnp.int32, (B, B), 0)
    qcol = lax.broadcasted_iota(jnp.int32, (B, B), 1)
    s = jnp.where(krow <= qcol, s, -jnp.inf)
    m0 = jnp.max(s, axis=0, keepdims=True)
    p = jnp.exp2(s - m0)
    acc0 = jnp.dot(vt_ref[0, j], p.astype(BF16), preferred_element_type=F32)


    def values(g, p_of):
        n0 = jnp.maximum(g, 0) * U
        pv = jnp.dot(vt_ref[0, n0], p_of(0), preferred_element_type=F32)
        for u in range(1, U):
            pv = pv + jnp.dot(vt_ref[0, n0 + u], p_of(u), preferred_element_type=F32)
        return pv

    def softmax(g, s_of, m):
        sn = [s_of(u) + bias_scr[pl.ds(g * U + u, 1), :] for u in range(U)]
        m_new = m
        for u in range(U):
            m_new = jnp.maximum(m_new, jnp.max(sn[u], axis=0, keepdims=True))
        alpha = jnp.exp2(m - m_new)
        return m_new, alpha, [jnp.exp2(sn[u] - m_new).astype(BF16) for u in range(U)]

    G = ATTN_GROUPS

    def body(t, carry):
        m, acc, alpha_last = carry
        acc = alpha_last * acc + values(G * t - 1, lambda u: pb_scr[u * B:(u + 1) * B, :])
        s_of = lambda u: sa_scr[u * B:(u + 1) * B, :]
        for i in range(G):
            if i + 1 < G:
                s_next = scores(G * t + i + 1)
            m, alpha, p_i = softmax(G * t + i, s_of, m)
            if i + 1 < G:
                acc = alpha * acc + values(G * t + i, lambda u, p_i=p_i: p_i[u])
                s_of = lambda u, s_next=s_next: s_next[u * B:(u + 1) * B]
        for u in range(U):
            pb_scr[u * B:(u + 1) * B, :] = p_i[u]
        sa_scr[...] = scores(G * t + G)
        return m, acc, alpha

    trips = (j + G * U - 1) // (G * U)
    _, acc, alpha_last = lax.fori_loop(0, trips, body, (m0, acc0, jnp.ones_like(m0)))
    acc = alpha_last * acc + values(G * trips - 1, lambda u: pb_scr[u * B:(u + 1) * B, :])
    o_ref[...] = (acc[:HEAD_DIM] / acc[HEAD_DIM:HEAD_DIM + 1]).T.astype(BF16)


def _moba_attention(proj, v_t, kmean):
    s = proj.shape[0]
    nb = s // MOBA_BLOCK
    return pl.pallas_call(
        functools.partial(_attn_kernel, nb=nb),
        grid=(HEADS, nb),
        in_specs=[
            pl.BlockSpec((MOBA_BLOCK, HEAD_DIM), lambda h, j: (j, h)),
            pl.BlockSpec((s, HEAD_DIM), lambda h, j: (0, HEADS + h)),
            pl.BlockSpec((1, nb, HEAD_DIM + ONES_ROWS, MOBA_BLOCK),
                         lambda h, j: (h, 0, 0, 0)),
            pl.BlockSpec((nb, HEAD_DIM), lambda h, j: (0, h)),
        ],
        out_specs=pl.BlockSpec((MOBA_BLOCK, HEAD_DIM), lambda h, j: (j, h)),
        out_shape=jax.ShapeDtypeStruct((s, HEADS * HEAD_DIM), BF16),
        scratch_shapes=[pltpu.VMEM((nb, MOBA_BLOCK), F32),
                        pltpu.VMEM((ATTN_UNROLL * MOBA_BLOCK, MOBA_BLOCK), F32),
                        pltpu.VMEM((ATTN_UNROLL * MOBA_BLOCK, MOBA_BLOCK), BF16)],
        compiler_params=pltpu.CompilerParams(
            dimension_semantics=("parallel", "arbitrary"), vmem_limit_bytes=VMEM_LIMIT),
        name="moba",
    )(proj, proj, v_t, kmean)


ROUTE_QBLOCKS = 4
TILES_PER_STEP = 8
SC_WINDOW = 128


def _top_blocks(gate, blk_f, nb):
    picks = []
    for _ in range(MOBA_TOPK):
        m = jnp.max(gate, axis=0, keepdims=True)
        first = jnp.min(jnp.where(gate == m, blk_f, float(nb)), axis=0, keepdims=True)
        pick = (blk_f == first) & (m > -jnp.inf)
        gate = jnp.where(pick, -jnp.inf, gate)
        picks.append(pick)
    return picks


def _route_kernel(q_ref, km_ref, info_ref, cnt_ref, run_scr, *, nb):
    jb = pl.program_id(1)
    L = ROUTE_QBLOCKS * MOBA_BLOCK

    @pl.when(jb == 0)
    def _():
        run_scr[...] = jnp.zeros_like(run_scr)

    km = km_ref[...]
    km_hi = km.astype(BF16)
    km_lo = (km - km_hi.astype(F32)).astype(BF16)
    gate2 = lax.dot_general(jnp.concatenate([km_hi, km_lo], axis=0), q_ref[...], NT,
                            preferred_element_type=F32)
    blk = lax.broadcasted_iota(jnp.int32, (nb, L), 0)
    blk_f = blk.astype(F32)
    qblk = jb * ROUTE_QBLOCKS + (lax.broadcasted_iota(jnp.int32, (nb, L), 1) >> 8)
    gate = jnp.where(blk < qblk, gate2[:nb] + gate2[nb:], -jnp.inf)
    picks = _top_blocks(gate, blk_f, nb)

    chosen = jnp.where(picks[0] | picks[1] | picks[2], 1.0, 0.0)
    before = jnp.where(lax.broadcasted_iota(jnp.int32, (L, L), 0)
                       < lax.broadcasted_iota(jnp.int32, (L, L), 1), 1.0, 0.0).astype(BF16)
    base = run_scr[:, 0:1] + jnp.dot(chosen.astype(BF16), before, preferred_element_type=F32)

    rows = []
    for pick in picks:
        rows.append(jnp.sum(jnp.where(pick, base, 0.0), axis=0, keepdims=True))
    for pick in picks:
        bid = jnp.sum(jnp.where(pick, blk_f, 0.0), axis=0, keepdims=True)
        valid = jnp.sum(jnp.where(pick, 1.0, 0.0), axis=0, keepdims=True)
        rows.append(jnp.where(valid > 0.0, bid, float(nb)))
    rows += [jnp.zeros((1, L), F32)] * (8 - len(rows))
    info_ref[...] = jnp.concatenate(rows, axis=0).astype(jnp.int32)

    run = run_scr[...] + jnp.sum(chosen, axis=1, keepdims=True)
    run_scr[...] = run
    cnt_ref[0] = run


def _route(proj, kmean):
    s = proj.shape[0]
    nb = s // MOBA_BLOCK
    L = ROUTE_QBLOCKS * MOBA_BLOCK
    steps = s // L
    return pl.pallas_call(
        functools.partial(_route_kernel, nb=nb),
        grid=(HEADS, steps),
        in_specs=[pl.BlockSpec((L, HEAD_DIM), lambda h, j: (j, h)),
                  pl.BlockSpec((nb, HEAD_DIM), lambda h, j: (0, h))],
        out_specs=[pl.BlockSpec((8, L), lambda h, j: (0, h * steps + j)),
                   pl.BlockSpec((1, nb, 128), lambda h, j: (h, 0, 0))],
        out_shape=[jax.ShapeDtypeStruct((8, HEADS * s), jnp.int32),
                   jax.ShapeDtypeStruct((HEADS, nb, 128), F32)],
        scratch_shapes=[pltpu.VMEM((nb, 128), F32)],
        compiler_params=pltpu.CompilerParams(
            dimension_semantics=("parallel", "arbitrary"), vmem_limit_bytes=VMEM_LIMIT),
        name="moba_route",
    )(proj, kmean)


def _dest_kernel(info_ref, cnt_ref, dest_ref, tb_ref, *, nb, s):
    h = pl.program_id(0)
    cap = 4 * s
    ntile = cap // MOBA_BLOCK
    cnt = cnt_ref[0]
    tiles = jnp.floor((cnt + float(MOBA_BLOCK - 1)) * (1.0 / MOBA_BLOCK))
    lower = jnp.where(lax.broadcasted_iota(jnp.int32, (nb, nb), 1)
                      < lax.broadcasted_iota(jnp.int32, (nb, nb), 0), 1.0, 0.0).astype(BF16)
    start = jnp.dot(lower, tiles.astype(BF16), preferred_element_type=F32)
    start1 = start[:, 0:1]
    tiles1 = tiles[:, 0:1]

    t_f = lax.broadcasted_iota(jnp.int32, (nb, ntile), 1).astype(F32)
    n_f = lax.broadcasted_iota(jnp.int32, (nb, ntile), 0).astype(F32)
    inside = (t_f >= start1) & (t_f < start1 + tiles1)
    tb = jnp.sum(jnp.where(inside, n_f, 0.0), axis=0, keepdims=True)
    used = jnp.sum(jnp.where(inside, 1.0, 0.0), axis=0, keepdims=True)
    tb = jnp.where(used > 0.0, tb, -1.0)
    tb_ref[0] = jnp.concatenate([tb, jnp.full((7, ntile), -1.0, F32)], axis=0).astype(jnp.int32)

    off1 = start1 * float(MOBA_BLOCK)
    CH = 2048
    blk_f = lax.broadcasted_iota(jnp.int32, (nb, CH), 0).astype(F32)
    lane = lax.broadcasted_iota(jnp.int32, (1, CH), 1)
    trash = HEADS * cap + (lane & (MOBA_BLOCK - 1))
    for ch in range(s // CH):
        sl = slice(ch * CH, (ch + 1) * CH)
        rows = []
        for r in range(MOBA_TOPK):
            rank = info_ref[r:r + 1, sl]
            bid = info_ref[MOBA_TOPK + r:MOBA_TOPK + r + 1, sl]
            off = jnp.sum(jnp.where(blk_f == bid.astype(F32), off1, 0.0), axis=0, keepdims=True)
            rows.append(jnp.where(bid < nb, h * cap + off.astype(jnp.int32) + rank, trash))
        rows += [jnp.zeros((1, CH), jnp.int32)] * (8 - len(rows))
        dest_ref[:, sl] = jnp.concatenate(rows, axis=0)


def _dest(info, cnt, s):
    nb = s // MOBA_BLOCK
    ntile = 4 * s // MOBA_BLOCK
    return pl.pallas_call(
        functools.partial(_dest_kernel, nb=nb, s=s),
        grid=(HEADS,),
        in_specs=[pl.BlockSpec((8, s), lambda h: (0, h)),
                  pl.BlockSpec((1, nb, 128), lambda h: (h, 0, 0))],
        out_specs=[pl.BlockSpec((8, s), lambda h: (0, h)),
                   pl.BlockSpec((1, 8, ntile), lambda h: (h, 0, 0))],
        out_shape=[jax.ShapeDtypeStruct((8, HEADS * s), jnp.int32),
                   jax.ShapeDtypeStruct((HEADS, 8, ntile), jnp.int32)],
        compiler_params=pltpu.CompilerParams(
            dimension_semantics=("parallel",), vmem_limit_bytes=VMEM_LIMIT),
        name="moba_dest",
    )(info, cnt)


def _sc_scatter_rows(rows, idx, n_out):
    m, c = rows.shape
    k = idx.shape[0]
    nwin = m // SC_WINDOW
    mesh = plsc.VectorSubcoreMesh(core_axis_name="core", subcore_axis_name="subcore")

    @pl.kernel(out_type=jax.ShapeDtypeStruct((n_out, c), rows.dtype), mesh=mesh)
    def kern(x_hbm, i_hbm, o_hbm):
        def body(x_vmem, i_vmem):
            pltpu.sync_copy(x_vmem, o_hbm.at[i_vmem.at[0]])

        pltpu.emit_pipeline(
            body, grid=(k * nwin,),
            in_specs=[pl.BlockSpec((SC_WINDOW, c), lambda i: (i % nwin, 0)),
                      pl.BlockSpec((1, SC_WINDOW), lambda i: (0, i))],
            out_specs=[],
            core_axis_name=("core", "subcore"),
            dimension_semantics=(pltpu.PARALLEL,),
        )(x_hbm, i_hbm)

    return kern(rows, idx.reshape(1, k * m))


def _sc_gather_rows(table, idx):
    m = idx.shape[0]
    c = table.shape[1]
    mesh = plsc.VectorSubcoreMesh(core_axis_name="core", subcore_axis_name="subcore")

    @pl.kernel(out_type=jax.ShapeDtypeStruct((m, c), table.dtype), mesh=mesh)
    def kern(x_hbm, i_hbm, o_hbm):
        def body(i_vmem, o_vmem):
            pltpu.sync_copy(x_hbm.at[i_vmem.at[0]], o_vmem)

        pltpu.emit_pipeline(
            body, grid=(m // SC_WINDOW,),
            in_specs=[pl.BlockSpec((1, SC_WINDOW), lambda i: (0, i))],
            out_specs=[pl.BlockSpec((SC_WINDOW, c), lambda i: (i, 0))],
            core_axis_name=("core", "subcore"),
            dimension_semantics=(pltpu.PARALLEL,),
        )(i_hbm, o_hbm)

    return kern(table, idx.reshape(1, m))


def _pack_partial_t(o_norm_t, lse):
    half = HEAD_DIM // 2
    u = lax.bitcast_convert_type(o_norm_t, jnp.uint32) + jnp.uint32(0x8000)
    word = (u[:half] & jnp.uint32(0xFFFF0000)) | (u[half:] >> 16)
    lse_bits = lax.bitcast_convert_type(jnp.broadcast_to(lse, word.shape), jnp.uint32)
    full = jnp.concatenate([word, lse_bits], axis=0)
    return lax.bitcast_convert_type(full, F32).T


def _unpack_partial(part):
    word = lax.bitcast_convert_type(part, jnp.uint32)
    hi = lax.bitcast_convert_type(word & jnp.uint32(0xFFFF0000), F32)
    lo = lax.bitcast_convert_type(word << 16, F32)
    lane = lax.broadcasted_iota(jnp.int32, word.shape, 1)
    o = jnp.where(lane < HEAD_DIM // 2, hi, pltpu.roll(lo, HEAD_DIM // 2, 1))
    return o, part[:, HEAD_DIM // 2:HEAD_DIM // 2 + 1]


def _tiles_kernel(tb_ref, qs_ref, k_ref, vt_ref, after_ref, o_ref, *, ntile):
    del after_ref
    h = pl.program_id(0)
    g = pl.program_id(1)
    B = MOBA_BLOCK
    base = h * ntile + g * TILES_PER_STEP

    @pl.when(tb_ref[base] >= 0)
    def _():
        blocks = [jnp.maximum(tb_ref[base + u], 0) for u in range(TILES_PER_STEP)]
        s_t = [lax.dot_general(k_ref[pl.ds(pl.multiple_of(n * B, B), B), :],
                               qs_ref[u * B:(u + 1) * B, :].astype(BF16), NT,
                               preferred_element_type=F32) for u, n in enumerate(blocks)]
        m = [jnp.max(x, axis=0, keepdims=True) for x in s_t]
        p = [jnp.exp2(x - mm).astype(BF16) for x, mm in zip(s_t, m)]
        acc = [jnp.dot(vt_ref[0, n], pp, preferred_element_type=F32) for n, pp in zip(blocks, p)]
        for u in range(TILES_PER_STEP):
            l = acc[u][HEAD_DIM:HEAD_DIM + 1]
            o_ref[u * B:(u + 1) * B, :] = _pack_partial_t(acc[u][:HEAD_DIM] / l,
                                                          m[u] + jnp.log(l) * LOG2_E)


def _tiles(tb, qsorted, proj, v_t, after, s):
    nb = s // MOBA_BLOCK
    ntile = 4 * s // MOBA_BLOCK
    steps = ntile // TILES_PER_STEP
    rows = TILES_PER_STEP * MOBA_BLOCK
    grid_spec = pltpu.PrefetchScalarGridSpec(
        num_scalar_prefetch=1,
        grid=(HEADS, steps),
        in_specs=[pl.BlockSpec((rows, HEAD_DIM), lambda h, g, tb: (h * steps + g, 0)),
                  pl.BlockSpec((s, HEAD_DIM), lambda h, g, tb: (0, HEADS + h)),
                  pl.BlockSpec((1, nb, HEAD_DIM + ONES_ROWS, MOBA_BLOCK), lambda h, g, tb: (h, 0, 0, 0)),
                  pl.BlockSpec((16, HEAD_DIM), lambda h, g, tb: (0, 0))],
        out_specs=pl.BlockSpec((rows, HEAD_DIM), lambda h, g, tb: (h * steps + g, 0)),
    )
    return pl.pallas_call(
        functools.partial(_tiles_kernel, ntile=ntile),
        grid_spec=grid_spec,
        out_shape=jax.ShapeDtypeStruct(qsorted.shape, F32),
        compiler_params=pltpu.CompilerParams(
            dimension_semantics=("parallel", "arbitrary"), vmem_limit_bytes=VMEM_LIMIT),
        name="moba_tiles",
    )(tb, qsorted, proj, v_t, after)


def _merge_kernel(q_ref, k_ref, v_ref, part_ref, o_ref):
    j = pl.program_id(0)
    B = MOBA_BLOCK
    heads = [slice(hh * HEAD_DIM, (hh + 1) * HEAD_DIM) for hh in range(HEADS)]
    causal = (lax.broadcasted_iota(jnp.int32, (B, B), 1) <= lax.broadcasted_iota(jnp.int32, (B, B), 0))
    s = [lax.dot_general(q_ref[:, hs], k_ref[:, hs], NT, preferred_element_type=F32) for hs in heads]
    s = [jnp.where(causal, x, -jnp.inf) for x in s]
    m_own = [jnp.max(x, axis=-1, keepdims=True) for x in s]
    p = [jnp.exp2(x - m) for x, m in zip(s, m_own)]
    l_own = [jnp.sum(x, axis=-1, keepdims=True) for x in p]
    o_own = [jnp.dot(x.astype(BF16), v_ref[:, hs], preferred_element_type=F32) for x, hs in zip(p, heads)]
    for hh, hs in enumerate(heads):
        lse_own = m_own[hh] + jnp.log(l_own[hh]) * LOG2_E
        parts = []
        for r in range(MOBA_TOPK):
            o_r, lse_r = _unpack_partial(part_ref[r, hh])
            valid = r < j
            parts.append((jnp.where(valid, o_r, 0.0), jnp.where(valid, lse_r, -jnp.inf)))
        m_all = lse_own
        for _, lse_r in parts:
            m_all = jnp.maximum(m_all, lse_r)
        w = jnp.exp2(lse_own - m_all)
        num = (w / l_own[hh]) * o_own[hh]
        den = w
        for o_r, lse_r in parts:
            w = jnp.exp2(lse_r - m_all)
            num = num + w * o_r
            den = den + w
        o_ref[:, hs] = (num / den).astype(BF16)


def _merge(proj, parts, s):
    nb = s // MOBA_BLOCK
    width = HEADS * HEAD_DIM
    return pl.pallas_call(
        _merge_kernel,
        grid=(nb,),
        in_specs=[pl.BlockSpec((MOBA_BLOCK, width), lambda j: (j, 0)),
                  pl.BlockSpec((MOBA_BLOCK, width), lambda j: (j, 1)),
                  pl.BlockSpec((MOBA_BLOCK, width), lambda j: (j, 2)),
                  pl.BlockSpec((MOBA_TOPK, HEADS, MOBA_BLOCK, HEAD_DIM), lambda j: (0, 0, j, 0))],
        out_specs=pl.BlockSpec((MOBA_BLOCK, width), lambda j: (j, 0)),
        out_shape=jax.ShapeDtypeStruct((s, width), BF16),
        compiler_params=pltpu.CompilerParams(
            dimension_semantics=("parallel",), vmem_limit_bytes=VMEM_LIMIT),
        name="moba_merge",
    )(proj, proj, proj, parts)


def _moba_sparse(proj, v_t, kmean, qf, run_during_scatter):
    s = proj.shape[0]
    cap = 4 * s
    n_rows = HEADS * cap + MOBA_BLOCK
    info, cnt = _route(proj, kmean)
    dest, tb = _dest(info, cnt, s)
    dest3 = dest[:MOBA_TOPK]
    qsorted = _sc_scatter_rows(qf.reshape(HEADS * s, HEAD_DIM), dest3, n_rows)
    after = run_during_scatter()
    osorted = _tiles(tb[:, 0, :].reshape(-1), qsorted, proj, v_t, after, s)
    parts = _sc_gather_rows(osorted, dest3.reshape(-1))
    return _merge(proj, parts.reshape(MOBA_TOPK, HEADS, s, HEAD_DIM), s)


def _delta_kernel(q_ref, k_ref, v_ref, qh_ref, kh_ref, vh_ref, z_ref, small_ref,
                  cwq_ref, cwk_ref, cwv_ref, alog_ref, dtb_ref, onw_ref,
                  o_ref, state_scr, xx_scr, *, hg):
    c = pl.program_id(0)
    C = DN_CHUNK

    @pl.when(c == 0)
    def _():
        state_scr[...] = jnp.zeros_like(state_scr)

    def conv_silu(idx, x_ref, halo_ref, cw_ref):
        halo = jnp.where(c == 0, 0.0, halo_ref[...].astype(F32))
        xx_scr[idx, 0:HALO, :] = halo
        xx_scr[idx, HALO:HALO + C, :] = x_ref[...].astype(F32)
        y = jnp.zeros((C, DN_HEADS_PER_STEP * HEAD_DIM), F32)
        for t in range(DN_CONV):
            off = HALO - (DN_CONV - 1) + t
            y = y + cw_ref[t:t + 1, :] * xx_scr[idx, off:off + C, :]
        return y * _sigmoid(y)

    q_all = conv_silu(0, q_ref, qh_ref, cwq_ref)
    k_all = conv_silu(1, k_ref, kh_ref, cwk_ref)
    v_all = conv_silu(2, v_ref, vh_ref, cwv_ref)

    small = small_ref[...]
    lane = lax.broadcasted_iota(jnp.int32, (C, 128), 1)
    beta_all = _sigmoid(small)
    xs = small + dtb_ref[...]
    softplus = jnp.maximum(xs, 0.0) + jnp.log(1.0 + jnp.exp(-jnp.abs(xs)))
    g_all = -jnp.exp(alog_ref[...]) * softplus

    row = lax.broadcasted_iota(jnp.int32, (C, C), 0)
    col = lax.broadcasted_iota(jnp.int32, (C, C), 1)
    tril = row >= col
    strict = row > col
    rxc = row ^ col
    eye = (row == col).astype(F32)
    tril_f = tril.astype(F32)

    heads = range(DN_HEADS_PER_STEP)
    sls = [slice(hh * HEAD_DIM, (hh + 1) * HEAD_DIM) for hh in heads]

    def hmap(f, *lists):
        return [f(*xs) for xs in zip(*lists)]

    def pick(table, lane_idx):
        return jnp.sum(jnp.where(lane == lane_idx, table, 0.0), axis=-1, keepdims=True)

    q = [q_all[:, sl] for sl in sls]
    k = [k_all[:, sl] for sl in sls]
    v = [v_all[:, sl] for sl in sls]
    q = hmap(lambda x: x * lax.rsqrt(jnp.sum(x * x, axis=-1, keepdims=True) + EPS) * (HEAD_DIM ** -0.5), q)
    k = hmap(lambda x: x * lax.rsqrt(jnp.sum(x * x, axis=-1, keepdims=True) + EPS), k)
    beta = [pick(beta_all, hg * DN_HEADS_PER_STEP + hh) for hh in heads]
    g = [pick(g_all, hg * DN_HEADS_PER_STEP + hh + HEADS) for hh in heads]

    gcum_b = hmap(lambda x: jnp.dot(tril_f, jnp.broadcast_to(x, (C, 128)),
                                    precision=lax.Precision.HIGHEST, preferred_element_type=F32), g)
    gc = [x[:, 0:1] for x in gcum_b]
    gr = [x.T[0:1, :] for x in gcum_b]
    decay = hmap(lambda c_, r_: jnp.where(tril, jnp.exp(jnp.where(tril, c_ - r_, 0.0)), 0.0), gc, gr)
    e_g = hmap(jnp.exp, gc)
    g_last = [x[C - 1:C, :] for x in gc]

    kb = hmap(lambda a_, b_: a_ * b_, k, beta)
    vb = hmap(lambda a_, b_: a_ * b_, v, beta)
    a = hmap(lambda kb_, q_, k_: lax.dot_general(
        jnp.concatenate([kb_, q_], axis=0).astype(BF16), k_.astype(BF16), NT,
        preferred_element_type=F32), kb, q, k)
    lmat = hmap(lambda a_, d_: jnp.where(strict, a_[:C] * d_, 0.0), a, decay)
    attn = hmap(lambda a_, d_: a_[C:] * d_, a, decay)

    d1 = hmap(lambda l_: jnp.where(rxc < 8, l_, 0.0), lmat)
    d2 = hmap(lambda x: _bdot(x, x), d1)
    d4 = hmap(lambda x: _bdot(x, x), d2)
    p1 = hmap(lambda x, y: _bdot(eye - x, eye + y), d1, d2)
    tmat = hmap(lambda x, y: _bdot(x, eye + y), p1, d4)

    def odd_rows(t, sz):
        return jnp.concatenate([t[b * 2 * sz + sz:(b + 1) * 2 * sz] for b in range(C // (2 * sz))], axis=0)

    def with_odd_rows(t, odd, sz):
        pieces = []
        for b in range(C // (2 * sz)):
            pieces += [t[b * 2 * sz:b * 2 * sz + sz], odd[b * sz:(b + 1) * sz]]
        return jnp.concatenate(pieces, axis=0)

    sz = 8
    while sz < C:
        off = hmap(lambda l_: jnp.where((rxc >= sz) & (rxc < 2 * sz), l_, 0.0), lmat)
        t_odd = hmap(lambda t_: odd_rows(t_, sz), tmat)
        x = hmap(_bdot, t_odd, off)
        x = hmap(_bdot, x, tmat)
        tmat = hmap(lambda t_, o_, x_: with_odd_rows(t_, o_ - x_, sz), tmat, t_odd, x)
        sz *= 2

    uw = hmap(lambda t_, vb_, kb_, e_: _bdot(t_, jnp.concatenate([vb_, kb_ * e_], axis=1)),
              tmat, vb, kb, e_g)
    state = [state_scr[hh] for hh in heads]
    wq = hmap(lambda uw_, q_, e_, s_: _bdot(jnp.concatenate([uw_[:, HEAD_DIM:], q_ * e_], axis=0), s_),
              uw, q, e_g, state)
    v_new = hmap(lambda uw_, wq_: uw_[:, :HEAD_DIM] - wq_[:C], uw, wq)
    av = hmap(_bdot, attn, v_new)
    kt = hmap(lambda k_, gl_, gc_: (k_ * jnp.exp(gl_ - gc_)).T, k, g_last, gc)
    ds = hmap(_bdot, kt, v_new)
    for hh in heads:
        state_scr[hh] = state[hh] * jnp.exp(g_last[hh]) + ds[hh]
        o = wq[hh][C:] + av[hh]
        y = o * lax.rsqrt(jnp.mean(o * o, axis=-1, keepdims=True) + EPS) * onw_ref[...]
        z = z_ref[:, sls[hh]].astype(F32)
        o_ref[:, sls[hh]] = (y * (z * _sigmoid(z))).astype(BF16)


def _gated_deltanet(proj, small, conv_w, alog_v, dtb_v, onw, hg):
    s = proj.shape[0]
    nc = s // DN_CHUNK
    rb = DN_CHUNK // HALO
    hp = DN_HEADS_PER_STEP
    ng = HEADS // hp
    width = hp * HEAD_DIM

    def col(part):
        return pl.BlockSpec((DN_CHUNK, width), lambda c: (c, part * ng + hg))

    def halo(part):
        return pl.BlockSpec((HALO, width), lambda c: (jnp.maximum(c * rb - 1, 0), part * ng + hg))

    def cw(part):
        return pl.BlockSpec((DN_CONV, width), lambda c: (0, part * ng + hg))

    vec = pl.BlockSpec((1, 128), lambda c: (0, 0))
    return pl.pallas_call(
        functools.partial(_delta_kernel, hg=hg),
        grid=(nc,),
        in_specs=[col(3), col(4), col(5),
                  halo(3), halo(4), halo(5),
                  col(6),
                  pl.BlockSpec((DN_CHUNK, 128), lambda c: (c, 0)),
                  cw(0), cw(1), cw(2),
                  vec, vec, vec],
        out_specs=pl.BlockSpec((DN_CHUNK, width), lambda c: (c, 0)),
        out_shape=jax.ShapeDtypeStruct((s, width), BF16),
        scratch_shapes=[pltpu.VMEM((hp, HEAD_DIM, HEAD_DIM), F32),
                        pltpu.VMEM((3, HALO + DN_CHUNK, width), F32)],
        compiler_params=pltpu.CompilerParams(
            dimension_semantics=("arbitrary",), vmem_limit_bytes=VMEM_LIMIT),
        name="deltanet",
    )(proj, proj, proj, proj, proj, proj, proj, small, conv_w, conv_w, conv_w, alog_v, dtb_v, onw)


def _mix_out_kernel(ya_ref, *rest):
    ng = HEADS // DN_HEADS_PER_STEP
    yd_refs = rest[:ng]
    ga_ref, gd_ref, x_ref, wa_ref, wd_ref, wo_ref, nw_ref, o_ref = rest[ng:]
    width = DN_HEADS_PER_STEP * HEAD_DIM
    pa = jnp.dot(ya_ref[...], wa_ref[...], preferred_element_type=F32)
    pd = jnp.dot(yd_refs[0][...], wd_ref[0:width, :], preferred_element_type=F32)
    for g in range(1, ng):
        pd = pd + jnp.dot(yd_refs[g][...], wd_ref[g * width:(g + 1) * width, :],
                          preferred_element_type=F32)
    merged = _sigmoid(ga_ref[...].astype(F32)) * pa + _sigmoid(gd_ref[...].astype(F32)) * pd
    mo = jnp.dot(merged.astype(BF16), wo_ref[...], preferred_element_type=F32)
    y = mo * lax.rsqrt(jnp.mean(mo * mo, axis=-1, keepdims=True) + EPS) * nw_ref[...]
    o_ref[...] = x_ref[...] + y


def _mix_out(ya, yds, proj, x2, wa, wd, wo, nw, *, tm=512):
    s = x2.shape[0]
    row = lambda i: (i, 0)
    full = lambda i: (0, 0)
    wspec = pl.BlockSpec((D_MODEL, D_MODEL), full)
    return pl.pallas_call(
        _mix_out_kernel,
        grid=(s // tm,),
        in_specs=[pl.BlockSpec((tm, D_MODEL), row)]
                 + [pl.BlockSpec((tm, yd.shape[1]), row) for yd in yds]
                 + [pl.BlockSpec((tm, D_MODEL), lambda i: (i, 7)),
                  pl.BlockSpec((tm, D_MODEL), lambda i: (i, 8)),
                  pl.BlockSpec((tm, D_MODEL), row),
                  wspec, wspec, wspec, pl.BlockSpec((1, D_MODEL), full)],
        out_specs=pl.BlockSpec((tm, D_MODEL), row),
        out_shape=jax.ShapeDtypeStruct((s, D_MODEL), F32),
        compiler_params=pltpu.CompilerParams(
            dimension_semantics=("parallel",), vmem_limit_bytes=VMEM_LIMIT),
        name="mix_out",
    )(ya, *yds, proj, proj, x2, wa, wd, wo, nw)


def _ffn_kernel(x_ref, npre_ref, wg_ref, wu_ref, wd_ref, npost_ref, o_ref):
    x = x_ref[...]
    h = (x * lax.rsqrt(jnp.mean(x * x, axis=-1, keepdims=True) + EPS) * npre_ref[...]).astype(BF16)
    acc = jnp.zeros(x.shape, F32)
    for cc in range(D_FF // FF_CHUNK):
        sl = slice(cc * FF_CHUNK, (cc + 1) * FF_CHUNK)
        g = jnp.dot(h, wg_ref[:, sl], preferred_element_type=F32)
        u = jnp.dot(h, wu_ref[:, sl], preferred_element_type=F32)
        act = (g * _sigmoid(g) * u).astype(BF16)
        acc = acc + jnp.dot(act, wd_ref[sl, :], preferred_element_type=F32)
    y = acc * lax.rsqrt(jnp.mean(acc * acc, axis=-1, keepdims=True) + EPS) * npost_ref[...]
    o_ref[...] = x + y


def _ffn(x1, npre, wg, wu, wd, npost, *, tm=512):
    s = x1.shape[0]
    row = lambda i: (i, 0)
    full = lambda i: (0, 0)
    once = pl.Buffered(1)
    return pl.pallas_call(
        _ffn_kernel,
        grid=(s // tm,),
        in_specs=[pl.BlockSpec((tm, D_MODEL), row), pl.BlockSpec((1, D_MODEL), full),
                  pl.BlockSpec((D_MODEL, D_FF), full, pipeline_mode=once),
                  pl.BlockSpec((D_MODEL, D_FF), full, pipeline_mode=once),
                  pl.BlockSpec((D_FF, D_MODEL), full, pipeline_mode=once),
                  pl.BlockSpec((1, D_MODEL), full)],
        out_specs=pl.BlockSpec((tm, D_MODEL), row),
        out_shape=jax.ShapeDtypeStruct((s, D_MODEL), F32),
        compiler_params=pltpu.CompilerParams(
            dimension_semantics=("parallel",), vmem_limit_bytes=VMEM_LIMIT),
        name="ffn",
    )(x1, npre, wg, wu, wd, npost)


def _rope_table(s):
    half = ROPE_DIM // 2
    rep = 128 // half
    inv = ROPE_THETA ** (-jnp.arange(half, dtype=F32) * 2.0 / ROPE_DIM)
    pos = (jnp.arange(s // rep)[:, None] * rep + jnp.arange(128)[None, :] // half).astype(F32)
    ang = pos * jnp.tile(inv, rep)[None, :]
    cos, sin = lax.optimization_barrier((jnp.cos(ang), jnp.sin(ang)))
    cos = cos.reshape(s, half)
    sin = sin.reshape(s, half)
    return jnp.concatenate([cos, sin, jnp.zeros((s, HEAD_DIM - ROPE_DIM), F32)], axis=-1)


def _layer(x2, l, norm_mix_pre, w_in, conv_w, a_log, dt_bias, o_norm_w, w_o_attn, w_o_delta,
           w_out, norm_mix_post, norm_ffn_pre, w_gate, w_up, w_down, norm_ffn_post):
    s = x2.shape[0]
    nb = s // MOBA_BLOCK
    wide = 7 * D_MODEL
    nsmall = 2 * HEADS
    w = w_in[l]
    w_a = w[:, :wide].astype(BF16)
    w_b = w[:, wide + nsmall:].astype(BF16)
    w_small = jnp.pad(w[:, wide:wide + nsmall], ((0, 0), (0, 128 - nsmall))).astype(BF16)

    proj, small, kmean, v_t, qf = _project(x2, norm_mix_pre[l][None, :], w_a, w_b, w_small,
                                           _rope_table(s))
    pad = lambda vec: jnp.pad(vec.astype(F32), (HEADS, 128 - 2 * HEADS))[None, :]
    delta = lambda hg: _gated_deltanet(proj, small, conv_w[l], pad(a_log[l]), pad(dt_bias[l]),
                                       o_norm_w[l][None, :], hg)
    yds = []
    ya = _moba_sparse(proj, v_t, kmean.reshape(nb, D_MODEL), qf,
                      lambda: yds.append(delta(0)) or yds[0])
    yds += [delta(hg) for hg in range(1, HEADS // DN_HEADS_PER_STEP)]

    x1 = _mix_out(ya, yds, proj, x2, w_o_attn[l].astype(BF16), w_o_delta[l].astype(BF16),
                  w_out[l].astype(BF16), norm_mix_post[l][None, :])
    return _ffn(x1, norm_ffn_pre[l][None, :], w_gate[l].astype(BF16), w_up[l].astype(BF16),
                w_down[l].astype(BF16), norm_ffn_post[l][None, :])


def kernel(x, norm_mix_pre, w_in, conv_w, a_log, dt_bias, o_norm_w, w_o_attn, w_o_delta, w_out,
           norm_mix_post, norm_ffn_pre, w_gate, w_up, w_down, norm_ffn_post):
    b, s, d = x.shape
    assert d == D_MODEL and s % 512 == 0
    outs = []
    for bi in range(b):
        x2 = x.reshape(s, d) if b == 1 else x[bi]
        for l in range(w_in.shape[0]):
            x2 = _layer(x2, l, norm_mix_pre, w_in, conv_w, a_log, dt_bias, o_norm_w, w_o_attn,
                        w_o_delta, w_out, norm_mix_post, norm_ffn_pre, w_gate, w_up, w_down,
                        norm_ffn_post)
        outs.append(x2)
    return outs[0].reshape(1, s, d) if b == 1 else jnp.stack(outs, axis=0)
```

```python
import functools
import math

import jax
import jax.numpy as jnp
from jax import lax
from jax.experimental import pallas as pl
from jax.experimental.pallas import tpu as pltpu
from jax.experimental.pallas import tpu_sc as plsc

D_MODEL = 1024
HEADS = 8
HEAD_DIM = 128
MOBA_BLOCK = 256
MOBA_TOPK = 3
ROPE_DIM = HEAD_DIM // 4
ROPE_THETA = 500000.0
DN_CONV = 4
ONES_ROWS = 16
DN_HEADS_PER_STEP = 4
DN_CHUNK = 256
D_FF = 2816
FF_CHUNK = 256
EPS = 1e-6
LOG2_E = math.log2(math.e)
ATTN_UNROLL = 4
ATTN_GROUPS = 2
HALO = 16

F32 = jnp.float32
BF16 = jnp.bfloat16
NT = (((1,), (1,)), ((), ()))

VMEM_LIMIT = 56 * 1024 * 1024


def _bdot(a, b):
    return jnp.dot(a.astype(BF16), b.astype(BF16), preferred_element_type=F32)


def _sigmoid(x):
    return 1.0 / (1.0 + jnp.exp(-x))


def _proj_kernel(x_ref, nw_ref, wa_ref, wb_ref, ws_ref, rope_ref,
                 out_ref, small_ref, kmean_ref, vt_ref, qf_ref, *, tm, q_scale):
    x = x_ref[...]
    ms = jnp.mean(x * x, axis=-1, keepdims=True)
    h = (x * lax.rsqrt(ms + EPS) * nw_ref[...]).astype(BF16)
    small_ref[...] = jnp.dot(h, ws_ref[...], preferred_element_type=F32)
    heads = [slice(hh * HEAD_DIM, (hh + 1) * HEAD_DIM) for hh in range(HEADS)]
    groups = [slice(g * MOBA_BLOCK, (g + 1) * MOBA_BLOCK) for g in range(tm // MOBA_BLOCK)]

    half = ROPE_DIM // 2
    tab = rope_ref[...]
    lane = lax.broadcasted_iota(jnp.int32, tab.shape, 1)
    cos_t = jnp.where(lane < half, tab, jnp.where(lane < ROPE_DIM, pltpu.roll(tab, half, 1), 1.0))
    s1_t = jnp.where(lane < half, -pltpu.roll(tab, HEAD_DIM - half, 1), 0.0)
    s2_t = jnp.where((lane >= half) & (lane < ROPE_DIM), tab, 0.0)

    def roped(a):
        return (a * cos_t + pltpu.roll(a, HEAD_DIM - half, 1) * s1_t + pltpu.roll(a, half, 1) * s2_t)

    na = wa_ref.shape[1] // D_MODEL
    for c in range(out_ref.shape[1] // D_MODEL):
        cols = slice(c * D_MODEL, (c + 1) * D_MODEL)
        w_c = wa_ref[:, cols] if c < na else wb_ref[:, (c - na) * D_MODEL:(c - na + 1) * D_MODEL]
        acc = jnp.dot(h, w_c, preferred_element_type=F32)
        if c == 0:
            for hh, hs in enumerate(heads):
                r = roped(acc[:, hs]) * q_scale
                out_ref[:, hs] = r.astype(BF16)
                qf_ref[hh] = r
        elif c == 1:
            for hs in heads:
                r = roped(acc[:, hs])
                out_ref[:, D_MODEL + hs.start:D_MODEL + hs.stop] = r.astype(BF16)
                for g, gs in enumerate(groups):
                    kmean_ref[0, g:g + 1, hs] = jnp.sum(r[gs], axis=0, keepdims=True) * (1.0 / MOBA_BLOCK)
        elif c == 2:
            out_ref[:, cols] = acc.astype(BF16)
            for hh, hs in enumerate(heads):
                for g, gs in enumerate(groups):
                    vt_ref[hh, g, 0:HEAD_DIM, :] = acc[gs, hs].T.astype(BF16)
                    vt_ref[hh, g, HEAD_DIM:, :] = jnp.ones((ONES_ROWS, MOBA_BLOCK), BF16)
        else:
            out_ref[:, cols] = acc.astype(BF16)


def _project(x2, norm_w, w_a, w_b, w_small, rope_t, *, tm=512):
    s = x2.shape[0]
    width = w_a.shape[1] + w_b.shape[1]
    once = pl.Buffered(1)
    nblk = tm // MOBA_BLOCK
    kern = functools.partial(_proj_kernel, tm=tm, q_scale=math.log2(math.e) / math.sqrt(HEAD_DIM))
    row = lambda i: (i, 0)
    full = lambda i: (0, 0)
    return pl.pallas_call(
        kern,
        grid=(s // tm,),
        in_specs=[
            pl.BlockSpec((tm, D_MODEL), row),
            pl.BlockSpec((1, D_MODEL), full),
            pl.BlockSpec((D_MODEL, w_a.shape[1]), full, pipeline_mode=once),
            pl.BlockSpec((D_MODEL, w_b.shape[1]), full, pipeline_mode=once),
            pl.BlockSpec((D_MODEL, 128), full),
            pl.BlockSpec((tm, HEAD_DIM), row),
        ],
        out_specs=[
            pl.BlockSpec((tm, width), row),
            pl.BlockSpec((tm, 128), row),
            pl.BlockSpec((1, nblk, D_MODEL), lambda i: (i, 0, 0)),
            pl.BlockSpec((HEADS, nblk, HEAD_DIM + ONES_ROWS, MOBA_BLOCK), lambda i: (0, i, 0, 0)),
            pl.BlockSpec((HEADS, tm, HEAD_DIM), lambda i: (0, i, 0)),
        ],
        out_shape=[
            jax.ShapeDtypeStruct((s, width), BF16),
            jax.ShapeDtypeStruct((s, 128), F32),
            jax.ShapeDtypeStruct((s // tm, nblk, D_MODEL), F32),
            jax.ShapeDtypeStruct((HEADS, s // MOBA_BLOCK, HEAD_DIM + ONES_ROWS, MOBA_BLOCK), BF16),
            jax.ShapeDtypeStruct((HEADS, s, HEAD_DIM), F32),
        ],
        compiler_params=pltpu.CompilerParams(
            dimension_semantics=("parallel",), vmem_limit_bytes=VMEM_LIMIT),
        name="proj",
    )(x2, norm_w, w_a, w_b, w_small, rope_t)


def _attn_kernel(q_ref, k_ref, vt_ref, km_ref, o_ref, bias_scr, sa_scr, pb_scr, *, nb):
    j = pl.program_id(1)
    q = q_ref[...]
    U = ATTN_UNROLL
    B = MOBA_BLOCK
    blk = lax.broadcasted_iota(jnp.int32, (nb, B), 0)
    blk_f = blk.astype(F32)

    def scores(g):
        n0 = jnp.minimum(g, nb // U - 1) * U
        kn = k_ref[pl.ds(pl.multiple_of(n0 * B, U * B), U * B), :]
        return lax.dot_general(kn, q, NT, preferred_element_type=F32)

    km = km_ref[...]
    km_hi = km.astype(BF16)
    km_lo = (km - km_hi.astype(F32)).astype(BF16)
    gate2 = lax.dot_general(jnp.concatenate([km_hi, km_lo], axis=0), q, NT,
                            preferred_element_type=F32)
    kj = k_ref[pl.ds(pl.multiple_of(j * B, B), B), :]
    s = lax.dot_general(kj, q, NT, preferred_element_type=F32)
    sa_scr[...] = scores(0)
    pb_scr[...] = jnp.zeros_like(pb_scr)

    gate = jnp.where(blk < j, gate2[:nb] + gate2[nb:], -jnp.inf)
    bias = jnp.full((nb, B), -jnp.inf, F32)
    for _ in range(MOBA_TOPK):
        m = jnp.max(gate, axis=0, keepdims=True)
        first = jnp.min(jnp.where(gate == m, blk_f, float(nb)), axis=0, keepdims=True)
        pick = (blk_f == first) & (m > -jnp.inf)
        bias = jnp.where(pick, 0.0, bias)
        gate = jnp.where(pick, -jnp.inf, gate)
    bias_scr[...] = bias

    krow = lax.broadcasted_iota(jnp.int32, (B, B), 0)
    qcol = lax.broadcasted_iota(jnp.int32, (B, B), 1)
    s = jnp.where(krow <= qcol, s, -jnp.inf)
    m0 = jnp.max(s, axis=0, keepdims=True)
    p = jnp.exp2(s - m0)
    acc0 = jnp.dot(vt_ref[0, j], p.astype(BF16), preferred_element_type=F32)


    def values(g, p_of):
        n0 = jnp.maximum(g, 0) * U
        pv = jnp.dot(vt_ref[0, n0], p_of(0), preferred_element_type=F32)
        for u in range(1, U):
            pv = pv + jnp.dot(vt_ref[0, n0 + u], p_of(u), preferred_element_type=F32)
        return pv

    def softmax(g, s_of, m):
        sn = [s_of(u) + bias_scr[pl.ds(g * U + u, 1), :] for u in range(U)]
        m_new = m
        for u in range(U):
            m_new = jnp.maximum(m_new, jnp.max(sn[u], axis=0, keepdims=True))
        alpha = jnp.exp2(m - m_new)
        return m_new, alpha, [jnp.exp2(sn[u] - m_new).astype(BF16) for u in range(U)]

    G = ATTN_GROUPS

    def body(t, carry):
        m, acc, alpha_last = carry
        acc = alpha_last * acc + values(G * t - 1, lambda u: pb_scr[u * B:(u + 1) * B, :])
        s_of = lambda u: sa_scr[u * B:(u + 1) * B, :]
        for i in range(G):
            if i + 1 < G:
                s_next = scores(G * t + i + 1)
            m, alpha, p_i = softmax(G * t + i, s_of, m)
            if i + 1 < G:
                acc = alpha * acc + values(G * t + i, lambda u, p_i=p_i: p_i[u])
                s_of = lambda u, s_next=s_next: s_next[u * B:(u + 1) * B]
        for u in range(U):
            pb_scr[u * B:(u + 1) * B, :] = p_i[u]
        sa_scr[...] = scores(G * t + G)
        return m, acc, alpha

    trips = (j + G * U - 1) // (G * U)
    _, acc, alpha_last = lax.fori_loop(0, trips, body, (m0, acc0, jnp.ones_like(m0)))
    acc = alpha_last * acc + values(G * trips - 1, lambda u: pb_scr[u * B:(u + 1) * B, :])
    o_ref[...] = (acc[:HEAD_DIM] / acc[HEAD_DIM:HEAD_DIM + 1]).T.astype(BF16)


def _moba_attention(proj, v_t, kmean):
    s = proj.shape[0]
    nb = s // MOBA_BLOCK
    return pl.pallas_call(
        functools.partial(_attn_kernel, nb=nb),
        grid=(HEADS, nb),
        in_specs=[
            pl.BlockSpec((MOBA_BLOCK, HEAD_DIM), lambda h, j: (j, h)),
            pl.BlockSpec((s, HEAD_DIM), lambda h, j: (0, HEADS + h)),
            pl.BlockSpec((1, nb, HEAD_DIM + ONES_ROWS, MOBA_BLOCK),
                         lambda h, j: (h, 0, 0, 0)),
            pl.BlockSpec((nb, HEAD_DIM), lambda h, j: (0, h)),
        ],
        out_specs=pl.BlockSpec((MOBA_BLOCK, HEAD_DIM), lambda h, j: (j, h)),
        out_shape=jax.ShapeDtypeStruct((s, HEADS * HEAD_DIM), BF16),
        scratch_shapes=[pltpu.VMEM((nb, MOBA_BLOCK), F32),
                        pltpu.VMEM((ATTN_UNROLL * MOBA_BLOCK, MOBA_BLOCK), F32),
                        pltpu.VMEM((ATTN_UNROLL * MOBA_BLOCK, MOBA_BLOCK), BF16)],
        compiler_params=pltpu.CompilerParams(
            dimension_semantics=("parallel", "arbitrary"), vmem_limit_bytes=VMEM_LIMIT),
        name="moba",
    )(proj, proj, v_t, kmean)


ROUTE_QBLOCKS = 8
TILES_PER_STEP = 16
SC_WINDOW = 128


def _top_blocks(gate, blk_f, nb):
    picks = []
    for _ in range(MOBA_TOPK):
        m = jnp.max(gate, axis=0, keepdims=True)
        first = jnp.min(jnp.where(gate == m, blk_f, float(nb)), axis=0, keepdims=True)
        pick = (blk_f == first) & (m > -jnp.inf)
        gate = jnp.where(pick, -jnp.inf, gate)
        picks.append(pick)
    return picks


def _route_kernel(q_ref, km_ref, info_ref, cnt_ref, run_scr, *, nb):
    jb = pl.program_id(1)
    L = ROUTE_QBLOCKS * MOBA_BLOCK

    @pl.when(jb == 0)
    def _():
        run_scr[...] = jnp.zeros_like(run_scr)

    km = km_ref[...]
    km_hi = km.astype(BF16)
    km_lo = (km - km_hi.astype(F32)).astype(BF16)
    gate2 = lax.dot_general(jnp.concatenate([km_hi, km_lo], axis=0), q_ref[...], NT,
                            preferred_element_type=F32)
    blk = lax.broadcasted_iota(jnp.int32, (nb, L), 0)
    blk_f = blk.astype(F32)
    qblk = jb * ROUTE_QBLOCKS + (lax.broadcasted_iota(jnp.int32, (nb, L), 1) >> 8)
    gate = jnp.where(blk < qblk, gate2[:nb] + gate2[nb:], -jnp.inf)
    picks = _top_blocks(gate, blk_f, nb)

    chosen = jnp.where(picks[0] | picks[1] | picks[2], 1.0, 0.0)
    B = MOBA_BLOCK
    before = jnp.where(lax.broadcasted_iota(jnp.int32, (B, B), 0)
                       < lax.broadcasted_iota(jnp.int32, (B, B), 1), 1.0, 0.0).astype(BF16)
    carry = run_scr[:, 0:1]
    base = []
    for b in range(ROUTE_QBLOCKS):
        c_b = chosen[:, b * B:(b + 1) * B]
        within = jnp.dot(c_b.astype(BF16), before, preferred_element_type=F32)
        base.append(carry + within)
        carry = carry + within[:, B - 1:B] + c_b[:, B - 1:B]
    base = jnp.concatenate(base, axis=1)

    rows = []
    for pick in picks:
        rows.append(jnp.sum(jnp.where(pick, base, 0.0), axis=0, keepdims=True))
    for pick in picks:
        bid = jnp.sum(jnp.where(pick, blk_f, 0.0), axis=0, keepdims=True)
        valid = jnp.sum(jnp.where(pick, 1.0, 0.0), axis=0, keepdims=True)
        rows.append(jnp.where(valid > 0.0, bid, float(nb)))
    rows += [jnp.zeros((1, L), F32)] * (8 - len(rows))
    info_ref[...] = jnp.concatenate(rows, axis=0).astype(jnp.int32)

    run = jnp.broadcast_to(carry, run_scr.shape)
    run_scr[...] = run
    cnt_ref[0] = run


def _route(proj, kmean):
    s = proj.shape[0]
    nb = s // MOBA_BLOCK
    L = ROUTE_QBLOCKS * MOBA_BLOCK
    steps = s // L
    return pl.pallas_call(
        functools.partial(_route_kernel, nb=nb),
        grid=(HEADS, steps),
        in_specs=[pl.BlockSpec((L, HEAD_DIM), lambda h, j: (j, h)),
                  pl.BlockSpec((nb, HEAD_DIM), lambda h, j: (0, h))],
        out_specs=[pl.BlockSpec((8, L), lambda h, j: (0, h * steps + j)),
                   pl.BlockSpec((1, nb, 128), lambda h, j: (h, 0, 0))],
        out_shape=[jax.ShapeDtypeStruct((8, HEADS * s), jnp.int32),
                   jax.ShapeDtypeStruct((HEADS, nb, 128), F32)],
        scratch_shapes=[pltpu.VMEM((nb, 128), F32)],
        compiler_params=pltpu.CompilerParams(
            dimension_semantics=("parallel", "arbitrary"), vmem_limit_bytes=VMEM_LIMIT),
        name="moba_route",
    )(proj, kmean)


def _dest_kernel(info_ref, cnt_ref, dest_ref, tb_ref, *, nb, s):
    h = pl.program_id(0)
    cap = 4 * s
    ntile = cap // MOBA_BLOCK
    cnt = cnt_ref[0]
    tiles = jnp.floor((cnt + float(MOBA_BLOCK - 1)) * (1.0 / MOBA_BLOCK))
    lower = jnp.where(lax.broadcasted_iota(jnp.int32, (nb, nb), 1)
                      < lax.broadcasted_iota(jnp.int32, (nb, nb), 0), 1.0, 0.0).astype(BF16)
    start = jnp.dot(lower, tiles.astype(BF16), preferred_element_type=F32)
    start1 = start[:, 0:1]
    tiles1 = tiles[:, 0:1]

    t_f = lax.broadcasted_iota(jnp.int32, (nb, ntile), 1).astype(F32)
    n_f = lax.broadcasted_iota(jnp.int32, (nb, ntile), 0).astype(F32)
    inside = (t_f >= start1) & (t_f < start1 + tiles1)
    tb = jnp.sum(jnp.where(inside, n_f, 0.0), axis=0, keepdims=True)
    used = jnp.sum(jnp.where(inside, 1.0, 0.0), axis=0, keepdims=True)
    tb = jnp.where(used > 0.0, tb, -1.0)
    tb_ref[0] = jnp.concatenate([tb, jnp.full((7, ntile), -1.0, F32)], axis=0).astype(jnp.int32)

    off1 = start1 * float(MOBA_BLOCK)
    CH = 2048
    blk_f = lax.broadcasted_iota(jnp.int32, (nb, CH), 0).astype(F32)
    lane = lax.broadcasted_iota(jnp.int32, (1, CH), 1)
    trash = HEADS * cap + (lane & (MOBA_BLOCK - 1))
    for ch in range(s // CH):
        sl = slice(ch * CH, (ch + 1) * CH)
        rows = []
        for r in range(MOBA_TOPK):
            rank = info_ref[r:r + 1, sl]
            bid = info_ref[MOBA_TOPK + r:MOBA_TOPK + r + 1, sl]
            off = jnp.sum(jnp.where(blk_f == bid.astype(F32), off1, 0.0), axis=0, keepdims=True)
            rows.append(jnp.where(bid < nb, h * cap + off.astype(jnp.int32) + rank, trash))
        rows += [jnp.zeros((1, CH), jnp.int32)] * (8 - len(rows))
        dest_ref[:, sl] = jnp.concatenate(rows, axis=0)


def _dest(info, cnt, s):
    nb = s // MOBA_BLOCK
    ntile = 4 * s // MOBA_BLOCK
    return pl.pallas_call(
        functools.partial(_dest_kernel, nb=nb, s=s),
        grid=(HEADS,),
        in_specs=[pl.BlockSpec((8, s), lambda h: (0, h)),
                  pl.BlockSpec((1, nb, 128), lambda h: (h, 0, 0))],
        out_specs=[pl.BlockSpec((8, s), lambda h: (0, h)),
                   pl.BlockSpec((1, 8, ntile), lambda h: (h, 0, 0))],
        out_shape=[jax.ShapeDtypeStruct((8, HEADS * s), jnp.int32),
                   jax.ShapeDtypeStruct((HEADS, 8, ntile), jnp.int32)],
        compiler_params=pltpu.CompilerParams(
            dimension_semantics=("parallel",), vmem_limit_bytes=VMEM_LIMIT),
        name="moba_dest",
    )(info, cnt)


def _sc_scatter_rows(rows, idx, n_out):
    m, c = rows.shape
    k = idx.shape[0]
    nwin = m // SC_WINDOW
    mesh = plsc.VectorSubcoreMesh(core_axis_name="core", subcore_axis_name="subcore")

    @pl.kernel(out_type=jax.ShapeDtypeStruct((n_out, c), rows.dtype), mesh=mesh)
    def kern(x_hbm, i_hbm, o_hbm):
        def body(x_vmem, i_vmem):
            pltpu.sync_copy(x_vmem, o_hbm.at[i_vmem.at[0]])

        pltpu.emit_pipeline(
            body, grid=(k * nwin,),
            in_specs=[pl.BlockSpec((SC_WINDOW, c), lambda i: (i % nwin, 0)),
                      pl.BlockSpec((1, SC_WINDOW), lambda i: (0, i))],
            out_specs=[],
            core_axis_name=("core", "subcore"),
            dimension_semantics=(pltpu.PARALLEL,),
        )(x_hbm, i_hbm)

    return kern(rows, idx.reshape(1, k * m))


def _sc_gather_rows(table, idx):
    m = idx.shape[0]
    c = table.shape[1]
    mesh = plsc.VectorSubcoreMesh(core_axis_name="core", subcore_axis_name="subcore")

    @pl.kernel(out_type=jax.ShapeDtypeStruct((m, c), table.dtype), mesh=mesh)
    def kern(x_hbm, i_hbm, o_hbm):
        def body(i_vmem, o_vmem):
            pltpu.sync_copy(x_hbm.at[i_vmem.at[0]], o_vmem)

        pltpu.emit_pipeline(
            body, grid=(m // SC_WINDOW,),
            in_specs=[pl.BlockSpec((1, SC_WINDOW), lambda i: (0, i))],
            out_specs=[pl.BlockSpec((SC_WINDOW, c), lambda i: (i, 0))],
            core_axis_name=("core", "subcore"),
            dimension_semantics=(pltpu.PARALLEL,),
        )(i_hbm, o_hbm)

    return kern(table, idx.reshape(1, m))


def _pack_partial_t(o_norm_t, lse):
    half = HEAD_DIM // 2
    u = lax.bitcast_convert_type(o_norm_t, jnp.uint32) + jnp.uint32(0x8000)
    word = (u[:half] & jnp.uint32(0xFFFF0000)) | (u[half:] >> 16)
    lse_bits = lax.bitcast_convert_type(jnp.broadcast_to(lse, word.shape), jnp.uint32)
    full = jnp.concatenate([word, lse_bits], axis=0)
    return lax.bitcast_convert_type(full, F32).T


def _unpack_partial(part):
    word = lax.bitcast_convert_type(part, jnp.uint32)
    hi = lax.bitcast_convert_type(word & jnp.uint32(0xFFFF0000), F32)
    lo = lax.bitcast_convert_type(word << 16, F32)
    lane = lax.broadcasted_iota(jnp.int32, word.shape, 1)
    o = jnp.where(lane < HEAD_DIM // 2, hi, pltpu.roll(lo, HEAD_DIM // 2, 1))
    return o, part[:, HEAD_DIM // 2:HEAD_DIM // 2 + 1]


def _tiles_kernel(tb_ref, qs_ref, k_ref, vt_ref, after_ref, o_ref, *, ntile):
    del after_ref
    h = pl.program_id(0)
    g = pl.program_id(1)
    B = MOBA_BLOCK
    base = h * ntile + g * TILES_PER_STEP

    @pl.when(tb_ref[base] >= 0)
    def _():
        blocks = [jnp.maximum(tb_ref[base + u], 0) for u in range(TILES_PER_STEP)]
        s_t = [lax.dot_general(k_ref[pl.ds(pl.multiple_of(n * B, B), B), :],
                               qs_ref[u * B:(u + 1) * B, :].astype(BF16), NT,
                               preferred_element_type=F32) for u, n in enumerate(blocks)]
        m = [jnp.max(x, axis=0, keepdims=True) for x in s_t]
        p = [jnp.exp2(x - mm).astype(BF16) for x, mm in zip(s_t, m)]
        acc = [jnp.dot(vt_ref[0, n], pp, preferred_element_type=F32) for n, pp in zip(blocks, p)]
        for u in range(TILES_PER_STEP):
            l = acc[u][HEAD_DIM:HEAD_DIM + 1]
            o_ref[u * B:(u + 1) * B, :] = _pack_partial_t(acc[u][:HEAD_DIM] / l,
                                                          m[u] + jnp.log(l) * LOG2_E)


def _tiles(tb, qsorted, proj, v_t, after, s):
    nb = s // MOBA_BLOCK
    ntile = 4 * s // MOBA_BLOCK
    steps = ntile // TILES_PER_STEP
    rows = TILES_PER_STEP * MOBA_BLOCK
    grid_spec = pltpu.PrefetchScalarGridSpec(
        num_scalar_prefetch=1,
        grid=(HEADS, steps),
        in_specs=[pl.BlockSpec((rows, HEAD_DIM), lambda h, g, tb: (h * steps + g, 0)),
                  pl.BlockSpec((s, HEAD_DIM), lambda h, g, tb: (0, HEADS + h)),
                  pl.BlockSpec((1, nb, HEAD_DIM + ONES_ROWS, MOBA_BLOCK), lambda h, g, tb: (h, 0, 0, 0)),
                  pl.BlockSpec((16, HEAD_DIM), lambda h, g, tb: (0, 0))],
        out_specs=pl.BlockSpec((rows, HEAD_DIM), lambda h, g, tb: (h * steps + g, 0)),
    )
    return pl.pallas_call(
        functools.partial(_tiles_kernel, ntile=ntile),
        grid_spec=grid_spec,
        out_shape=jax.ShapeDtypeStruct(qsorted.shape, F32),
        compiler_params=pltpu.CompilerParams(
            dimension_semantics=("parallel", "arbitrary"), vmem_limit_bytes=VMEM_LIMIT),
        name="moba_tiles",
    )(tb, qsorted, proj, v_t, after)


def _merge_kernel(q_ref, k_ref, v_ref, part_ref, o_ref):
    j = pl.program_id(0)
    B = MOBA_BLOCK
    heads = [slice(hh * HEAD_DIM, (hh + 1) * HEAD_DIM) for hh in range(HEADS)]
    causal = (lax.broadcasted_iota(jnp.int32, (B, B), 1) <= lax.broadcasted_iota(jnp.int32, (B, B), 0))
    s = [lax.dot_general(q_ref[:, hs], k_ref[:, hs], NT, preferred_element_type=F32) for hs in heads]
    s = [jnp.where(causal, x, -jnp.inf) for x in s]
    m_own = [jnp.max(x, axis=-1, keepdims=True) for x in s]
    p = [jnp.exp2(x - m) for x, m in zip(s, m_own)]
    l_own = [jnp.sum(x, axis=-1, keepdims=True) for x in p]
    o_own = [jnp.dot(x.astype(BF16), v_ref[:, hs], preferred_element_type=F32) for x, hs in zip(p, heads)]
    for hh, hs in enumerate(heads):
        lse_own = m_own[hh] + jnp.log(l_own[hh]) * LOG2_E
        parts = []
        for r in range(MOBA_TOPK):
            o_r, lse_r = _unpack_partial(part_ref[r, hh])
            valid = r < j
            parts.append((jnp.where(valid, o_r, 0.0), jnp.where(valid, lse_r, -jnp.inf)))
        m_all = lse_own
        for _, lse_r in parts:
            m_all = jnp.maximum(m_all, lse_r)
        w = jnp.exp2(lse_own - m_all)
        num = (w / l_own[hh]) * o_own[hh]
        den = w
        for o_r, lse_r in parts:
            w = jnp.exp2(lse_r - m_all)
            num = num + w * o_r
            den = den + w
        o_ref[:, hs] = (num / den).astype(BF16)


def _merge(proj, parts, s):
    nb = s // MOBA_BLOCK
    width = HEADS * HEAD_DIM
    return pl.pallas_call(
        _merge_kernel,
        grid=(nb,),
        in_specs=[pl.BlockSpec((MOBA_BLOCK, width), lambda j: (j, 0)),
                  pl.BlockSpec((MOBA_BLOCK, width), lambda j: (j, 1)),
                  pl.BlockSpec((MOBA_BLOCK, width), lambda j: (j, 2)),
                  pl.BlockSpec((MOBA_TOPK, HEADS, MOBA_BLOCK, HEAD_DIM), lambda j: (0, 0, j, 0))],
        out_specs=pl.BlockSpec((MOBA_BLOCK, width), lambda j: (j, 0)),
        out_shape=jax.ShapeDtypeStruct((s, width), BF16),
        compiler_params=pltpu.CompilerParams(
            dimension_semantics=("parallel",), vmem_limit_bytes=VMEM_LIMIT),
        name="moba_merge",
    )(proj, proj, proj, parts)


def _moba_sparse(proj, v_t, kmean, qf, run_during_scatter):
    s = proj.shape[0]
    cap = 4 * s
    n_rows = HEADS * cap + MOBA_BLOCK
    info, cnt = _route(proj, kmean)
    dest, tb = _dest(info, cnt, s)
    dest3 = dest[:MOBA_TOPK]
    qsorted = _sc_scatter_rows(qf.reshape(HEADS * s, HEAD_DIM), dest3, n_rows)
    after = run_during_scatter()
    osorted = _tiles(tb[:, 0, :].reshape(-1), qsorted, proj, v_t, after, s)
    parts = _sc_gather_rows(osorted, dest3.reshape(-1))
    return _merge(proj, parts.reshape(MOBA_TOPK, HEADS, s, HEAD_DIM), s)


def _delta_kernel(q_ref, k_ref, v_ref, qh_ref, kh_ref, vh_ref, z_ref, small_ref,
                  cwq_ref, cwk_ref, cwv_ref, alog_ref, dtb_ref, onw_ref,
                  o_ref, state_scr, xx_scr, *, hg):
    c = pl.program_id(0)
    C = DN_CHUNK

    @pl.when(c == 0)
    def _():
        state_scr[...] = jnp.zeros_like(state_scr)

    def conv_silu(idx, x_ref, halo_ref, cw_ref):
        halo = jnp.where(c == 0, 0.0, halo_ref[...].astype(F32))
        xx_scr[idx, 0:HALO, :] = halo
        xx_scr[idx, HALO:HALO + C, :] = x_ref[...].astype(F32)
        y = jnp.zeros((C, DN_HEADS_PER_STEP * HEAD_DIM), F32)
        for t in range(DN_CONV):
            off = HALO - (DN_CONV - 1) + t
            y = y + cw_ref[t:t + 1, :] * xx_scr[idx, off:off + C, :]
        return y * _sigmoid(y)

    q_all = conv_silu(0, q_ref, qh_ref, cwq_ref)
    k_all = conv_silu(1, k_ref, kh_ref, cwk_ref)
    v_all = conv_silu(2, v_ref, vh_ref, cwv_ref)

    small = small_ref[...]
    lane = lax.broadcasted_iota(jnp.int32, (C, 128), 1)
    beta_all = _sigmoid(small)
    xs = small + dtb_ref[...]
    softplus = jnp.maximum(xs, 0.0) + jnp.log(1.0 + jnp.exp(-jnp.abs(xs)))
    g_all = -jnp.exp(alog_ref[...]) * softplus

    row = lax.broadcasted_iota(jnp.int32, (C, C), 0)
    col = lax.broadcasted_iota(jnp.int32, (C, C), 1)
    tril = row >= col
    strict = row > col
    rxc = row ^ col
    eye = (row == col).astype(F32)
    assert C == 2 * HEAD_DIM

    heads = range(DN_HEADS_PER_STEP)
    sls = [slice(hh * HEAD_DIM, (hh + 1) * HEAD_DIM) for hh in heads]

    def hmap(f, *lists):
        return [f(*xs) for xs in zip(*lists)]

    def pieces(x, n):
        out, rest = [], x
        for _ in range(n):
            p = rest.astype(BF16)
            out.append(p)
            rest = rest - p.astype(F32)
        return jnp.concatenate(out, axis=1)

    hp = DN_HEADS_PER_STEP
    ones = jnp.ones((HEAD_DIM, HEAD_DIM), BF16)

    def unit_rows(x):
        ss = jnp.dot((x * x).astype(BF16), ones, preferred_element_type=F32)
        return x * lax.rsqrt(ss + EPS)

    def selector(first_lane, n):
        src = lax.broadcasted_iota(jnp.int32, (128, hp * HEAD_DIM), 0)
        dst = lax.broadcasted_iota(jnp.int32, (128, hp * HEAD_DIM), 1) >> 7
        sel = jnp.where(src == first_lane + dst, 1.0, 0.0).astype(BF16)
        return jnp.concatenate([sel] * n, axis=0)

    q = [unit_rows(q_all[:, sl]) * (HEAD_DIM ** -0.5) for sl in sls]
    k = [unit_rows(k_all[:, sl]) for sl in sls]
    v = [v_all[:, sl] for sl in sls]
    beta_rep = jnp.dot(pieces(beta_all, 1), selector(hg * hp, 1), preferred_element_type=F32)
    beta = [beta_rep[:, sl] for sl in sls]

    cum3 = jnp.dot(jnp.where(tril, 1.0, 0.0).astype(BF16), pieces(g_all, 3), preferred_element_type=F32)
    gcum_all = cum3[:, :128] + cum3[:, 128:256] + cum3[:, 256:]
    gcum_rep = jnp.dot(pieces(gcum_all, 3), selector(hg * hp + HEADS, 3), preferred_element_type=F32)
    gcum_b = [gcum_rep[:, sl] for sl in sls]
    gr = [x.T[0:1, :] for x in gcum_b]
    decay = hmap(lambda c_, r_: jnp.where(
        tril, jnp.exp(jnp.where(tril, jnp.concatenate([c_, c_], axis=1) - r_, 0.0)), 0.0), gcum_b, gr)
    e_g = hmap(jnp.exp, gcum_b)
    g_last = [x[C - 1:C, :] for x in gcum_b]

    kb = hmap(lambda a_, b_: a_ * b_, k, beta)
    vb = hmap(lambda a_, b_: a_ * b_, v, beta)
    a = hmap(lambda kb_, q_, k_: lax.dot_general(
        jnp.concatenate([kb_, q_], axis=0).astype(BF16), k_.astype(BF16), NT,
        preferred_element_type=F32), kb, q, k)
    lmat = hmap(lambda a_, d_: jnp.where(strict, a_[:C] * d_, 0.0), a, decay)
    attn = hmap(lambda a_, d_: a_[C:] * d_, a, decay)

    d1 = hmap(lambda l_: jnp.where(rxc < 8, l_, 0.0), lmat)
    d2 = hmap(lambda x: _bdot(x, x), d1)
    d4 = hmap(lambda x: _bdot(x, x), d2)
    p1 = hmap(lambda x, y: _bdot(eye - x, eye + y), d1, d2)
    tmat = hmap(lambda x, y: _bdot(x, eye + y), p1, d4)

    def odd_rows(t, sz):
        return jnp.concatenate([t[b * 2 * sz + sz:(b + 1) * 2 * sz] for b in range(C // (2 * sz))], axis=0)

    def with_odd_rows(t, odd, sz):
        pieces = []
        for b in range(C // (2 * sz)):
            pieces += [t[b * 2 * sz:b * 2 * sz + sz], odd[b * sz:(b + 1) * sz]]
        return jnp.concatenate(pieces, axis=0)

    sz = 8
    while sz < C:
        off = hmap(lambda l_: jnp.where((rxc >= sz) & (rxc < 2 * sz), l_, 0.0), lmat)
        t_odd = hmap(lambda t_: odd_rows(t_, sz), tmat)
        x = hmap(_bdot, t_odd, off)
        x = hmap(_bdot, x, tmat)
        tmat = hmap(lambda t_, o_, x_: with_odd_rows(t_, o_ - x_, sz), tmat, t_odd, x)
        sz *= 2

    uw = hmap(lambda t_, vb_, kb_, e_: _bdot(t_, jnp.concatenate([vb_, kb_ * e_], axis=1)),
              tmat, vb, kb, e_g)
    state = [state_scr[hh] for hh in heads]
    wq = hmap(lambda uw_, q_, e_, s_: _bdot(jnp.concatenate([uw_[:, HEAD_DIM:], q_ * e_], axis=0), s_),
              uw, q, e_g, state)
    v_new = hmap(lambda uw_, wq_: uw_[:, :HEAD_DIM] - wq_[:C], uw, wq)
    av = hmap(_bdot, attn, v_new)
    kt = hmap(lambda k_, gl_, gc_: (k_ * jnp.exp(gl_ - gc_)).T, k, g_last, gcum_b)
    ds = hmap(_bdot, kt, v_new)
    for hh in heads:
        state_scr[hh] = state[hh] * jnp.exp(g_last[hh]) + ds[hh]
        o = wq[hh][C:] + av[hh]
        y = o * lax.rsqrt(jnp.mean(o * o, axis=-1, keepdims=True) + EPS) * onw_ref[...]
        z = z_ref[:, sls[hh]].astype(F32)
        o_ref[:, sls[hh]] = (y * (z * _sigmoid(z))).astype(BF16)


def _gated_deltanet(proj, small, conv_w, alog_v, dtb_v, onw, hg):
    s = proj.shape[0]
    nc = s // DN_CHUNK
    rb = DN_CHUNK // HALO
    hp = DN_HEADS_PER_STEP
    ng = HEADS // hp
    width = hp * HEAD_DIM

    def col(part):
        return pl.BlockSpec((DN_CHUNK, width), lambda c: (c, part * ng + hg))

    def halo(part):
        return pl.BlockSpec((HALO, width), lambda c: (jnp.maximum(c * rb - 1, 0), part * ng + hg))

    def cw(part):
        return pl.BlockSpec((DN_CONV, width), lambda c: (0, part * ng + hg))

    vec = pl.BlockSpec((1, 128), lambda c: (0, 0))
    return pl.pallas_call(
        functools.partial(_delta_kernel, hg=hg),
        grid=(nc,),
        in_specs=[col(3), col(4), col(5),
                  halo(3), halo(4), halo(5),
                  col(6),
                  pl.BlockSpec((DN_CHUNK, 128), lambda c: (c, 0)),
                  cw(0), cw(1), cw(2),
                  vec, vec, vec],
        out_specs=pl.BlockSpec((DN_CHUNK, width), lambda c: (c, 0)),
        out_shape=jax.ShapeDtypeStruct((s, width), BF16),
        scratch_shapes=[pltpu.VMEM((hp, HEAD_DIM, HEAD_DIM), F32),
                        pltpu.VMEM((3, HALO + DN_CHUNK, width), F32)],
        compiler_params=pltpu.CompilerParams(
            dimension_semantics=("arbitrary",), vmem_limit_bytes=VMEM_LIMIT),
        name="deltanet",
    )(proj, proj, proj, proj, proj, proj, proj, small, conv_w, conv_w, conv_w, alog_v, dtb_v, onw)


def _mix_out_kernel(ya_ref, *rest):
    ng = HEADS // DN_HEADS_PER_STEP
    yd_refs = rest[:ng]
    ga_ref, gd_ref, x_ref, wa_ref, wd_ref, wo_ref, nw_ref, o_ref = rest[ng:]
    width = DN_HEADS_PER_STEP * HEAD_DIM
    pa = jnp.dot(ya_ref[...], wa_ref[...], preferred_element_type=F32)
    pd = jnp.dot(yd_refs[0][...], wd_ref[0:width, :], preferred_element_type=F32)
    for g in range(1, ng):
        pd = pd + jnp.dot(yd_refs[g][...], wd_ref[g * width:(g + 1) * width, :],
                          preferred_element_type=F32)
    merged = _sigmoid(ga_ref[...].astype(F32)) * pa + _sigmoid(gd_ref[...].astype(F32)) * pd
    mo = jnp.dot(merged.astype(BF16), wo_ref[...], preferred_element_type=F32)
    y = mo * lax.rsqrt(jnp.mean(mo * mo, axis=-1, keepdims=True) + EPS) * nw_ref[...]
    o_ref[...] = x_ref[...] + y


def _mix_out(ya, yds, proj, x2, wa, wd, wo, nw, *, tm=512):
    s = x2.shape[0]
    row = lambda i: (i, 0)
    full = lambda i: (0, 0)
    wspec = pl.BlockSpec((D_MODEL, D_MODEL), full)
    return pl.pallas_call(
        _mix_out_kernel,
        grid=(s // tm,),
        in_specs=[pl.BlockSpec((tm, D_MODEL), row)]
                 + [pl.BlockSpec((tm, yd.shape[1]), row) for yd in yds]
                 + [pl.BlockSpec((tm, D_MODEL), lambda i: (i, 7)),
                  pl.BlockSpec((tm, D_MODEL), lambda i: (i, 8)),
                  pl.BlockSpec((tm, D_MODEL), row),
                  wspec, wspec, wspec, pl.BlockSpec((1, D_MODEL), full)],
        out_specs=pl.BlockSpec((tm, D_MODEL), row),
        out_shape=jax.ShapeDtypeStruct((s, D_MODEL), F32),
        compiler_params=pltpu.CompilerParams(
            dimension_semantics=("parallel",), vmem_limit_bytes=VMEM_LIMIT),
        name="mix_out",
    )(ya, *yds, proj, proj, x2, wa, wd, wo, nw)


def _ffn_kernel(x_ref, npre_ref, wg_ref, wu_ref, wd_ref, npost_ref, o_ref):
    x = x_ref[...]
    h = (x * lax.rsqrt(jnp.mean(x * x, axis=-1, keepdims=True) + EPS) * npre_ref[...]).astype(BF16)
    acc = jnp.zeros(x.shape, F32)
    for cc in range(D_FF // FF_CHUNK):
        sl = slice(cc * FF_CHUNK, (cc + 1) * FF_CHUNK)
        g = jnp.dot(h, wg_ref[:, sl], preferred_element_type=F32)
        u = jnp.dot(h, wu_ref[:, sl], preferred_element_type=F32)
        act = (g * _sigmoid(g) * u).astype(BF16)
        acc = acc + jnp.dot(act, wd_ref[sl, :], preferred_element_type=F32)
    y = acc * lax.rsqrt(jnp.mean(acc * acc, axis=-1, keepdims=True) + EPS) * npost_ref[...]
    o_ref[...] = x + y


def _ffn(x1, npre, wg, wu, wd, npost, *, tm=512):
    s = x1.shape[0]
    row = lambda i: (i, 0)
    full = lambda i: (0, 0)
    once = pl.Buffered(1)
    return pl.pallas_call(
        _ffn_kernel,
        grid=(s // tm,),
        in_specs=[pl.BlockSpec((tm, D_MODEL), row), pl.BlockSpec((1, D_MODEL), full),
                  pl.BlockSpec((D_MODEL, D_FF), full, pipeline_mode=once),
                  pl.BlockSpec((D_MODEL, D_FF), full, pipeline_mode=once),
                  pl.BlockSpec((D_FF, D_MODEL), full, pipeline_mode=once),
                  pl.BlockSpec((1, D_MODEL), full)],
        out_specs=pl.BlockSpec((tm, D_MODEL), row),
        out_shape=jax.ShapeDtypeStruct((s, D_MODEL), F32),
        compiler_params=pltpu.CompilerParams(
            dimension_semantics=("parallel",), vmem_limit_bytes=VMEM_LIMIT),
        name="ffn",
    )(x1, npre, wg, wu, wd, npost)


def _rope_table(s):
    half = ROPE_DIM // 2
    rep = 128 // half
    inv = ROPE_THETA ** (-jnp.arange(half, dtype=F32) * 2.0 / ROPE_DIM)
    pos = (jnp.arange(s // rep)[:, None] * rep + jnp.arange(128)[None, :] // half).astype(F32)
    ang = pos * jnp.tile(inv, rep)[None, :]
    cos, sin = lax.optimization_barrier((jnp.cos(ang), jnp.sin(ang)))
    cos = cos.reshape(s, half)
    sin = sin.reshape(s, half)
    return jnp.concatenate([cos, sin, jnp.zeros((s, HEAD_DIM - ROPE_DIM), F32)], axis=-1)


def _layer(x2, l, norm_mix_pre, w_in, conv_w, a_log, dt_bias, o_norm_w, w_o_attn, w_o_delta,
           w_out, norm_mix_post, norm_ffn_pre, w_gate, w_up, w_down, norm_ffn_post):
    s = x2.shape[0]
    nb = s // MOBA_BLOCK
    wide = 7 * D_MODEL
    nsmall = 2 * HEADS
    w = w_in[l]
    w_a = w[:, :wide].astype(BF16)
    w_b = w[:, wide + nsmall:].astype(BF16)
    w_small = jnp.pad(w[:, wide:wide + nsmall], ((0, 0), (0, 128 - nsmall))).astype(BF16)

    proj, small, kmean, v_t, qf = _project(x2, norm_mix_pre[l][None, :], w_a, w_b, w_small,
                                           _rope_table(s))
    pad = lambda vec: jnp.pad(vec.astype(F32), (HEADS, 128 - 2 * HEADS))[None, :]
    delta = lambda hg: _gated_deltanet(proj, small, conv_w[l], pad(a_log[l]), pad(dt_bias[l]),
                                       o_norm_w[l][None, :], hg)
    yds = []
    ya = _moba_sparse(proj, v_t, kmean.reshape(nb, D_MODEL), qf,
                      lambda: yds.append(delta(0)) or yds[0])
    yds += [delta(hg) for hg in range(1, HEADS // DN_HEADS_PER_STEP)]

    x1 = _mix_out(ya, yds, proj, x2, w_o_attn[l].astype(BF16), w_o_delta[l].astype(BF16),
                  w_out[l].astype(BF16), norm_mix_post[l][None, :])
    return _ffn(x1, norm_ffn_pre[l][None, :], w_gate[l].astype(BF16), w_up[l].astype(BF16),
                w_down[l].astype(BF16), norm_ffn_post[l][None, :])


def kernel(x, norm_mix_pre, w_in, conv_w, a_log, dt_bias, o_norm_w, w_o_attn, w_o_delta, w_out,
           norm_mix_post, norm_ffn_pre, w_gate, w_up, w_down, norm_ffn_post):
    b, s, d = x.shape
    assert d == D_MODEL and s % 512 == 0
    outs = []
    for bi in range(b):
        x2 = x.reshape(s, d) if b == 1 else x[bi]
        for l in range(w_in.shape[0]):
            x2 = _layer(x2, l, norm_mix_pre, w_in, conv_w, a_log, dt_bias, o_norm_w, w_o_attn,
                        w_o_delta, w_out, norm_mix_post, norm_ffn_pre, w_gate, w_up, w_down,
                        norm_ffn_post)
        outs.append(x2)
    return outs[0].reshape(1, s, d) if b == 1 else jnp.stack(outs, axis=0)
```

```python
import functools
import math

import jax
import jax.numpy as jnp
from jax import lax
from jax.experimental import pallas as pl
from jax.experimental.pallas import tpu as pltpu
from jax.experimental.pallas import tpu_sc as plsc

D_MODEL = 1024
HEADS = 8
HEAD_DIM = 128
MOBA_BLOCK = 256
MOBA_TOPK = 3
ROPE_DIM = HEAD_DIM // 4
ROPE_THETA = 500000.0
DN_CONV = 4
ONES_ROWS = 16
DN_HEADS_PER_STEP = 4
DN_CHUNK = 256
D_FF = 2816
FF_CHUNK = 256
EPS = 1e-6
LOG2_E = math.log2(math.e)
ATTN_UNROLL = 4
ATTN_GROUPS = 2
HALO = 16

F32 = jnp.float32
BF16 = jnp.bfloat16
NT = (((1,), (1,)), ((), ()))

VMEM_LIMIT = 56 * 1024 * 1024


def _bdot(a, b):
    return jnp.dot(a.astype(BF16), b.astype(BF16), preferred_element_type=F32)


def _sigmoid(x):
    return 1.0 / (1.0 + jnp.exp(-x))


def _proj_kernel(x_ref, nw_ref, wa_ref, wb_ref, ws_ref, rope_ref,
                 out_ref, small_ref, kmean_ref, vt_ref, qf_ref, *, tm, q_scale):
    x = x_ref[...]
    ms = jnp.mean(x * x, axis=-1, keepdims=True)
    h = (x * lax.rsqrt(ms + EPS) * nw_ref[...]).astype(BF16)
    small_ref[...] = jnp.dot(h, ws_ref[...], preferred_element_type=F32)
    heads = [slice(hh * HEAD_DIM, (hh + 1) * HEAD_DIM) for hh in range(HEADS)]
    groups = [slice(g * MOBA_BLOCK, (g + 1) * MOBA_BLOCK) for g in range(tm // MOBA_BLOCK)]

    half = ROPE_DIM // 2
    tab = rope_ref[...]
    lane = lax.broadcasted_iota(jnp.int32, tab.shape, 1)
    cos_t = jnp.where(lane < half, tab, jnp.where(lane < ROPE_DIM, pltpu.roll(tab, half, 1), 1.0))
    s1_t = jnp.where(lane < half, -pltpu.roll(tab, HEAD_DIM - half, 1), 0.0)
    s2_t = jnp.where((lane >= half) & (lane < ROPE_DIM), tab, 0.0)

    def roped(a):
        return (a * cos_t + pltpu.roll(a, HEAD_DIM - half, 1) * s1_t + pltpu.roll(a, half, 1) * s2_t)

    na = wa_ref.shape[1] // D_MODEL
    for c in range(out_ref.shape[1] // D_MODEL):
        cols = slice(c * D_MODEL, (c + 1) * D_MODEL)
        w_c = wa_ref[:, cols] if c < na else wb_ref[:, (c - na) * D_MODEL:(c - na + 1) * D_MODEL]
        acc = jnp.dot(h, w_c, preferred_element_type=F32)
        if c == 0:
            for hh, hs in enumerate(heads):
                r = roped(acc[:, hs]) * q_scale
                out_ref[:, hs] = r.astype(BF16)
                qf_ref[hh] = r
        elif c == 1:
            for hs in heads:
                r = roped(acc[:, hs])
                out_ref[:, D_MODEL + hs.start:D_MODEL + hs.stop] = r.astype(BF16)
                for g, gs in enumerate(groups):
                    kmean_ref[0, g:g + 1, hs] = jnp.sum(r[gs], axis=0, keepdims=True) * (1.0 / MOBA_BLOCK)
        elif c == 2:
            out_ref[:, cols] = acc.astype(BF16)
            for hh, hs in enumerate(heads):
                for g, gs in enumerate(groups):
                    vt_ref[hh, g, 0:HEAD_DIM, :] = acc[gs, hs].T.astype(BF16)
                    vt_ref[hh, g, HEAD_DIM:, :] = jnp.ones((ONES_ROWS, MOBA_BLOCK), BF16)
        else:
            out_ref[:, cols] = acc.astype(BF16)


def _wprep_kernel(a_ref, e_ref, wa_ref, wb_ref, ws_ref, *, n_a, n_small):
    c = pl.program_id(0)

    @pl.when(c < n_a)
    def _():
        wa_ref[...] = a_ref[...].astype(BF16)

    @pl.when(c >= n_a)
    def _():
        a = a_ref[...]
        shifted = pltpu.roll(a, D_MODEL - n_small, 1)
        tail = pltpu.roll(e_ref[...], 128 - n_small, 1)
        lane = lax.broadcasted_iota(jnp.int32, tail.shape, 1)
        last = jnp.where(lane >= 128 - n_small, tail, shifted[:, D_MODEL - 128:])
        wb_ref[...] = jnp.concatenate([shifted[:, :D_MODEL - 128], last], axis=1).astype(BF16)

    @pl.when(c == n_a)
    def _():
        first = a_ref[:, 0:128]
        lane = lax.broadcasted_iota(jnp.int32, first.shape, 1)
        ws_ref[...] = jnp.where(lane < n_small, first, 0.0).astype(BF16)


def _wprep(w_in, l, n_a, n_b, n_small):
    d = w_in.shape[1]
    return pl.pallas_call(
        functools.partial(_wprep_kernel, n_a=n_a, n_small=n_small),
        grid=(n_a + n_b,),
        in_specs=[pl.BlockSpec((None, d, D_MODEL), lambda c: (l, 0, c)),
                  pl.BlockSpec((None, d, 128), lambda c: (l, 0, (c + 1) * (D_MODEL // 128)))],
        out_specs=[pl.BlockSpec((d, D_MODEL), lambda c: (0, jnp.minimum(c, n_a - 1))),
                   pl.BlockSpec((d, D_MODEL), lambda c: (0, jnp.maximum(c - n_a, 0))),
                   pl.BlockSpec((d, 128), lambda c: (0, 0))],
        out_shape=[jax.ShapeDtypeStruct((d, n_a * D_MODEL), BF16),
                   jax.ShapeDtypeStruct((d, n_b * D_MODEL), BF16),
                   jax.ShapeDtypeStruct((d, 128), BF16)],
        compiler_params=pltpu.CompilerParams(
            dimension_semantics=("arbitrary",), vmem_limit_bytes=VMEM_LIMIT),
        name="wprep",
    )(w_in, w_in)


def _project(x2, norm_w, w_a, w_b, w_small, rope_t, *, tm=512):
    s = x2.shape[0]
    width = w_a.shape[1] + w_b.shape[1]
    once = pl.Buffered(1)
    nblk = tm // MOBA_BLOCK
    kern = functools.partial(_proj_kernel, tm=tm, q_scale=math.log2(math.e) / math.sqrt(HEAD_DIM))
    row = lambda i: (i, 0)
    full = lambda i: (0, 0)
    return pl.pallas_call(
        kern,
        grid=(s // tm,),
        in_specs=[
            pl.BlockSpec((tm, D_MODEL), row),
            pl.BlockSpec((1, D_MODEL), full),
            pl.BlockSpec((D_MODEL, w_a.shape[1]), full, pipeline_mode=once),
            pl.BlockSpec((D_MODEL, w_b.shape[1]), full, pipeline_mode=once),
            pl.BlockSpec((D_MODEL, 128), full),
            pl.BlockSpec((tm, HEAD_DIM), row),
        ],
        out_specs=[
            pl.BlockSpec((tm, width), row),
            pl.BlockSpec((tm, 128), row),
            pl.BlockSpec((1, nblk, D_MODEL), lambda i: (i, 0, 0)),
            pl.BlockSpec((HEADS, nblk, HEAD_DIM + ONES_ROWS, MOBA_BLOCK), lambda i: (0, i, 0, 0)),
            pl.BlockSpec((HEADS, tm, HEAD_DIM), lambda i: (0, i, 0)),
        ],
        out_shape=[
            jax.ShapeDtypeStruct((s, width), BF16),
            jax.ShapeDtypeStruct((s, 128), F32),
            jax.ShapeDtypeStruct((s // tm, nblk, D_MODEL), F32),
            jax.ShapeDtypeStruct((HEADS, s // MOBA_BLOCK, HEAD_DIM + ONES_ROWS, MOBA_BLOCK), BF16),
            jax.ShapeDtypeStruct((HEADS, s, HEAD_DIM), F32),
        ],
        compiler_params=pltpu.CompilerParams(
            dimension_semantics=("parallel",), vmem_limit_bytes=VMEM_LIMIT),
        name="proj",
    )(x2, norm_w, w_a, w_b, w_small, rope_t)


def _attn_kernel(q_ref, k_ref, vt_ref, km_ref, o_ref, bias_scr, sa_scr, pb_scr, *, nb):
    j = pl.program_id(1)
    q = q_ref[...]
    U = ATTN_UNROLL
    B = MOBA_BLOCK
    blk = lax.broadcasted_iota(jnp.int32, (nb, B), 0)
    blk_f = blk.astype(F32)

    def scores(g):
        n0 = jnp.minimum(g, nb // U - 1) * U
        kn = k_ref[pl.ds(pl.multiple_of(n0 * B, U * B), U * B), :]
        return lax.dot_general(kn, q, NT, preferred_element_type=F32)

    km = km_ref[...]
    km_hi = km.astype(BF16)
    km_lo = (km - km_hi.astype(F32)).astype(BF16)
    gate2 = lax.dot_general(jnp.concatenate([km_hi, km_lo], axis=0), q, NT,
                            preferred_element_type=F32)
    kj = k_ref[pl.ds(pl.multiple_of(j * B, B), B), :]
    s = lax.dot_general(kj, q, NT, preferred_element_type=F32)
    sa_scr[...] = scores(0)
    pb_scr[...] = jnp.zeros_like(pb_scr)

    gate = jnp.where(blk < j, gate2[:nb] + gate2[nb:], -jnp.inf)
    bias = jnp.full((nb, B), -jnp.inf, F32)
    for _ in range(MOBA_TOPK):
        m = jnp.max(gate, axis=0, keepdims=True)
        first = jnp.min(jnp.where(gate == m, blk_f, float(nb)), axis=0, keepdims=True)
        pick = (blk_f == first) & (m > -jnp.inf)
        bias = jnp.where(pick, 0.0, bias)
        gate = jnp.where(pick, -jnp.inf, gate)
    bias_scr[...] = bias

    krow = lax.broadcasted_iota(jnp.int32, (B, B), 0)
    qcol = lax.broadcasted_iota(jnp.int32, (B, B), 1)
    s = jnp.where(krow <= qcol, s, -jnp.inf)
    m0 = jnp.max(s, axis=0, keepdims=True)
    p = jnp.exp2(s - m0)
    acc0 = jnp.dot(vt_ref[0, j], p.astype(BF16), preferred_element_type=F32)


    def values(g, p_of):
        n0 = jnp.maximum(g, 0) * U
        pv = jnp.dot(vt_ref[0, n0], p_of(0), preferred_element_type=F32)
        for u in range(1, U):
            pv = pv + jnp.dot(vt_ref[0, n0 + u], p_of(u), preferred_element_type=F32)
        return pv

    def softmax(g, s_of, m):
        sn = [s_of(u) + bias_scr[pl.ds(g * U + u, 1), :] for u in range(U)]
        m_new = m
        for u in range(U):
            m_new = jnp.maximum(m_new, jnp.max(sn[u], axis=0, keepdims=True))
        alpha = jnp.exp2(m - m_new)
        return m_new, alpha, [jnp.exp2(sn[u] - m_new).astype(BF16) for u in range(U)]

    G = ATTN_GROUPS

    def body(t, carry):
        m, acc, alpha_last = carry
        acc = alpha_last * acc + values(G * t - 1, lambda u: pb_scr[u * B:(u + 1) * B, :])
        s_of = lambda u: sa_scr[u * B:(u + 1) * B, :]
        for i in range(G):
            if i + 1 < G:
                s_next = scores(G * t + i + 1)
            m, alpha, p_i = softmax(G * t + i, s_of, m)
            if i + 1 < G:
                acc = alpha * acc + values(G * t + i, lambda u, p_i=p_i: p_i[u])
                s_of = lambda u, s_next=s_next: s_next[u * B:(u + 1) * B]
        for u in range(U):
            pb_scr[u * B:(u + 1) * B, :] = p_i[u]
        sa_scr[...] = scores(G * t + G)
        return m, acc, alpha

    trips = (j + G * U - 1) // (G * U)
    _, acc, alpha_last = lax.fori_loop(0, trips, body, (m0, acc0, jnp.ones_like(m0)))
    acc = alpha_last * acc + values(G * trips - 1, lambda u: pb_scr[u * B:(u + 1) * B, :])
    o_ref[...] = (acc[:HEAD_DIM] / acc[HEAD_DIM:HEAD_DIM + 1]).T.astype(BF16)


def _moba_attention(proj, v_t, kmean):
    s = proj.shape[0]
    nb = s // MOBA_BLOCK
    return pl.pallas_call(
        functools.partial(_attn_kernel, nb=nb),
        grid=(HEADS, nb),
        in_specs=[
            pl.BlockSpec((MOBA_BLOCK, HEAD_DIM), lambda h, j: (j, h)),
            pl.BlockSpec((s, HEAD_DIM), lambda h, j: (0, HEADS + h)),
            pl.BlockSpec((1, nb, HEAD_DIM + ONES_ROWS, MOBA_BLOCK),
                         lambda h, j: (h, 0, 0, 0)),
            pl.BlockSpec((nb, HEAD_DIM), lambda h, j: (0, h)),
        ],
        out_specs=pl.BlockSpec((MOBA_BLOCK, HEAD_DIM), lambda h, j: (j, h)),
        out_shape=jax.ShapeDtypeStruct((s, HEADS * HEAD_DIM), BF16),
        scratch_shapes=[pltpu.VMEM((nb, MOBA_BLOCK), F32),
                        pltpu.VMEM((ATTN_UNROLL * MOBA_BLOCK, MOBA_BLOCK), F32),
                        pltpu.VMEM((ATTN_UNROLL * MOBA_BLOCK, MOBA_BLOCK), BF16)],
        compiler_params=pltpu.CompilerParams(
            dimension_semantics=("parallel", "arbitrary"), vmem_limit_bytes=VMEM_LIMIT),
        name="moba",
    )(proj, proj, v_t, kmean)


ROUTE_QBLOCKS = 8
TILES_PER_STEP = 16
SC_WINDOW = 128


def _top_blocks(gate, blk_f, nb):
    picks = []
    for _ in range(MOBA_TOPK):
        m = jnp.max(gate, axis=0, keepdims=True)
        first = jnp.min(jnp.where(gate == m, blk_f, float(nb)), axis=0, keepdims=True)
        pick = (blk_f == first) & (m > -jnp.inf)
        gate = jnp.where(pick, -jnp.inf, gate)
        picks.append(pick)
    return picks


def _route_kernel(q_ref, km_ref, info_ref, cnt_ref, run_scr, *, nb):
    jb = pl.program_id(1)
    L = ROUTE_QBLOCKS * MOBA_BLOCK

    @pl.when(jb == 0)
    def _():
        run_scr[...] = jnp.zeros_like(run_scr)

    km = km_ref[...]
    km_hi = km.astype(BF16)
    km_lo = (km - km_hi.astype(F32)).astype(BF16)
    gate2 = lax.dot_general(jnp.concatenate([km_hi, km_lo], axis=0), q_ref[...], NT,
                            preferred_element_type=F32)
    blk = lax.broadcasted_iota(jnp.int32, (nb, L), 0)
    blk_f = blk.astype(F32)
    qblk = jb * ROUTE_QBLOCKS + (lax.broadcasted_iota(jnp.int32, (nb, L), 1) >> 8)
    gate = jnp.where(blk < qblk, gate2[:nb] + gate2[nb:], -jnp.inf)
    picks = _top_blocks(gate, blk_f, nb)

    chosen = jnp.where(picks[0] | picks[1] | picks[2], 1.0, 0.0)
    B = MOBA_BLOCK
    before = jnp.where(lax.broadcasted_iota(jnp.int32, (B, B), 0)
                       < lax.broadcasted_iota(jnp.int32, (B, B), 1), 1.0, 0.0).astype(BF16)
    carry = run_scr[:, 0:1]
    base = []
    for b in range(ROUTE_QBLOCKS):
        c_b = chosen[:, b * B:(b + 1) * B]
        within = jnp.dot(c_b.astype(BF16), before, preferred_element_type=F32)
        base.append(carry + within)
        carry = carry + within[:, B - 1:B] + c_b[:, B - 1:B]
    base = jnp.concatenate(base, axis=1)

    rows = []
    for pick in picks:
        rows.append(jnp.sum(jnp.where(pick, base, 0.0), axis=0, keepdims=True))
    for pick in picks:
        bid = jnp.sum(jnp.where(pick, blk_f, 0.0), axis=0, keepdims=True)
        valid = jnp.sum(jnp.where(pick, 1.0, 0.0), axis=0, keepdims=True)
        rows.append(jnp.where(valid > 0.0, bid, float(nb)))
    rows += [jnp.zeros((1, L), F32)] * (8 - len(rows))
    info_ref[...] = jnp.concatenate(rows, axis=0).astype(jnp.int32)

    run = jnp.broadcast_to(carry, run_scr.shape)
    run_scr[...] = run
    cnt_ref[0] = run


def _route(proj, kmean):
    s = proj.shape[0]
    nb = s // MOBA_BLOCK
    L = ROUTE_QBLOCKS * MOBA_BLOCK
    steps = s // L
    return pl.pallas_call(
        functools.partial(_route_kernel, nb=nb),
        grid=(HEADS, steps),
        in_specs=[pl.BlockSpec((L, HEAD_DIM), lambda h, j: (j, h)),
                  pl.BlockSpec((nb, HEAD_DIM), lambda h, j: (0, h))],
        out_specs=[pl.BlockSpec((8, L), lambda h, j: (0, h * steps + j)),
                   pl.BlockSpec((1, nb, 128), lambda h, j: (h, 0, 0))],
        out_shape=[jax.ShapeDtypeStruct((8, HEADS * s), jnp.int32),
                   jax.ShapeDtypeStruct((HEADS, nb, 128), F32)],
        scratch_shapes=[pltpu.VMEM((nb, 128), F32)],
        compiler_params=pltpu.CompilerParams(
            dimension_semantics=("parallel", "arbitrary"), vmem_limit_bytes=VMEM_LIMIT),
        name="moba_route",
    )(proj, kmean)


def _dest_kernel(info_ref, cnt_ref, dest_ref, tb_ref, *, nb, s):
    h = pl.program_id(0)
    cap = 4 * s
    ntile = cap // MOBA_BLOCK
    cnt = cnt_ref[0]
    tiles = jnp.floor((cnt + float(MOBA_BLOCK - 1)) * (1.0 / MOBA_BLOCK))
    lower = jnp.where(lax.broadcasted_iota(jnp.int32, (nb, nb), 1)
                      < lax.broadcasted_iota(jnp.int32, (nb, nb), 0), 1.0, 0.0).astype(BF16)
    start = jnp.dot(lower, tiles.astype(BF16), preferred_element_type=F32)
    start1 = start[:, 0:1]
    tiles1 = tiles[:, 0:1]

    t_f = lax.broadcasted_iota(jnp.int32, (nb, ntile), 1).astype(F32)
    n_f = lax.broadcasted_iota(jnp.int32, (nb, ntile), 0).astype(F32)
    inside = (t_f >= start1) & (t_f < start1 + tiles1)
    tb = jnp.sum(jnp.where(inside, n_f, 0.0), axis=0, keepdims=True)
    used = jnp.sum(jnp.where(inside, 1.0, 0.0), axis=0, keepdims=True)
    tb = jnp.where(used > 0.0, tb, -1.0)
    tb_ref[0] = jnp.concatenate([tb, jnp.full((7, ntile), -1.0, F32)], axis=0).astype(jnp.int32)

    off1 = start1 * float(MOBA_BLOCK)
    CH = 2048
    blk_f = lax.broadcasted_iota(jnp.int32, (nb, CH), 0).astype(F32)
    lane = lax.broadcasted_iota(jnp.int32, (1, CH), 1)
    trash = HEADS * cap + (lane & (MOBA_BLOCK - 1))
    for ch in range(s // CH):
        sl = slice(ch * CH, (ch + 1) * CH)
        rows = []
        for r in range(MOBA_TOPK):
            rank = info_ref[r:r + 1, sl]
            bid = info_ref[MOBA_TOPK + r:MOBA_TOPK + r + 1, sl]
            off = jnp.sum(jnp.where(blk_f == bid.astype(F32), off1, 0.0), axis=0, keepdims=True)
            rows.append(jnp.where(bid < nb, h * cap + off.astype(jnp.int32) + rank, trash))
        rows += [jnp.zeros((1, CH), jnp.int32)] * (8 - len(rows))
        dest_ref[:, sl] = jnp.concatenate(rows, axis=0)


def _dest(info, cnt, s):
    nb = s // MOBA_BLOCK
    ntile = 4 * s // MOBA_BLOCK
    return pl.pallas_call(
        functools.partial(_dest_kernel, nb=nb, s=s),
        grid=(HEADS,),
        in_specs=[pl.BlockSpec((8, s), lambda h: (0, h)),
                  pl.BlockSpec((1, nb, 128), lambda h: (h, 0, 0))],
        out_specs=[pl.BlockSpec((8, s), lambda h: (0, h)),
                   pl.BlockSpec((1, 8, ntile), lambda h: (h, 0, 0))],
        out_shape=[jax.ShapeDtypeStruct((8, HEADS * s), jnp.int32),
                   jax.ShapeDtypeStruct((HEADS, 8, ntile), jnp.int32)],
        compiler_params=pltpu.CompilerParams(
            dimension_semantics=("parallel",), vmem_limit_bytes=VMEM_LIMIT),
        name="moba_dest",
    )(info, cnt)


def _sc_scatter_rows(rows, idx, n_out):
    m, c = rows.shape
    k = idx.shape[0]
    nwin = m // SC_WINDOW
    mesh = plsc.VectorSubcoreMesh(core_axis_name="core", subcore_axis_name="subcore")

    @pl.kernel(out_type=jax.ShapeDtypeStruct((n_out, c), rows.dtype), mesh=mesh)
    def kern(x_hbm, i_hbm, o_hbm):
        def body(x_vmem, i_vmem):
            pltpu.sync_copy(x_vmem, o_hbm.at[i_vmem.at[0]])

        pltpu.emit_pipeline(
            body, grid=(k * nwin,),
            in_specs=[pl.BlockSpec((SC_WINDOW, c), lambda i: (i % nwin, 0)),
                      pl.BlockSpec((1, SC_WINDOW), lambda i: (0, i))],
            out_specs=[],
            core_axis_name=("core", "subcore"),
            dimension_semantics=(pltpu.PARALLEL,),
        )(x_hbm, i_hbm)

    return kern(rows, idx.reshape(1, k * m))


def _sc_gather_rows(table, idx):
    m = idx.shape[0]
    c = table.shape[1]
    mesh = plsc.VectorSubcoreMesh(core_axis_name="core", subcore_axis_name="subcore")

    @pl.kernel(out_type=jax.ShapeDtypeStruct((m, c), table.dtype), mesh=mesh)
    def kern(x_hbm, i_hbm, o_hbm):
        def body(i_vmem, o_vmem):
            pltpu.sync_copy(x_hbm.at[i_vmem.at[0]], o_vmem)

        pltpu.emit_pipeline(
            body, grid=(m // SC_WINDOW,),
            in_specs=[pl.BlockSpec((1, SC_WINDOW), lambda i: (0, i))],
            out_specs=[pl.BlockSpec((SC_WINDOW, c), lambda i: (i, 0))],
            core_axis_name=("core", "subcore"),
            dimension_semantics=(pltpu.PARALLEL,),
        )(i_hbm, o_hbm)

    return kern(table, idx.reshape(1, m))


def _pack_partial_t(o_norm_t, lse):
    half = HEAD_DIM // 2
    u = lax.bitcast_convert_type(o_norm_t, jnp.uint32) + jnp.uint32(0x8000)
    word = (u[:half] & jnp.uint32(0xFFFF0000)) | (u[half:] >> 16)
    lse_bits = lax.bitcast_convert_type(jnp.broadcast_to(lse, word.shape), jnp.uint32)
    full = jnp.concatenate([word, lse_bits], axis=0)
    return lax.bitcast_convert_type(full, F32).T


def _unpack_partial(part):
    half = HEAD_DIM // 2
    lane = lax.broadcasted_iota(jnp.int32, part.shape, 1)
    swapped = pltpu.roll(part, half, 1)
    lse = jnp.where(lane < half, swapped, part)
    word = lax.bitcast_convert_type(jnp.where(lane < half, part, swapped), jnp.uint32)
    o = jnp.where(lane < half, lax.bitcast_convert_type(word & jnp.uint32(0xFFFF0000), F32),
                  lax.bitcast_convert_type(word << 16, F32))
    return o, lse


def _tiles_kernel(tb_ref, qs_ref, k_ref, vt_ref, after_ref, o_ref, *, ntile):
    del after_ref
    h = pl.program_id(0)
    g = pl.program_id(1)
    B = MOBA_BLOCK
    base = h * ntile + g * TILES_PER_STEP

    @pl.when(tb_ref[base] >= 0)
    def _():
        blocks = [jnp.maximum(tb_ref[base + u], 0) for u in range(TILES_PER_STEP)]
        s_t = [lax.dot_general(k_ref[pl.ds(pl.multiple_of(n * B, B), B), :],
                               qs_ref[u * B:(u + 1) * B, :].astype(BF16), NT,
                               preferred_element_type=F32) for u, n in enumerate(blocks)]
        m = [jnp.max(x, axis=0, keepdims=True) for x in s_t]
        p = [jnp.exp2(x - mm).astype(BF16) for x, mm in zip(s_t, m)]
        acc = [jnp.dot(vt_ref[0, n], pp, preferred_element_type=F32) for n, pp in zip(blocks, p)]
        for u in range(TILES_PER_STEP):
            l = acc[u][HEAD_DIM:HEAD_DIM + 1]
            o_ref[u * B:(u + 1) * B, :] = _pack_partial_t(acc[u][:HEAD_DIM] / l,
                                                          m[u] + jnp.log(l) * LOG2_E)


def _tiles(tb, qsorted, proj, v_t, after, s):
    nb = s // MOBA_BLOCK
    ntile = 4 * s // MOBA_BLOCK
    steps = ntile // TILES_PER_STEP
    rows = TILES_PER_STEP * MOBA_BLOCK
    grid_spec = pltpu.PrefetchScalarGridSpec(
        num_scalar_prefetch=1,
        grid=(HEADS, steps),
        in_specs=[pl.BlockSpec((rows, HEAD_DIM), lambda h, g, tb: (h * steps + g, 0)),
                  pl.BlockSpec((s, HEAD_DIM), lambda h, g, tb: (0, HEADS + h)),
                  pl.BlockSpec((1, nb, HEAD_DIM + ONES_ROWS, MOBA_BLOCK), lambda h, g, tb: (h, 0, 0, 0)),
                  pl.BlockSpec((16, HEAD_DIM), lambda h, g, tb: (0, 0))],
        out_specs=pl.BlockSpec((rows, HEAD_DIM), lambda h, g, tb: (h * steps + g, 0)),
    )
    return pl.pallas_call(
        functools.partial(_tiles_kernel, ntile=ntile),
        grid_spec=grid_spec,
        out_shape=jax.ShapeDtypeStruct(qsorted.shape, F32),
        compiler_params=pltpu.CompilerParams(
            dimension_semantics=("parallel", "arbitrary"), vmem_limit_bytes=VMEM_LIMIT),
        name="moba_tiles",
    )(tb, qsorted, proj, v_t, after)


def _merge_kernel(q_ref, k_ref, v_ref, part_ref, o_ref):
    j = pl.program_id(0)
    B = MOBA_BLOCK
    heads = [slice(hh * HEAD_DIM, (hh + 1) * HEAD_DIM) for hh in range(HEADS)]
    causal = (lax.broadcasted_iota(jnp.int32, (B, B), 1) <= lax.broadcasted_iota(jnp.int32, (B, B), 0))
    s = [lax.dot_general(q_ref[:, hs], k_ref[:, hs], NT, preferred_element_type=F32) for hs in heads]
    s = [jnp.where(causal, x, -jnp.inf) for x in s]
    m_own = [jnp.broadcast_to(jnp.max(x, axis=-1, keepdims=True), (B, HEAD_DIM)) for x in s]
    p = [jnp.exp2(x - jnp.concatenate([m, m], axis=1)) for x, m in zip(s, m_own)]
    ones = jnp.ones((B, HEAD_DIM), BF16)
    ol = [jnp.dot(x.astype(BF16), jnp.concatenate([v_ref[:, hs], ones], axis=1),
                  preferred_element_type=F32) for x, hs in zip(p, heads)]
    for hh, hs in enumerate(heads):
        o_own = ol[hh][:, :HEAD_DIM]
        l_own = ol[hh][:, HEAD_DIM:]
        lse_own = m_own[hh] + jnp.log(l_own) * LOG2_E
        parts = []
        for r in range(MOBA_TOPK):
            o_r, lse_r = _unpack_partial(part_ref[r, hh])
            valid = r < j
            parts.append((jnp.where(valid, o_r, 0.0), jnp.where(valid, lse_r, -jnp.inf)))
        m_all = lse_own
        for _, lse_r in parts:
            m_all = jnp.maximum(m_all, lse_r)
        w = jnp.exp2(lse_own - m_all)
        num = (w / l_own) * o_own
        den = w
        for o_r, lse_r in parts:
            w = jnp.exp2(lse_r - m_all)
            num = num + w * o_r
            den = den + w
        o_ref[:, hs] = (num / den).astype(BF16)


def _merge(proj, parts, s):
    nb = s // MOBA_BLOCK
    width = HEADS * HEAD_DIM
    return pl.pallas_call(
        _merge_kernel,
        grid=(nb,),
        in_specs=[pl.BlockSpec((MOBA_BLOCK, width), lambda j: (j, 0)),
                  pl.BlockSpec((MOBA_BLOCK, width), lambda j: (j, 1)),
                  pl.BlockSpec((MOBA_BLOCK, width), lambda j: (j, 2)),
                  pl.BlockSpec((MOBA_TOPK, HEADS, MOBA_BLOCK, HEAD_DIM), lambda j: (0, 0, j, 0))],
        out_specs=pl.BlockSpec((MOBA_BLOCK, width), lambda j: (j, 0)),
        out_shape=jax.ShapeDtypeStruct((s, width), BF16),
        compiler_params=pltpu.CompilerParams(
            dimension_semantics=("parallel",), vmem_limit_bytes=VMEM_LIMIT),
        name="moba_merge",
    )(proj, proj, proj, parts)


def _moba_sparse(proj, v_t, kmean, qf, run_during_scatter):
    s = proj.shape[0]
    cap = 4 * s
    n_rows = HEADS * cap + MOBA_BLOCK
    info, cnt = _route(proj, kmean)
    dest, tb = _dest(info, cnt, s)
    dest3 = dest[:MOBA_TOPK]
    qsorted = _sc_scatter_rows(qf.reshape(HEADS * s, HEAD_DIM), dest3, n_rows)
    after = run_during_scatter()
    osorted = _tiles(tb[:, 0, :].reshape(-1), qsorted, proj, v_t, after, s)
    parts = _sc_gather_rows(osorted, dest3.reshape(-1))
    return _merge(proj, parts.reshape(MOBA_TOPK, HEADS, s, HEAD_DIM), s)


def _delta_kernel(q_ref, k_ref, v_ref, qh_ref, kh_ref, vh_ref, z_ref, small_ref,
                  cwq_ref, cwk_ref, cwv_ref, alog_ref, dtb_ref, onw_ref,
                  o_ref, state_scr, xx_scr, *, hg):
    c = pl.program_id(0)
    C = DN_CHUNK

    @pl.when(c == 0)
    def _():
        state_scr[...] = jnp.zeros_like(state_scr)

    def conv_silu(idx, x_ref, halo_ref, cw_ref):
        halo = jnp.where(c == 0, 0.0, halo_ref[...].astype(F32))
        xx_scr[idx, 0:HALO, :] = halo
        xx_scr[idx, HALO:HALO + C, :] = x_ref[...].astype(F32)
        y = jnp.zeros((C, DN_HEADS_PER_STEP * HEAD_DIM), F32)
        for t in range(DN_CONV):
            off = HALO - (DN_CONV - 1) + t
            y = y + cw_ref[t:t + 1, :] * xx_scr[idx, off:off + C, :]
        return y * _sigmoid(y)

    q_all = conv_silu(0, q_ref, qh_ref, cwq_ref)
    k_all = conv_silu(1, k_ref, kh_ref, cwk_ref)
    v_all = conv_silu(2, v_ref, vh_ref, cwv_ref)

    small = small_ref[...]
    lane = lax.broadcasted_iota(jnp.int32, (C, 128), 1)
    beta_all = _sigmoid(small)
    xs = small + dtb_ref[...]
    softplus = jnp.maximum(xs, 0.0) + jnp.log(1.0 + jnp.exp(-jnp.abs(xs)))
    g_all = -jnp.exp(alog_ref[...]) * softplus

    row = lax.broadcasted_iota(jnp.int32, (C, C), 0)
    col = lax.broadcasted_iota(jnp.int32, (C, C), 1)
    tril = row >= col
    strict = row > col
    rxc = row ^ col
    eye = (row == col).astype(F32)
    assert C == 2 * HEAD_DIM

    heads = range(DN_HEADS_PER_STEP)
    sls = [slice(hh * HEAD_DIM, (hh + 1) * HEAD_DIM) for hh in heads]

    def hmap(f, *lists):
        return [f(*xs) for xs in zip(*lists)]

    def pieces(x, n):
        out, rest = [], x
        for _ in range(n):
            p = rest.astype(BF16)
            out.append(p)
            rest = rest - p.astype(F32)
        return jnp.concatenate(out, axis=1)

    hp = DN_HEADS_PER_STEP
    ones = jnp.ones((HEAD_DIM, HEAD_DIM), BF16)

    def unit_rows(x):
        ss = jnp.dot((x * x).astype(BF16), ones, preferred_element_type=F32)
        return x * lax.rsqrt(ss + EPS)

    def selector(first_lane, n):
        src = lax.broadcasted_iota(jnp.int32, (128, hp * HEAD_DIM), 0)
        dst = lax.broadcasted_iota(jnp.int32, (128, hp * HEAD_DIM), 1) >> 7
        sel = jnp.where(src == first_lane + dst, 1.0, 0.0).astype(BF16)
        return jnp.concatenate([sel] * n, axis=0)

    q = [unit_rows(q_all[:, sl]) * (HEAD_DIM ** -0.5) for sl in sls]
    k = [unit_rows(k_all[:, sl]) for sl in sls]
    v = [v_all[:, sl] for sl in sls]
    beta_rep = jnp.dot(pieces(beta_all, 1), selector(hg * hp, 1), preferred_element_type=F32)
    beta = [beta_rep[:, sl] for sl in sls]

    cum3 = jnp.dot(jnp.where(tril, 1.0, 0.0).astype(BF16), pieces(g_all, 3), preferred_element_type=F32)
    gcum_all = cum3[:, :128] + cum3[:, 128:256] + cum3[:, 256:]
    gcum_rep = jnp.dot(pieces(gcum_all, 3), selector(hg * hp + HEADS, 3), preferred_element_type=F32)
    gcum_b = [gcum_rep[:, sl] for sl in sls]
    gr = [x.T[0:1, :] for x in gcum_b]
    decay = hmap(lambda c_, r_: jnp.where(
        tril, jnp.exp(jnp.where(tril, jnp.concatenate([c_, c_], axis=1) - r_, 0.0)), 0.0), gcum_b, gr)
    e_g = hmap(jnp.exp, gcum_b)
    g_last = [x[C - 1:C, :] for x in gcum_b]

    kb = hmap(lambda a_, b_: a_ * b_, k, beta)
    vb = hmap(lambda a_, b_: a_ * b_, v, beta)
    a = hmap(lambda kb_, q_, k_: lax.dot_general(
        jnp.concatenate([kb_, q_], axis=0).astype(BF16), k_.astype(BF16), NT,
        preferred_element_type=F32), kb, q, k)
    lmat = hmap(lambda a_, d_: jnp.where(strict, a_[:C] * d_, 0.0), a, decay)
    attn = hmap(lambda a_, d_: a_[C:] * d_, a, decay)

    d1 = hmap(lambda l_: jnp.where(rxc < 8, l_, 0.0), lmat)
    d2 = hmap(lambda x: _bdot(x, x), d1)
    d4 = hmap(lambda x: _bdot(x, x), d2)
    p1 = hmap(lambda x, y: _bdot(eye - x, eye + y), d1, d2)
    tmat = hmap(lambda x, y: _bdot(x, eye + y), p1, d4)

    def odd_rows(t, sz):
        return jnp.concatenate([t[b * 2 * sz + sz:(b + 1) * 2 * sz] for b in range(C // (2 * sz))], axis=0)

    def with_odd_rows(t, odd, sz):
        pieces = []
        for b in range(C // (2 * sz)):
            pieces += [t[b * 2 * sz:b * 2 * sz + sz], odd[b * sz:(b + 1) * sz]]
        return jnp.concatenate(pieces, axis=0)

    sz = 8
    while sz < C:
        off = hmap(lambda l_: jnp.where((rxc >= sz) & (rxc < 2 * sz), l_, 0.0), lmat)
        t_odd = hmap(lambda t_: odd_rows(t_, sz), tmat)
        x = hmap(_bdot, t_odd, off)
        x = hmap(_bdot, x, tmat)
        tmat = hmap(lambda t_, o_, x_: with_odd_rows(t_, o_ - x_, sz), tmat, t_odd, x)
        sz *= 2

    uw = hmap(lambda t_, vb_, kb_, e_: _bdot(t_, jnp.concatenate([vb_, kb_ * e_], axis=1)),
              tmat, vb, kb, e_g)
    state = [state_scr[hh] for hh in heads]
    wq = hmap(lambda uw_, q_, e_, s_: _bdot(jnp.concatenate([uw_[:, HEAD_DIM:], q_ * e_], axis=0), s_),
              uw, q, e_g, state)
    v_new = hmap(lambda uw_, wq_: uw_[:, :HEAD_DIM] - wq_[:C], uw, wq)
    av = hmap(_bdot, attn, v_new)
    kt = hmap(lambda k_, gl_, gc_: (k_ * jnp.exp(gl_ - gc_)).T, k, g_last, gcum_b)
    ds = hmap(_bdot, kt, v_new)
    for hh in heads:
        state_scr[hh] = state[hh] * jnp.exp(g_last[hh]) + ds[hh]
        o = wq[hh][C:] + av[hh]
        y = o * lax.rsqrt(jnp.mean(o * o, axis=-1, keepdims=True) + EPS) * onw_ref[...]
        z = z_ref[:, sls[hh]].astype(F32)
        o_ref[:, sls[hh]] = (y * (z * _sigmoid(z))).astype(BF16)


def _gated_deltanet(proj, small, conv_w, alog_v, dtb_v, onw, hg):
    s = proj.shape[0]
    nc = s // DN_CHUNK
    rb = DN_CHUNK // HALO
    hp = DN_HEADS_PER_STEP
    ng = HEADS // hp
    width = hp * HEAD_DIM

    def col(part):
        return pl.BlockSpec((DN_CHUNK, width), lambda c: (c, part * ng + hg))

    def halo(part):
        return pl.BlockSpec((HALO, width), lambda c: (jnp.maximum(c * rb - 1, 0), part * ng + hg))

    def cw(part):
        return pl.BlockSpec((DN_CONV, width), lambda c: (0, part * ng + hg))

    vec = pl.BlockSpec((1, 128), lambda c: (0, 0))
    return pl.pallas_call(
        functools.partial(_delta_kernel, hg=hg),
        grid=(nc,),
        in_specs=[col(3), col(4), col(5),
                  halo(3), halo(4), halo(5),
                  col(6),
                  pl.BlockSpec((DN_CHUNK, 128), lambda c: (c, 0)),
                  cw(0), cw(1), cw(2),
                  vec, vec, vec],
        out_specs=pl.BlockSpec((DN_CHUNK, width), lambda c: (c, 0)),
        out_shape=jax.ShapeDtypeStruct((s, width), BF16),
        scratch_shapes=[pltpu.VMEM((hp, HEAD_DIM, HEAD_DIM), F32),
                        pltpu.VMEM((3, HALO + DN_CHUNK, width), F32)],
        compiler_params=pltpu.CompilerParams(
            dimension_semantics=("arbitrary",), vmem_limit_bytes=VMEM_LIMIT),
        name="deltanet",
    )(proj, proj, proj, proj, proj, proj, proj, small, conv_w, conv_w, conv_w, alog_v, dtb_v, onw)


def _mix_out_kernel(ya_ref, *rest):
    ng = HEADS // DN_HEADS_PER_STEP
    yd_refs = rest[:ng]
    ga_ref, gd_ref, x_ref, wa_ref, wd_ref, wo_ref, nw_ref, o_ref = rest[ng:]
    width = DN_HEADS_PER_STEP * HEAD_DIM
    pa = jnp.dot(ya_ref[...], wa_ref[...], preferred_element_type=F32)
    pd = jnp.dot(yd_refs[0][...], wd_ref[0:width, :], preferred_element_type=F32)
    for g in range(1, ng):
        pd = pd + jnp.dot(yd_refs[g][...], wd_ref[g * width:(g + 1) * width, :],
                          preferred_element_type=F32)
    merged = _sigmoid(ga_ref[...].astype(F32)) * pa + _sigmoid(gd_ref[...].astype(F32)) * pd
    mo = jnp.dot(merged.astype(BF16), wo_ref[...], preferred_element_type=F32)
    y = mo * lax.rsqrt(jnp.mean(mo * mo, axis=-1, keepdims=True) + EPS) * nw_ref[...]
    o_ref[...] = x_ref[...] + y


def _mix_out(ya, yds, proj, x2, wa, wd, wo, nw, *, tm=512):
    s = x2.shape[0]
    row = lambda i: (i, 0)
    full = lambda i: (0, 0)
    wspec = pl.BlockSpec((D_MODEL, D_MODEL), full)
    return pl.pallas_call(
        _mix_out_kernel,
        grid=(s // tm,),
        in_specs=[pl.BlockSpec((tm, D_MODEL), row)]
                 + [pl.BlockSpec((tm, yd.shape[1]), row) for yd in yds]
                 + [pl.BlockSpec((tm, D_MODEL), lambda i: (i, 7)),
                  pl.BlockSpec((tm, D_MODEL), lambda i: (i, 8)),
                  pl.BlockSpec((tm, D_MODEL), row),
                  wspec, wspec, wspec, pl.BlockSpec((1, D_MODEL), full)],
        out_specs=pl.BlockSpec((tm, D_MODEL), row),
        out_shape=jax.ShapeDtypeStruct((s, D_MODEL), F32),
        compiler_params=pltpu.CompilerParams(
            dimension_semantics=("parallel",), vmem_limit_bytes=VMEM_LIMIT),
        name="mix_out",
    )(ya, *yds, proj, proj, x2, wa, wd, wo, nw)


def _ffn_kernel(x_ref, npre_ref, wg_ref, wu_ref, wd_ref, npost_ref, o_ref):
    x = x_ref[...]
    h = (x * lax.rsqrt(jnp.mean(x * x, axis=-1, keepdims=True) + EPS) * npre_ref[...]).astype(BF16)
    acc = jnp.zeros(x.shape, F32)
    for cc in range(D_FF // FF_CHUNK):
        sl = slice(cc * FF_CHUNK, (cc + 1) * FF_CHUNK)
        g = jnp.dot(h, wg_ref[:, sl], preferred_element_type=F32)
        u = jnp.dot(h, wu_ref[:, sl], preferred_element_type=F32)
        act = (g * _sigmoid(g) * u).astype(BF16)
        acc = acc + jnp.dot(act, wd_ref[sl, :], preferred_element_type=F32)
    y = acc * lax.rsqrt(jnp.mean(acc * acc, axis=-1, keepdims=True) + EPS) * npost_ref[...]
    o_ref[...] = x + y


def _ffn(x1, npre, wg, wu, wd, npost, *, tm=512):
    s = x1.shape[0]
    row = lambda i: (i, 0)
    full = lambda i: (0, 0)
    once = pl.Buffered(1)
    return pl.pallas_call(
        _ffn_kernel,
        grid=(s // tm,),
        in_specs=[pl.BlockSpec((tm, D_MODEL), row), pl.BlockSpec((1, D_MODEL), full),
                  pl.BlockSpec((D_MODEL, D_FF), full, pipeline_mode=once),
                  pl.BlockSpec((D_MODEL, D_FF), full, pipeline_mode=once),
                  pl.BlockSpec((D_FF, D_MODEL), full, pipeline_mode=once),
                  pl.BlockSpec((1, D_MODEL), full)],
        out_specs=pl.BlockSpec((tm, D_MODEL), row),
        out_shape=jax.ShapeDtypeStruct((s, D_MODEL), F32),
        compiler_params=pltpu.CompilerParams(
            dimension_semantics=("parallel",), vmem_limit_bytes=VMEM_LIMIT),
        name="ffn",
    )(x1, npre, wg, wu, wd, npost)


def _rope_table(s):
    half = ROPE_DIM // 2
    rep = 128 // half
    inv = ROPE_THETA ** (-jnp.arange(half, dtype=F32) * 2.0 / ROPE_DIM)
    pos = (jnp.arange(s // rep)[:, None] * rep + jnp.arange(128)[None, :] // half).astype(F32)
    ang = pos * jnp.tile(inv, rep)[None, :]
    cos, sin = lax.optimization_barrier((jnp.cos(ang), jnp.sin(ang)))
    cos = cos.reshape(s, half)
    sin = sin.reshape(s, half)
    return jnp.concatenate([cos, sin, jnp.zeros((s, HEAD_DIM - ROPE_DIM), F32)], axis=-1)


def _layer(x2, l, norm_mix_pre, w_in, conv_w, a_log, dt_bias, o_norm_w, w_o_attn, w_o_delta,
           w_out, norm_mix_post, norm_ffn_pre, w_gate, w_up, w_down, norm_ffn_post):
    s = x2.shape[0]
    nb = s // MOBA_BLOCK
    wide = 7 * D_MODEL
    nsmall = 2 * HEADS
    w_a, w_b, w_small = _wprep(w_in, l, wide // D_MODEL, 2, nsmall)

    proj, small, kmean, v_t, qf = _project(x2, norm_mix_pre[l][None, :], w_a, w_b, w_small,
                                           _rope_table(s))
    pad = lambda vec: jnp.pad(vec.astype(F32), (HEADS, 128 - 2 * HEADS))[None, :]
    delta = lambda hg: _gated_deltanet(proj, small, conv_w[l], pad(a_log[l]), pad(dt_bias[l]),
                                       o_norm_w[l][None, :], hg)
    yds = []
    ya = _moba_sparse(proj, v_t, kmean.reshape(nb, D_MODEL), qf,
                      lambda: yds.append(delta(0)) or yds[0])
    yds += [delta(hg) for hg in range(1, HEADS // DN_HEADS_PER_STEP)]

    x1 = _mix_out(ya, yds, proj, x2, w_o_attn[l].astype(BF16), w_o_delta[l].astype(BF16),
                  w_out[l].astype(BF16), norm_mix_post[l][None, :])
    return _ffn(x1, norm_ffn_pre[l][None, :], w_gate[l].astype(BF16), w_up[l].astype(BF16),
                w_down[l].astype(BF16), norm_ffn_post[l][None, :])


def kernel(x, norm_mix_pre, w_in, conv_w, a_log, dt_bias, o_norm_w, w_o_attn, w_o_delta, w_out,
           norm_mix_post, norm_ffn_pre, w_gate, w_up, w_down, norm_ffn_post):
    b, s, d = x.shape
    assert d == D_MODEL and s % 512 == 0
    outs = []
    for bi in range(b):
        x2 = x.reshape(s, d) if b == 1 else x[bi]
        for l in range(w_in.shape[0]):
            x2 = _layer(x2, l, norm_mix_pre, w_in, conv_w, a_log, dt_bias, o_norm_w, w_o_attn,
                        w_o_delta, w_out, norm_mix_post, norm_ffn_pre, w_gate, w_up, w_down,
                        norm_ffn_post)
        outs.append(x2)
    return outs[0].reshape(1, s, d) if b == 1 else jnp.stack(outs, axis=0)
```

```python
import functools
import math

import jax
import jax.numpy as jnp
from jax import lax
from jax.experimental import pallas as pl
from jax.experimental.pallas import tpu as pltpu
from jax.experimental.pallas import tpu_sc as plsc

D_MODEL = 1024
HEADS = 8
HEAD_DIM = 128
MOBA_BLOCK = 256
MOBA_TOPK = 3
ROPE_DIM = HEAD_DIM // 4
ROPE_THETA = 500000.0
DN_CONV = 4
ONES_ROWS = 16
DN_HEADS_PER_STEP = 4
DN_CHUNK = 256
D_FF = 2816
FF_CHUNK = 256
EPS = 1e-6
LOG2_E = math.log2(math.e)
ATTN_UNROLL = 4
ATTN_GROUPS = 2
HALO = 16

F32 = jnp.float32
BF16 = jnp.bfloat16
NT = (((1,), (1,)), ((), ()))

VMEM_LIMIT = 56 * 1024 * 1024


def _bdot(a, b):
    return jnp.dot(a.astype(BF16), b.astype(BF16), preferred_element_type=F32)


def _sigmoid(x):
    return 1.0 / (1.0 + jnp.exp(-x))


def _proj_kernel(x_ref, nw_ref, wt_ref, rope_ref,
                 out_ref, small_ref, kmean_ref, vt_ref, qf_ref, *, tm, q_scale, n_wide, n_small):
    x = x_ref[...]
    ms = jnp.mean(x * x, axis=-1, keepdims=True)
    h = (x * lax.rsqrt(ms + EPS) * nw_ref[...]).astype(BF16)
    narrow = lax.dot_general(h, wt_ref[n_wide * D_MODEL:n_wide * D_MODEL + 128, :], NT,
                             preferred_element_type=F32)
    small_ref[...] = jnp.where(lax.broadcasted_iota(jnp.int32, narrow.shape, 1) < n_small, narrow, 0.0)
    heads = [slice(hh * HEAD_DIM, (hh + 1) * HEAD_DIM) for hh in range(HEADS)]
    groups = [slice(g * MOBA_BLOCK, (g + 1) * MOBA_BLOCK) for g in range(tm // MOBA_BLOCK)]

    half = ROPE_DIM // 2
    tab = rope_ref[...]
    lane = lax.broadcasted_iota(jnp.int32, tab.shape, 1)
    cos_t = jnp.where(lane < half, tab, jnp.where(lane < ROPE_DIM, pltpu.roll(tab, half, 1), 1.0))
    s1_t = jnp.where(lane < half, -pltpu.roll(tab, HEAD_DIM - half, 1), 0.0)
    s2_t = jnp.where((lane >= half) & (lane < ROPE_DIM), tab, 0.0)

    def roped(a):
        return (a * cos_t + pltpu.roll(a, HEAD_DIM - half, 1) * s1_t + pltpu.roll(a, half, 1) * s2_t)

    for c in range(out_ref.shape[1] // D_MODEL):
        cols = slice(c * D_MODEL, (c + 1) * D_MODEL)
        r0 = c * D_MODEL + (n_small if c >= n_wide else 0)
        acc = lax.dot_general(h, wt_ref[r0:r0 + D_MODEL, :], NT, preferred_element_type=F32)
        if c == 0:
            for hh, hs in enumerate(heads):
                r = roped(acc[:, hs]) * q_scale
                out_ref[:, hs] = r.astype(BF16)
                qf_ref[hh] = r
        elif c == 1:
            for hs in heads:
                r = roped(acc[:, hs])
                out_ref[:, D_MODEL + hs.start:D_MODEL + hs.stop] = r.astype(BF16)
                for g, gs in enumerate(groups):
                    kmean_ref[0, g:g + 1, hs] = jnp.sum(r[gs], axis=0, keepdims=True) * (1.0 / MOBA_BLOCK)
        elif c == 2:
            out_ref[:, cols] = acc.astype(BF16)
            for hh, hs in enumerate(heads):
                for g, gs in enumerate(groups):
                    vt_ref[hh, g, 0:HEAD_DIM, :] = acc[gs, hs].T.astype(BF16)
                    vt_ref[hh, g, HEAD_DIM:, :] = jnp.ones((ONES_ROWS, MOBA_BLOCK), BF16)
        else:
            out_ref[:, cols] = acc.astype(BF16)


def _project(x2, norm_w, w_t, rope_t, n_wide, n_small, *, tm=512):
    s = x2.shape[0]
    width = (w_t.shape[0] - n_small) // D_MODEL * D_MODEL
    once = pl.Buffered(1)
    nblk = tm // MOBA_BLOCK
    kern = functools.partial(_proj_kernel, tm=tm, q_scale=math.log2(math.e) / math.sqrt(HEAD_DIM),
                             n_wide=n_wide, n_small=n_small)
    row = lambda i: (i, 0)
    full = lambda i: (0, 0)
    return pl.pallas_call(
        kern,
        grid=(s // tm,),
        in_specs=[
            pl.BlockSpec((tm, D_MODEL), row),
            pl.BlockSpec((1, D_MODEL), full),
            pl.BlockSpec(w_t.shape, full, pipeline_mode=once),
            pl.BlockSpec((tm, HEAD_DIM), row),
        ],
        out_specs=[
            pl.BlockSpec((tm, width), row),
            pl.BlockSpec((tm, 128), row),
            pl.BlockSpec((1, nblk, D_MODEL), lambda i: (i, 0, 0)),
            pl.BlockSpec((HEADS, nblk, HEAD_DIM + ONES_ROWS, MOBA_BLOCK), lambda i: (0, i, 0, 0)),
            pl.BlockSpec((HEADS, tm, HEAD_DIM), lambda i: (0, i, 0)),
        ],
        out_shape=[
            jax.ShapeDtypeStruct((s, width), BF16),
            jax.ShapeDtypeStruct((s, 128), F32),
            jax.ShapeDtypeStruct((s // tm, nblk, D_MODEL), F32),
            jax.ShapeDtypeStruct((HEADS, s // MOBA_BLOCK, HEAD_DIM + ONES_ROWS, MOBA_BLOCK), BF16),
            jax.ShapeDtypeStruct((HEADS, s, HEAD_DIM), F32),
        ],
        compiler_params=pltpu.CompilerParams(
            dimension_semantics=("parallel",), vmem_limit_bytes=VMEM_LIMIT),
        name="proj",
    )(x2, norm_w, w_t, rope_t)


def _attn_kernel(q_ref, k_ref, vt_ref, km_ref, o_ref, bias_scr, sa_scr, pb_scr, *, nb):
    j = pl.program_id(1)
    q = q_ref[...]
    U = ATTN_UNROLL
    B = MOBA_BLOCK
    blk = lax.broadcasted_iota(jnp.int32, (nb, B), 0)
    blk_f = blk.astype(F32)

    def scores(g):
        n0 = jnp.minimum(g, nb // U - 1) * U
        kn = k_ref[pl.ds(pl.multiple_of(n0 * B, U * B), U * B), :]
        return lax.dot_general(kn, q, NT, preferred_element_type=F32)

    km = km_ref[...]
    km_hi = km.astype(BF16)
    km_lo = (km - km_hi.astype(F32)).astype(BF16)
    gate2 = lax.dot_general(jnp.concatenate([km_hi, km_lo], axis=0), q, NT,
                            preferred_element_type=F32)
    kj = k_ref[pl.ds(pl.multiple_of(j * B, B), B), :]
    s = lax.dot_general(kj, q, NT, preferred_element_type=F32)
    sa_scr[...] = scores(0)
    pb_scr[...] = jnp.zeros_like(pb_scr)

    gate = jnp.where(blk < j, gate2[:nb] + gate2[nb:], -jnp.inf)
    bias = jnp.full((nb, B), -jnp.inf, F32)
    for _ in range(MOBA_TOPK):
        m = jnp.max(gate, axis=0, keepdims=True)
        first = jnp.min(jnp.where(gate == m, blk_f, float(nb)), axis=0, keepdims=True)
        pick = (blk_f == first) & (m > -jnp.inf)
        bias = jnp.where(pick, 0.0, bias)
        gate = jnp.where(pick, -jnp.inf, gate)
    bias_scr[...] = bias

    krow = lax.broadcasted_iota(jnp.int32, (B, B), 0)
    qcol = lax.broadcasted_iota(jnp.int32, (B, B), 1)
    s = jnp.where(krow <= qcol, s, -jnp.inf)
    m0 = jnp.max(s, axis=0, keepdims=True)
    p = jnp.exp2(s - m0)
    acc0 = jnp.dot(vt_ref[0, j], p.astype(BF16), preferred_element_type=F32)


    def values(g, p_of):
        n0 = jnp.maximum(g, 0) * U
        pv = jnp.dot(vt_ref[0, n0], p_of(0), preferred_element_type=F32)
        for u in range(1, U):
            pv = pv + jnp.dot(vt_ref[0, n0 + u], p_of(u), preferred_element_type=F32)
        return pv

    def softmax(g, s_of, m):
        sn = [s_of(u) + bias_scr[pl.ds(g * U + u, 1), :] for u in range(U)]
        m_new = m
        for u in range(U):
            m_new = jnp.maximum(m_new, jnp.max(sn[u], axis=0, keepdims=True))
        alpha = jnp.exp2(m - m_new)
        return m_new, alpha, [jnp.exp2(sn[u] - m_new).astype(BF16) for u in range(U)]

    G = ATTN_GROUPS

    def body(t, carry):
        m, acc, alpha_last = carry
        acc = alpha_last * acc + values(G * t - 1, lambda u: pb_scr[u * B:(u + 1) * B, :])
        s_of = lambda u: sa_scr[u * B:(u + 1) * B, :]
        for i in range(G):
            if i + 1 < G:
                s_next = scores(G * t + i + 1)
            m, alpha, p_i = softmax(G * t + i, s_of, m)
            if i + 1 < G:
                acc = alpha * acc + values(G * t + i, lambda u, p_i=p_i: p_i[u])
                s_of = lambda u, s_next=s_next: s_next[u * B:(u + 1) * B]
        for u in range(U):
            pb_scr[u * B:(u + 1) * B, :] = p_i[u]
        sa_scr[...] = scores(G * t + G)
        return m, acc, alpha

    trips = (j + G * U - 1) // (G * U)
    _, acc, alpha_last = lax.fori_loop(0, trips, body, (m0, acc0, jnp.ones_like(m0)))
    acc = alpha_last * acc + values(G * trips - 1, lambda u: pb_scr[u * B:(u + 1) * B, :])
    o_ref[...] = (acc[:HEAD_DIM] / acc[HEAD_DIM:HEAD_DIM + 1]).T.astype(BF16)


def _moba_attention(proj, v_t, kmean):
    s = proj.shape[0]
    nb = s // MOBA_BLOCK
    return pl.pallas_call(
        functools.partial(_attn_kernel, nb=nb),
        grid=(HEADS, nb),
        in_specs=[
            pl.BlockSpec((MOBA_BLOCK, HEAD_DIM), lambda h, j: (j, h)),
            pl.BlockSpec((s, HEAD_DIM), lambda h, j: (0, HEADS + h)),
            pl.BlockSpec((1, nb, HEAD_DIM + ONES_ROWS, MOBA_BLOCK),
                         lambda h, j: (h, 0, 0, 0)),
            pl.BlockSpec((nb, HEAD_DIM), lambda h, j: (0, h)),
        ],
        out_specs=pl.BlockSpec((MOBA_BLOCK, HEAD_DIM), lambda h, j: (j, h)),
        out_shape=jax.ShapeDtypeStruct((s, HEADS * HEAD_DIM), BF16),
        scratch_shapes=[pltpu.VMEM((nb, MOBA_BLOCK), F32),
                        pltpu.VMEM((ATTN_UNROLL * MOBA_BLOCK, MOBA_BLOCK), F32),
                        pltpu.VMEM((ATTN_UNROLL * MOBA_BLOCK, MOBA_BLOCK), BF16)],
        compiler_params=pltpu.CompilerParams(
            dimension_semantics=("parallel", "arbitrary"), vmem_limit_bytes=VMEM_LIMIT),
        name="moba",
    )(proj, proj, v_t, kmean)


ROUTE_QBLOCKS = 8
TILES_PER_STEP = 32
SC_WINDOW = 128


def _top_blocks(gate, blk_f, nb):
    picks = []
    for _ in range(MOBA_TOPK):
        m = jnp.max(gate, axis=0, keepdims=True)
        first = jnp.min(jnp.where(gate == m, blk_f, float(nb)), axis=0, keepdims=True)
        pick = (blk_f == first) & (m > -jnp.inf)
        gate = jnp.where(pick, -jnp.inf, gate)
        picks.append(pick)
    return picks


def _route_kernel(q_ref, km_ref, info_ref, cnt_ref, run_scr, *, nb):
    jb = pl.program_id(1)
    L = ROUTE_QBLOCKS * MOBA_BLOCK

    @pl.when(jb == 0)
    def _():
        run_scr[...] = jnp.zeros_like(run_scr)

    km = km_ref[...]
    km_hi = km.astype(BF16)
    km_lo = (km - km_hi.astype(F32)).astype(BF16)
    gate2 = lax.dot_general(jnp.concatenate([km_hi, km_lo], axis=0), q_ref[...], NT,
                            preferred_element_type=F32)
    blk = lax.broadcasted_iota(jnp.int32, (nb, L), 0)
    blk_f = blk.astype(F32)
    qblk = jb * ROUTE_QBLOCKS + (lax.broadcasted_iota(jnp.int32, (nb, L), 1) >> 8)
    gate = jnp.where(blk < qblk, gate2[:nb] + gate2[nb:], -jnp.inf)
    picks = _top_blocks(gate, blk_f, nb)

    chosen = jnp.where(picks[0] | picks[1] | picks[2], 1.0, 0.0)
    B = MOBA_BLOCK
    before = jnp.where(lax.broadcasted_iota(jnp.int32, (B, B), 0)
                       < lax.broadcasted_iota(jnp.int32, (B, B), 1), 1.0, 0.0).astype(BF16)
    carry = run_scr[:, 0:1]
    base = []
    for b in range(ROUTE_QBLOCKS):
        c_b = chosen[:, b * B:(b + 1) * B]
        within = jnp.dot(c_b.astype(BF16), before, preferred_element_type=F32)
        base.append(carry + within)
        carry = carry + within[:, B - 1:B] + c_b[:, B - 1:B]
    base = jnp.concatenate(base, axis=1)

    rows = []
    for pick in picks:
        rows.append(jnp.sum(jnp.where(pick, base, 0.0), axis=0, keepdims=True))
    for pick in picks:
        bid = jnp.sum(jnp.where(pick, blk_f, 0.0), axis=0, keepdims=True)
        valid = jnp.sum(jnp.where(pick, 1.0, 0.0), axis=0, keepdims=True)
        rows.append(jnp.where(valid > 0.0, bid, float(nb)))
    rows += [jnp.zeros((1, L), F32)] * (8 - len(rows))
    info_ref[...] = jnp.concatenate(rows, axis=0).astype(jnp.int32)

    run = jnp.broadcast_to(carry, run_scr.shape)
    run_scr[...] = run
    cnt_ref[0] = run


def _route(proj, kmean):
    s = proj.shape[0]
    nb = s // MOBA_BLOCK
    L = ROUTE_QBLOCKS * MOBA_BLOCK
    steps = s // L
    return pl.pallas_call(
        functools.partial(_route_kernel, nb=nb),
        grid=(HEADS, steps),
        in_specs=[pl.BlockSpec((L, HEAD_DIM), lambda h, j: (j, h)),
                  pl.BlockSpec((nb, HEAD_DIM), lambda h, j: (0, h))],
        out_specs=[pl.BlockSpec((8, L), lambda h, j: (0, h * steps + j)),
                   pl.BlockSpec((1, nb, 128), lambda h, j: (h, 0, 0))],
        out_shape=[jax.ShapeDtypeStruct((8, HEADS * s), jnp.int32),
                   jax.ShapeDtypeStruct((HEADS, nb, 128), F32)],
        scratch_shapes=[pltpu.VMEM((nb, 128), F32)],
        compiler_params=pltpu.CompilerParams(
            dimension_semantics=("parallel", "arbitrary"), vmem_limit_bytes=VMEM_LIMIT),
        name="moba_route",
    )(proj, kmean)


def _dest_kernel(info_ref, cnt_ref, dest_ref, tb_ref, *, nb, s):
    h = pl.program_id(0)
    cap = 4 * s
    ntile = cap // MOBA_BLOCK
    cnt = cnt_ref[0]
    tiles = jnp.floor((cnt + float(MOBA_BLOCK - 1)) * (1.0 / MOBA_BLOCK))
    lower = jnp.where(lax.broadcasted_iota(jnp.int32, (nb, nb), 1)
                      < lax.broadcasted_iota(jnp.int32, (nb, nb), 0), 1.0, 0.0).astype(BF16)
    start = jnp.dot(lower, tiles.astype(BF16), preferred_element_type=F32)
    start1 = start[:, 0:1]
    tiles1 = tiles[:, 0:1]

    t_f = lax.broadcasted_iota(jnp.int32, (nb, ntile), 1).astype(F32)
    n_f = lax.broadcasted_iota(jnp.int32, (nb, ntile), 0).astype(F32)
    inside = (t_f >= start1) & (t_f < start1 + tiles1)
    tb = jnp.sum(jnp.where(inside, n_f, 0.0), axis=0, keepdims=True)
    used = jnp.sum(jnp.where(inside, 1.0, 0.0), axis=0, keepdims=True)
    tb = jnp.where(used > 0.0, tb, -1.0)
    tb_ref[0] = jnp.concatenate([tb, jnp.full((7, ntile), -1.0, F32)], axis=0).astype(jnp.int32)

    off1 = start1 * float(MOBA_BLOCK)
    CH = 2048
    blk_f = lax.broadcasted_iota(jnp.int32, (nb, CH), 0).astype(F32)
    lane = lax.broadcasted_iota(jnp.int32, (1, CH), 1)
    trash = HEADS * cap + (lane & (MOBA_BLOCK - 1))
    for ch in range(s // CH):
        sl = slice(ch * CH, (ch + 1) * CH)
        rows = []
        for r in range(MOBA_TOPK):
            rank = info_ref[r:r + 1, sl]
            bid = info_ref[MOBA_TOPK + r:MOBA_TOPK + r + 1, sl]
            off = jnp.sum(jnp.where(blk_f == bid.astype(F32), off1, 0.0), axis=0, keepdims=True)
            rows.append(jnp.where(bid < nb, h * cap + off.astype(jnp.int32) + rank, trash))
        rows += [jnp.zeros((1, CH), jnp.int32)] * (8 - len(rows))
        dest_ref[:, sl] = jnp.concatenate(rows, axis=0)


def _dest(info, cnt, s):
    nb = s // MOBA_BLOCK
    ntile = 4 * s // MOBA_BLOCK
    return pl.pallas_call(
        functools.partial(_dest_kernel, nb=nb, s=s),
        grid=(HEADS,),
        in_specs=[pl.BlockSpec((8, s), lambda h: (0, h)),
                  pl.BlockSpec((1, nb, 128), lambda h: (h, 0, 0))],
        out_specs=[pl.BlockSpec((8, s), lambda h: (0, h)),
                   pl.BlockSpec((1, 8, ntile), lambda h: (h, 0, 0))],
        out_shape=[jax.ShapeDtypeStruct((8, HEADS * s), jnp.int32),
                   jax.ShapeDtypeStruct((HEADS, 8, ntile), jnp.int32)],
        compiler_params=pltpu.CompilerParams(
            dimension_semantics=("parallel",), vmem_limit_bytes=VMEM_LIMIT),
        name="moba_dest",
    )(info, cnt)


def _sc_scatter_rows(rows, idx, n_out):
    m, c = rows.shape
    k = idx.shape[0]
    nwin = m // SC_WINDOW
    mesh = plsc.VectorSubcoreMesh(core_axis_name="core", subcore_axis_name="subcore")

    @pl.kernel(out_type=jax.ShapeDtypeStruct((n_out, c), rows.dtype), mesh=mesh)
    def kern(x_hbm, i_hbm, o_hbm):
        def body(x_vmem, i_vmem):
            pltpu.sync_copy(x_vmem, o_hbm.at[i_vmem.at[0]])

        pltpu.emit_pipeline(
            body, grid=(k * nwin,),
            in_specs=[pl.BlockSpec((SC_WINDOW, c), lambda i: (i % nwin, 0)),
                      pl.BlockSpec((1, SC_WINDOW), lambda i: (0, i))],
            out_specs=[],
            core_axis_name=("core", "subcore"),
            dimension_semantics=(pltpu.PARALLEL,),
        )(x_hbm, i_hbm)

    return kern(rows, idx.reshape(1, k * m))


def _sc_gather_rows(table, idx):
    m = idx.shape[0]
    c = table.shape[1]
    mesh = plsc.VectorSubcoreMesh(core_axis_name="core", subcore_axis_name="subcore")

    @pl.kernel(out_type=jax.ShapeDtypeStruct((m, c), table.dtype), mesh=mesh)
    def kern(x_hbm, i_hbm, o_hbm):
        def body(i_vmem, o_vmem):
            pltpu.sync_copy(x_hbm.at[i_vmem.at[0]], o_vmem)

        pltpu.emit_pipeline(
            body, grid=(m // SC_WINDOW,),
            in_specs=[pl.BlockSpec((1, SC_WINDOW), lambda i: (0, i))],
            out_specs=[pl.BlockSpec((SC_WINDOW, c), lambda i: (i, 0))],
            core_axis_name=("core", "subcore"),
            dimension_semantics=(pltpu.PARALLEL,),
        )(i_hbm, o_hbm)

    return kern(table, idx.reshape(1, m))


def _pack_partial_t(o_norm_t, lse):
    half = HEAD_DIM // 2
    u = lax.bitcast_convert_type(o_norm_t, jnp.uint32) + jnp.uint32(0x8000)
    word = (u[:half] & jnp.uint32(0xFFFF0000)) | (u[half:] >> 16)
    lse_bits = lax.bitcast_convert_type(jnp.broadcast_to(lse, word.shape), jnp.uint32)
    full = jnp.concatenate([word, lse_bits], axis=0)
    return lax.bitcast_convert_type(full, F32).T


def _unpack_partial(part):
    half = HEAD_DIM // 2
    lane = lax.broadcasted_iota(jnp.int32, part.shape, 1)
    swapped = pltpu.roll(part, half, 1)
    lse = jnp.where(lane < half, swapped, part)
    word = lax.bitcast_convert_type(jnp.where(lane < half, part, swapped), jnp.uint32)
    o = jnp.where(lane < half, lax.bitcast_convert_type(word & jnp.uint32(0xFFFF0000), F32),
                  lax.bitcast_convert_type(word << 16, F32))
    return o, lse


def _tiles_kernel(tb_ref, qs_ref, k_ref, vt_ref, after_ref, o_ref, *, ntile):
    del after_ref
    h = pl.program_id(0)
    g = pl.program_id(1)
    B = MOBA_BLOCK
    base = h * ntile + g * TILES_PER_STEP

    @pl.when(tb_ref[base] >= 0)
    def _():
        blocks = [jnp.maximum(tb_ref[base + u], 0) for u in range(TILES_PER_STEP)]
        s_t = [lax.dot_general(k_ref[pl.ds(pl.multiple_of(n * B, B), B), :],
                               qs_ref[u * B:(u + 1) * B, :].astype(BF16), NT,
                               preferred_element_type=F32) for u, n in enumerate(blocks)]
        m = [jnp.max(x, axis=0, keepdims=True) for x in s_t]
        p = [jnp.exp2(x - mm).astype(BF16) for x, mm in zip(s_t, m)]
        acc = [jnp.dot(vt_ref[0, n], pp, preferred_element_type=F32) for n, pp in zip(blocks, p)]
        for u in range(TILES_PER_STEP):
            l = acc[u][HEAD_DIM:HEAD_DIM + 1]
            o_ref[u * B:(u + 1) * B, :] = _pack_partial_t(acc[u][:HEAD_DIM] / l,
                                                          m[u] + jnp.log(l) * LOG2_E)


def _tiles(tb, qsorted, proj, v_t, after, s):
    nb = s // MOBA_BLOCK
    ntile = 4 * s // MOBA_BLOCK
    steps = ntile // TILES_PER_STEP
    rows = TILES_PER_STEP * MOBA_BLOCK
    grid_spec = pltpu.PrefetchScalarGridSpec(
        num_scalar_prefetch=1,
        grid=(HEADS, steps),
        in_specs=[pl.BlockSpec((rows, HEAD_DIM), lambda h, g, tb: (h * steps + g, 0)),
                  pl.BlockSpec((s, HEAD_DIM), lambda h, g, tb: (0, HEADS + h)),
                  pl.BlockSpec((1, nb, HEAD_DIM + ONES_ROWS, MOBA_BLOCK), lambda h, g, tb: (h, 0, 0, 0)),
                  pl.BlockSpec((16, HEAD_DIM), lambda h, g, tb: (0, 0))],
        out_specs=pl.BlockSpec((rows, HEAD_DIM), lambda h, g, tb: (h * steps + g, 0)),
    )
    return pl.pallas_call(
        functools.partial(_tiles_kernel, ntile=ntile),
        grid_spec=grid_spec,
        out_shape=jax.ShapeDtypeStruct(qsorted.shape, F32),
        compiler_params=pltpu.CompilerParams(
            dimension_semantics=("parallel", "arbitrary"), vmem_limit_bytes=VMEM_LIMIT),
        name="moba_tiles",
    )(tb, qsorted, proj, v_t, after)


def _merge_kernel(q_ref, k_ref, v_ref, part_ref, o_ref):
    j = pl.program_id(0)
    B = MOBA_BLOCK
    heads = [slice(hh * HEAD_DIM, (hh + 1) * HEAD_DIM) for hh in range(HEADS)]
    causal = (lax.broadcasted_iota(jnp.int32, (B, B), 1) <= lax.broadcasted_iota(jnp.int32, (B, B), 0))
    s = [lax.dot_general(q_ref[:, hs], k_ref[:, hs], NT, preferred_element_type=F32) for hs in heads]
    s = [jnp.where(causal, x, -jnp.inf) for x in s]
    m_own = [jnp.broadcast_to(jnp.max(x, axis=-1, keepdims=True), (B, HEAD_DIM)) for x in s]
    p = [jnp.exp2(x - jnp.concatenate([m, m], axis=1)) for x, m in zip(s, m_own)]
    ones = jnp.ones((B, HEAD_DIM), BF16)
    ol = [jnp.dot(x.astype(BF16), jnp.concatenate([v_ref[:, hs], ones], axis=1),
                  preferred_element_type=F32) for x, hs in zip(p, heads)]
    for hh, hs in enumerate(heads):
        o_own = ol[hh][:, :HEAD_DIM]
        l_own = ol[hh][:, HEAD_DIM:]
        lse_own = m_own[hh] + jnp.log(l_own) * LOG2_E
        parts = []
        for r in range(MOBA_TOPK):
            o_r, lse_r = _unpack_partial(part_ref[r, hh])
            valid = r < j
            parts.append((jnp.where(valid, o_r, 0.0), jnp.where(valid, lse_r, -jnp.inf)))
        m_all = lse_own
        for _, lse_r in parts:
            m_all = jnp.maximum(m_all, lse_r)
        w = jnp.exp2(lse_own - m_all)
        num = (w / l_own) * o_own
        den = w
        for o_r, lse_r in parts:
            w = jnp.exp2(lse_r - m_all)
            num = num + w * o_r
            den = den + w
        o_ref[:, hs] = (num / den).astype(BF16)


def _merge(proj, parts, s):
    nb = s // MOBA_BLOCK
    width = HEADS * HEAD_DIM
    return pl.pallas_call(
        _merge_kernel,
        grid=(nb,),
        in_specs=[pl.BlockSpec((MOBA_BLOCK, width), lambda j: (j, 0)),
                  pl.BlockSpec((MOBA_BLOCK, width), lambda j: (j, 1)),
                  pl.BlockSpec((MOBA_BLOCK, width), lambda j: (j, 2)),
                  pl.BlockSpec((MOBA_TOPK, HEADS, MOBA_BLOCK, HEAD_DIM), lambda j: (0, 0, j, 0))],
        out_specs=pl.BlockSpec((MOBA_BLOCK, width), lambda j: (j, 0)),
        out_shape=jax.ShapeDtypeStruct((s, width), BF16),
        compiler_params=pltpu.CompilerParams(
            dimension_semantics=("parallel",), vmem_limit_bytes=VMEM_LIMIT),
        name="moba_merge",
    )(proj, proj, proj, parts)


def _moba_sparse(proj, v_t, kmean, qf, run_during_scatter):
    s = proj.shape[0]
    cap = 4 * s
    n_rows = HEADS * cap + MOBA_BLOCK
    info, cnt = _route(proj, kmean)
    dest, tb = _dest(info, cnt, s)
    dest3 = dest[:MOBA_TOPK]
    qsorted = _sc_scatter_rows(qf.reshape(HEADS * s, HEAD_DIM), dest3, n_rows)
    after = run_during_scatter()
    osorted = _tiles(tb[:, 0, :].reshape(-1), qsorted, proj, v_t, after, s)
    parts = _sc_gather_rows(osorted, dest3.reshape(-1))
    return _merge(proj, parts.reshape(MOBA_TOPK, HEADS, s, HEAD_DIM), s)


def _delta_kernel(q_ref, k_ref, v_ref, qh_ref, kh_ref, vh_ref, z_ref, small_ref,
                  cwq_ref, cwk_ref, cwv_ref, alog_ref, dtb_ref, onw_ref,
                  o_ref, state_scr, xx_scr, *, hg):
    c = pl.program_id(0)
    C = DN_CHUNK

    @pl.when(c == 0)
    def _():
        state_scr[...] = jnp.zeros_like(state_scr)

    shifts = DN_CONV - 1
    r_out = lax.broadcasted_iota(jnp.int32, (shifts * C, C), 0)
    r_in = lax.broadcasted_iota(jnp.int32, (shifts * C, C), 1)
    delay = jnp.where((r_out & (C - 1)) - r_in == (r_out >> 8) + 1, 1.0, 0.0).astype(BF16)
    assert C == 256

    def conv_silu(idx, x_ref, halo_ref, cw_ref):
        x = x_ref[...]
        delayed = jnp.dot(delay, x, preferred_element_type=F32)
        halo = jnp.where(c == 0, 0.0, halo_ref[...].astype(F32))
        xx_scr[idx, 0:HALO, :] = halo
        xx_scr[idx, HALO:HALO + HALO, :] = x_ref[0:HALO, :].astype(F32)
        y = cw_ref[shifts:shifts + 1, :] * x.astype(F32)
        for j in range(1, shifts + 1):
            d_j = delayed[(j - 1) * C:j * C]
            head8 = xx_scr[idx, HALO - j:HALO - j + 8, :]
            d_j = jnp.concatenate([head8, d_j[8:]], axis=0)
            y = y + cw_ref[shifts - j:shifts - j + 1, :] * d_j
        return y * _sigmoid(y)

    q_all = conv_silu(0, q_ref, qh_ref, cwq_ref)
    k_all = conv_silu(1, k_ref, kh_ref, cwk_ref)
    v_all = conv_silu(2, v_ref, vh_ref, cwv_ref)

    small = small_ref[...]
    lane = lax.broadcasted_iota(jnp.int32, (C, 128), 1)
    beta_all = _sigmoid(small)
    xs = small + dtb_ref[...]
    softplus = jnp.maximum(xs, 0.0) + jnp.log(1.0 + jnp.exp(-jnp.abs(xs)))
    g_all = -jnp.exp(alog_ref[...]) * softplus

    row = lax.broadcasted_iota(jnp.int32, (C, C), 0)
    col = lax.broadcasted_iota(jnp.int32, (C, C), 1)
    tril = row >= col
    strict = row > col
    rxc = row ^ col
    eye = (row == col).astype(F32)
    assert C == 2 * HEAD_DIM

    heads = range(DN_HEADS_PER_STEP)
    sls = [slice(hh * HEAD_DIM, (hh + 1) * HEAD_DIM) for hh in heads]

    def hmap(f, *lists):
        return [f(*xs) for xs in zip(*lists)]

    def pieces(x, n):
        out, rest = [], x
        for _ in range(n):
            p = rest.astype(BF16)
            out.append(p)
            rest = rest - p.astype(F32)
        return jnp.concatenate(out, axis=1)

    hp = DN_HEADS_PER_STEP
    ones = jnp.ones((HEAD_DIM, HEAD_DIM), BF16)

    def unit_rows(x):
        ss = jnp.dot((x * x).astype(BF16), ones, preferred_element_type=F32)
        return x * lax.rsqrt(ss + EPS)

    def selector(first_lane, n):
        src = lax.broadcasted_iota(jnp.int32, (128, hp * HEAD_DIM), 0)
        dst = lax.broadcasted_iota(jnp.int32, (128, hp * HEAD_DIM), 1) >> 7
        sel = jnp.where(src == first_lane + dst, 1.0, 0.0).astype(BF16)
        return jnp.concatenate([sel] * n, axis=0)

    q = [unit_rows(q_all[:, sl]) * (HEAD_DIM ** -0.5) for sl in sls]
    k = [unit_rows(k_all[:, sl]) for sl in sls]
    v = [v_all[:, sl] for sl in sls]
    beta_rep = jnp.dot(pieces(beta_all, 1), selector(hg * hp, 1), preferred_element_type=F32)
    beta = [beta_rep[:, sl] for sl in sls]

    cum3 = jnp.dot(jnp.where(tril, 1.0, 0.0).astype(BF16), pieces(g_all, 3), preferred_element_type=F32)
    gcum_all = cum3[:, :128] + cum3[:, 128:256] + cum3[:, 256:]
    gcum_rep = jnp.dot(pieces(gcum_all, 3), selector(hg * hp + HEADS, 3), preferred_element_type=F32)
    gcum_b = [gcum_rep[:, sl] for sl in sls]
    gr = [x.T[0:1, :] for x in gcum_b]
    decay = hmap(lambda c_, r_: jnp.where(
        tril, jnp.exp(jnp.where(tril, jnp.concatenate([c_, c_], axis=1) - r_, 0.0)), 0.0), gcum_b, gr)
    e_g = hmap(jnp.exp, gcum_b)
    g_last = [x[C - 1:C, :] for x in gcum_b]

    kb = hmap(lambda a_, b_: a_ * b_, k, beta)
    vb = hmap(lambda a_, b_: a_ * b_, v, beta)
    a = hmap(lambda kb_, q_, k_: lax.dot_general(
        jnp.concatenate([kb_, q_], axis=0).astype(BF16), k_.astype(BF16), NT,
        preferred_element_type=F32), kb, q, k)
    lmat = hmap(lambda a_, d_: jnp.where(strict, a_[:C] * d_, 0.0), a, decay)
    attn = hmap(lambda a_, d_: a_[C:] * d_, a, decay)

    d1 = hmap(lambda l_: jnp.where(rxc < 8, l_, 0.0), lmat)
    d2 = hmap(lambda x: _bdot(x, x), d1)
    d4 = hmap(lambda x: _bdot(x, x), d2)
    p1 = hmap(lambda x, y: _bdot(eye - x, eye + y), d1, d2)
    tmat = hmap(lambda x, y: _bdot(x, eye + y), p1, d4)

    def odd_rows(t, sz):
        return jnp.concatenate([t[b * 2 * sz + sz:(b + 1) * 2 * sz] for b in range(C // (2 * sz))], axis=0)

    def with_odd_rows(t, odd, sz):
        pieces = []
        for b in range(C // (2 * sz)):
            pieces += [t[b * 2 * sz:b * 2 * sz + sz], odd[b * sz:(b + 1) * sz]]
        return jnp.concatenate(pieces, axis=0)

    sz = 8
    while sz < C:
        off = hmap(lambda l_: jnp.where((rxc >= sz) & (rxc < 2 * sz), l_, 0.0), lmat)
        t_odd = hmap(lambda t_: odd_rows(t_, sz), tmat)
        x = hmap(_bdot, t_odd, off)
        x = hmap(_bdot, x, tmat)
        tmat = hmap(lambda t_, o_, x_: with_odd_rows(t_, o_ - x_, sz), tmat, t_odd, x)
        sz *= 2

    uw = hmap(lambda t_, vb_, kb_, e_: _bdot(t_, jnp.concatenate([vb_, kb_ * e_], axis=1)),
              tmat, vb, kb, e_g)
    state = [state_scr[hh] for hh in heads]
    wq = hmap(lambda uw_, q_, e_, s_: _bdot(jnp.concatenate([uw_[:, HEAD_DIM:], q_ * e_], axis=0), s_),
              uw, q, e_g, state)
    v_new = hmap(lambda uw_, wq_: uw_[:, :HEAD_DIM] - wq_[:C], uw, wq)
    av = hmap(_bdot, attn, v_new)
    kt = hmap(lambda k_, gl_, gc_: (k_ * jnp.exp(gl_ - gc_)).T, k, g_last, gcum_b)
    ds = hmap(_bdot, kt, v_new)
    for hh in heads:
        state_scr[hh] = state[hh] * jnp.exp(g_last[hh]) + ds[hh]
        o = wq[hh][C:] + av[hh]
        y = o * lax.rsqrt(jnp.mean(o * o, axis=-1, keepdims=True) + EPS) * onw_ref[...]
        z = z_ref[:, sls[hh]].astype(F32)
        o_ref[:, sls[hh]] = (y * (z * _sigmoid(z))).astype(BF16)


def _gated_deltanet(proj, small, conv_w, alog_v, dtb_v, onw, hg):
    s = proj.shape[0]
    nc = s // DN_CHUNK
    rb = DN_CHUNK // HALO
    hp = DN_HEADS_PER_STEP
    ng = HEADS // hp
    width = hp * HEAD_DIM

    def col(part):
        return pl.BlockSpec((DN_CHUNK, width), lambda c: (c, part * ng + hg))

    def halo(part):
        return pl.BlockSpec((HALO, width), lambda c: (jnp.maximum(c * rb - 1, 0), part * ng + hg))

    def cw(part):
        return pl.BlockSpec((DN_CONV, width), lambda c: (0, part * ng + hg))

    vec = pl.BlockSpec((1, 128), lambda c: (0, 0))
    return pl.pallas_call(
        functools.partial(_delta_kernel, hg=hg),
        grid=(nc,),
        in_specs=[col(3), col(4), col(5),
                  halo(3), halo(4), halo(5),
                  col(6),
                  pl.BlockSpec((DN_CHUNK, 128), lambda c: (c, 0)),
                  cw(0), cw(1), cw(2),
                  vec, vec, vec],
        out_specs=pl.BlockSpec((DN_CHUNK, width), lambda c: (c, 0)),
        out_shape=jax.ShapeDtypeStruct((s, width), BF16),
        scratch_shapes=[pltpu.VMEM((hp, HEAD_DIM, HEAD_DIM), F32),
                        pltpu.VMEM((3, HALO + DN_CHUNK, width), F32)],
        compiler_params=pltpu.CompilerParams(
            dimension_semantics=("arbitrary",), vmem_limit_bytes=VMEM_LIMIT),
        name="deltanet",
    )(proj, proj, proj, proj, proj, proj, proj, small, conv_w, conv_w, conv_w, alog_v, dtb_v, onw)


def _mix_out_kernel(ya_ref, *rest):
    ng = HEADS // DN_HEADS_PER_STEP
    yd_refs = rest[:ng]
    ga_ref, gd_ref, x_ref, wa_ref, wd_ref, wo_ref, nw_ref, o_ref = rest[ng:]
    width = DN_HEADS_PER_STEP * HEAD_DIM
    pa = jnp.dot(ya_ref[...], wa_ref[...], preferred_element_type=F32)
    pd = jnp.dot(yd_refs[0][...], wd_ref[0:width, :], preferred_element_type=F32)
    for g in range(1, ng):
        pd = pd + jnp.dot(yd_refs[g][...], wd_ref[g * width:(g + 1) * width, :],
                          preferred_element_type=F32)
    merged = _sigmoid(ga_ref[...].astype(F32)) * pa + _sigmoid(gd_ref[...].astype(F32)) * pd
    mo = jnp.dot(merged.astype(BF16), wo_ref[...], preferred_element_type=F32)
    y = mo * lax.rsqrt(jnp.mean(mo * mo, axis=-1, keepdims=True) + EPS) * nw_ref[...]
    o_ref[...] = x_ref[...] + y


def _mix_out(ya, yds, proj, x2, wa, wd, wo, nw, *, tm=512):
    s = x2.shape[0]
    row = lambda i: (i, 0)
    full = lambda i: (0, 0)
    wspec = pl.BlockSpec((D_MODEL, D_MODEL), full)
    return pl.pallas_call(
        _mix_out_kernel,
        grid=(s // tm,),
        in_specs=[pl.BlockSpec((tm, D_MODEL), row)]
                 + [pl.BlockSpec((tm, yd.shape[1]), row) for yd in yds]
                 + [pl.BlockSpec((tm, D_MODEL), lambda i: (i, 7)),
                  pl.BlockSpec((tm, D_MODEL), lambda i: (i, 8)),
                  pl.BlockSpec((tm, D_MODEL), row),
                  wspec, wspec, wspec, pl.BlockSpec((1, D_MODEL), full)],
        out_specs=pl.BlockSpec((tm, D_MODEL), row),
        out_shape=jax.ShapeDtypeStruct((s, D_MODEL), F32),
        compiler_params=pltpu.CompilerParams(
            dimension_semantics=("parallel",), vmem_limit_bytes=VMEM_LIMIT),
        name="mix_out",
    )(ya, *yds, proj, proj, x2, wa, wd, wo, nw)


def _ffn_kernel(x_ref, npre_ref, wg_ref, wu_ref, wd_ref, npost_ref, o_ref):
    x = x_ref[...]
    h = (x * lax.rsqrt(jnp.mean(x * x, axis=-1, keepdims=True) + EPS) * npre_ref[...]).astype(BF16)
    acc = jnp.zeros(x.shape, F32)
    for cc in range(D_FF // FF_CHUNK):
        sl = slice(cc * FF_CHUNK, (cc + 1) * FF_CHUNK)
        g = jnp.dot(h, wg_ref[:, sl], preferred_element_type=F32)
        u = jnp.dot(h, wu_ref[:, sl], preferred_element_type=F32)
        act = (g * _sigmoid(g) * u).astype(BF16)
        acc = acc + jnp.dot(act, wd_ref[sl, :], preferred_element_type=F32)
    y = acc * lax.rsqrt(jnp.mean(acc * acc, axis=-1, keepdims=True) + EPS) * npost_ref[...]
    o_ref[...] = x + y


def _ffn(x1, npre, wg, wu, wd, npost, *, tm=512):
    s = x1.shape[0]
    row = lambda i: (i, 0)
    full = lambda i: (0, 0)
    once = pl.Buffered(1)
    return pl.pallas_call(
        _ffn_kernel,
        grid=(s // tm,),
        in_specs=[pl.BlockSpec((tm, D_MODEL), row), pl.BlockSpec((1, D_MODEL), full),
                  pl.BlockSpec((D_MODEL, D_FF), full, pipeline_mode=once),
                  pl.BlockSpec((D_MODEL, D_FF), full, pipeline_mode=once),
                  pl.BlockSpec((D_FF, D_MODEL), full, pipeline_mode=once),
                  pl.BlockSpec((1, D_MODEL), full)],
        out_specs=pl.BlockSpec((tm, D_MODEL), row),
        out_shape=jax.ShapeDtypeStruct((s, D_MODEL), F32),
        compiler_params=pltpu.CompilerParams(
            dimension_semantics=("parallel",), vmem_limit_bytes=VMEM_LIMIT),
        name="ffn",
    )(x1, npre, wg, wu, wd, npost)


def _rope_table(s):
    half = ROPE_DIM // 2
    rep = 128 // half
    inv = ROPE_THETA ** (-jnp.arange(half, dtype=F32) * 2.0 / ROPE_DIM)
    pos = (jnp.arange(s // rep)[:, None] * rep + jnp.arange(128)[None, :] // half).astype(F32)
    ang = pos * jnp.tile(inv, rep)[None, :]
    cos, sin = lax.optimization_barrier((jnp.cos(ang), jnp.sin(ang)))
    cos = cos.reshape(s, half)
    sin = sin.reshape(s, half)
    return jnp.concatenate([cos, sin, jnp.zeros((s, HEAD_DIM - ROPE_DIM), F32)], axis=-1)


def _layer(x2, l, norm_mix_pre, w_in, conv_w, a_log, dt_bias, o_norm_w, w_o_attn, w_o_delta,
           w_out, norm_mix_post, norm_ffn_pre, w_gate, w_up, w_down, norm_ffn_post):
    s = x2.shape[0]
    nb = s // MOBA_BLOCK
    wide = 7 * D_MODEL
    nsmall = 2 * HEADS
    w_t = jnp.swapaxes(w_in[l], 0, 1).astype(BF16)

    proj, small, kmean, v_t, qf = _project(x2, norm_mix_pre[l][None, :], w_t, _rope_table(s),
                                           wide // D_MODEL, nsmall)
    pad = lambda vec: jnp.pad(vec.astype(F32), (HEADS, 128 - 2 * HEADS))[None, :]
    delta = lambda hg: _gated_deltanet(proj, small, conv_w[l], pad(a_log[l]), pad(dt_bias[l]),
                                       o_norm_w[l][None, :], hg)
    yds = []
    ya = _moba_sparse(proj, v_t, kmean.reshape(nb, D_MODEL), qf,
                      lambda: yds.append(delta(0)) or yds[0])
    yds += [delta(hg) for hg in range(1, HEADS // DN_HEADS_PER_STEP)]

    x1 = _mix_out(ya, yds, proj, x2, w_o_attn[l].astype(BF16), w_o_delta[l].astype(BF16),
                  w_out[l].astype(BF16), norm_mix_post[l][None, :])
    return _ffn(x1, norm_ffn_pre[l][None, :], w_gate[l].astype(BF16), w_up[l].astype(BF16),
                w_down[l].astype(BF16), norm_ffn_post[l][None, :])


def kernel(x, norm_mix_pre, w_in, conv_w, a_log, dt_bias, o_norm_w, w_o_attn, w_o_delta, w_out,
           norm_mix_post, norm_ffn_pre, w_gate, w_up, w_down, norm_ffn_post):
    b, s, d = x.shape
    assert d == D_MODEL and s % 512 == 0
    outs = []
    for bi in range(b):
        x2 = x.reshape(s, d) if b == 1 else x[bi]
        for l in range(w_in.shape[0]):
            x2 = _layer(x2, l, norm_mix_pre, w_in, conv_w, a_log, dt_bias, o_norm_w, w_o_attn,
                        w_o_delta, w_out, norm_mix_post, norm_ffn_pre, w_gate, w_up, w_down,
                        norm_ffn_post)
        outs.append(x2)
    return outs[0].reshape(1, s, d) if b == 1 else jnp.stack(outs, axis=0)
```

```python
import functools
import math

import jax
import jax.numpy as jnp
from jax import lax
from jax.experimental import pallas as pl
from jax.experimental.pallas import tpu as pltpu
from jax.experimental.pallas import tpu_sc as plsc

D_MODEL = 1024
HEADS = 8
HEAD_DIM = 128
MOBA_BLOCK = 256
MOBA_TOPK = 3
ROPE_DIM = HEAD_DIM // 4
ROPE_THETA = 500000.0
DN_CONV = 4
ONES_ROWS = 16
DN_HEADS_PER_STEP = 4
DN_CHUNK = 256
D_FF = 2816
FF_CHUNK = 256
EPS = 1e-6
LOG2_E = math.log2(math.e)
HALO = 16

F32 = jnp.float32
BF16 = jnp.bfloat16
NT = (((1,), (1,)), ((), ()))

VMEM_LIMIT = 56 * 1024 * 1024


def _bdot(a, b):
    return jnp.dot(a.astype(BF16), b.astype(BF16), preferred_element_type=F32)


def _sigmoid(x):
    return 1.0 / (1.0 + jnp.exp(-x))


def _proj_kernel(x_ref, nw_ref, wt_ref, rope_ref,
                 out_ref, small_ref, kmean_ref, vt_ref, qf_ref, *, tm, q_scale, n_wide, n_small):
    x = x_ref[...]
    ms = jnp.mean(x * x, axis=-1, keepdims=True)
    h = (x * lax.rsqrt(ms + EPS) * nw_ref[...]).astype(BF16)
    narrow = lax.dot_general(h, wt_ref[n_wide * D_MODEL:n_wide * D_MODEL + 128, :], NT,
                             preferred_element_type=F32)
    small_ref[...] = jnp.where(lax.broadcasted_iota(jnp.int32, narrow.shape, 1) < n_small, narrow, 0.0)
    heads = [slice(hh * HEAD_DIM, (hh + 1) * HEAD_DIM) for hh in range(HEADS)]
    groups = [slice(g * MOBA_BLOCK, (g + 1) * MOBA_BLOCK) for g in range(tm // MOBA_BLOCK)]

    half = ROPE_DIM // 2
    tab = rope_ref[...]
    lane = lax.broadcasted_iota(jnp.int32, tab.shape, 1)
    cos_t = jnp.where(lane < half, tab, jnp.where(lane < ROPE_DIM, pltpu.roll(tab, half, 1), 1.0))
    s1_t = jnp.where(lane < half, -pltpu.roll(tab, HEAD_DIM - half, 1), 0.0)
    s2_t = jnp.where((lane >= half) & (lane < ROPE_DIM), tab, 0.0)

    def roped(a):
        return (a * cos_t + pltpu.roll(a, HEAD_DIM - half, 1) * s1_t + pltpu.roll(a, half, 1) * s2_t)

    for c in range(out_ref.shape[1] // D_MODEL):
        cols = slice(c * D_MODEL, (c + 1) * D_MODEL)
        r0 = c * D_MODEL + (n_small if c >= n_wide else 0)
        acc = lax.dot_general(h, wt_ref[r0:r0 + D_MODEL, :], NT, preferred_element_type=F32)
        if c == 0:
            for hh, hs in enumerate(heads):
                r = roped(acc[:, hs]) * q_scale
                out_ref[:, hs] = r.astype(BF16)
                qf_ref[hh] = r
        elif c == 1:
            for hs in heads:
                r = roped(acc[:, hs])
                out_ref[:, D_MODEL + hs.start:D_MODEL + hs.stop] = r.astype(BF16)
                for g, gs in enumerate(groups):
                    kmean_ref[0, g:g + 1, hs] = jnp.sum(r[gs], axis=0, keepdims=True) * (1.0 / MOBA_BLOCK)
        elif c == 2:
            out_ref[:, cols] = acc.astype(BF16)
            for hh, hs in enumerate(heads):
                for g, gs in enumerate(groups):
                    vt_ref[hh, g, 0:HEAD_DIM, :] = acc[gs, hs].T.astype(BF16)
                    vt_ref[hh, g, HEAD_DIM:, :] = jnp.ones((ONES_ROWS, MOBA_BLOCK), BF16)
        else:
            out_ref[:, cols] = acc.astype(BF16)


def _project(x2, norm_w, w_t, rope_t, n_wide, n_small, *, tm=512):
    s = x2.shape[0]
    width = (w_t.shape[0] - n_small) // D_MODEL * D_MODEL
    once = pl.Buffered(1)
    nblk = tm // MOBA_BLOCK
    kern = functools.partial(_proj_kernel, tm=tm, q_scale=math.log2(math.e) / math.sqrt(HEAD_DIM),
                             n_wide=n_wide, n_small=n_small)
    row = lambda i: (i, 0)
    full = lambda i: (0, 0)
    return pl.pallas_call(
        kern,
        grid=(s // tm,),
        in_specs=[
            pl.BlockSpec((tm, D_MODEL), row),
            pl.BlockSpec((1, D_MODEL), full),
            pl.BlockSpec(w_t.shape, full, pipeline_mode=once),
            pl.BlockSpec((tm, HEAD_DIM), row),
        ],
        out_specs=[
            pl.BlockSpec((tm, width), row),
            pl.BlockSpec((tm, 128), row),
            pl.BlockSpec((1, nblk, D_MODEL), lambda i: (i, 0, 0)),
            pl.BlockSpec((HEADS, nblk, HEAD_DIM + ONES_ROWS, MOBA_BLOCK), lambda i: (0, i, 0, 0)),
            pl.BlockSpec((HEADS, tm, HEAD_DIM), lambda i: (0, i, 0)),
        ],
        out_shape=[
            jax.ShapeDtypeStruct((s, width), BF16),
            jax.ShapeDtypeStruct((s, 128), F32),
            jax.ShapeDtypeStruct((s // tm, nblk, D_MODEL), F32),
            jax.ShapeDtypeStruct((HEADS, s // MOBA_BLOCK, HEAD_DIM + ONES_ROWS, MOBA_BLOCK), BF16),
            jax.ShapeDtypeStruct((HEADS, s, HEAD_DIM), F32),
        ],
        compiler_params=pltpu.CompilerParams(
            dimension_semantics=("parallel",), vmem_limit_bytes=VMEM_LIMIT),
        name="proj",
    )(x2, norm_w, w_t, rope_t)


ROUTE_QBLOCKS = 8
TILES_PER_STEP = 32
SC_WINDOW = 128


def _top_blocks(gate, blk_f, nb):
    picks = []
    for _ in range(MOBA_TOPK):
        m = jnp.max(gate, axis=0, keepdims=True)
        first = jnp.min(jnp.where(gate == m, blk_f, float(nb)), axis=0, keepdims=True)
        pick = (blk_f == first) & (m > -jnp.inf)
        gate = jnp.where(pick, -jnp.inf, gate)
        picks.append(pick)
    return picks


def _route_kernel(q_ref, km_ref, info_ref, cnt_ref, run_scr, *, nb):
    jb = pl.program_id(1)
    L = ROUTE_QBLOCKS * MOBA_BLOCK

    @pl.when(jb == 0)
    def _():
        run_scr[...] = jnp.zeros_like(run_scr)

    km = km_ref[...]
    km_hi = km.astype(BF16)
    km_lo = (km - km_hi.astype(F32)).astype(BF16)
    gate2 = lax.dot_general(jnp.concatenate([km_hi, km_lo], axis=0), q_ref[...], NT,
                            preferred_element_type=F32)
    blk = lax.broadcasted_iota(jnp.int32, (nb, L), 0)
    blk_f = blk.astype(F32)
    qblk = jb * ROUTE_QBLOCKS + (lax.broadcasted_iota(jnp.int32, (nb, L), 1) >> 8)
    gate = jnp.where(blk < qblk, gate2[:nb] + gate2[nb:], -jnp.inf)
    picks = _top_blocks(gate, blk_f, nb)

    chosen = jnp.where(picks[0] | picks[1] | picks[2], 1.0, 0.0)
    B = MOBA_BLOCK
    before = jnp.where(lax.broadcasted_iota(jnp.int32, (B, B), 0)
                       < lax.broadcasted_iota(jnp.int32, (B, B), 1), 1.0, 0.0).astype(BF16)
    carry = run_scr[:, 0:1]
    base = []
    for b in range(ROUTE_QBLOCKS):
        c_b = chosen[:, b * B:(b + 1) * B]
        within = jnp.dot(c_b.astype(BF16), before, preferred_element_type=F32)
        base.append(carry + within)
        carry = carry + within[:, B - 1:B] + c_b[:, B - 1:B]
    base = jnp.concatenate(base, axis=1)

    rows = []
    for pick in picks:
        rows.append(jnp.sum(jnp.where(pick, base, 0.0), axis=0, keepdims=True))
    for pick in picks:
        bid = jnp.sum(jnp.where(pick, blk_f, 0.0), axis=0, keepdims=True)
        valid = jnp.sum(jnp.where(pick, 1.0, 0.0), axis=0, keepdims=True)
        rows.append(jnp.where(valid > 0.0, bid, float(nb)))
    rows += [jnp.zeros((1, L), F32)] * (8 - len(rows))
    info_ref[...] = jnp.concatenate(rows, axis=0).astype(jnp.int32)

    run = jnp.broadcast_to(carry, run_scr.shape)
    run_scr[...] = run
    cnt_ref[0] = run


def _route(proj, kmean):
    s = proj.shape[0]
    nb = s // MOBA_BLOCK
    L = ROUTE_QBLOCKS * MOBA_BLOCK
    steps = s // L
    return pl.pallas_call(
        functools.partial(_route_kernel, nb=nb),
        grid=(HEADS, steps),
        in_specs=[pl.BlockSpec((L, HEAD_DIM), lambda h, j: (j, h)),
                  pl.BlockSpec((nb, HEAD_DIM), lambda h, j: (0, h))],
        out_specs=[pl.BlockSpec((8, L), lambda h, j: (0, h * steps + j)),
                   pl.BlockSpec((1, nb, 128), lambda h, j: (h, 0, 0))],
        out_shape=[jax.ShapeDtypeStruct((8, HEADS * s), jnp.int32),
                   jax.ShapeDtypeStruct((HEADS, nb, 128), F32)],
        scratch_shapes=[pltpu.VMEM((nb, 128), F32)],
        compiler_params=pltpu.CompilerParams(
            dimension_semantics=("parallel", "arbitrary"), vmem_limit_bytes=VMEM_LIMIT),
        name="moba_route",
    )(proj, kmean)


def _dest_kernel(info_ref, cnt_ref, dest_ref, tb_ref, *, nb, s):
    h = pl.program_id(0)
    cap = 4 * s
    ntile = cap // MOBA_BLOCK
    cnt = cnt_ref[0]
    tiles = jnp.floor((cnt + float(MOBA_BLOCK - 1)) * (1.0 / MOBA_BLOCK))
    lower = jnp.where(lax.broadcasted_iota(jnp.int32, (nb, nb), 1)
                      < lax.broadcasted_iota(jnp.int32, (nb, nb), 0), 1.0, 0.0).astype(BF16)
    start = jnp.dot(lower, tiles.astype(BF16), preferred_element_type=F32)
    start1 = start[:, 0:1]
    tiles1 = tiles[:, 0:1]

    t_f = lax.broadcasted_iota(jnp.int32, (nb, ntile), 1).astype(F32)
    n_f = lax.broadcasted_iota(jnp.int32, (nb, ntile), 0).astype(F32)
    inside = (t_f >= start1) & (t_f < start1 + tiles1)
    tb = jnp.sum(jnp.where(inside, n_f, 0.0), axis=0, keepdims=True)
    used = jnp.sum(jnp.where(inside, 1.0, 0.0), axis=0, keepdims=True)
    tb = jnp.where(used > 0.0, tb, -1.0)
    tb_ref[0] = jnp.concatenate([tb, jnp.full((7, ntile), -1.0, F32)], axis=0).astype(jnp.int32)

    off1 = start1 * float(MOBA_BLOCK)
    CH = 2048
    blk_f = lax.broadcasted_iota(jnp.int32, (nb, CH), 0).astype(F32)
    lane = lax.broadcasted_iota(jnp.int32, (1, CH), 1)
    trash = HEADS * cap + (lane & (MOBA_BLOCK - 1))
    for ch in range(s // CH):
        sl = slice(ch * CH, (ch + 1) * CH)
        rows = []
        for r in range(MOBA_TOPK):
            rank = info_ref[r:r + 1, sl]
            bid = info_ref[MOBA_TOPK + r:MOBA_TOPK + r + 1, sl]
            off = jnp.sum(jnp.where(blk_f == bid.astype(F32), off1, 0.0), axis=0, keepdims=True)
            rows.append(jnp.where(bid < nb, h * cap + off.astype(jnp.int32) + rank, trash))
        rows += [jnp.zeros((1, CH), jnp.int32)] * (8 - len(rows))
        dest_ref[:, sl] = jnp.concatenate(rows, axis=0)


def _dest(info, cnt, s):
    nb = s // MOBA_BLOCK
    ntile = 4 * s // MOBA_BLOCK
    return pl.pallas_call(
        functools.partial(_dest_kernel, nb=nb, s=s),
        grid=(HEADS,),
        in_specs=[pl.BlockSpec((8, s), lambda h: (0, h)),
                  pl.BlockSpec((1, nb, 128), lambda h: (h, 0, 0))],
        out_specs=[pl.BlockSpec((8, s), lambda h: (0, h)),
                   pl.BlockSpec((1, 8, ntile), lambda h: (h, 0, 0))],
        out_shape=[jax.ShapeDtypeStruct((8, HEADS * s), jnp.int32),
                   jax.ShapeDtypeStruct((HEADS, 8, ntile), jnp.int32)],
        compiler_params=pltpu.CompilerParams(
            dimension_semantics=("parallel",), vmem_limit_bytes=VMEM_LIMIT),
        name="moba_dest",
    )(info, cnt)


def _sc_scatter_rows(rows, idx, n_out):
    m, c = rows.shape
    k = idx.shape[0]
    nwin = m // SC_WINDOW
    mesh = plsc.VectorSubcoreMesh(core_axis_name="core", subcore_axis_name="subcore")

    @pl.kernel(out_type=jax.ShapeDtypeStruct((n_out, c), rows.dtype), mesh=mesh)
    def kern(x_hbm, i_hbm, o_hbm):
        def body(x_vmem, i_vmem):
            pltpu.sync_copy(x_vmem, o_hbm.at[i_vmem.at[0]])

        pltpu.emit_pipeline(
            body, grid=(k * nwin,),
            in_specs=[pl.BlockSpec((SC_WINDOW, c), lambda i: (i % nwin, 0)),
                      pl.BlockSpec((1, SC_WINDOW), lambda i: (0, i))],
            out_specs=[],
            core_axis_name=("core", "subcore"),
            dimension_semantics=(pltpu.PARALLEL,),
        )(x_hbm, i_hbm)

    return kern(rows, idx.reshape(1, k * m))


def _sc_gather_rows(table, idx):
    m = idx.shape[0]
    c = table.shape[1]
    mesh = plsc.VectorSubcoreMesh(core_axis_name="core", subcore_axis_name="subcore")

    @pl.kernel(out_type=jax.ShapeDtypeStruct((m, c), table.dtype), mesh=mesh)
    def kern(x_hbm, i_hbm, o_hbm):
        def body(i_vmem, o_vmem):
            pltpu.sync_copy(x_hbm.at[i_vmem.at[0]], o_vmem)

        pltpu.emit_pipeline(
            body, grid=(m // SC_WINDOW,),
            in_specs=[pl.BlockSpec((1, SC_WINDOW), lambda i: (0, i))],
            out_specs=[pl.BlockSpec((SC_WINDOW, c), lambda i: (i, 0))],
            core_axis_name=("core", "subcore"),
            dimension_semantics=(pltpu.PARALLEL,),
        )(i_hbm, o_hbm)

    return kern(table, idx.reshape(1, m))


def _pack_partial_t(o_norm_t, lse):
    half = HEAD_DIM // 2
    u = lax.bitcast_convert_type(o_norm_t, jnp.uint32) + jnp.uint32(0x8000)
    word = (u[:half] & jnp.uint32(0xFFFF0000)) | (u[half:] >> 16)
    lse_bits = lax.bitcast_convert_type(jnp.broadcast_to(lse, word.shape), jnp.uint32)
    full = jnp.concatenate([word, lse_bits], axis=0)
    return lax.bitcast_convert_type(full, F32).T


def _unpack_partial(part):
    half = HEAD_DIM // 2
    lane = lax.broadcasted_iota(jnp.int32, part.shape, 1)
    swapped = pltpu.roll(part, half, 1)
    lse = jnp.where(lane < half, swapped, part)
    word = lax.bitcast_convert_type(jnp.where(lane < half, part, swapped), jnp.uint32)
    o = jnp.where(lane < half, lax.bitcast_convert_type(word & jnp.uint32(0xFFFF0000), F32),
                  lax.bitcast_convert_type(word << 16, F32))
    return o, lse


def _tiles_kernel(tb_ref, qs_ref, k_ref, vt_ref, after_ref, o_ref, *, ntile):
    del after_ref
    h = pl.program_id(0)
    g = pl.program_id(1)
    B = MOBA_BLOCK
    base = h * ntile + g * TILES_PER_STEP

    @pl.when(tb_ref[base] >= 0)
    def _():
        blocks = [jnp.maximum(tb_ref[base + u], 0) for u in range(TILES_PER_STEP)]
        s_t = [lax.dot_general(k_ref[pl.ds(pl.multiple_of(n * B, B), B), :],
                               qs_ref[u * B:(u + 1) * B, :].astype(BF16), NT,
                               preferred_element_type=F32) for u, n in enumerate(blocks)]
        m = [jnp.max(x, axis=0, keepdims=True) for x in s_t]
        p = [jnp.exp2(x - mm).astype(BF16) for x, mm in zip(s_t, m)]
        acc = [jnp.dot(vt_ref[0, n], pp, preferred_element_type=F32) for n, pp in zip(blocks, p)]
        for u in range(TILES_PER_STEP):
            l = acc[u][HEAD_DIM:HEAD_DIM + 1]
            o_ref[u * B:(u + 1) * B, :] = _pack_partial_t(acc[u][:HEAD_DIM] / l,
                                                          m[u] + jnp.log(l) * LOG2_E)


def _tiles(tb, qsorted, proj, v_t, after, s):
    nb = s // MOBA_BLOCK
    ntile = 4 * s // MOBA_BLOCK
    steps = ntile // TILES_PER_STEP
    rows = TILES_PER_STEP * MOBA_BLOCK
    grid_spec = pltpu.PrefetchScalarGridSpec(
        num_scalar_prefetch=1,
        grid=(HEADS, steps),
        in_specs=[pl.BlockSpec((rows, HEAD_DIM), lambda h, g, tb: (h * steps + g, 0)),
                  pl.BlockSpec((s, HEAD_DIM), lambda h, g, tb: (0, HEADS + h)),
                  pl.BlockSpec((1, nb, HEAD_DIM + ONES_ROWS, MOBA_BLOCK), lambda h, g, tb: (h, 0, 0, 0)),
                  pl.BlockSpec((16, HEAD_DIM), lambda h, g, tb: (0, 0))],
        out_specs=pl.BlockSpec((rows, HEAD_DIM), lambda h, g, tb: (h * steps + g, 0)),
    )
    return pl.pallas_call(
        functools.partial(_tiles_kernel, ntile=ntile),
        grid_spec=grid_spec,
        out_shape=jax.ShapeDtypeStruct(qsorted.shape, F32),
        compiler_params=pltpu.CompilerParams(
            dimension_semantics=("parallel", "arbitrary"), vmem_limit_bytes=VMEM_LIMIT),
        name="moba_tiles",
    )(tb, qsorted, proj, v_t, after)


def _merge_kernel(q_ref, k_ref, v_ref, part_ref, o_ref):
    j = pl.program_id(0)
    B = MOBA_BLOCK
    heads = [slice(hh * HEAD_DIM, (hh + 1) * HEAD_DIM) for hh in range(HEADS)]
    causal = (lax.broadcasted_iota(jnp.int32, (B, B), 1) <= lax.broadcasted_iota(jnp.int32, (B, B), 0))
    s = [lax.dot_general(q_ref[:, hs], k_ref[:, hs], NT, preferred_element_type=F32) for hs in heads]
    s = [jnp.where(causal, x, -jnp.inf) for x in s]
    m_own = [jnp.broadcast_to(jnp.max(x, axis=-1, keepdims=True), (B, HEAD_DIM)) for x in s]
    p = [jnp.exp2(x - jnp.concatenate([m, m], axis=1)) for x, m in zip(s, m_own)]
    ones = jnp.ones((B, HEAD_DIM), BF16)
    ol = [jnp.dot(x.astype(BF16), jnp.concatenate([v_ref[:, hs], ones], axis=1),
                  preferred_element_type=F32) for x, hs in zip(p, heads)]
    for hh, hs in enumerate(heads):
        o_own = ol[hh][:, :HEAD_DIM]
        l_own = ol[hh][:, HEAD_DIM:]
        lse_own = m_own[hh] + jnp.log(l_own) * LOG2_E
        parts = []
        for r in range(MOBA_TOPK):
            o_r, lse_r = _unpack_partial(part_ref[r, hh])
            valid = r < j
            parts.append((jnp.where(valid, o_r, 0.0), jnp.where(valid, lse_r, -jnp.inf)))
        m_all = lse_own
        for _, lse_r in parts:
            m_all = jnp.maximum(m_all, lse_r)
        w = jnp.exp2(lse_own - m_all)
        num = (w / l_own) * o_own
        den = w
        for o_r, lse_r in parts:
            w = jnp.exp2(lse_r - m_all)
            num = num + w * o_r
            den = den + w
        o_ref[:, hs] = (num / den).astype(BF16)


def _merge(proj, parts, s):
    nb = s // MOBA_BLOCK
    width = HEADS * HEAD_DIM
    return pl.pallas_call(
        _merge_kernel,
        grid=(nb,),
        in_specs=[pl.BlockSpec((MOBA_BLOCK, width), lambda j: (j, 0)),
                  pl.BlockSpec((MOBA_BLOCK, width), lambda j: (j, 1)),
                  pl.BlockSpec((MOBA_BLOCK, width), lambda j: (j, 2)),
                  pl.BlockSpec((MOBA_TOPK, HEADS, MOBA_BLOCK, HEAD_DIM), lambda j: (0, 0, j, 0))],
        out_specs=pl.BlockSpec((MOBA_BLOCK, width), lambda j: (j, 0)),
        out_shape=jax.ShapeDtypeStruct((s, width), BF16),
        compiler_params=pltpu.CompilerParams(
            dimension_semantics=("parallel",), vmem_limit_bytes=VMEM_LIMIT),
        name="moba_merge",
    )(proj, proj, proj, parts)


def _moba_sparse(proj, v_t, kmean, qf, run_during_scatter):
    s = proj.shape[0]
    cap = 4 * s
    n_rows = HEADS * cap + MOBA_BLOCK
    info, cnt = _route(proj, kmean)
    dest, tb = _dest(info, cnt, s)
    dest3 = dest[:MOBA_TOPK]
    qsorted = _sc_scatter_rows(qf.reshape(HEADS * s, HEAD_DIM), dest3, n_rows)
    after = run_during_scatter()
    osorted = _tiles(tb[:, 0, :].reshape(-1), qsorted, proj, v_t, after, s)
    parts = _sc_gather_rows(osorted, dest3.reshape(-1))
    return _merge(proj, parts.reshape(MOBA_TOPK, HEADS, s, HEAD_DIM), s)


def _delta_kernel(q_ref, k_ref, v_ref, qh_ref, kh_ref, vh_ref, z_ref, small_ref,
                  cwq_ref, cwk_ref, cwv_ref, alog_ref, dtb_ref, onw_ref,
                  o_ref, state_scr, xx_scr, *, hg):
    c = pl.program_id(0)
    C = DN_CHUNK

    @pl.when(c == 0)
    def _():
        state_scr[...] = jnp.zeros_like(state_scr)

    shifts = DN_CONV - 1
    r_out = lax.broadcasted_iota(jnp.int32, (shifts * C, C), 0)
    r_in = lax.broadcasted_iota(jnp.int32, (shifts * C, C), 1)
    delay = jnp.where((r_out & (C - 1)) - r_in == (r_out >> 8) + 1, 1.0, 0.0).astype(BF16)
    assert C == 256

    def conv_silu(idx, x_ref, halo_ref, cw_ref):
        x = x_ref[...]
        delayed = jnp.dot(delay, x, preferred_element_type=F32)
        halo = jnp.where(c == 0, 0.0, halo_ref[...].astype(F32))
        xx_scr[idx, 0:HALO, :] = halo
        xx_scr[idx, HALO:HALO + HALO, :] = x_ref[0:HALO, :].astype(F32)
        y = cw_ref[shifts:shifts + 1, :] * x.astype(F32)
        for j in range(1, shifts + 1):
            d_j = delayed[(j - 1) * C:j * C]
            head8 = xx_scr[idx, HALO - j:HALO - j + 8, :]
            d_j = jnp.concatenate([head8, d_j[8:]], axis=0)
            y = y + cw_ref[shifts - j:shifts - j + 1, :] * d_j
        return y * _sigmoid(y)

    q_all = conv_silu(0, q_ref, qh_ref, cwq_ref)
    k_all = conv_silu(1, k_ref, kh_ref, cwk_ref)
    v_all = conv_silu(2, v_ref, vh_ref, cwv_ref)

    small = small_ref[...]
    beta_all = _sigmoid(small)
    xs = small + dtb_ref[...]
    softplus = jnp.maximum(xs, 0.0) + jnp.log(1.0 + jnp.exp(-jnp.abs(xs)))
    g_all = -jnp.exp(alog_ref[...]) * softplus

    row = lax.broadcasted_iota(jnp.int32, (C, C), 0)
    col = lax.broadcasted_iota(jnp.int32, (C, C), 1)
    tril = row >= col
    strict = row > col
    rxc = row ^ col
    eye = (row == col).astype(F32)
    assert C == 2 * HEAD_DIM

    heads = range(DN_HEADS_PER_STEP)
    sls = [slice(hh * HEAD_DIM, (hh + 1) * HEAD_DIM) for hh in heads]

    def hmap(f, *lists):
        return [f(*xs) for xs in zip(*lists)]

    def pieces(x, n):
        out, rest = [], x
        for _ in range(n):
            p = rest.astype(BF16)
            out.append(p)
            rest = rest - p.astype(F32)
        return jnp.concatenate(out, axis=1)

    hp = DN_HEADS_PER_STEP
    same_half = jnp.where((row >> 7) == (col >> 7), 1.0, 0.0).astype(BF16)

    def unit_rows_pair(xq, xk):
        sq = jnp.concatenate([xq * xq, xk * xk], axis=1).astype(BF16)
        inv = lax.rsqrt(jnp.dot(sq, same_half, preferred_element_type=F32) + EPS)
        return xq * inv[:, :HEAD_DIM], xk * inv[:, HEAD_DIM:]

    def selector(first_lane, n):
        src = lax.broadcasted_iota(jnp.int32, (128, hp * HEAD_DIM), 0)
        dst = lax.broadcasted_iota(jnp.int32, (128, hp * HEAD_DIM), 1) >> 7
        sel = jnp.where(src == first_lane + dst, 1.0, 0.0).astype(BF16)
        return jnp.concatenate([sel] * n, axis=0)

    qk = [unit_rows_pair(q_all[:, sl], k_all[:, sl]) for sl in sls]
    q = [x[0] * (HEAD_DIM ** -0.5) for x in qk]
    k = [x[1] for x in qk]
    v = [v_all[:, sl] for sl in sls]
    beta_rep = jnp.dot(pieces(beta_all, 1), selector(hg * hp, 1), preferred_element_type=F32)
    beta = [beta_rep[:, sl] for sl in sls]

    cum3 = jnp.dot(jnp.where(tril, 1.0, 0.0).astype(BF16), pieces(g_all, 3), preferred_element_type=F32)
    gcum_all = cum3[:, :128] + cum3[:, 128:256] + cum3[:, 256:]
    gcum_rep = jnp.dot(pieces(gcum_all, 3), selector(hg * hp + HEADS, 3), preferred_element_type=F32)
    gcum_b = [gcum_rep[:, sl] for sl in sls]
    gr = [x.T[0:1, :] for x in gcum_b]
    decay = hmap(lambda c_, r_: jnp.where(
        tril, jnp.exp(jnp.where(tril, jnp.concatenate([c_, c_], axis=1) - r_, 0.0)), 0.0), gcum_b, gr)
    e_g = hmap(jnp.exp, gcum_b)
    g_last = [x[C - 1:C, :] for x in gcum_b]

    kb = hmap(lambda a_, b_: a_ * b_, k, beta)
    vb = hmap(lambda a_, b_: a_ * b_, v, beta)
    a = hmap(lambda kb_, q_, k_: lax.dot_general(
        jnp.concatenate([kb_, q_], axis=0).astype(BF16), k_.astype(BF16), NT,
        preferred_element_type=F32), kb, q, k)
    lmat = hmap(lambda a_, d_: jnp.where(strict, a_[:C] * d_, 0.0), a, decay)
    attn = hmap(lambda a_, d_: a_[C:] * d_, a, decay)

    d1 = hmap(lambda l_: jnp.where(rxc < 8, l_, 0.0), lmat)
    d2 = hmap(lambda x: _bdot(x, x), d1)
    d4 = hmap(lambda x: _bdot(x, x), d2)
    p1 = hmap(lambda x, y: _bdot(eye - x, eye + y), d1, d2)
    tmat = hmap(lambda x, y: _bdot(x, eye + y), p1, d4)

    def odd_rows(t, sz):
        return jnp.concatenate([t[b * 2 * sz + sz:(b + 1) * 2 * sz] for b in range(C // (2 * sz))], axis=0)

    def with_odd_rows(t, odd, sz):
        pieces = []
        for b in range(C // (2 * sz)):
            pieces += [t[b * 2 * sz:b * 2 * sz + sz], odd[b * sz:(b + 1) * sz]]
        return jnp.concatenate(pieces, axis=0)

    sz = 8
    while sz < C:
        off = hmap(lambda l_: jnp.where((rxc >= sz) & (rxc < 2 * sz), l_, 0.0), lmat)
        t_odd = hmap(lambda t_: odd_rows(t_, sz), tmat)
        x = hmap(_bdot, t_odd, off)
        x = hmap(_bdot, x, tmat)
        tmat = hmap(lambda t_, o_, x_: with_odd_rows(t_, o_ - x_, sz), tmat, t_odd, x)
        sz *= 2

    uw = hmap(lambda t_, vb_, kb_, e_: _bdot(t_, jnp.concatenate([vb_, kb_ * e_], axis=1)),
              tmat, vb, kb, e_g)
    state = [state_scr[hh] for hh in heads]
    wq = hmap(lambda uw_, q_, e_, s_: _bdot(jnp.concatenate([uw_[:, HEAD_DIM:], q_ * e_], axis=0), s_),
              uw, q, e_g, state)
    v_new = hmap(lambda uw_, wq_: uw_[:, :HEAD_DIM] - wq_[:C], uw, wq)
    av = hmap(_bdot, attn, v_new)
    kt = hmap(lambda k_, gl_, gc_: (k_ * jnp.exp(gl_ - gc_)).T, k, g_last, gcum_b)
    ds = hmap(_bdot, kt, v_new)
    for hh in heads:
        state_scr[hh] = state[hh] * jnp.exp(g_last[hh]) + ds[hh]
        o = wq[hh][C:] + av[hh]
        y = o * lax.rsqrt(jnp.mean(o * o, axis=-1, keepdims=True) + EPS) * onw_ref[...]
        z = z_ref[:, sls[hh]].astype(F32)
        o_ref[:, sls[hh]] = (y * (z * _sigmoid(z))).astype(BF16)


def _gated_deltanet(proj, small, conv_w, alog_v, dtb_v, onw, hg):
    s = proj.shape[0]
    nc = s // DN_CHUNK
    rb = DN_CHUNK // HALO
    hp = DN_HEADS_PER_STEP
    ng = HEADS // hp
    width = hp * HEAD_DIM

    def col(part):
        return pl.BlockSpec((DN_CHUNK, width), lambda c: (c, part * ng + hg))

    def halo(part):
        return pl.BlockSpec((HALO, width), lambda c: (jnp.maximum(c * rb - 1, 0), part * ng + hg))

    def cw(part):
        return pl.BlockSpec((DN_CONV, width), lambda c: (0, part * ng + hg))

    vec = pl.BlockSpec((1, 128), lambda c: (0, 0))
    return pl.pallas_call(
        functools.partial(_delta_kernel, hg=hg),
        grid=(nc,),
        in_specs=[col(3), col(4), col(5),
                  halo(3), halo(4), halo(5),
                  col(6),
                  pl.BlockSpec((DN_CHUNK, 128), lambda c: (c, 0)),
                  cw(0), cw(1), cw(2),
                  vec, vec, vec],
        out_specs=pl.BlockSpec((DN_CHUNK, width), lambda c: (c, 0)),
        out_shape=jax.ShapeDtypeStruct((s, width), BF16),
        scratch_shapes=[pltpu.VMEM((hp, HEAD_DIM, HEAD_DIM), F32),
                        pltpu.VMEM((3, HALO + DN_CHUNK, width), F32)],
        compiler_params=pltpu.CompilerParams(
            dimension_semantics=("arbitrary",), vmem_limit_bytes=VMEM_LIMIT),
        name="deltanet",
    )(proj, proj, proj, proj, proj, proj, proj, small, conv_w, conv_w, conv_w, alog_v, dtb_v, onw)


def _mix_out_kernel(ya_ref, *rest):
    ng = HEADS // DN_HEADS_PER_STEP
    yd_refs = rest[:ng]
    ga_ref, gd_ref, x_ref, wa_ref, wd_ref, wo_ref, nw_ref, o_ref = rest[ng:]
    width = DN_HEADS_PER_STEP * HEAD_DIM
    pa = jnp.dot(ya_ref[...], wa_ref[...], preferred_element_type=F32)
    pd = jnp.dot(yd_refs[0][...], wd_ref[0:width, :], preferred_element_type=F32)
    for g in range(1, ng):
        pd = pd + jnp.dot(yd_refs[g][...], wd_ref[g * width:(g + 1) * width, :],
                          preferred_element_type=F32)
    merged = _sigmoid(ga_ref[...].astype(F32)) * pa + _sigmoid(gd_ref[...].astype(F32)) * pd
    mo = jnp.dot(merged.astype(BF16), wo_ref[...], preferred_element_type=F32)
    y = mo * lax.rsqrt(jnp.mean(mo * mo, axis=-1, keepdims=True) + EPS) * nw_ref[...]
    o_ref[...] = x_ref[...] + y


def _mix_out(ya, yds, proj, x2, wa, wd, wo, nw, *, tm=512):
    s = x2.shape[0]
    row = lambda i: (i, 0)
    full = lambda i: (0, 0)
    wspec = pl.BlockSpec((D_MODEL, D_MODEL), full)
    return pl.pallas_call(
        _mix_out_kernel,
        grid=(s // tm,),
        in_specs=[pl.BlockSpec((tm, D_MODEL), row)]
                 + [pl.BlockSpec((tm, yd.shape[1]), row) for yd in yds]
                 + [pl.BlockSpec((tm, D_MODEL), lambda i: (i, 7)),
                  pl.BlockSpec((tm, D_MODEL), lambda i: (i, 8)),
                  pl.BlockSpec((tm, D_MODEL), row),
                  wspec, wspec, wspec, pl.BlockSpec((1, D_MODEL), full)],
        out_specs=pl.BlockSpec((tm, D_MODEL), row),
        out_shape=jax.ShapeDtypeStruct((s, D_MODEL), F32),
        compiler_params=pltpu.CompilerParams(
            dimension_semantics=("parallel",), vmem_limit_bytes=VMEM_LIMIT),
        name="mix_out",
    )(ya, *yds, proj, proj, x2, wa, wd, wo, nw)


def _ffn_kernel(x_ref, npre_ref, wg_ref, wu_ref, wd_ref, npost_ref, o_ref):
    x = x_ref[...]
    h = (x * lax.rsqrt(jnp.mean(x * x, axis=-1, keepdims=True) + EPS) * npre_ref[...]).astype(BF16)
    acc = jnp.zeros(x.shape, F32)
    for cc in range(D_FF // FF_CHUNK):
        sl = slice(cc * FF_CHUNK, (cc + 1) * FF_CHUNK)
        g = jnp.dot(h, wg_ref[:, sl], preferred_element_type=F32)
        u = jnp.dot(h, wu_ref[:, sl], preferred_element_type=F32)
        act = (g * _sigmoid(g) * u).astype(BF16)
        acc = acc + jnp.dot(act, wd_ref[sl, :], preferred_element_type=F32)
    y = acc * lax.rsqrt(jnp.mean(acc * acc, axis=-1, keepdims=True) + EPS) * npost_ref[...]
    o_ref[...] = x + y


def _ffn(x1, npre, wg, wu, wd, npost, *, tm=512):
    s = x1.shape[0]
    row = lambda i: (i, 0)
    full = lambda i: (0, 0)
    once = pl.Buffered(1)
    return pl.pallas_call(
        _ffn_kernel,
        grid=(s // tm,),
        in_specs=[pl.BlockSpec((tm, D_MODEL), row), pl.BlockSpec((1, D_MODEL), full),
                  pl.BlockSpec((D_MODEL, D_FF), full, pipeline_mode=once),
                  pl.BlockSpec((D_MODEL, D_FF), full, pipeline_mode=once),
                  pl.BlockSpec((D_FF, D_MODEL), full, pipeline_mode=once),
                  pl.BlockSpec((1, D_MODEL), full)],
        out_specs=pl.BlockSpec((tm, D_MODEL), row),
        out_shape=jax.ShapeDtypeStruct((s, D_MODEL), F32),
        compiler_params=pltpu.CompilerParams(
            dimension_semantics=("parallel",), vmem_limit_bytes=VMEM_LIMIT),
        name="ffn",
    )(x1, npre, wg, wu, wd, npost)


def _rope_table(s):
    half = ROPE_DIM // 2
    rep = 128 // half
    inv = ROPE_THETA ** (-jnp.arange(half, dtype=F32) * 2.0 / ROPE_DIM)
    pos = (jnp.arange(s // rep)[:, None] * rep + jnp.arange(128)[None, :] // half).astype(F32)
    ang = pos * jnp.tile(inv, rep)[None, :]
    cos, sin = lax.optimization_barrier((jnp.cos(ang), jnp.sin(ang)))
    cos = cos.reshape(s, half)
    sin = sin.reshape(s, half)
    return jnp.concatenate([cos, sin, jnp.zeros((s, HEAD_DIM - ROPE_DIM), F32)], axis=-1)


def _layer(x2, l, norm_mix_pre, w_in, conv_w, a_log, dt_bias, o_norm_w, w_o_attn, w_o_delta,
           w_out, norm_mix_post, norm_ffn_pre, w_gate, w_up, w_down, norm_ffn_post):
    s = x2.shape[0]
    nb = s // MOBA_BLOCK
    wide = 7 * D_MODEL
    nsmall = 2 * HEADS
    w_t = jnp.swapaxes(w_in[l], 0, 1).astype(BF16)

    proj, small, kmean, v_t, qf = _project(x2, norm_mix_pre[l][None, :], w_t, _rope_table(s),
                                           wide // D_MODEL, nsmall)
    pad = lambda vec: jnp.pad(vec.astype(F32), (HEADS, 128 - 2 * HEADS))[None, :]
    delta = lambda hg: _gated_deltanet(proj, small, conv_w[l], pad(a_log[l]), pad(dt_bias[l]),
                                       o_norm_w[l][None, :], hg)
    yds = []
    ya = _moba_sparse(proj, v_t, kmean.reshape(nb, D_MODEL), qf,
                      lambda: yds.append(delta(0)) or yds[0])
    yds += [delta(hg) for hg in range(1, HEADS // DN_HEADS_PER_STEP)]

    x1 = _mix_out(ya, yds, proj, x2, w_o_attn[l].astype(BF16), w_o_delta[l].astype(BF16),
                  w_out[l].astype(BF16), norm_mix_post[l][None, :])
    return _ffn(x1, norm_ffn_pre[l][None, :], w_gate[l].astype(BF16), w_up[l].astype(BF16),
                w_down[l].astype(BF16), norm_ffn_post[l][None, :])


def kernel(x, norm_mix_pre, w_in, conv_w, a_log, dt_bias, o_norm_w, w_o_attn, w_o_delta, w_out,
           norm_mix_post, norm_ffn_pre, w_gate, w_up, w_down, norm_ffn_post):
    b, s, d = x.shape
    assert d == D_MODEL and s % 512 == 0
    outs = []
    for bi in range(b):
        x2 = x.reshape(s, d) if b == 1 else x[bi]
        for l in range(w_in.shape[0]):
            x2 = _layer(x2, l, norm_mix_pre, w_in, conv_w, a_log, dt_bias, o_norm_w, w_o_attn,
                        w_o_delta, w_out, norm_mix_post, norm_ffn_pre, w_gate, w_up, w_down,
                        norm_ffn_post)
        outs.append(x2)
    return outs[0].reshape(1, s, d) if b == 1 else jnp.stack(outs, axis=0)
```

```python
import functools
import math

import jax
import jax.numpy as jnp
from jax import lax
from jax.experimental import pallas as pl
from jax.experimental.pallas import tpu as pltpu
from jax.experimental.pallas import tpu_sc as plsc

D_MODEL = 1024
HEADS = 8
HEAD_DIM = 128
MOBA_BLOCK = 256
MOBA_TOPK = 3
ROPE_DIM = HEAD_DIM // 4
ROPE_THETA = 500000.0
DN_CONV = 4
ONES_ROWS = 16
DN_HEADS_PER_STEP = 8
DN_CHUNK = 256
D_FF = 2816
FF_CHUNK = 256
EPS = 1e-6
LOG2_E = math.log2(math.e)
HALO = 16

F32 = jnp.float32
BF16 = jnp.bfloat16
NT = (((1,), (1,)), ((), ()))

VMEM_LIMIT = 56 * 1024 * 1024


def _bdot(a, b):
    return jnp.dot(a.astype(BF16), b.astype(BF16), preferred_element_type=F32)


def _sigmoid(x):
    return 1.0 / (1.0 + jnp.exp(-x))


def _proj_kernel(x_ref, nw_ref, wt_ref, rope_ref,
                 out_ref, small_ref, kmean_ref, vt_ref, qf_ref, *, tm, q_scale, n_wide, n_small):
    x = x_ref[...]
    ms = jnp.mean(x * x, axis=-1, keepdims=True)
    h = (x * lax.rsqrt(ms + EPS) * nw_ref[...]).astype(BF16)
    narrow = lax.dot_general(h, wt_ref[n_wide * D_MODEL:n_wide * D_MODEL + 128, :], NT,
                             preferred_element_type=F32)
    small_ref[...] = jnp.where(lax.broadcasted_iota(jnp.int32, narrow.shape, 1) < n_small, narrow, 0.0)
    heads = [slice(hh * HEAD_DIM, (hh + 1) * HEAD_DIM) for hh in range(HEADS)]
    groups = [slice(g * MOBA_BLOCK, (g + 1) * MOBA_BLOCK) for g in range(tm // MOBA_BLOCK)]

    half = ROPE_DIM // 2
    tab = rope_ref[...]
    lane = lax.broadcasted_iota(jnp.int32, tab.shape, 1)
    cos_t = jnp.where(lane < half, tab, jnp.where(lane < ROPE_DIM, pltpu.roll(tab, half, 1), 1.0))
    s1_t = jnp.where(lane < half, -pltpu.roll(tab, HEAD_DIM - half, 1), 0.0)
    s2_t = jnp.where((lane >= half) & (lane < ROPE_DIM), tab, 0.0)

    def roped(a):
        return (a * cos_t + pltpu.roll(a, HEAD_DIM - half, 1) * s1_t + pltpu.roll(a, half, 1) * s2_t)

    for c in range(out_ref.shape[1] // D_MODEL):
        cols = slice(c * D_MODEL, (c + 1) * D_MODEL)
        r0 = c * D_MODEL + (n_small if c >= n_wide else 0)
        acc = lax.dot_general(h, wt_ref[r0:r0 + D_MODEL, :], NT, preferred_element_type=F32)
        if c == 0:
            for hh, hs in enumerate(heads):
                r = roped(acc[:, hs]) * q_scale
                out_ref[:, hs] = r.astype(BF16)
                qf_ref[hh] = r
        elif c == 1:
            for hs in heads:
                r = roped(acc[:, hs])
                out_ref[:, D_MODEL + hs.start:D_MODEL + hs.stop] = r.astype(BF16)
                for g, gs in enumerate(groups):
                    kmean_ref[0, g:g + 1, hs] = jnp.sum(r[gs], axis=0, keepdims=True) * (1.0 / MOBA_BLOCK)
        elif c == 2:
            out_ref[:, cols] = acc.astype(BF16)
            for hh, hs in enumerate(heads):
                for g, gs in enumerate(groups):
                    vt_ref[hh, g, 0:HEAD_DIM, :] = acc[gs, hs].T.astype(BF16)
                    vt_ref[hh, g, HEAD_DIM:, :] = jnp.ones((ONES_ROWS, MOBA_BLOCK), BF16)
        else:
            out_ref[:, cols] = acc.astype(BF16)


def _project(x2, norm_w, w_t, rope_t, n_wide, n_small, *, tm=512):
    s = x2.shape[0]
    width = (w_t.shape[0] - n_small) // D_MODEL * D_MODEL
    once = pl.Buffered(1)
    nblk = tm // MOBA_BLOCK
    kern = functools.partial(_proj_kernel, tm=tm, q_scale=math.log2(math.e) / math.sqrt(HEAD_DIM),
                             n_wide=n_wide, n_small=n_small)
    row = lambda i: (i, 0)
    full = lambda i: (0, 0)
    return pl.pallas_call(
        kern,
        grid=(s // tm,),
        in_specs=[
            pl.BlockSpec((tm, D_MODEL), row),
            pl.BlockSpec((1, D_MODEL), full),
            pl.BlockSpec(w_t.shape, full, pipeline_mode=once),
            pl.BlockSpec((tm, HEAD_DIM), row),
        ],
        out_specs=[
            pl.BlockSpec((tm, width), row),
            pl.BlockSpec((tm, 128), row),
            pl.BlockSpec((1, nblk, D_MODEL), lambda i: (i, 0, 0)),
            pl.BlockSpec((HEADS, nblk, HEAD_DIM + ONES_ROWS, MOBA_BLOCK), lambda i: (0, i, 0, 0)),
            pl.BlockSpec((HEADS, tm, HEAD_DIM), lambda i: (0, i, 0)),
        ],
        out_shape=[
            jax.ShapeDtypeStruct((s, width), BF16),
            jax.ShapeDtypeStruct((s, 128), F32),
            jax.ShapeDtypeStruct((s // tm, nblk, D_MODEL), F32),
            jax.ShapeDtypeStruct((HEADS, s // MOBA_BLOCK, HEAD_DIM + ONES_ROWS, MOBA_BLOCK), BF16),
            jax.ShapeDtypeStruct((HEADS, s, HEAD_DIM), F32),
        ],
        compiler_params=pltpu.CompilerParams(
            dimension_semantics=("parallel",), vmem_limit_bytes=VMEM_LIMIT),
        name="proj",
    )(x2, norm_w, w_t, rope_t)


ROUTE_QBLOCKS = 8
TILES_PER_STEP = 32
SC_WINDOW = 128


def _top_blocks(gate, blk_f, nb):
    picks = []
    for _ in range(MOBA_TOPK):
        m = jnp.max(gate, axis=0, keepdims=True)
        first = jnp.min(jnp.where(gate == m, blk_f, float(nb)), axis=0, keepdims=True)
        pick = (blk_f == first) & (m > -jnp.inf)
        gate = jnp.where(pick, -jnp.inf, gate)
        picks.append(pick)
    return picks


def _route_kernel(q_ref, km_ref, info_ref, cnt_ref, run_scr, *, nb):
    jb = pl.program_id(1)
    L = ROUTE_QBLOCKS * MOBA_BLOCK

    @pl.when(jb == 0)
    def _():
        run_scr[...] = jnp.zeros_like(run_scr)

    km = km_ref[...]
    km_hi = km.astype(BF16)
    km_lo = (km - km_hi.astype(F32)).astype(BF16)
    gate2 = lax.dot_general(jnp.concatenate([km_hi, km_lo], axis=0), q_ref[...], NT,
                            preferred_element_type=F32)
    blk = lax.broadcasted_iota(jnp.int32, (nb, L), 0)
    blk_f = blk.astype(F32)
    qblk = jb * ROUTE_QBLOCKS + (lax.broadcasted_iota(jnp.int32, (nb, L), 1) >> 8)
    gate = jnp.where(blk < qblk, gate2[:nb] + gate2[nb:], -jnp.inf)
    picks = _top_blocks(gate, blk_f, nb)

    chosen = jnp.where(picks[0] | picks[1] | picks[2], 1.0, 0.0)
    B = MOBA_BLOCK
    before = jnp.where(lax.broadcasted_iota(jnp.int32, (B, B), 0)
                       < lax.broadcasted_iota(jnp.int32, (B, B), 1), 1.0, 0.0).astype(BF16)
    carry = run_scr[:, 0:1]
    base = []
    for b in range(ROUTE_QBLOCKS):
        c_b = chosen[:, b * B:(b + 1) * B]
        within = jnp.dot(c_b.astype(BF16), before, preferred_element_type=F32)
        base.append(carry + within)
        carry = carry + within[:, B - 1:B] + c_b[:, B - 1:B]
    base = jnp.concatenate(base, axis=1)

    rows = []
    for pick in picks:
        rows.append(jnp.sum(jnp.where(pick, base, 0.0), axis=0, keepdims=True))
    for pick in picks:
        bid = jnp.sum(jnp.where(pick, blk_f, 0.0), axis=0, keepdims=True)
        valid = jnp.sum(jnp.where(pick, 1.0, 0.0), axis=0, keepdims=True)
        rows.append(jnp.where(valid > 0.0, bid, float(nb)))
    rows += [jnp.zeros((1, L), F32)] * (8 - len(rows))
    info_ref[...] = jnp.concatenate(rows, axis=0).astype(jnp.int32)

    run = jnp.broadcast_to(carry, run_scr.shape)
    run_scr[...] = run
    cnt_ref[0] = run


def _route(proj, kmean):
    s = proj.shape[0]
    nb = s // MOBA_BLOCK
    L = ROUTE_QBLOCKS * MOBA_BLOCK
    steps = s // L
    return pl.pallas_call(
        functools.partial(_route_kernel, nb=nb),
        grid=(HEADS, steps),
        in_specs=[pl.BlockSpec((L, HEAD_DIM), lambda h, j: (j, h)),
                  pl.BlockSpec((nb, HEAD_DIM), lambda h, j: (0, h))],
        out_specs=[pl.BlockSpec((8, L), lambda h, j: (0, h * steps + j)),
                   pl.BlockSpec((1, nb, 128), lambda h, j: (h, 0, 0))],
        out_shape=[jax.ShapeDtypeStruct((8, HEADS * s), jnp.int32),
                   jax.ShapeDtypeStruct((HEADS, nb, 128), F32)],
        scratch_shapes=[pltpu.VMEM((nb, 128), F32)],
        compiler_params=pltpu.CompilerParams(
            dimension_semantics=("parallel", "arbitrary"), vmem_limit_bytes=VMEM_LIMIT),
        name="moba_route",
    )(proj, kmean)


def _dest_kernel(info_ref, cnt_ref, dest_ref, tb_ref, *, nb, s):
    h = pl.program_id(0)
    cap = 4 * s
    ntile = cap // MOBA_BLOCK
    cnt = cnt_ref[0]
    tiles = jnp.floor((cnt + float(MOBA_BLOCK - 1)) * (1.0 / MOBA_BLOCK))
    lower = jnp.where(lax.broadcasted_iota(jnp.int32, (nb, nb), 1)
                      < lax.broadcasted_iota(jnp.int32, (nb, nb), 0), 1.0, 0.0).astype(BF16)
    start = jnp.dot(lower, tiles.astype(BF16), preferred_element_type=F32)
    start1 = start[:, 0:1]
    tiles1 = tiles[:, 0:1]

    t_f = lax.broadcasted_iota(jnp.int32, (nb, ntile), 1).astype(F32)
    n_f = lax.broadcasted_iota(jnp.int32, (nb, ntile), 0).astype(F32)
    inside = (t_f >= start1) & (t_f < start1 + tiles1)
    tb = jnp.sum(jnp.where(inside, n_f, 0.0), axis=0, keepdims=True)
    used = jnp.sum(jnp.where(inside, 1.0, 0.0), axis=0, keepdims=True)
    tb = jnp.where(used > 0.0, tb, -1.0)
    tb_ref[0] = jnp.concatenate([tb, jnp.full((7, ntile), -1.0, F32)], axis=0).astype(jnp.int32)

    off1 = start1 * float(MOBA_BLOCK)
    CH = 2048
    blk_f = lax.broadcasted_iota(jnp.int32, (nb, CH), 0).astype(F32)
    lane = lax.broadcasted_iota(jnp.int32, (1, CH), 1)
    trash = HEADS * cap + (lane & (MOBA_BLOCK - 1))
    for ch in range(s // CH):
        sl = slice(ch * CH, (ch + 1) * CH)
        rows = []
        for r in range(MOBA_TOPK):
            rank = info_ref[r:r + 1, sl]
            bid = info_ref[MOBA_TOPK + r:MOBA_TOPK + r + 1, sl]
            off = jnp.sum(jnp.where(blk_f == bid.astype(F32), off1, 0.0), axis=0, keepdims=True)
            rows.append(jnp.where(bid < nb, h * cap + off.astype(jnp.int32) + rank, trash))
        rows += [jnp.zeros((1, CH), jnp.int32)] * (8 - len(rows))
        dest_ref[:, sl] = jnp.concatenate(rows, axis=0)


def _dest(info, cnt, s):
    nb = s // MOBA_BLOCK
    ntile = 4 * s // MOBA_BLOCK
    return pl.pallas_call(
        functools.partial(_dest_kernel, nb=nb, s=s),
        grid=(HEADS,),
        in_specs=[pl.BlockSpec((8, s), lambda h: (0, h)),
                  pl.BlockSpec((1, nb, 128), lambda h: (h, 0, 0))],
        out_specs=[pl.BlockSpec((8, s), lambda h: (0, h)),
                   pl.BlockSpec((1, 8, ntile), lambda h: (h, 0, 0))],
        out_shape=[jax.ShapeDtypeStruct((8, HEADS * s), jnp.int32),
                   jax.ShapeDtypeStruct((HEADS, 8, ntile), jnp.int32)],
        compiler_params=pltpu.CompilerParams(
            dimension_semantics=("parallel",), vmem_limit_bytes=VMEM_LIMIT),
        name="moba_dest",
    )(info, cnt)


def _sc_scatter_rows(rows, idx, n_out):
    m, c = rows.shape
    k = idx.shape[0]
    nwin = m // SC_WINDOW
    mesh = plsc.VectorSubcoreMesh(core_axis_name="core", subcore_axis_name="subcore")

    @pl.kernel(out_type=jax.ShapeDtypeStruct((n_out, c), rows.dtype), mesh=mesh)
    def kern(x_hbm, i_hbm, o_hbm):
        def body(x_vmem, i_vmem):
            pltpu.sync_copy(x_vmem, o_hbm.at[i_vmem.at[0]])

        pltpu.emit_pipeline(
            body, grid=(k * nwin,),
            in_specs=[pl.BlockSpec((SC_WINDOW, c), lambda i: (i % nwin, 0)),
                      pl.BlockSpec((1, SC_WINDOW), lambda i: (0, i))],
            out_specs=[],
            core_axis_name=("core", "subcore"),
            dimension_semantics=(pltpu.PARALLEL,),
        )(x_hbm, i_hbm)

    return kern(rows, idx.reshape(1, k * m))


def _sc_gather_rows(table, idx):
    m = idx.shape[0]
    c = table.shape[1]
    mesh = plsc.VectorSubcoreMesh(core_axis_name="core", subcore_axis_name="subcore")

    @pl.kernel(out_type=jax.ShapeDtypeStruct((m, c), table.dtype), mesh=mesh)
    def kern(x_hbm, i_hbm, o_hbm):
        def body(i_vmem, o_vmem):
            pltpu.sync_copy(x_hbm.at[i_vmem.at[0]], o_vmem)

        pltpu.emit_pipeline(
            body, grid=(m // SC_WINDOW,),
            in_specs=[pl.BlockSpec((1, SC_WINDOW), lambda i: (0, i))],
            out_specs=[pl.BlockSpec((SC_WINDOW, c), lambda i: (i, 0))],
            core_axis_name=("core", "subcore"),
            dimension_semantics=(pltpu.PARALLEL,),
        )(i_hbm, o_hbm)

    return kern(table, idx.reshape(1, m))


def _pack_partial_t(o_norm_t, lse):
    half = HEAD_DIM // 2
    u = lax.bitcast_convert_type(o_norm_t, jnp.uint32) + jnp.uint32(0x8000)
    word = (u[:half] & jnp.uint32(0xFFFF0000)) | (u[half:] >> 16)
    lse_bits = lax.bitcast_convert_type(jnp.broadcast_to(lse, word.shape), jnp.uint32)
    full = jnp.concatenate([word, lse_bits], axis=0)
    return lax.bitcast_convert_type(full, F32).T


def _unpack_partial(part):
    half = HEAD_DIM // 2
    lane = lax.broadcasted_iota(jnp.int32, part.shape, 1)
    swapped = pltpu.roll(part, half, 1)
    lse = jnp.where(lane < half, swapped, part)
    word = lax.bitcast_convert_type(jnp.where(lane < half, part, swapped), jnp.uint32)
    o = jnp.where(lane < half, lax.bitcast_convert_type(word & jnp.uint32(0xFFFF0000), F32),
                  lax.bitcast_convert_type(word << 16, F32))
    return o, lse


def _tiles_kernel(tb_ref, qs_ref, k_ref, vt_ref, after_ref, o_ref, *, ntile):
    del after_ref
    h = pl.program_id(0)
    g = pl.program_id(1)
    B = MOBA_BLOCK
    base = h * ntile + g * TILES_PER_STEP

    @pl.when(tb_ref[base] >= 0)
    def _():
        blocks = [jnp.maximum(tb_ref[base + u], 0) for u in range(TILES_PER_STEP)]
        s_t = [lax.dot_general(k_ref[pl.ds(pl.multiple_of(n * B, B), B), :],
                               qs_ref[u * B:(u + 1) * B, :].astype(BF16), NT,
                               preferred_element_type=F32) for u, n in enumerate(blocks)]
        m = [jnp.max(x, axis=0, keepdims=True) for x in s_t]
        p = [jnp.exp2(x - mm).astype(BF16) for x, mm in zip(s_t, m)]
        acc = [jnp.dot(vt_ref[0, n], pp, preferred_element_type=F32) for n, pp in zip(blocks, p)]
        for u in range(TILES_PER_STEP):
            l = acc[u][HEAD_DIM:HEAD_DIM + 1]
            o_ref[u * B:(u + 1) * B, :] = _pack_partial_t(acc[u][:HEAD_DIM] / l,
                                                          m[u] + jnp.log(l) * LOG2_E)


def _tiles(tb, qsorted, proj, v_t, after, s):
    nb = s // MOBA_BLOCK
    ntile = 4 * s // MOBA_BLOCK
    steps = ntile // TILES_PER_STEP
    rows = TILES_PER_STEP * MOBA_BLOCK
    grid_spec = pltpu.PrefetchScalarGridSpec(
        num_scalar_prefetch=1,
        grid=(HEADS, steps),
        in_specs=[pl.BlockSpec((rows, HEAD_DIM), lambda h, g, tb: (h * steps + g, 0)),
                  pl.BlockSpec((s, HEAD_DIM), lambda h, g, tb: (0, HEADS + h)),
                  pl.BlockSpec((1, nb, HEAD_DIM + ONES_ROWS, MOBA_BLOCK), lambda h, g, tb: (h, 0, 0, 0)),
                  pl.BlockSpec((16, HEAD_DIM), lambda h, g, tb: (0, 0))],
        out_specs=pl.BlockSpec((rows, HEAD_DIM), lambda h, g, tb: (h * steps + g, 0)),
    )
    return pl.pallas_call(
        functools.partial(_tiles_kernel, ntile=ntile),
        grid_spec=grid_spec,
        out_shape=jax.ShapeDtypeStruct(qsorted.shape, F32),
        compiler_params=pltpu.CompilerParams(
            dimension_semantics=("parallel", "arbitrary"), vmem_limit_bytes=VMEM_LIMIT),
        name="moba_tiles",
    )(tb, qsorted, proj, v_t, after)


def _merge_kernel(q_ref, k_ref, v_ref, part_ref, o_ref):
    j = pl.program_id(0)
    B = MOBA_BLOCK
    heads = [slice(hh * HEAD_DIM, (hh + 1) * HEAD_DIM) for hh in range(HEADS)]
    causal = (lax.broadcasted_iota(jnp.int32, (B, B), 1) <= lax.broadcasted_iota(jnp.int32, (B, B), 0))
    s = [lax.dot_general(q_ref[:, hs], k_ref[:, hs], NT, preferred_element_type=F32) for hs in heads]
    s = [jnp.where(causal, x, -jnp.inf) for x in s]
    m_own = [jnp.broadcast_to(jnp.max(x, axis=-1, keepdims=True), (B, HEAD_DIM)) for x in s]
    p = [jnp.exp2(x - jnp.concatenate([m, m], axis=1)) for x, m in zip(s, m_own)]
    ones = jnp.ones((B, HEAD_DIM), BF16)
    ol = [jnp.dot(x.astype(BF16), jnp.concatenate([v_ref[:, hs], ones], axis=1),
                  preferred_element_type=F32) for x, hs in zip(p, heads)]
    for hh, hs in enumerate(heads):
        o_own = ol[hh][:, :HEAD_DIM]
        l_own = ol[hh][:, HEAD_DIM:]
        lse_own = m_own[hh] + jnp.log(l_own) * LOG2_E
        parts = []
        for r in range(MOBA_TOPK):
            o_r, lse_r = _unpack_partial(part_ref[r, hh])
            valid = r < j
            parts.append((jnp.where(valid, o_r, 0.0), jnp.where(valid, lse_r, -jnp.inf)))
        m_all = lse_own
        for _, lse_r in parts:
            m_all = jnp.maximum(m_all, lse_r)
        w = jnp.exp2(lse_own - m_all)
        num = (w / l_own) * o_own
        den = w
        for o_r, lse_r in parts:
            w = jnp.exp2(lse_r - m_all)
            num = num + w * o_r
            den = den + w
        o_ref[:, hs] = (num / den).astype(BF16)


def _merge(proj, parts, s):
    nb = s // MOBA_BLOCK
    width = HEADS * HEAD_DIM
    return pl.pallas_call(
        _merge_kernel,
        grid=(nb,),
        in_specs=[pl.BlockSpec((MOBA_BLOCK, width), lambda j: (j, 0)),
                  pl.BlockSpec((MOBA_BLOCK, width), lambda j: (j, 1)),
                  pl.BlockSpec((MOBA_BLOCK, width), lambda j: (j, 2)),
                  pl.BlockSpec((MOBA_TOPK, HEADS, MOBA_BLOCK, HEAD_DIM), lambda j: (0, 0, j, 0))],
        out_specs=pl.BlockSpec((MOBA_BLOCK, width), lambda j: (j, 0)),
        out_shape=jax.ShapeDtypeStruct((s, width), BF16),
        compiler_params=pltpu.CompilerParams(
            dimension_semantics=("parallel",), vmem_limit_bytes=VMEM_LIMIT),
        name="moba_merge",
    )(proj, proj, proj, parts)


def _moba_sparse(proj, v_t, kmean, qf, run_during_scatter, run_during_gather):
    s = proj.shape[0]
    cap = 4 * s
    n_rows = HEADS * cap + MOBA_BLOCK
    info, cnt = _route(proj, kmean)
    dest, tb = _dest(info, cnt, s)
    dest3 = dest[:MOBA_TOPK]
    qsorted = _sc_scatter_rows(qf.reshape(HEADS * s, HEAD_DIM), dest3, n_rows)
    after = run_during_scatter()
    osorted = _tiles(tb[:, 0, :].reshape(-1), qsorted, proj, v_t, after, s)
    parts = _sc_gather_rows(osorted, dest3.reshape(-1))
    run_during_gather(osorted)
    return _merge(proj, parts.reshape(MOBA_TOPK, HEADS, s, HEAD_DIM), s)


def _delta_kernel(q_ref, k_ref, v_ref, qh_ref, kh_ref, vh_ref, z_ref, small_ref,
                  cwq_ref, cwk_ref, cwv_ref, alog_ref, dtb_ref, onw_ref, st_in_ref, *rest, hg, c0):
    o_ref, st_out_ref, state_scr, xx_scr = rest[-4:]
    c = c0 + pl.program_id(0)
    C = DN_CHUNK

    @pl.when(pl.program_id(0) == 0)
    def _():
        state_scr[...] = st_in_ref[...]

    shifts = DN_CONV - 1
    r_out = lax.broadcasted_iota(jnp.int32, (shifts * C, C), 0)
    r_in = lax.broadcasted_iota(jnp.int32, (shifts * C, C), 1)
    delay = jnp.where((r_out & (C - 1)) - r_in == (r_out >> 8) + 1, 1.0, 0.0).astype(BF16)
    assert C == 256

    def conv_silu(idx, x_ref, halo_ref, cw_ref):
        x = x_ref[...]
        delayed = jnp.dot(delay, x, preferred_element_type=F32)
        halo = jnp.where(c == 0, 0.0, halo_ref[...].astype(F32))
        xx_scr[idx, 0:HALO, :] = halo
        xx_scr[idx, HALO:HALO + HALO, :] = x_ref[0:HALO, :].astype(F32)
        y = cw_ref[shifts:shifts + 1, :] * x.astype(F32)
        for j in range(1, shifts + 1):
            d_j = delayed[(j - 1) * C:j * C]
            head8 = xx_scr[idx, HALO - j:HALO - j + 8, :]
            d_j = jnp.concatenate([head8, d_j[8:]], axis=0)
            y = y + cw_ref[shifts - j:shifts - j + 1, :] * d_j
        return y * _sigmoid(y)

    q_all = conv_silu(0, q_ref, qh_ref, cwq_ref)
    k_all = conv_silu(1, k_ref, kh_ref, cwk_ref)
    v_all = conv_silu(2, v_ref, vh_ref, cwv_ref)

    small = small_ref[...]
    beta_all = _sigmoid(small)
    xs = small + dtb_ref[...]
    softplus = jnp.maximum(xs, 0.0) + jnp.log(1.0 + jnp.exp(-jnp.abs(xs)))
    g_all = -jnp.exp(alog_ref[...]) * softplus

    row = lax.broadcasted_iota(jnp.int32, (C, C), 0)
    col = lax.broadcasted_iota(jnp.int32, (C, C), 1)
    tril = row >= col
    strict = row > col
    rxc = row ^ col
    eye = (row == col).astype(F32)
    assert C == 2 * HEAD_DIM

    heads = range(DN_HEADS_PER_STEP)
    sls = [slice(hh * HEAD_DIM, (hh + 1) * HEAD_DIM) for hh in heads]

    def hmap(f, *lists):
        return [f(*xs) for xs in zip(*lists)]

    def pieces(x, n):
        out, rest = [], x
        for _ in range(n):
            p = rest.astype(BF16)
            out.append(p)
            rest = rest - p.astype(F32)
        return jnp.concatenate(out, axis=1)

    hp = DN_HEADS_PER_STEP
    same_half = jnp.where((row >> 7) == (col >> 7), 1.0, 0.0).astype(BF16)

    def unit_rows_pair(xq, xk):
        sq = jnp.concatenate([xq * xq, xk * xk], axis=1).astype(BF16)
        inv = lax.rsqrt(jnp.dot(sq, same_half, preferred_element_type=F32) + EPS)
        return xq * inv[:, :HEAD_DIM], xk * inv[:, HEAD_DIM:]

    def selector(first_lane, n):
        src = lax.broadcasted_iota(jnp.int32, (128, hp * HEAD_DIM), 0)
        dst = lax.broadcasted_iota(jnp.int32, (128, hp * HEAD_DIM), 1) >> 7
        sel = jnp.where(src == first_lane + dst, 1.0, 0.0).astype(BF16)
        return jnp.concatenate([sel] * n, axis=0)

    qk = [unit_rows_pair(q_all[:, sl], k_all[:, sl]) for sl in sls]
    q = [x[0] * (HEAD_DIM ** -0.5) for x in qk]
    k = [x[1] for x in qk]
    v = [v_all[:, sl] for sl in sls]
    beta_rep = jnp.dot(pieces(beta_all, 1), selector(hg * hp, 1), preferred_element_type=F32)
    beta = [beta_rep[:, sl] for sl in sls]

    cum3 = jnp.dot(jnp.where(tril, 1.0, 0.0).astype(BF16), pieces(g_all, 3), preferred_element_type=F32)
    gcum_all = cum3[:, :128] + cum3[:, 128:256] + cum3[:, 256:]
    gcum_rep = jnp.dot(pieces(gcum_all, 3), selector(hg * hp + HEADS, 3), preferred_element_type=F32)
    gcum_b = [gcum_rep[:, sl] for sl in sls]
    gr = [x.T[0:1, :] for x in gcum_b]
    decay = hmap(lambda c_, r_: jnp.where(
        tril, jnp.exp(jnp.where(tril, jnp.concatenate([c_, c_], axis=1) - r_, 0.0)), 0.0), gcum_b, gr)
    e_g = hmap(jnp.exp, gcum_b)
    g_last = [x[C - 1:C, :] for x in gcum_b]

    kb = hmap(lambda a_, b_: a_ * b_, k, beta)
    vb = hmap(lambda a_, b_: a_ * b_, v, beta)
    a = hmap(lambda kb_, q_, k_: lax.dot_general(
        jnp.concatenate([kb_, q_], axis=0).astype(BF16), k_.astype(BF16), NT,
        preferred_element_type=F32), kb, q, k)
    lmat = hmap(lambda a_, d_: jnp.where(strict, a_[:C] * d_, 0.0), a, decay)
    attn = hmap(lambda a_, d_: a_[C:] * d_, a, decay)

    d1 = hmap(lambda l_: jnp.where(rxc < 8, l_, 0.0), lmat)
    d2 = hmap(lambda x: _bdot(x, x), d1)
    d4 = hmap(lambda x: _bdot(x, x), d2)
    p1 = hmap(lambda x, y: _bdot(eye - x, eye + y), d1, d2)
    tmat = hmap(lambda x, y: _bdot(x, eye + y), p1, d4)

    def odd_rows(t, sz):
        return jnp.concatenate([t[b * 2 * sz + sz:(b + 1) * 2 * sz] for b in range(C // (2 * sz))], axis=0)

    def with_odd_rows(t, odd, sz):
        pieces = []
        for b in range(C // (2 * sz)):
            pieces += [t[b * 2 * sz:b * 2 * sz + sz], odd[b * sz:(b + 1) * sz]]
        return jnp.concatenate(pieces, axis=0)

    sz = 8
    while sz < C:
        off = hmap(lambda l_: jnp.where((rxc >= sz) & (rxc < 2 * sz), l_, 0.0), lmat)
        t_odd = hmap(lambda t_: odd_rows(t_, sz), tmat)
        x = hmap(_bdot, t_odd, off)
        x = hmap(_bdot, x, tmat)
        tmat = hmap(lambda t_, o_, x_: with_odd_rows(t_, o_ - x_, sz), tmat, t_odd, x)
        sz *= 2

    uw = hmap(lambda t_, vb_, kb_, e_: _bdot(t_, jnp.concatenate([vb_, kb_ * e_], axis=1)),
              tmat, vb, kb, e_g)
    state = [state_scr[hh] for hh in heads]
    wq = hmap(lambda uw_, q_, e_, s_: _bdot(jnp.concatenate([uw_[:, HEAD_DIM:], q_ * e_], axis=0), s_),
              uw, q, e_g, state)
    v_new = hmap(lambda uw_, wq_: uw_[:, :HEAD_DIM] - wq_[:C], uw, wq)
    av = hmap(_bdot, attn, v_new)
    kt = hmap(lambda k_, gl_, gc_: (k_ * jnp.exp(gl_ - gc_)).T, k, g_last, gcum_b)
    ds = hmap(_bdot, kt, v_new)
    for hh in heads:
        new_state = state[hh] * jnp.exp(g_last[hh]) + ds[hh]
        state_scr[hh] = new_state
        st_out_ref[hh] = new_state
        o = wq[hh][C:] + av[hh]
        y = o * lax.rsqrt(jnp.mean(o * o, axis=-1, keepdims=True) + EPS) * onw_ref[...]
        z = z_ref[:, sls[hh]].astype(F32)
        o_ref[:, sls[hh]] = (y * (z * _sigmoid(z))).astype(BF16)


def _gated_deltanet(proj, small, conv_w, alog_v, dtb_v, onw, state, c0, n_chunks, yd_prev=None, after=None):
    s = proj.shape[0]
    rb = DN_CHUNK // HALO
    hp = DN_HEADS_PER_STEP
    assert hp == HEADS
    width = hp * HEAD_DIM

    def col(part):
        return pl.BlockSpec((DN_CHUNK, width), lambda c: (c0 + c, part))

    def halo(part):
        return pl.BlockSpec((HALO, width), lambda c: (jnp.maximum((c0 + c) * rb - 1, 0), part))

    def cw(part):
        return pl.BlockSpec((DN_CONV, width), lambda c: (0, part))

    vec = pl.BlockSpec((1, 128), lambda c: (0, 0))
    st_spec = pl.BlockSpec((hp, HEAD_DIM, HEAD_DIM), lambda c: (0, 0, 0))
    extras = [a for a in (yd_prev, after) if a is not None]
    n_fixed = 15
    return pl.pallas_call(
        functools.partial(_delta_kernel, hg=0, c0=c0),
        grid=(n_chunks,),
        in_specs=[col(3), col(4), col(5),
                  halo(3), halo(4), halo(5),
                  col(6),
                  pl.BlockSpec((DN_CHUNK, 128), lambda c: (c0 + c, 0)),
                  cw(0), cw(1), cw(2),
                  vec, vec, vec, st_spec] + [pl.BlockSpec(memory_space=pl.ANY)] * len(extras),
        out_specs=[pl.BlockSpec((DN_CHUNK, width), lambda c: (c0 + c, 0)), st_spec],
        out_shape=[jax.ShapeDtypeStruct((s, width), BF16),
                   jax.ShapeDtypeStruct((hp, HEAD_DIM, HEAD_DIM), F32)],
        input_output_aliases={n_fixed: 0} if yd_prev is not None else {},
        scratch_shapes=[pltpu.VMEM((hp, HEAD_DIM, HEAD_DIM), F32),
                        pltpu.VMEM((3, HALO + DN_CHUNK, width), F32)],
        compiler_params=pltpu.CompilerParams(
            dimension_semantics=("arbitrary",), vmem_limit_bytes=VMEM_LIMIT),
        name="deltanet",
    )(proj, proj, proj, proj, proj, proj, proj, small, conv_w, conv_w, conv_w, alog_v, dtb_v, onw,
      state, *extras)


def _mix_out_kernel(ya_ref, *rest):
    ng = HEADS // DN_HEADS_PER_STEP
    yd_refs = rest[:ng]
    ga_ref, gd_ref, x_ref, wa_ref, wd_ref, wo_ref, nw_ref, o_ref = rest[ng:]
    width = DN_HEADS_PER_STEP * HEAD_DIM
    pa = jnp.dot(ya_ref[...], wa_ref[...], preferred_element_type=F32)
    pd = jnp.dot(yd_refs[0][...], wd_ref[0:width, :], preferred_element_type=F32)
    for g in range(1, ng):
        pd = pd + jnp.dot(yd_refs[g][...], wd_ref[g * width:(g + 1) * width, :],
                          preferred_element_type=F32)
    merged = _sigmoid(ga_ref[...].astype(F32)) * pa + _sigmoid(gd_ref[...].astype(F32)) * pd
    mo = jnp.dot(merged.astype(BF16), wo_ref[...], preferred_element_type=F32)
    y = mo * lax.rsqrt(jnp.mean(mo * mo, axis=-1, keepdims=True) + EPS) * nw_ref[...]
    o_ref[...] = x_ref[...] + y


def _mix_out(ya, yds, proj, x2, wa, wd, wo, nw, *, tm=512):
    s = x2.shape[0]
    row = lambda i: (i, 0)
    full = lambda i: (0, 0)
    wspec = pl.BlockSpec((D_MODEL, D_MODEL), full)
    return pl.pallas_call(
        _mix_out_kernel,
        grid=(s // tm,),
        in_specs=[pl.BlockSpec((tm, D_MODEL), row)]
                 + [pl.BlockSpec((tm, yd.shape[1]), row) for yd in yds]
                 + [pl.BlockSpec((tm, D_MODEL), lambda i: (i, 7)),
                  pl.BlockSpec((tm, D_MODEL), lambda i: (i, 8)),
                  pl.BlockSpec((tm, D_MODEL), row),
                  wspec, wspec, wspec, pl.BlockSpec((1, D_MODEL), full)],
        out_specs=pl.BlockSpec((tm, D_MODEL), row),
        out_shape=jax.ShapeDtypeStruct((s, D_MODEL), F32),
        compiler_params=pltpu.CompilerParams(
            dimension_semantics=("parallel",), vmem_limit_bytes=VMEM_LIMIT),
        name="mix_out",
    )(ya, *yds, proj, proj, x2, wa, wd, wo, nw)


def _ffn_kernel(x_ref, npre_ref, wg_ref, wu_ref, wd_ref, npost_ref, o_ref):
    x = x_ref[...]
    h = (x * lax.rsqrt(jnp.mean(x * x, axis=-1, keepdims=True) + EPS) * npre_ref[...]).astype(BF16)
    acc = jnp.zeros(x.shape, F32)
    for cc in range(D_FF // FF_CHUNK):
        sl = slice(cc * FF_CHUNK, (cc + 1) * FF_CHUNK)
        g = jnp.dot(h, wg_ref[:, sl], preferred_element_type=F32)
        u = jnp.dot(h, wu_ref[:, sl], preferred_element_type=F32)
        act = (g * _sigmoid(g) * u).astype(BF16)
        acc = acc + jnp.dot(act, wd_ref[sl, :], preferred_element_type=F32)
    y = acc * lax.rsqrt(jnp.mean(acc * acc, axis=-1, keepdims=True) + EPS) * npost_ref[...]
    o_ref[...] = x + y


def _ffn(x1, npre, wg, wu, wd, npost, *, tm=512):
    s = x1.shape[0]
    row = lambda i: (i, 0)
    full = lambda i: (0, 0)
    once = pl.Buffered(1)
    return pl.pallas_call(
        _ffn_kernel,
        grid=(s // tm,),
        in_specs=[pl.BlockSpec((tm, D_MODEL), row), pl.BlockSpec((1, D_MODEL), full),
                  pl.BlockSpec((D_MODEL, D_FF), full, pipeline_mode=once),
                  pl.BlockSpec((D_MODEL, D_FF), full, pipeline_mode=once),
                  pl.BlockSpec((D_FF, D_MODEL), full, pipeline_mode=once),
                  pl.BlockSpec((1, D_MODEL), full)],
        out_specs=pl.BlockSpec((tm, D_MODEL), row),
        out_shape=jax.ShapeDtypeStruct((s, D_MODEL), F32),
        compiler_params=pltpu.CompilerParams(
            dimension_semantics=("parallel",), vmem_limit_bytes=VMEM_LIMIT),
        name="ffn",
    )(x1, npre, wg, wu, wd, npost)


def _rope_table(s):
    half = ROPE_DIM // 2
    rep = 128 // half
    inv = ROPE_THETA ** (-jnp.arange(half, dtype=F32) * 2.0 / ROPE_DIM)
    pos = (jnp.arange(s // rep)[:, None] * rep + jnp.arange(128)[None, :] // half).astype(F32)
    ang = pos * jnp.tile(inv, rep)[None, :]
    cos, sin = lax.optimization_barrier((jnp.cos(ang), jnp.sin(ang)))
    cos = cos.reshape(s, half)
    sin = sin.reshape(s, half)
    return jnp.concatenate([cos, sin, jnp.zeros((s, HEAD_DIM - ROPE_DIM), F32)], axis=-1)


def _layer(x2, l, norm_mix_pre, w_in, conv_w, a_log, dt_bias, o_norm_w, w_o_attn, w_o_delta,
           w_out, norm_mix_post, norm_ffn_pre, w_gate, w_up, w_down, norm_ffn_post):
    s = x2.shape[0]
    nb = s // MOBA_BLOCK
    wide = 7 * D_MODEL
    nsmall = 2 * HEADS
    w_t = jnp.swapaxes(w_in[l], 0, 1).astype(BF16)

    proj, small, kmean, v_t, qf = _project(x2, norm_mix_pre[l][None, :], w_t, _rope_table(s),
                                           wide // D_MODEL, nsmall)
    pad = lambda vec: jnp.pad(vec.astype(F32), (HEADS, 128 - 2 * HEADS))[None, :]
    half = (s // DN_CHUNK) // 2
    delta = functools.partial(_gated_deltanet, proj, small, conv_w[l], pad(a_log[l]), pad(dt_bias[l]),
                              o_norm_w[l][None, :])
    res = {}

    def first_half():
        res["yd"], res["state"] = delta(jnp.zeros((HEADS, HEAD_DIM, HEAD_DIM), F32), 0, half)
        return res["yd"]

    def second_half(after):
        res["yd"], _ = delta(res["state"], half, s // DN_CHUNK - half, yd_prev=res["yd"], after=after)

    ya = _moba_sparse(proj, v_t, kmean.reshape(nb, D_MODEL), qf, first_half, second_half)

    x1 = _mix_out(ya, [res["yd"]], proj, x2, w_o_attn[l].astype(BF16), w_o_delta[l].astype(BF16),
                  w_out[l].astype(BF16), norm_mix_post[l][None, :])
    return _ffn(x1, norm_ffn_pre[l][None, :], w_gate[l].astype(BF16), w_up[l].astype(BF16),
                w_down[l].astype(BF16), norm_ffn_post[l][None, :])


def kernel(x, norm_mix_pre, w_in, conv_w, a_log, dt_bias, o_norm_w, w_o_attn, w_o_delta, w_out,
           norm_mix_post, norm_ffn_pre, w_gate, w_up, w_down, norm_ffn_post):
    b, s, d = x.shape
    assert d == D_MODEL and s % 512 == 0
    outs = []
    for bi in range(b):
        x2 = x.reshape(s, d) if b == 1 else x[bi]
        for l in range(w_in.shape[0]):
            x2 = _layer(x2, l, norm_mix_pre, w_in, conv_w, a_log, dt_bias, o_norm_w, w_o_attn,
                        w_o_delta, w_out, norm_mix_post, norm_ffn_pre, w_gate, w_up, w_down,
                        norm_ffn_post)
        outs.append(x2)
    return outs[0].reshape(1, s, d) if b == 1 else jnp.stack(outs, axis=0)
```

```python
import functools
import math

import jax
import jax.numpy as jnp
from jax import lax
from jax.experimental import pallas as pl
from jax.experimental.pallas import tpu as pltpu
from jax.experimental.pallas import tpu_sc as plsc

D_MODEL = 1024
HEADS = 8
HEAD_DIM = 128
MOBA_BLOCK = 256
MOBA_TOPK = 3
ROPE_DIM = HEAD_DIM // 4
ROPE_THETA = 500000.0
DN_CONV = 4
ONES_ROWS = 16
DN_HEADS_PER_STEP = 8
DN_CHUNK = 256
D_FF = 2816
FF_CHUNK = 256
EPS = 1e-6
LOG2_E = math.log2(math.e)
HALO = 16

F32 = jnp.float32
BF16 = jnp.bfloat16
NT = (((1,), (1,)), ((), ()))

VMEM_LIMIT = 56 * 1024 * 1024


def _bdot(a, b):
    return jnp.dot(a.astype(BF16), b.astype(BF16), preferred_element_type=F32)


def _sigmoid(x):
    return 1.0 / (1.0 + jnp.exp(-x))


def _proj_kernel(x_ref, nw_ref, wt_ref, rope_ref, *rest, tm, q_scale, n_wide, n_small):
    out_ref, small_ref, kmean_ref, vt_ref, qf_ref = rest[-5:]
    x = x_ref[...]
    ms = jnp.mean(x * x, axis=-1, keepdims=True)
    h = (x * lax.rsqrt(ms + EPS) * nw_ref[...]).astype(BF16)
    narrow = lax.dot_general(h, wt_ref[n_wide * D_MODEL:n_wide * D_MODEL + 128, :], NT,
                             preferred_element_type=F32)
    small_ref[...] = jnp.where(lax.broadcasted_iota(jnp.int32, narrow.shape, 1) < n_small, narrow, 0.0)
    heads = [slice(hh * HEAD_DIM, (hh + 1) * HEAD_DIM) for hh in range(HEADS)]
    groups = [slice(g * MOBA_BLOCK, (g + 1) * MOBA_BLOCK) for g in range(tm // MOBA_BLOCK)]

    half = ROPE_DIM // 2
    tab = rope_ref[...]
    lane = lax.broadcasted_iota(jnp.int32, tab.shape, 1)
    cos_t = jnp.where(lane < half, tab, jnp.where(lane < ROPE_DIM, pltpu.roll(tab, half, 1), 1.0))
    s1_t = jnp.where(lane < half, -pltpu.roll(tab, HEAD_DIM - half, 1), 0.0)
    s2_t = jnp.where((lane >= half) & (lane < ROPE_DIM), tab, 0.0)

    def roped(a):
        return (a * cos_t + pltpu.roll(a, HEAD_DIM - half, 1) * s1_t + pltpu.roll(a, half, 1) * s2_t)

    for c in range(out_ref.shape[1] // D_MODEL):
        cols = slice(c * D_MODEL, (c + 1) * D_MODEL)
        r0 = c * D_MODEL + (n_small if c >= n_wide else 0)
        acc = lax.dot_general(h, wt_ref[r0:r0 + D_MODEL, :], NT, preferred_element_type=F32)
        if c == 0:
            for hh, hs in enumerate(heads):
                r = roped(acc[:, hs]) * q_scale
                out_ref[:, hs] = r.astype(BF16)
                qf_ref[hh] = r
        elif c == 1:
            for hs in heads:
                r = roped(acc[:, hs])
                out_ref[:, D_MODEL + hs.start:D_MODEL + hs.stop] = r.astype(BF16)
                for g, gs in enumerate(groups):
                    kmean_ref[0, g:g + 1, hs] = jnp.sum(r[gs], axis=0, keepdims=True) * (1.0 / MOBA_BLOCK)
        elif c == 2:
            out_ref[:, cols] = acc.astype(BF16)
            for hh, hs in enumerate(heads):
                for g, gs in enumerate(groups):
                    vt_ref[hh, g, 0:HEAD_DIM, :] = acc[gs, hs].T.astype(BF16)
                    vt_ref[hh, g, HEAD_DIM:, :] = jnp.ones((ONES_ROWS, MOBA_BLOCK), BF16)
        else:
            out_ref[:, cols] = acc.astype(BF16)


def _project(x2, norm_w, w_t, rope_t, n_wide, n_small, after=(), *, tm=512):
    s = x2.shape[0]
    width = (w_t.shape[0] - n_small) // D_MODEL * D_MODEL
    once = pl.Buffered(1)
    nblk = tm // MOBA_BLOCK
    kern = functools.partial(_proj_kernel, tm=tm, q_scale=math.log2(math.e) / math.sqrt(HEAD_DIM),
                             n_wide=n_wide, n_small=n_small)
    row = lambda i: (i, 0)
    full = lambda i: (0, 0)
    return pl.pallas_call(
        kern,
        grid=(s // tm,),
        in_specs=[
            pl.BlockSpec((tm, D_MODEL), row),
            pl.BlockSpec((1, D_MODEL), full),
            pl.BlockSpec(w_t.shape, full, pipeline_mode=once),
            pl.BlockSpec((tm, HEAD_DIM), row),
        ] + [pl.BlockSpec(memory_space=pl.ANY)] * len(after),
        out_specs=[
            pl.BlockSpec((tm, width), row),
            pl.BlockSpec((tm, 128), row),
            pl.BlockSpec((1, nblk, D_MODEL), lambda i: (i, 0, 0)),
            pl.BlockSpec((HEADS, nblk, HEAD_DIM + ONES_ROWS, MOBA_BLOCK), lambda i: (0, i, 0, 0)),
            pl.BlockSpec((HEADS, tm, HEAD_DIM), lambda i: (0, i, 0)),
        ],
        out_shape=[
            jax.ShapeDtypeStruct((s, width), BF16),
            jax.ShapeDtypeStruct((s, 128), F32),
            jax.ShapeDtypeStruct((s // tm, nblk, D_MODEL), F32),
            jax.ShapeDtypeStruct((HEADS, s // MOBA_BLOCK, HEAD_DIM + ONES_ROWS, MOBA_BLOCK), BF16),
            jax.ShapeDtypeStruct((HEADS, s, HEAD_DIM), F32),
        ],
        compiler_params=pltpu.CompilerParams(
            dimension_semantics=("parallel",), vmem_limit_bytes=VMEM_LIMIT),
        name="proj",
    )(x2, norm_w, w_t, rope_t, *after)


ROUTE_QBLOCKS = 8
TILES_PER_STEP = 32
SC_WINDOW = 128


def _top_blocks(gate, blk_f, nb):
    picks = []
    for _ in range(MOBA_TOPK):
        m = jnp.max(gate, axis=0, keepdims=True)
        first = jnp.min(jnp.where(gate == m, blk_f, float(nb)), axis=0, keepdims=True)
        pick = (blk_f == first) & (m > -jnp.inf)
        gate = jnp.where(pick, -jnp.inf, gate)
        picks.append(pick)
    return picks


def _route_kernel(q_ref, km_ref, info_ref, cnt_ref, run_scr, *, nb):
    jb = pl.program_id(1)
    L = ROUTE_QBLOCKS * MOBA_BLOCK

    @pl.when(jb == 0)
    def _():
        run_scr[...] = jnp.zeros_like(run_scr)

    km = km_ref[...]
    km_hi = km.astype(BF16)
    km_lo = (km - km_hi.astype(F32)).astype(BF16)
    gate2 = lax.dot_general(jnp.concatenate([km_hi, km_lo], axis=0), q_ref[...], NT,
                            preferred_element_type=F32)
    blk = lax.broadcasted_iota(jnp.int32, (nb, L), 0)
    blk_f = blk.astype(F32)
    qblk = jb * ROUTE_QBLOCKS + (lax.broadcasted_iota(jnp.int32, (nb, L), 1) >> 8)
    gate = jnp.where(blk < qblk, gate2[:nb] + gate2[nb:], -jnp.inf)
    picks = _top_blocks(gate, blk_f, nb)

    chosen = jnp.where(picks[0] | picks[1] | picks[2], 1.0, 0.0)
    B = MOBA_BLOCK
    before = jnp.where(lax.broadcasted_iota(jnp.int32, (B, B), 0)
                       < lax.broadcasted_iota(jnp.int32, (B, B), 1), 1.0, 0.0).astype(BF16)
    carry = run_scr[:, 0:1]
    base = []
    for b in range(ROUTE_QBLOCKS):
        c_b = chosen[:, b * B:(b + 1) * B]
        within = jnp.dot(c_b.astype(BF16), before, preferred_element_type=F32)
        base.append(carry + within)
        carry = carry + within[:, B - 1:B] + c_b[:, B - 1:B]
    base = jnp.concatenate(base, axis=1)

    rows = []
    for pick in picks:
        rows.append(jnp.sum(jnp.where(pick, base, 0.0), axis=0, keepdims=True))
    for pick in picks:
        bid = jnp.sum(jnp.where(pick, blk_f, 0.0), axis=0, keepdims=True)
        valid = jnp.sum(jnp.where(pick, 1.0, 0.0), axis=0, keepdims=True)
        rows.append(jnp.where(valid > 0.0, bid, float(nb)))
    rows += [jnp.zeros((1, L), F32)] * (8 - len(rows))
    info_ref[...] = jnp.concatenate(rows, axis=0).astype(jnp.int32)

    run = jnp.broadcast_to(carry, run_scr.shape)
    run_scr[...] = run
    cnt_ref[0] = run


def _route(proj, kmean):
    s = proj.shape[0]
    nb = s // MOBA_BLOCK
    L = ROUTE_QBLOCKS * MOBA_BLOCK
    steps = s // L
    return pl.pallas_call(
        functools.partial(_route_kernel, nb=nb),
        grid=(HEADS, steps),
        in_specs=[pl.BlockSpec((L, HEAD_DIM), lambda h, j: (j, h)),
                  pl.BlockSpec((nb, HEAD_DIM), lambda h, j: (0, h))],
        out_specs=[pl.BlockSpec((8, L), lambda h, j: (0, h * steps + j)),
                   pl.BlockSpec((1, nb, 128), lambda h, j: (h, 0, 0))],
        out_shape=[jax.ShapeDtypeStruct((8, HEADS * s), jnp.int32),
                   jax.ShapeDtypeStruct((HEADS, nb, 128), F32)],
        scratch_shapes=[pltpu.VMEM((nb, 128), F32)],
        compiler_params=pltpu.CompilerParams(
            dimension_semantics=("parallel", "arbitrary"), vmem_limit_bytes=VMEM_LIMIT),
        name="moba_route",
    )(proj, kmean)


def _dest_kernel(info_ref, cnt_ref, dest_ref, tb_ref, *, nb, s):
    h = pl.program_id(0)
    cap = 4 * s
    ntile = cap // MOBA_BLOCK
    cnt = cnt_ref[0]
    tiles = jnp.floor((cnt + float(MOBA_BLOCK - 1)) * (1.0 / MOBA_BLOCK))
    lower = jnp.where(lax.broadcasted_iota(jnp.int32, (nb, nb), 1)
                      < lax.broadcasted_iota(jnp.int32, (nb, nb), 0), 1.0, 0.0).astype(BF16)
    start = jnp.dot(lower, tiles.astype(BF16), preferred_element_type=F32)
    start1 = start[:, 0:1]
    tiles1 = tiles[:, 0:1]

    t_f = lax.broadcasted_iota(jnp.int32, (nb, ntile), 1).astype(F32)
    n_f = lax.broadcasted_iota(jnp.int32, (nb, ntile), 0).astype(F32)
    inside = (t_f >= start1) & (t_f < start1 + tiles1)
    tb = jnp.sum(jnp.where(inside, n_f, 0.0), axis=0, keepdims=True)
    used = jnp.sum(jnp.where(inside, 1.0, 0.0), axis=0, keepdims=True)
    tb = jnp.where(used > 0.0, tb, -1.0)
    tb_ref[0] = jnp.concatenate([tb, jnp.full((7, ntile), -1.0, F32)], axis=0).astype(jnp.int32)

    off1 = start1 * float(MOBA_BLOCK)
    CH = 2048
    blk_f = lax.broadcasted_iota(jnp.int32, (nb, CH), 0).astype(F32)
    lane = lax.broadcasted_iota(jnp.int32, (1, CH), 1)
    trash = HEADS * cap + (lane & (MOBA_BLOCK - 1))
    for ch in range(s // CH):
        sl = slice(ch * CH, (ch + 1) * CH)
        rows = []
        for r in range(MOBA_TOPK):
            rank = info_ref[r:r + 1, sl]
            bid = info_ref[MOBA_TOPK + r:MOBA_TOPK + r + 1, sl]
            off = jnp.sum(jnp.where(blk_f == bid.astype(F32), off1, 0.0), axis=0, keepdims=True)
            rows.append(jnp.where(bid < nb, h * cap + off.astype(jnp.int32) + rank, trash))
        rows += [jnp.zeros((1, CH), jnp.int32)] * (8 - len(rows))
        dest_ref[:, sl] = jnp.concatenate(rows, axis=0)


def _dest(info, cnt, s):
    nb = s // MOBA_BLOCK
    ntile = 4 * s // MOBA_BLOCK
    return pl.pallas_call(
        functools.partial(_dest_kernel, nb=nb, s=s),
        grid=(HEADS,),
        in_specs=[pl.BlockSpec((8, s), lambda h: (0, h)),
                  pl.BlockSpec((1, nb, 128), lambda h: (h, 0, 0))],
        out_specs=[pl.BlockSpec((8, s), lambda h: (0, h)),
                   pl.BlockSpec((1, 8, ntile), lambda h: (h, 0, 0))],
        out_shape=[jax.ShapeDtypeStruct((8, HEADS * s), jnp.int32),
                   jax.ShapeDtypeStruct((HEADS, 8, ntile), jnp.int32)],
        compiler_params=pltpu.CompilerParams(
            dimension_semantics=("parallel",), vmem_limit_bytes=VMEM_LIMIT),
        name="moba_dest",
    )(info, cnt)


def _sc_scatter_rows(rows, idx, n_out):
    m, c = rows.shape
    k = idx.shape[0]
    nwin = m // SC_WINDOW
    mesh = plsc.VectorSubcoreMesh(core_axis_name="core", subcore_axis_name="subcore")

    @pl.kernel(out_type=jax.ShapeDtypeStruct((n_out, c), rows.dtype), mesh=mesh)
    def kern(x_hbm, i_hbm, o_hbm):
        def body(x_vmem, i_vmem):
            pltpu.sync_copy(x_vmem, o_hbm.at[i_vmem.at[0]])

        pltpu.emit_pipeline(
            body, grid=(k * nwin,),
            in_specs=[pl.BlockSpec((SC_WINDOW, c), lambda i: (i % nwin, 0)),
                      pl.BlockSpec((1, SC_WINDOW), lambda i: (0, i))],
            out_specs=[],
            core_axis_name=("core", "subcore"),
            dimension_semantics=(pltpu.PARALLEL,),
        )(x_hbm, i_hbm)

    return kern(rows, idx.reshape(1, k * m))


def _sc_gather_rows(table, idx):
    m = idx.shape[0]
    c = table.shape[1]
    mesh = plsc.VectorSubcoreMesh(core_axis_name="core", subcore_axis_name="subcore")

    @pl.kernel(out_type=jax.ShapeDtypeStruct((m, c), table.dtype), mesh=mesh)
    def kern(x_hbm, i_hbm, o_hbm):
        def body(i_vmem, o_vmem):
            pltpu.sync_copy(x_hbm.at[i_vmem.at[0]], o_vmem)

        pltpu.emit_pipeline(
            body, grid=(m // SC_WINDOW,),
            in_specs=[pl.BlockSpec((1, SC_WINDOW), lambda i: (0, i))],
            out_specs=[pl.BlockSpec((SC_WINDOW, c), lambda i: (i, 0))],
            core_axis_name=("core", "subcore"),
            dimension_semantics=(pltpu.PARALLEL,),
        )(i_hbm, o_hbm)

    return kern(table, idx.reshape(1, m))


def _pack_partial_t(o_norm_t, lse):
    half = HEAD_DIM // 2
    u = lax.bitcast_convert_type(o_norm_t, jnp.uint32) + jnp.uint32(0x8000)
    word = (u[:half] & jnp.uint32(0xFFFF0000)) | (u[half:] >> 16)
    lse_bits = lax.bitcast_convert_type(jnp.broadcast_to(lse, word.shape), jnp.uint32)
    full = jnp.concatenate([word, lse_bits], axis=0)
    return lax.bitcast_convert_type(full, F32).T


def _unpack_partial(part):
    half = HEAD_DIM // 2
    lane = lax.broadcasted_iota(jnp.int32, part.shape, 1)
    swapped = pltpu.roll(part, half, 1)
    lse = jnp.where(lane < half, swapped, part)
    word = lax.bitcast_convert_type(jnp.where(lane < half, part, swapped), jnp.uint32)
    o = jnp.where(lane < half, lax.bitcast_convert_type(word & jnp.uint32(0xFFFF0000), F32),
                  lax.bitcast_convert_type(word << 16, F32))
    return o, lse


def _tiles_kernel(tb_ref, qs_ref, k_ref, vt_ref, after_ref, o_ref, *, ntile):
    del after_ref
    h = pl.program_id(0)
    g = pl.program_id(1)
    B = MOBA_BLOCK
    base = h * ntile + g * TILES_PER_STEP

    @pl.when(tb_ref[base] >= 0)
    def _():
        blocks = [jnp.maximum(tb_ref[base + u], 0) for u in range(TILES_PER_STEP)]
        s_t = [lax.dot_general(k_ref[pl.ds(pl.multiple_of(n * B, B), B), :],
                               qs_ref[u * B:(u + 1) * B, :].astype(BF16), NT,
                               preferred_element_type=F32) for u, n in enumerate(blocks)]
        m = [jnp.max(x, axis=0, keepdims=True) for x in s_t]
        p = [jnp.exp2(x - mm).astype(BF16) for x, mm in zip(s_t, m)]
        acc = [jnp.dot(vt_ref[0, n], pp, preferred_element_type=F32) for n, pp in zip(blocks, p)]
        for u in range(TILES_PER_STEP):
            l = acc[u][HEAD_DIM:HEAD_DIM + 1]
            o_ref[u * B:(u + 1) * B, :] = _pack_partial_t(acc[u][:HEAD_DIM] / l,
                                                          m[u] + jnp.log(l) * LOG2_E)


def _tiles(tb, qsorted, proj, v_t, after, s):
    nb = s // MOBA_BLOCK
    ntile = 4 * s // MOBA_BLOCK
    steps = ntile // TILES_PER_STEP
    rows = TILES_PER_STEP * MOBA_BLOCK
    grid_spec = pltpu.PrefetchScalarGridSpec(
        num_scalar_prefetch=1,
        grid=(HEADS, steps),
        in_specs=[pl.BlockSpec((rows, HEAD_DIM), lambda h, g, tb: (h * steps + g, 0)),
                  pl.BlockSpec((s, HEAD_DIM), lambda h, g, tb: (0, HEADS + h)),
                  pl.BlockSpec((1, nb, HEAD_DIM + ONES_ROWS, MOBA_BLOCK), lambda h, g, tb: (h, 0, 0, 0)),
                  pl.BlockSpec((16, HEAD_DIM), lambda h, g, tb: (0, 0))],
        out_specs=pl.BlockSpec((rows, HEAD_DIM), lambda h, g, tb: (h * steps + g, 0)),
    )
    return pl.pallas_call(
        functools.partial(_tiles_kernel, ntile=ntile),
        grid_spec=grid_spec,
        out_shape=jax.ShapeDtypeStruct(qsorted.shape, F32),
        compiler_params=pltpu.CompilerParams(
            dimension_semantics=("parallel", "arbitrary"), vmem_limit_bytes=VMEM_LIMIT),
        name="moba_tiles",
    )(tb, qsorted, proj, v_t, after)


def _merge_kernel(q_ref, k_ref, v_ref, part_ref, o_ref):
    j = pl.program_id(0)
    B = MOBA_BLOCK
    heads = [slice(hh * HEAD_DIM, (hh + 1) * HEAD_DIM) for hh in range(HEADS)]
    causal = (lax.broadcasted_iota(jnp.int32, (B, B), 1) <= lax.broadcasted_iota(jnp.int32, (B, B), 0))
    s = [lax.dot_general(q_ref[:, hs], k_ref[:, hs], NT, preferred_element_type=F32) for hs in heads]
    s = [jnp.where(causal, x, -jnp.inf) for x in s]
    m_own = [jnp.broadcast_to(jnp.max(x, axis=-1, keepdims=True), (B, HEAD_DIM)) for x in s]
    p = [jnp.exp2(x - jnp.concatenate([m, m], axis=1)) for x, m in zip(s, m_own)]
    ones = jnp.ones((B, HEAD_DIM), BF16)
    ol = [jnp.dot(x.astype(BF16), jnp.concatenate([v_ref[:, hs], ones], axis=1),
                  preferred_element_type=F32) for x, hs in zip(p, heads)]
    for hh, hs in enumerate(heads):
        o_own = ol[hh][:, :HEAD_DIM]
        l_own = ol[hh][:, HEAD_DIM:]
        lse_own = m_own[hh] + jnp.log(l_own) * LOG2_E
        parts = []
        for r in range(MOBA_TOPK):
            o_r, lse_r = _unpack_partial(part_ref[r, hh])
            valid = r < j
            parts.append((jnp.where(valid, o_r, 0.0), jnp.where(valid, lse_r, -jnp.inf)))
        m_all = lse_own
        for _, lse_r in parts:
            m_all = jnp.maximum(m_all, lse_r)
        w = jnp.exp2(lse_own - m_all)
        num = (w / l_own) * o_own
        den = w
        for o_r, lse_r in parts:
            w = jnp.exp2(lse_r - m_all)
            num = num + w * o_r
            den = den + w
        o_ref[:, hs] = (num / den).astype(BF16)


def _merge(proj, parts, s):
    nb = s // MOBA_BLOCK
    width = HEADS * HEAD_DIM
    return pl.pallas_call(
        _merge_kernel,
        grid=(nb,),
        in_specs=[pl.BlockSpec((MOBA_BLOCK, width), lambda j: (j, 0)),
                  pl.BlockSpec((MOBA_BLOCK, width), lambda j: (j, 1)),
                  pl.BlockSpec((MOBA_BLOCK, width), lambda j: (j, 2)),
                  pl.BlockSpec((MOBA_TOPK, HEADS, MOBA_BLOCK, HEAD_DIM), lambda j: (0, 0, j, 0))],
        out_specs=pl.BlockSpec((MOBA_BLOCK, width), lambda j: (j, 0)),
        out_shape=jax.ShapeDtypeStruct((s, width), BF16),
        compiler_params=pltpu.CompilerParams(
            dimension_semantics=("parallel",), vmem_limit_bytes=VMEM_LIMIT),
        name="moba_merge",
    )(proj, proj, proj, parts)


def _moba_sparse(proj, v_t, kmean, qf, run_during_scatter, run_during_gather):
    s = proj.shape[0]
    cap = 4 * s
    n_rows = HEADS * cap + MOBA_BLOCK
    info, cnt = _route(proj, kmean)
    dest, tb = _dest(info, cnt, s)
    dest3 = dest[:MOBA_TOPK]
    qsorted = _sc_scatter_rows(qf.reshape(HEADS * s, HEAD_DIM), dest3, n_rows)
    after = run_during_scatter()
    osorted = _tiles(tb[:, 0, :].reshape(-1), qsorted, proj, v_t, after, s)
    parts = _sc_gather_rows(osorted, dest3.reshape(-1))
    run_during_gather(osorted)
    return _merge(proj, parts.reshape(MOBA_TOPK, HEADS, s, HEAD_DIM), s)


def _delta_kernel(q_ref, k_ref, v_ref, qh_ref, kh_ref, vh_ref, z_ref, small_ref,
                  cwq_ref, cwk_ref, cwv_ref, alog_ref, dtb_ref, onw_ref, st_in_ref, *rest, hg, c0):
    o_ref, st_out_ref, state_scr, xx_scr = rest[-4:]
    c = c0 + pl.program_id(0)
    C = DN_CHUNK

    @pl.when(pl.program_id(0) == 0)
    def _():
        state_scr[...] = st_in_ref[...]

    shifts = DN_CONV - 1
    r_out = lax.broadcasted_iota(jnp.int32, (shifts * C, C), 0)
    r_in = lax.broadcasted_iota(jnp.int32, (shifts * C, C), 1)
    delay = jnp.where((r_out & (C - 1)) - r_in == (r_out >> 8) + 1, 1.0, 0.0).astype(BF16)
    assert C == 256

    def conv_silu(idx, x_ref, halo_ref, cw_ref):
        x = x_ref[...]
        delayed = jnp.dot(delay, x, preferred_element_type=F32)
        halo = jnp.where(c == 0, 0.0, halo_ref[...].astype(F32))
        xx_scr[idx, 0:HALO, :] = halo
        xx_scr[idx, HALO:HALO + HALO, :] = x_ref[0:HALO, :].astype(F32)
        y = cw_ref[shifts:shifts + 1, :] * x.astype(F32)
        for j in range(1, shifts + 1):
            d_j = delayed[(j - 1) * C:j * C]
            head8 = xx_scr[idx, HALO - j:HALO - j + 8, :]
            d_j = jnp.concatenate([head8, d_j[8:]], axis=0)
            y = y + cw_ref[shifts - j:shifts - j + 1, :] * d_j
        return y * _sigmoid(y)

    q_all = conv_silu(0, q_ref, qh_ref, cwq_ref)
    k_all = conv_silu(1, k_ref, kh_ref, cwk_ref)
    v_all = conv_silu(2, v_ref, vh_ref, cwv_ref)

    small = small_ref[...]
    beta_all = _sigmoid(small)
    xs = small + dtb_ref[...]
    softplus = jnp.maximum(xs, 0.0) + jnp.log(1.0 + jnp.exp(-jnp.abs(xs)))
    g_all = -jnp.exp(alog_ref[...]) * softplus

    row = lax.broadcasted_iota(jnp.int32, (C, C), 0)
    col = lax.broadcasted_iota(jnp.int32, (C, C), 1)
    tril = row >= col
    strict = row > col
    rxc = row ^ col
    eye = (row == col).astype(F32)
    assert C == 2 * HEAD_DIM

    heads = range(DN_HEADS_PER_STEP)
    sls = [slice(hh * HEAD_DIM, (hh + 1) * HEAD_DIM) for hh in heads]

    def hmap(f, *lists):
        return [f(*xs) for xs in zip(*lists)]

    def pieces(x, n):
        out, rest = [], x
        for _ in range(n):
            p = rest.astype(BF16)
            out.append(p)
            rest = rest - p.astype(F32)
        return jnp.concatenate(out, axis=1)

    hp = DN_HEADS_PER_STEP
    same_half = jnp.where((row >> 7) == (col >> 7), 1.0, 0.0).astype(BF16)

    def unit_rows_pair(xq, xk):
        sq = jnp.concatenate([xq * xq, xk * xk], axis=1).astype(BF16)
        inv = lax.rsqrt(jnp.dot(sq, same_half, preferred_element_type=F32) + EPS)
        return xq * inv[:, :HEAD_DIM], xk * inv[:, HEAD_DIM:]

    def selector(first_lane, n):
        src = lax.broadcasted_iota(jnp.int32, (128, hp * HEAD_DIM), 0)
        dst = lax.broadcasted_iota(jnp.int32, (128, hp * HEAD_DIM), 1) >> 7
        sel = jnp.where(src == first_lane + dst, 1.0, 0.0).astype(BF16)
        return jnp.concatenate([sel] * n, axis=0)

    qk = [unit_rows_pair(q_all[:, sl], k_all[:, sl]) for sl in sls]
    q = [x[0] * (HEAD_DIM ** -0.5) for x in qk]
    k = [x[1] for x in qk]
    v = [v_all[:, sl] for sl in sls]
    beta_rep = jnp.dot(pieces(beta_all, 1), selector(hg * hp, 1), preferred_element_type=F32)
    beta = [beta_rep[:, sl] for sl in sls]

    cum3 = jnp.dot(jnp.where(tril, 1.0, 0.0).astype(BF16), pieces(g_all, 3), preferred_element_type=F32)
    gcum_all = cum3[:, :128] + cum3[:, 128:256] + cum3[:, 256:]
    gcum_rep = jnp.dot(pieces(gcum_all, 3), selector(hg * hp + HEADS, 3), preferred_element_type=F32)
    gcum_b = [gcum_rep[:, sl] for sl in sls]
    gr = [x.T[0:1, :] for x in gcum_b]
    decay = hmap(lambda c_, r_: jnp.where(
        tril, jnp.exp(jnp.where(tril, jnp.concatenate([c_, c_], axis=1) - r_, 0.0)), 0.0), gcum_b, gr)
    e_g = hmap(jnp.exp, gcum_b)
    g_last = [x[C - 1:C, :] for x in gcum_b]

    kb = hmap(lambda a_, b_: a_ * b_, k, beta)
    vb = hmap(lambda a_, b_: a_ * b_, v, beta)
    a = hmap(lambda kb_, q_, k_: lax.dot_general(
        jnp.concatenate([kb_, q_], axis=0).astype(BF16), k_.astype(BF16), NT,
        preferred_element_type=F32), kb, q, k)
    lmat = hmap(lambda a_, d_: jnp.where(strict, a_[:C] * d_, 0.0), a, decay)
    attn = hmap(lambda a_, d_: a_[C:] * d_, a, decay)

    d1 = hmap(lambda l_: jnp.where(rxc < 8, l_, 0.0), lmat)
    d2 = hmap(lambda x: _bdot(x, x), d1)
    d4 = hmap(lambda x: _bdot(x, x), d2)
    p1 = hmap(lambda x, y: _bdot(eye - x, eye + y), d1, d2)
    tmat = hmap(lambda x, y: _bdot(x, eye + y), p1, d4)

    def odd_rows(t, sz):
        return jnp.concatenate([t[b * 2 * sz + sz:(b + 1) * 2 * sz] for b in range(C // (2 * sz))], axis=0)

    def with_odd_rows(t, odd, sz):
        pieces = []
        for b in range(C // (2 * sz)):
            pieces += [t[b * 2 * sz:b * 2 * sz + sz], odd[b * sz:(b + 1) * sz]]
        return jnp.concatenate(pieces, axis=0)

    sz = 8
    while sz < C:
        off = hmap(lambda l_: jnp.where((rxc >= sz) & (rxc < 2 * sz), l_, 0.0), lmat)
        t_odd = hmap(lambda t_: odd_rows(t_, sz), tmat)
        x = hmap(_bdot, t_odd, off)
        x = hmap(_bdot, x, tmat)
        tmat = hmap(lambda t_, o_, x_: with_odd_rows(t_, o_ - x_, sz), tmat, t_odd, x)
        sz *= 2

    uw = hmap(lambda t_, vb_, kb_, e_: _bdot(t_, jnp.concatenate([vb_, kb_ * e_], axis=1)),
              tmat, vb, kb, e_g)
    state = [state_scr[hh] for hh in heads]
    wq = hmap(lambda uw_, q_, e_, s_: _bdot(jnp.concatenate([uw_[:, HEAD_DIM:], q_ * e_], axis=0), s_),
              uw, q, e_g, state)
    v_new = hmap(lambda uw_, wq_: uw_[:, :HEAD_DIM] - wq_[:C], uw, wq)
    av = hmap(_bdot, attn, v_new)
    kt = hmap(lambda k_, gl_, gc_: (k_ * jnp.exp(gl_ - gc_)).T, k, g_last, gcum_b)
    ds = hmap(_bdot, kt, v_new)
    for hh in heads:
        new_state = state[hh] * jnp.exp(g_last[hh]) + ds[hh]
        state_scr[hh] = new_state
        st_out_ref[hh] = new_state
        o = wq[hh][C:] + av[hh]
        y = o * lax.rsqrt(jnp.mean(o * o, axis=-1, keepdims=True) + EPS) * onw_ref[...]
        z = z_ref[:, sls[hh]].astype(F32)
        o_ref[:, sls[hh]] = (y * (z * _sigmoid(z))).astype(BF16)


def _gated_deltanet(proj, small, conv_w, alog_v, dtb_v, onw, state, c0, n_chunks, yd_prev=None, after=None):
    s = proj.shape[0]
    rb = DN_CHUNK // HALO
    hp = DN_HEADS_PER_STEP
    assert hp == HEADS
    width = hp * HEAD_DIM

    def col(part):
        return pl.BlockSpec((DN_CHUNK, width), lambda c: (c0 + c, part))

    def halo(part):
        return pl.BlockSpec((HALO, width), lambda c: (jnp.maximum((c0 + c) * rb - 1, 0), part))

    def cw(part):
        return pl.BlockSpec((DN_CONV, width), lambda c: (0, part))

    vec = pl.BlockSpec((1, 128), lambda c: (0, 0))
    st_spec = pl.BlockSpec((hp, HEAD_DIM, HEAD_DIM), lambda c: (0, 0, 0))
    extras = [a for a in (yd_prev, after) if a is not None]
    n_fixed = 15
    return pl.pallas_call(
        functools.partial(_delta_kernel, hg=0, c0=c0),
        grid=(n_chunks,),
        in_specs=[col(3), col(4), col(5),
                  halo(3), halo(4), halo(5),
                  col(6),
                  pl.BlockSpec((DN_CHUNK, 128), lambda c: (c0 + c, 0)),
                  cw(0), cw(1), cw(2),
                  vec, vec, vec, st_spec] + [pl.BlockSpec(memory_space=pl.ANY)] * len(extras),
        out_specs=[pl.BlockSpec((DN_CHUNK, width), lambda c: (c0 + c, 0)), st_spec],
        out_shape=[jax.ShapeDtypeStruct((s, width), BF16),
                   jax.ShapeDtypeStruct((hp, HEAD_DIM, HEAD_DIM), F32)],
        input_output_aliases={n_fixed: 0} if yd_prev is not None else {},
        scratch_shapes=[pltpu.VMEM((hp, HEAD_DIM, HEAD_DIM), F32),
                        pltpu.VMEM((3, HALO + DN_CHUNK, width), F32)],
        compiler_params=pltpu.CompilerParams(
            dimension_semantics=("arbitrary",), vmem_limit_bytes=VMEM_LIMIT),
        name="deltanet",
    )(proj, proj, proj, proj, proj, proj, proj, small, conv_w, conv_w, conv_w, alog_v, dtb_v, onw,
      state, *extras)


def _mix_out_kernel(ya_ref, *rest):
    ng = HEADS // DN_HEADS_PER_STEP
    yd_refs = rest[:ng]
    ga_ref, gd_ref, x_ref, wa_ref, wd_ref, wo_ref, nw_ref, o_ref = rest[ng:]
    width = DN_HEADS_PER_STEP * HEAD_DIM
    pa = jnp.dot(ya_ref[...], wa_ref[...], preferred_element_type=F32)
    pd = jnp.dot(yd_refs[0][...], wd_ref[0:width, :], preferred_element_type=F32)
    for g in range(1, ng):
        pd = pd + jnp.dot(yd_refs[g][...], wd_ref[g * width:(g + 1) * width, :],
                          preferred_element_type=F32)
    merged = _sigmoid(ga_ref[...].astype(F32)) * pa + _sigmoid(gd_ref[...].astype(F32)) * pd
    mo = jnp.dot(merged.astype(BF16), wo_ref[...], preferred_element_type=F32)
    y = mo * lax.rsqrt(jnp.mean(mo * mo, axis=-1, keepdims=True) + EPS) * nw_ref[...]
    o_ref[...] = x_ref[...] + y


def _mix_out(ya, yds, proj, x2, wa, wd, wo, nw, *, tm=512):
    s = x2.shape[0]
    row = lambda i: (i, 0)
    full = lambda i: (0, 0)
    wspec = pl.BlockSpec((D_MODEL, D_MODEL), full)
    return pl.pallas_call(
        _mix_out_kernel,
        grid=(s // tm,),
        in_specs=[pl.BlockSpec((tm, D_MODEL), row)]
                 + [pl.BlockSpec((tm, yd.shape[1]), row) for yd in yds]
                 + [pl.BlockSpec((tm, D_MODEL), lambda i: (i, 7)),
                  pl.BlockSpec((tm, D_MODEL), lambda i: (i, 8)),
                  pl.BlockSpec((tm, D_MODEL), row),
                  wspec, wspec, wspec, pl.BlockSpec((1, D_MODEL), full)],
        out_specs=pl.BlockSpec((tm, D_MODEL), row),
        out_shape=jax.ShapeDtypeStruct((s, D_MODEL), F32),
        compiler_params=pltpu.CompilerParams(
            dimension_semantics=("parallel",), vmem_limit_bytes=VMEM_LIMIT),
        name="mix_out",
    )(ya, *yds, proj, proj, x2, wa, wd, wo, nw)


def _ffn_kernel(x_ref, npre_ref, wg_ref, wu_ref, wd_ref, npost_ref, o_ref):
    x = x_ref[...]
    h = (x * lax.rsqrt(jnp.mean(x * x, axis=-1, keepdims=True) + EPS) * npre_ref[...]).astype(BF16)
    acc = jnp.zeros(x.shape, F32)
    for cc in range(D_FF // FF_CHUNK):
        sl = slice(cc * FF_CHUNK, (cc + 1) * FF_CHUNK)
        g = jnp.dot(h, wg_ref[:, sl], preferred_element_type=F32)
        u = jnp.dot(h, wu_ref[:, sl], preferred_element_type=F32)
        act = (g * _sigmoid(g) * u).astype(BF16)
        acc = acc + jnp.dot(act, wd_ref[sl, :], preferred_element_type=F32)
    y = acc * lax.rsqrt(jnp.mean(acc * acc, axis=-1, keepdims=True) + EPS) * npost_ref[...]
    o_ref[...] = x + y


def _ffn(x1, npre, wg, wu, wd, npost, *, tm=512):
    s = x1.shape[0]
    row = lambda i: (i, 0)
    full = lambda i: (0, 0)
    once = pl.Buffered(1)
    return pl.pallas_call(
        _ffn_kernel,
        grid=(s // tm,),
        in_specs=[pl.BlockSpec((tm, D_MODEL), row), pl.BlockSpec((1, D_MODEL), full),
                  pl.BlockSpec((D_MODEL, D_FF), full, pipeline_mode=once),
                  pl.BlockSpec((D_MODEL, D_FF), full, pipeline_mode=once),
                  pl.BlockSpec((D_FF, D_MODEL), full, pipeline_mode=once),
                  pl.BlockSpec((1, D_MODEL), full)],
        out_specs=pl.BlockSpec((tm, D_MODEL), row),
        out_shape=jax.ShapeDtypeStruct((s, D_MODEL), F32),
        compiler_params=pltpu.CompilerParams(
            dimension_semantics=("parallel",), vmem_limit_bytes=VMEM_LIMIT),
        name="ffn",
    )(x1, npre, wg, wu, wd, npost)


def _rope_table(s):
    half = ROPE_DIM // 2
    rep = 128 // half
    inv = ROPE_THETA ** (-jnp.arange(half, dtype=F32) * 2.0 / ROPE_DIM)
    pos = (jnp.arange(s // rep)[:, None] * rep + jnp.arange(128)[None, :] // half).astype(F32)
    ang = pos * jnp.tile(inv, rep)[None, :]
    cos, sin = lax.optimization_barrier((jnp.cos(ang), jnp.sin(ang)))
    cos = cos.reshape(s, half)
    sin = sin.reshape(s, half)
    return jnp.concatenate([cos, sin, jnp.zeros((s, HEAD_DIM - ROPE_DIM), F32)], axis=-1)


def _layer(x2, l, norm_mix_pre, w_in, conv_w, a_log, dt_bias, o_norm_w, w_o_attn, w_o_delta,
           w_out, norm_mix_post, norm_ffn_pre, w_gate, w_up, w_down, norm_ffn_post):
    s = x2.shape[0]
    nb = s // MOBA_BLOCK
    wide = 7 * D_MODEL
    nsmall = 2 * HEADS
    w_t = jnp.swapaxes(w_in[l], 0, 1).astype(BF16)
    later_w = [w[l].astype(BF16) for w in (w_o_attn, w_o_delta, w_out, w_gate, w_up, w_down)]

    proj, small, kmean, v_t, qf = _project(x2, norm_mix_pre[l][None, :], w_t, _rope_table(s),
                                           wide // D_MODEL, nsmall, after=later_w)
    pad = lambda vec: jnp.pad(vec.astype(F32), (HEADS, 128 - 2 * HEADS))[None, :]
    half = (s // DN_CHUNK) // 2
    delta = functools.partial(_gated_deltanet, proj, small, conv_w[l], pad(a_log[l]), pad(dt_bias[l]),
                              o_norm_w[l][None, :])
    res = {}

    def first_half():
        res["yd"], res["state"] = delta(jnp.zeros((HEADS, HEAD_DIM, HEAD_DIM), F32), 0, half)
        return res["yd"]

    def second_half(after):
        res["yd"], _ = delta(res["state"], half, s // DN_CHUNK - half, yd_prev=res["yd"], after=after)

    ya = _moba_sparse(proj, v_t, kmean.reshape(nb, D_MODEL), qf, first_half, second_half)

    x1 = _mix_out(ya, [res["yd"]], proj, x2, *later_w[:3], norm_mix_post[l][None, :])
    return _ffn(x1, norm_ffn_pre[l][None, :], *later_w[3:], norm_ffn_post[l][None, :])


def kernel(x, norm_mix_pre, w_in, conv_w, a_log, dt_bias, o_norm_w, w_o_attn, w_o_delta, w_out,
           norm_mix_post, norm_ffn_pre, w_gate, w_up, w_down, norm_ffn_post):
    b, s, d = x.shape
    assert d == D_MODEL and s % 512 == 0
    outs = []
    for bi in range(b):
        x2 = x.reshape(s, d) if b == 1 else x[bi]
        for l in range(w_in.shape[0]):
            x2 = _layer(x2, l, norm_mix_pre, w_in, conv_w, a_log, dt_bias, o_norm_w, w_o_attn,
                        w_o_delta, w_out, norm_mix_post, norm_ffn_pre, w_gate, w_up, w_down,
                        norm_ffn_post)
        outs.append(x2)
    return outs[0].reshape(1, s, d) if b == 1 else jnp.stack(outs, axis=0)
```

```python
import functools
import math

import jax
import jax.numpy as jnp
from jax import lax
from jax.experimental import pallas as pl
from jax.experimental.pallas import tpu as pltpu
from jax.experimental.pallas import tpu_sc as plsc

D_MODEL = 1024
HEADS = 8
HEAD_DIM = 128
MOBA_BLOCK = 256
MOBA_TOPK = 3
ROPE_DIM = HEAD_DIM // 4
ROPE_THETA = 500000.0
DN_CONV = 4
ONES_ROWS = 16
DN_HEADS_PER_STEP = 8
DN_CHUNK = 256
D_FF = 2816
FF_CHUNK = 256
EPS = 1e-6
LOG2_E = math.log2(math.e)
HALO = 16

F32 = jnp.float32
BF16 = jnp.bfloat16
NT = (((1,), (1,)), ((), ()))

VMEM_LIMIT = 56 * 1024 * 1024


def _bdot(a, b):
    return jnp.dot(a.astype(BF16), b.astype(BF16), preferred_element_type=F32)


def _sigmoid(x):
    return 1.0 / (1.0 + jnp.exp(-x))


def _proj_kernel(x_ref, nw_ref, wt_ref, rope_ref, *rest, tm, q_scale, n_wide, n_small):
    out_ref, small_ref, kmean_ref, vt_ref, qf_ref = rest[-5:]
    x = x_ref[...]
    ms = jnp.mean(x * x, axis=-1, keepdims=True)
    h = (x * lax.rsqrt(ms + EPS) * nw_ref[...]).astype(BF16)
    narrow = lax.dot_general(h, wt_ref[n_wide * D_MODEL:n_wide * D_MODEL + 128, :], NT,
                             preferred_element_type=F32)
    small_ref[...] = jnp.where(lax.broadcasted_iota(jnp.int32, narrow.shape, 1) < n_small, narrow, 0.0)
    heads = [slice(hh * HEAD_DIM, (hh + 1) * HEAD_DIM) for hh in range(HEADS)]
    groups = [slice(g * MOBA_BLOCK, (g + 1) * MOBA_BLOCK) for g in range(tm // MOBA_BLOCK)]

    half = ROPE_DIM // 2
    tab = rope_ref[...]
    lane = lax.broadcasted_iota(jnp.int32, tab.shape, 1)
    cos_t = jnp.where(lane < half, tab, jnp.where(lane < ROPE_DIM, pltpu.roll(tab, half, 1), 1.0))
    s1_t = jnp.where(lane < half, -pltpu.roll(tab, HEAD_DIM - half, 1), 0.0)
    s2_t = jnp.where((lane >= half) & (lane < ROPE_DIM), tab, 0.0)

    def roped(a):
        return (a * cos_t + pltpu.roll(a, HEAD_DIM - half, 1) * s1_t + pltpu.roll(a, half, 1) * s2_t)

    for c in range(out_ref.shape[1] // D_MODEL):
        cols = slice(c * D_MODEL, (c + 1) * D_MODEL)
        r0 = c * D_MODEL + (n_small if c >= n_wide else 0)
        acc = lax.dot_general(h, wt_ref[r0:r0 + D_MODEL, :], NT, preferred_element_type=F32)
        if c == 0:
            for hh, hs in enumerate(heads):
                r = roped(acc[:, hs]) * q_scale
                out_ref[:, hs] = r.astype(BF16)
                qf_ref[hh] = r
        elif c == 1:
            for hs in heads:
                r = roped(acc[:, hs])
                out_ref[:, D_MODEL + hs.start:D_MODEL + hs.stop] = r.astype(BF16)
                for g, gs in enumerate(groups):
                    kmean_ref[0, g:g + 1, hs] = jnp.sum(r[gs], axis=0, keepdims=True) * (1.0 / MOBA_BLOCK)
        elif c == 2:
            out_ref[:, cols] = acc.astype(BF16)
            for hh, hs in enumerate(heads):
                for g, gs in enumerate(groups):
                    vt_ref[hh, g, 0:HEAD_DIM, :] = acc[gs, hs].T.astype(BF16)
                    vt_ref[hh, g, HEAD_DIM:, :] = jnp.ones((ONES_ROWS, MOBA_BLOCK), BF16)
        else:
            out_ref[:, cols] = acc.astype(BF16)


def _project(x2, norm_w, w_t, rope_t, n_wide, n_small, after=(), *, tm=512):
    s = x2.shape[0]
    width = (w_t.shape[0] - n_small) // D_MODEL * D_MODEL
    once = pl.Buffered(1)
    nblk = tm // MOBA_BLOCK
    kern = functools.partial(_proj_kernel, tm=tm, q_scale=math.log2(math.e) / math.sqrt(HEAD_DIM),
                             n_wide=n_wide, n_small=n_small)
    row = lambda i: (i, 0)
    full = lambda i: (0, 0)
    return pl.pallas_call(
        kern,
        grid=(s // tm,),
        in_specs=[
            pl.BlockSpec((tm, D_MODEL), row),
            pl.BlockSpec((1, D_MODEL), full),
            pl.BlockSpec(w_t.shape, full, pipeline_mode=once),
            pl.BlockSpec((tm, HEAD_DIM), row),
        ] + [pl.BlockSpec(memory_space=pl.ANY)] * len(after),
        out_specs=[
            pl.BlockSpec((tm, width), row),
            pl.BlockSpec((tm, 128), row),
            pl.BlockSpec((1, nblk, D_MODEL), lambda i: (i, 0, 0)),
            pl.BlockSpec((HEADS, nblk, HEAD_DIM + ONES_ROWS, MOBA_BLOCK), lambda i: (0, i, 0, 0)),
            pl.BlockSpec((HEADS, tm, HEAD_DIM), lambda i: (0, i, 0)),
        ],
        out_shape=[
            jax.ShapeDtypeStruct((s, width), BF16),
            jax.ShapeDtypeStruct((s, 128), F32),
            jax.ShapeDtypeStruct((s // tm, nblk, D_MODEL), F32),
            jax.ShapeDtypeStruct((HEADS, s // MOBA_BLOCK, HEAD_DIM + ONES_ROWS, MOBA_BLOCK), BF16),
            jax.ShapeDtypeStruct((HEADS, s, HEAD_DIM), F32),
        ],
        compiler_params=pltpu.CompilerParams(
            dimension_semantics=("parallel",), vmem_limit_bytes=VMEM_LIMIT),
        name="proj",
    )(x2, norm_w, w_t, rope_t, *after)


ROUTE_QBLOCKS = 8
TILES_PER_STEP = 32
SC_WINDOW = 128
DEST_LANE_CHUNK = 2048


def _top_blocks(gate, blk_f, nb):
    picks = []
    for _ in range(MOBA_TOPK):
        m = jnp.max(gate, axis=0, keepdims=True)
        first = jnp.min(jnp.where(gate == m, blk_f, float(nb)), axis=0, keepdims=True)
        pick = (blk_f == first) & (m > -jnp.inf)
        gate = jnp.where(pick, -jnp.inf, gate)
        picks.append(pick)
    return picks


def _route_kernel(q_ref, km_ref, info_ref, cnt_ref, run_scr, *, nb, steps):
    jb = pl.program_id(1)

    @pl.when(jb == 0)
    def _():
        run_scr[...] = jnp.zeros_like(run_scr)

    for v in range(steps):
        pl.when(jb == v)(functools.partial(_route_step, q_ref, km_ref, info_ref, cnt_ref, run_scr, v, nb))


def _route_step(q_ref, km_ref, info_ref, cnt_ref, run_scr, jb, nb_all):
    L = ROUTE_QBLOCKS * MOBA_BLOCK
    nb = min(nb_all, ROUTE_QBLOCKS * (jb + 1))
    km = km_ref[0:nb, :]
    km_hi = km.astype(BF16)
    km_lo = (km - km_hi.astype(F32)).astype(BF16)
    gate2 = lax.dot_general(jnp.concatenate([km_hi, km_lo], axis=0), q_ref[...], NT,
                            preferred_element_type=F32)
    blk = lax.broadcasted_iota(jnp.int32, (nb, L), 0)
    blk_f = blk.astype(F32)
    qblk = jb * ROUTE_QBLOCKS + (lax.broadcasted_iota(jnp.int32, (nb, L), 1) >> 8)
    gate = jnp.where(blk < qblk, gate2[:nb] + gate2[nb:], -jnp.inf)
    picks = _top_blocks(gate, blk_f, nb_all)

    chosen = jnp.where(picks[0] | picks[1] | picks[2], 1.0, 0.0)
    B = MOBA_BLOCK
    before = jnp.where(lax.broadcasted_iota(jnp.int32, (B, B), 0)
                       < lax.broadcasted_iota(jnp.int32, (B, B), 1), 1.0, 0.0).astype(BF16)
    carry = run_scr[0:nb, 0:1]
    base = []
    for b in range(ROUTE_QBLOCKS):
        c_b = chosen[:, b * B:(b + 1) * B]
        within = jnp.dot(c_b.astype(BF16), before, preferred_element_type=F32)
        base.append(carry + within)
        carry = carry + within[:, B - 1:B] + c_b[:, B - 1:B]
    base = jnp.concatenate(base, axis=1)

    rows = []
    for pick in picks:
        rows.append(jnp.sum(jnp.where(pick, base, 0.0), axis=0, keepdims=True))
    for pick in picks:
        bid = jnp.sum(jnp.where(pick, blk_f, 0.0), axis=0, keepdims=True)
        valid = jnp.sum(jnp.where(pick, 1.0, 0.0), axis=0, keepdims=True)
        rows.append(jnp.where(valid > 0.0, bid, float(nb_all)))
    rows += [jnp.zeros((1, L), F32)] * (8 - len(rows))
    info_ref[...] = jnp.concatenate(rows, axis=0).astype(jnp.int32)

    run_scr[0:nb, :] = jnp.broadcast_to(carry, (nb, run_scr.shape[1]))
    cnt_ref[0] = run_scr[...]


def _route(proj, kmean):
    s = proj.shape[0]
    nb = s // MOBA_BLOCK
    L = ROUTE_QBLOCKS * MOBA_BLOCK
    steps = s // L
    return pl.pallas_call(
        functools.partial(_route_kernel, nb=nb, steps=steps),
        grid=(HEADS, steps),
        in_specs=[pl.BlockSpec((L, HEAD_DIM), lambda h, j: (j, h)),
                  pl.BlockSpec((nb, HEAD_DIM), lambda h, j: (0, h))],
        out_specs=[pl.BlockSpec((8, L), lambda h, j: (0, h * steps + j)),
                   pl.BlockSpec((1, nb, 128), lambda h, j: (h, 0, 0))],
        out_shape=[jax.ShapeDtypeStruct((8, HEADS * s), jnp.int32),
                   jax.ShapeDtypeStruct((HEADS, nb, 128), F32)],
        scratch_shapes=[pltpu.VMEM((nb, 128), F32)],
        compiler_params=pltpu.CompilerParams(
            dimension_semantics=("parallel", "arbitrary"), vmem_limit_bytes=VMEM_LIMIT),
        name="moba_route",
    )(proj, kmean)


def _dest_kernel(info_ref, cnt_ref, dest_ref, tb_ref, *, nb, s):
    h = pl.program_id(0)
    cap = 4 * s
    ntile = cap // MOBA_BLOCK
    cnt = cnt_ref[0]
    tiles = jnp.floor((cnt + float(MOBA_BLOCK - 1)) * (1.0 / MOBA_BLOCK))
    lower = jnp.where(lax.broadcasted_iota(jnp.int32, (nb, nb), 1)
                      < lax.broadcasted_iota(jnp.int32, (nb, nb), 0), 1.0, 0.0).astype(BF16)
    start = jnp.dot(lower, tiles.astype(BF16), preferred_element_type=F32)
    start1 = start[:, 0:1]
    tiles1 = tiles[:, 0:1]

    t_f = lax.broadcasted_iota(jnp.int32, (nb, ntile), 1).astype(F32)
    n_f = lax.broadcasted_iota(jnp.int32, (nb, ntile), 0).astype(F32)
    inside = (t_f >= start1) & (t_f < start1 + tiles1)
    tb = jnp.sum(jnp.where(inside, n_f, 0.0), axis=0, keepdims=True)
    used = jnp.sum(jnp.where(inside, 1.0, 0.0), axis=0, keepdims=True)
    tb = jnp.where(used > 0.0, tb, -1.0)
    tb_ref[0] = jnp.concatenate([tb, jnp.full((7, ntile), -1.0, F32)], axis=0).astype(jnp.int32)

    off1 = start1 * float(MOBA_BLOCK)
    CH = DEST_LANE_CHUNK
    blk_f = lax.broadcasted_iota(jnp.int32, (nb, CH), 0).astype(F32)
    lane = lax.broadcasted_iota(jnp.int32, (1, CH), 1)
    trash = HEADS * cap + (lane & (MOBA_BLOCK - 1))
    for ch in range(s // CH):
        sl = slice(ch * CH, (ch + 1) * CH)
        rows = []
        for r in range(MOBA_TOPK):
            rank = info_ref[r:r + 1, sl]
            bid = info_ref[MOBA_TOPK + r:MOBA_TOPK + r + 1, sl]
            off = jnp.sum(jnp.where(blk_f == bid.astype(F32), off1, 0.0), axis=0, keepdims=True)
            rows.append(jnp.where(bid < nb, h * cap + off.astype(jnp.int32) + rank, trash))
        rows += [jnp.zeros((1, CH), jnp.int32)] * (8 - len(rows))
        dest_ref[:, sl] = jnp.concatenate(rows, axis=0)


def _dest(info, cnt, s):
    nb = s // MOBA_BLOCK
    ntile = 4 * s // MOBA_BLOCK
    return pl.pallas_call(
        functools.partial(_dest_kernel, nb=nb, s=s),
        grid=(HEADS,),
        in_specs=[pl.BlockSpec((8, s), lambda h: (0, h)),
                  pl.BlockSpec((1, nb, 128), lambda h: (h, 0, 0))],
        out_specs=[pl.BlockSpec((8, s), lambda h: (0, h)),
                   pl.BlockSpec((1, 8, ntile), lambda h: (h, 0, 0))],
        out_shape=[jax.ShapeDtypeStruct((8, HEADS * s), jnp.int32),
                   jax.ShapeDtypeStruct((HEADS, 8, ntile), jnp.int32)],
        compiler_params=pltpu.CompilerParams(
            dimension_semantics=("parallel",), vmem_limit_bytes=VMEM_LIMIT),
        name="moba_dest",
    )(info, cnt)


def _sc_scatter_rows(rows, idx, n_out):
    m, c = rows.shape
    k = idx.shape[0]
    nwin = m // SC_WINDOW
    mesh = plsc.VectorSubcoreMesh(core_axis_name="core", subcore_axis_name="subcore")

    @pl.kernel(out_type=jax.ShapeDtypeStruct((n_out, c), rows.dtype), mesh=mesh)
    def kern(x_hbm, i_hbm, o_hbm):
        def body(x_vmem, i_vmem):
            pltpu.sync_copy(x_vmem, o_hbm.at[i_vmem.at[0]])

        pltpu.emit_pipeline(
            body, grid=(k * nwin,),
            in_specs=[pl.BlockSpec((SC_WINDOW, c), lambda i: (i % nwin, 0)),
                      pl.BlockSpec((1, SC_WINDOW), lambda i: (0, i))],
            out_specs=[],
            core_axis_name=("core", "subcore"),
            dimension_semantics=(pltpu.PARALLEL,),
        )(x_hbm, i_hbm)

    return kern(rows, idx.reshape(1, k * m))


def _sc_gather_rows(table, idx):
    m = idx.shape[0]
    c = table.shape[1]
    mesh = plsc.VectorSubcoreMesh(core_axis_name="core", subcore_axis_name="subcore")

    @pl.kernel(out_type=jax.ShapeDtypeStruct((m, c), table.dtype), mesh=mesh)
    def kern(x_hbm, i_hbm, o_hbm):
        def body(i_vmem, o_vmem):
            pltpu.sync_copy(x_hbm.at[i_vmem.at[0]], o_vmem)

        pltpu.emit_pipeline(
            body, grid=(m // SC_WINDOW,),
            in_specs=[pl.BlockSpec((1, SC_WINDOW), lambda i: (0, i))],
            out_specs=[pl.BlockSpec((SC_WINDOW, c), lambda i: (i, 0))],
            core_axis_name=("core", "subcore"),
            dimension_semantics=(pltpu.PARALLEL,),
        )(i_hbm, o_hbm)

    return kern(table, idx.reshape(1, m))


def _pack_partial_t(o_norm_t, lse):
    half = HEAD_DIM // 2
    u = lax.bitcast_convert_type(o_norm_t, jnp.uint32) + jnp.uint32(0x8000)
    word = (u[:half] & jnp.uint32(0xFFFF0000)) | (u[half:] >> 16)
    lse_bits = lax.bitcast_convert_type(jnp.broadcast_to(lse, word.shape), jnp.uint32)
    full = jnp.concatenate([word, lse_bits], axis=0)
    return lax.bitcast_convert_type(full, F32).T


def _unpack_partial(part):
    half = HEAD_DIM // 2
    lane = lax.broadcasted_iota(jnp.int32, part.shape, 1)
    swapped = pltpu.roll(part, half, 1)
    lse = jnp.where(lane < half, swapped, part)
    word = lax.bitcast_convert_type(jnp.where(lane < half, part, swapped), jnp.uint32)
    o = jnp.where(lane < half, lax.bitcast_convert_type(word & jnp.uint32(0xFFFF0000), F32),
                  lax.bitcast_convert_type(word << 16, F32))
    return o, lse


def _tiles_kernel(tb_ref, qs_ref, k_ref, vt_ref, after_ref, o_ref, *, ntile):
    del after_ref
    h = pl.program_id(0)
    g = pl.program_id(1)
    B = MOBA_BLOCK
    base = h * ntile + g * TILES_PER_STEP

    @pl.when(tb_ref[base] >= 0)
    def _():
        blocks = [jnp.maximum(tb_ref[base + u], 0) for u in range(TILES_PER_STEP)]
        s_t = [lax.dot_general(k_ref[pl.ds(pl.multiple_of(n * B, B), B), :],
                               qs_ref[u * B:(u + 1) * B, :].astype(BF16), NT,
                               preferred_element_type=F32) for u, n in enumerate(blocks)]
        m = [jnp.max(x, axis=0, keepdims=True) for x in s_t]
        p = [jnp.exp2(x - mm).astype(BF16) for x, mm in zip(s_t, m)]
        acc = [jnp.dot(vt_ref[0, n], pp, preferred_element_type=F32) for n, pp in zip(blocks, p)]
        for u in range(TILES_PER_STEP):
            l = acc[u][HEAD_DIM:HEAD_DIM + 1]
            o_ref[u * B:(u + 1) * B, :] = _pack_partial_t(acc[u][:HEAD_DIM] / l,
                                                          m[u] + jnp.log(l) * LOG2_E)


def _tiles(tb, qsorted, proj, v_t, after, s):
    nb = s // MOBA_BLOCK
    ntile = 4 * s // MOBA_BLOCK
    steps = ntile // TILES_PER_STEP
    rows = TILES_PER_STEP * MOBA_BLOCK
    grid_spec = pltpu.PrefetchScalarGridSpec(
        num_scalar_prefetch=1,
        grid=(HEADS, steps),
        in_specs=[pl.BlockSpec((rows, HEAD_DIM), lambda h, g, tb: (h * steps + g, 0)),
                  pl.BlockSpec((s, HEAD_DIM), lambda h, g, tb: (0, HEADS + h)),
                  pl.BlockSpec((1, nb, HEAD_DIM + ONES_ROWS, MOBA_BLOCK), lambda h, g, tb: (h, 0, 0, 0)),
                  pl.BlockSpec((16, HEAD_DIM), lambda h, g, tb: (0, 0))],
        out_specs=pl.BlockSpec((rows, HEAD_DIM), lambda h, g, tb: (h * steps + g, 0)),
    )
    return pl.pallas_call(
        functools.partial(_tiles_kernel, ntile=ntile),
        grid_spec=grid_spec,
        out_shape=jax.ShapeDtypeStruct(qsorted.shape, F32),
        compiler_params=pltpu.CompilerParams(
            dimension_semantics=("parallel", "arbitrary"), vmem_limit_bytes=VMEM_LIMIT),
        name="moba_tiles",
    )(tb, qsorted, proj, v_t, after)


def _merge_kernel(q_ref, k_ref, v_ref, part_ref, o_ref):
    j = pl.program_id(0)
    B = MOBA_BLOCK
    heads = [slice(hh * HEAD_DIM, (hh + 1) * HEAD_DIM) for hh in range(HEADS)]
    causal = (lax.broadcasted_iota(jnp.int32, (B, B), 1) <= lax.broadcasted_iota(jnp.int32, (B, B), 0))
    s = [lax.dot_general(q_ref[:, hs], k_ref[:, hs], NT, preferred_element_type=F32) for hs in heads]
    s = [jnp.where(causal, x, -jnp.inf) for x in s]
    m_own = [jnp.broadcast_to(jnp.max(x, axis=-1, keepdims=True), (B, HEAD_DIM)) for x in s]
    p = [jnp.exp2(x - jnp.concatenate([m, m], axis=1)) for x, m in zip(s, m_own)]
    ones = jnp.ones((B, HEAD_DIM), BF16)
    ol = [jnp.dot(x.astype(BF16), jnp.concatenate([v_ref[:, hs], ones], axis=1),
                  preferred_element_type=F32) for x, hs in zip(p, heads)]
    for hh, hs in enumerate(heads):
        o_own = ol[hh][:, :HEAD_DIM]
        l_own = ol[hh][:, HEAD_DIM:]
        lse_own = m_own[hh] + jnp.log(l_own) * LOG2_E
        parts = []
        for r in range(MOBA_TOPK):
            o_r, lse_r = _unpack_partial(part_ref[r, hh])
            valid = r < j
            parts.append((jnp.where(valid, o_r, 0.0), jnp.where(valid, lse_r, -jnp.inf)))
        m_all = lse_own
        for _, lse_r in parts:
            m_all = jnp.maximum(m_all, lse_r)
        w = jnp.exp2(lse_own - m_all)
        num = (w / l_own) * o_own
        den = w
        for o_r, lse_r in parts:
            w = jnp.exp2(lse_r - m_all)
            num = num + w * o_r
            den = den + w
        o_ref[:, hs] = (num / den).astype(BF16)


def _merge(proj, parts, s):
    nb = s // MOBA_BLOCK
    width = HEADS * HEAD_DIM
    return pl.pallas_call(
        _merge_kernel,
        grid=(nb,),
        in_specs=[pl.BlockSpec((MOBA_BLOCK, width), lambda j: (j, 0)),
                  pl.BlockSpec((MOBA_BLOCK, width), lambda j: (j, 1)),
                  pl.BlockSpec((MOBA_BLOCK, width), lambda j: (j, 2)),
                  pl.BlockSpec((MOBA_TOPK, HEADS, MOBA_BLOCK, HEAD_DIM), lambda j: (0, 0, j, 0))],
        out_specs=pl.BlockSpec((MOBA_BLOCK, width), lambda j: (j, 0)),
        out_shape=jax.ShapeDtypeStruct((s, width), BF16),
        compiler_params=pltpu.CompilerParams(
            dimension_semantics=("parallel",), vmem_limit_bytes=VMEM_LIMIT),
        name="moba_merge",
    )(proj, proj, proj, parts)


def _moba_sparse(proj, v_t, kmean, qf, run_during_scatter, run_during_gather):
    s = proj.shape[0]
    cap = 4 * s
    n_rows = HEADS * cap + MOBA_BLOCK
    info, cnt = _route(proj, kmean)
    dest, tb = _dest(info, cnt, s)
    dest3 = dest[:MOBA_TOPK]
    qsorted = _sc_scatter_rows(qf.reshape(HEADS * s, HEAD_DIM), dest3, n_rows)
    after = run_during_scatter()
    osorted = _tiles(tb[:, 0, :].reshape(-1), qsorted, proj, v_t, after, s)
    parts = _sc_gather_rows(osorted, dest3.reshape(-1))
    run_during_gather(osorted)
    return _merge(proj, parts.reshape(MOBA_TOPK, HEADS, s, HEAD_DIM), s)


def _delta_kernel(q_ref, k_ref, v_ref, qh_ref, kh_ref, vh_ref, z_ref, small_ref,
                  cwq_ref, cwk_ref, cwv_ref, alog_ref, dtb_ref, onw_ref, st_in_ref, *rest, hg, c0):
    o_ref, st_out_ref, state_scr, xx_scr = rest[-4:]
    c = c0 + pl.program_id(0)
    C = DN_CHUNK

    @pl.when(pl.program_id(0) == 0)
    def _():
        state_scr[...] = st_in_ref[...]

    shifts = DN_CONV - 1
    r_out = lax.broadcasted_iota(jnp.int32, (shifts * C, C), 0)
    r_in = lax.broadcasted_iota(jnp.int32, (shifts * C, C), 1)
    delay = jnp.where((r_out & (C - 1)) - r_in == (r_out >> 8) + 1, 1.0, 0.0).astype(BF16)
    assert C == 256

    def conv_silu(idx, x_ref, halo_ref, cw_ref):
        x = x_ref[...]
        delayed = jnp.dot(delay, x, preferred_element_type=F32)
        halo = jnp.where(c == 0, 0.0, halo_ref[...].astype(F32))
        xx_scr[idx, 0:HALO, :] = halo
        xx_scr[idx, HALO:HALO + HALO, :] = x_ref[0:HALO, :].astype(F32)
        y = cw_ref[shifts:shifts + 1, :] * x.astype(F32)
        for j in range(1, shifts + 1):
            d_j = delayed[(j - 1) * C:j * C]
            head8 = xx_scr[idx, HALO - j:HALO - j + 8, :]
            d_j = jnp.concatenate([head8, d_j[8:]], axis=0)
            y = y + cw_ref[shifts - j:shifts - j + 1, :] * d_j
        return y * _sigmoid(y)

    q_all = conv_silu(0, q_ref, qh_ref, cwq_ref)
    k_all = conv_silu(1, k_ref, kh_ref, cwk_ref)
    v_all = conv_silu(2, v_ref, vh_ref, cwv_ref)

    small = small_ref[...]
    beta_all = _sigmoid(small)
    xs = small + dtb_ref[...]
    softplus = jnp.maximum(xs, 0.0) + jnp.log(1.0 + jnp.exp(-jnp.abs(xs)))
    g_all = -jnp.exp(alog_ref[...]) * softplus

    row = lax.broadcasted_iota(jnp.int32, (C, C), 0)
    col = lax.broadcasted_iota(jnp.int32, (C, C), 1)
    tril = row >= col
    strict = row > col
    rxc = row ^ col
    eye = (row == col).astype(F32)
    assert C == 2 * HEAD_DIM

    heads = range(DN_HEADS_PER_STEP)
    sls = [slice(hh * HEAD_DIM, (hh + 1) * HEAD_DIM) for hh in heads]

    def hmap(f, *lists):
        return [f(*xs) for xs in zip(*lists)]

    def pieces(x, n):
        out, rest = [], x
        for _ in range(n):
            p = rest.astype(BF16)
            out.append(p)
            rest = rest - p.astype(F32)
        return jnp.concatenate(out, axis=1)

    hp = DN_HEADS_PER_STEP
    same_half = jnp.where((row >> 7) == (col >> 7), 1.0, 0.0).astype(BF16)

    def unit_rows_pair(xq, xk):
        sq = jnp.concatenate([xq * xq, xk * xk], axis=1).astype(BF16)
        inv = lax.rsqrt(jnp.dot(sq, same_half, preferred_element_type=F32) + EPS)
        return xq * inv[:, :HEAD_DIM], xk * inv[:, HEAD_DIM:]

    def selector(first_lane, n):
        src = lax.broadcasted_iota(jnp.int32, (128, hp * HEAD_DIM), 0)
        dst = lax.broadcasted_iota(jnp.int32, (128, hp * HEAD_DIM), 1) >> 7
        sel = jnp.where(src == first_lane + dst, 1.0, 0.0).astype(BF16)
        return jnp.concatenate([sel] * n, axis=0)

    qk = [unit_rows_pair(q_all[:, sl], k_all[:, sl]) for sl in sls]
    q = [x[0] * (HEAD_DIM ** -0.5) for x in qk]
    k = [x[1] for x in qk]
    v = [v_all[:, sl] for sl in sls]
    beta_rep = jnp.dot(pieces(beta_all, 1), selector(hg * hp, 1), preferred_element_type=F32)
    beta = [beta_rep[:, sl] for sl in sls]

    cum3 = jnp.dot(jnp.where(tril, 1.0, 0.0).astype(BF16), pieces(g_all, 3), preferred_element_type=F32)
    gcum_all = cum3[:, :128] + cum3[:, 128:256] + cum3[:, 256:]
    gcum_rep = jnp.dot(pieces(gcum_all, 3), selector(hg * hp + HEADS, 3), preferred_element_type=F32)
    gcum_b = [gcum_rep[:, sl] for sl in sls]
    gr = [x.T[0:1, :] for x in gcum_b]
    decay = hmap(lambda c_, r_: jnp.where(
        tril, jnp.exp(jnp.where(tril, jnp.concatenate([c_, c_], axis=1) - r_, 0.0)), 0.0), gcum_b, gr)
    e_g = hmap(jnp.exp, gcum_b)
    g_last = [x[C - 1:C, :] for x in gcum_b]

    kb = hmap(lambda a_, b_: a_ * b_, k, beta)
    vb = hmap(lambda a_, b_: a_ * b_, v, beta)
    a = hmap(lambda kb_, q_, k_: lax.dot_general(
        jnp.concatenate([kb_, q_], axis=0).astype(BF16), k_.astype(BF16), NT,
        preferred_element_type=F32), kb, q, k)
    lmat = hmap(lambda a_, d_: jnp.where(strict, a_[:C] * d_, 0.0), a, decay)
    attn = hmap(lambda a_, d_: a_[C:] * d_, a, decay)

    d1 = hmap(lambda l_: jnp.where(rxc < 8, l_, 0.0), lmat)
    d2 = hmap(lambda x: _bdot(x, x), d1)
    d4 = hmap(lambda x: _bdot(x, x), d2)
    p1 = hmap(lambda x, y: _bdot(eye - x, eye + y), d1, d2)
    tmat = hmap(lambda x, y: _bdot(x, eye + y), p1, d4)

    def odd_rows(t, sz):
        return jnp.concatenate([t[b * 2 * sz + sz:(b + 1) * 2 * sz] for b in range(C // (2 * sz))], axis=0)

    def with_odd_rows(t, odd, sz):
        pieces = []
        for b in range(C // (2 * sz)):
            pieces += [t[b * 2 * sz:b * 2 * sz + sz], odd[b * sz:(b + 1) * sz]]
        return jnp.concatenate(pieces, axis=0)

    sz = 8
    while sz < C:
        off = hmap(lambda l_: jnp.where((rxc >= sz) & (rxc < 2 * sz), l_, 0.0), lmat)
        t_odd = hmap(lambda t_: odd_rows(t_, sz), tmat)
        x = hmap(_bdot, t_odd, off)
        x = hmap(_bdot, x, tmat)
        tmat = hmap(lambda t_, o_, x_: with_odd_rows(t_, o_ - x_, sz), tmat, t_odd, x)
        sz *= 2

    uw = hmap(lambda t_, vb_, kb_, e_: _bdot(t_, jnp.concatenate([vb_, kb_ * e_], axis=1)),
              tmat, vb, kb, e_g)
    state = [state_scr[hh] for hh in heads]
    wq = hmap(lambda uw_, q_, e_, s_: _bdot(jnp.concatenate([uw_[:, HEAD_DIM:], q_ * e_], axis=0), s_),
              uw, q, e_g, state)
    v_new = hmap(lambda uw_, wq_: uw_[:, :HEAD_DIM] - wq_[:C], uw, wq)
    av = hmap(_bdot, attn, v_new)
    kt = hmap(lambda k_, gl_, gc_: (k_ * jnp.exp(gl_ - gc_)).T, k, g_last, gcum_b)
    ds = hmap(_bdot, kt, v_new)
    for hh in heads:
        new_state = state[hh] * jnp.exp(g_last[hh]) + ds[hh]
        state_scr[hh] = new_state
        st_out_ref[hh] = new_state
        o = wq[hh][C:] + av[hh]
        y = o * lax.rsqrt(jnp.mean(o * o, axis=-1, keepdims=True) + EPS) * onw_ref[...]
        z = z_ref[:, sls[hh]].astype(F32)
        o_ref[:, sls[hh]] = (y * (z * _sigmoid(z))).astype(BF16)


def _gated_deltanet(proj, small, conv_w, alog_v, dtb_v, onw, state, c0, n_chunks, yd_prev=None, after=None):
    s = proj.shape[0]
    rb = DN_CHUNK // HALO
    hp = DN_HEADS_PER_STEP
    assert hp == HEADS
    width = hp * HEAD_DIM

    def col(part):
        return pl.BlockSpec((DN_CHUNK, width), lambda c: (c0 + c, part))

    def halo(part):
        return pl.BlockSpec((HALO, width), lambda c: (jnp.maximum((c0 + c) * rb - 1, 0), part))

    def cw(part):
        return pl.BlockSpec((DN_CONV, width), lambda c: (0, part))

    vec = pl.BlockSpec((1, 128), lambda c: (0, 0))
    st_spec = pl.BlockSpec((hp, HEAD_DIM, HEAD_DIM), lambda c: (0, 0, 0))
    extras = [a for a in (yd_prev, after) if a is not None]
    n_fixed = 15
    return pl.pallas_call(
        functools.partial(_delta_kernel, hg=0, c0=c0),
        grid=(n_chunks,),
        in_specs=[col(3), col(4), col(5),
                  halo(3), halo(4), halo(5),
                  col(6),
                  pl.BlockSpec((DN_CHUNK, 128), lambda c: (c0 + c, 0)),
                  cw(0), cw(1), cw(2),
                  vec, vec, vec, st_spec] + [pl.BlockSpec(memory_space=pl.ANY)] * len(extras),
        out_specs=[pl.BlockSpec((DN_CHUNK, width), lambda c: (c0 + c, 0)), st_spec],
        out_shape=[jax.ShapeDtypeStruct((s, width), BF16),
                   jax.ShapeDtypeStruct((hp, HEAD_DIM, HEAD_DIM), F32)],
        input_output_aliases={n_fixed: 0} if yd_prev is not None else {},
        scratch_shapes=[pltpu.VMEM((hp, HEAD_DIM, HEAD_DIM), F32),
                        pltpu.VMEM((3, HALO + DN_CHUNK, width), F32)],
        compiler_params=pltpu.CompilerParams(
            dimension_semantics=("arbitrary",), vmem_limit_bytes=VMEM_LIMIT),
        name="deltanet",
    )(proj, proj, proj, proj, proj, proj, proj, small, conv_w, conv_w, conv_w, alog_v, dtb_v, onw,
      state, *extras)


def _mix_out_kernel(ya_ref, *rest):
    ng = HEADS // DN_HEADS_PER_STEP
    yd_refs = rest[:ng]
    ga_ref, gd_ref, x_ref, wa_ref, wd_ref, wo_ref, nw_ref, o_ref = rest[ng:]
    width = DN_HEADS_PER_STEP * HEAD_DIM
    pa = jnp.dot(ya_ref[...], wa_ref[...], preferred_element_type=F32)
    pd = jnp.dot(yd_refs[0][...], wd_ref[0:width, :], preferred_element_type=F32)
    for g in range(1, ng):
        pd = pd + jnp.dot(yd_refs[g][...], wd_ref[g * width:(g + 1) * width, :],
                          preferred_element_type=F32)
    merged = _sigmoid(ga_ref[...].astype(F32)) * pa + _sigmoid(gd_ref[...].astype(F32)) * pd
    mo = jnp.dot(merged.astype(BF16), wo_ref[...], preferred_element_type=F32)
    y = mo * lax.rsqrt(jnp.mean(mo * mo, axis=-1, keepdims=True) + EPS) * nw_ref[...]
    o_ref[...] = x_ref[...] + y


def _mix_out(ya, yds, proj, x2, wa, wd, wo, nw, *, tm=512):
    s = x2.shape[0]
    row = lambda i: (i, 0)
    full = lambda i: (0, 0)
    wspec = pl.BlockSpec((D_MODEL, D_MODEL), full)
    return pl.pallas_call(
        _mix_out_kernel,
        grid=(s // tm,),
        in_specs=[pl.BlockSpec((tm, D_MODEL), row)]
                 + [pl.BlockSpec((tm, yd.shape[1]), row) for yd in yds]
                 + [pl.BlockSpec((tm, D_MODEL), lambda i: (i, 7)),
                  pl.BlockSpec((tm, D_MODEL), lambda i: (i, 8)),
                  pl.BlockSpec((tm, D_MODEL), row),
                  wspec, wspec, wspec, pl.BlockSpec((1, D_MODEL), full)],
        out_specs=pl.BlockSpec((tm, D_MODEL), row),
        out_shape=jax.ShapeDtypeStruct((s, D_MODEL), F32),
        compiler_params=pltpu.CompilerParams(
            dimension_semantics=("parallel",), vmem_limit_bytes=VMEM_LIMIT),
        name="mix_out",
    )(ya, *yds, proj, proj, x2, wa, wd, wo, nw)


def _ffn_kernel(x_ref, npre_ref, wg_ref, wu_ref, wd_ref, npost_ref, o_ref):
    x = x_ref[...]
    h = (x * lax.rsqrt(jnp.mean(x * x, axis=-1, keepdims=True) + EPS) * npre_ref[...]).astype(BF16)
    acc = jnp.zeros(x.shape, F32)
    for cc in range(D_FF // FF_CHUNK):
        sl = slice(cc * FF_CHUNK, (cc + 1) * FF_CHUNK)
        g = jnp.dot(h, wg_ref[:, sl], preferred_element_type=F32)
        u = jnp.dot(h, wu_ref[:, sl], preferred_element_type=F32)
        act = (g * _sigmoid(g) * u).astype(BF16)
        acc = acc + jnp.dot(act, wd_ref[sl, :], preferred_element_type=F32)
    y = acc * lax.rsqrt(jnp.mean(acc * acc, axis=-1, keepdims=True) + EPS) * npost_ref[...]
    o_ref[...] = x + y


def _ffn(x1, npre, wg, wu, wd, npost, *, tm=512):
    s = x1.shape[0]
    row = lambda i: (i, 0)
    full = lambda i: (0, 0)
    once = pl.Buffered(1)
    return pl.pallas_call(
        _ffn_kernel,
        grid=(s // tm,),
        in_specs=[pl.BlockSpec((tm, D_MODEL), row), pl.BlockSpec((1, D_MODEL), full),
                  pl.BlockSpec((D_MODEL, D_FF), full, pipeline_mode=once),
                  pl.BlockSpec((D_MODEL, D_FF), full, pipeline_mode=once),
                  pl.BlockSpec((D_FF, D_MODEL), full, pipeline_mode=once),
                  pl.BlockSpec((1, D_MODEL), full)],
        out_specs=pl.BlockSpec((tm, D_MODEL), row),
        out_shape=jax.ShapeDtypeStruct((s, D_MODEL), F32),
        compiler_params=pltpu.CompilerParams(
            dimension_semantics=("parallel",), vmem_limit_bytes=VMEM_LIMIT),
        name="ffn",
    )(x1, npre, wg, wu, wd, npost)


def _rope_table(s):
    half = ROPE_DIM // 2
    rep = 128 // half
    inv = ROPE_THETA ** (-jnp.arange(half, dtype=F32) * 2.0 / ROPE_DIM)
    pos = (jnp.arange(s // rep)[:, None] * rep + jnp.arange(128)[None, :] // half).astype(F32)
    ang = pos * jnp.tile(inv, rep)[None, :]
    cos, sin = lax.optimization_barrier((jnp.cos(ang), jnp.sin(ang)))
    cos = cos.reshape(s, half)
    sin = sin.reshape(s, half)
    return jnp.concatenate([cos, sin, jnp.zeros((s, HEAD_DIM - ROPE_DIM), F32)], axis=-1)


def _layer(x2, l, norm_mix_pre, w_in, conv_w, a_log, dt_bias, o_norm_w, w_o_attn, w_o_delta,
           w_out, norm_mix_post, norm_ffn_pre, w_gate, w_up, w_down, norm_ffn_post):
    s = x2.shape[0]
    nb = s // MOBA_BLOCK
    wide = 7 * D_MODEL
    nsmall = 2 * HEADS
    w_t = jnp.swapaxes(w_in[l], 0, 1).astype(BF16)
    later_w = [w[l].astype(BF16) for w in (w_o_attn, w_o_delta, w_out, w_gate, w_up, w_down)]

    proj, small, kmean, v_t, qf = _project(x2, norm_mix_pre[l][None, :], w_t, _rope_table(s),
                                           wide // D_MODEL, nsmall, after=later_w)
    pad = lambda vec: jnp.pad(vec.astype(F32), (HEADS, 128 - 2 * HEADS))[None, :]
    half = (s // DN_CHUNK) // 2
    delta = functools.partial(_gated_deltanet, proj, small, conv_w[l], pad(a_log[l]), pad(dt_bias[l]),
                              o_norm_w[l][None, :])
    res = {}

    def first_half():
        res["yd"], res["state"] = delta(jnp.zeros((HEADS, HEAD_DIM, HEAD_DIM), F32), 0, half)
        return res["yd"]

    def second_half(after):
        res["yd"], _ = delta(res["state"], half, s // DN_CHUNK - half, yd_prev=res["yd"], after=after)

    ya = _moba_sparse(proj, v_t, kmean.reshape(nb, D_MODEL), qf, first_half, second_half)

    x1 = _mix_out(ya, [res["yd"]], proj, x2, *later_w[:3], norm_mix_post[l][None, :])
    return _ffn(x1, norm_ffn_pre[l][None, :], *later_w[3:], norm_ffn_post[l][None, :])


def kernel(x, norm_mix_pre, w_in, conv_w, a_log, dt_bias, o_norm_w, w_o_attn, w_o_delta, w_out,
           norm_mix_post, norm_ffn_pre, w_gate, w_up, w_down, norm_ffn_post):
    b, s, d = x.shape
    assert d == D_MODEL and s % 512 == 0
    outs = []
    for bi in range(b):
        x2 = x.reshape(s, d) if b == 1 else x[bi]
        for l in range(w_in.shape[0]):
            x2 = _layer(x2, l, norm_mix_pre, w_in, conv_w, a_log, dt_bias, o_norm_w, w_o_attn,
                        w_o_delta, w_out, norm_mix_post, norm_ffn_pre, w_gate, w_up, w_down,
                        norm_ffn_post)
        outs.append(x2)
    return outs[0].reshape(1, s, d) if b == 1 else jnp.stack(outs, axis=0)
```

```python
import functools
import math

import jax
import jax.numpy as jnp
from jax import lax
from jax.experimental import pallas as pl
from jax.experimental.pallas import tpu as pltpu
from jax.experimental.pallas import tpu_sc as plsc

D_MODEL = 1024
HEADS = 8
HEAD_DIM = 128
MOBA_BLOCK = 256
MOBA_TOPK = 3
ROPE_DIM = HEAD_DIM // 4
ROPE_THETA = 500000.0
DN_CONV = 4
ONES_ROWS = 16
DN_HEADS_PER_STEP = 8
DN_CHUNK = 256
D_FF = 2816
FF_CHUNK = 256
EPS = 1e-6
LOG2_E = math.log2(math.e)
HALO = 16

F32 = jnp.float32
BF16 = jnp.bfloat16
NT = (((1,), (1,)), ((), ()))

VMEM_LIMIT = 56 * 1024 * 1024


def _bdot(a, b):
    return jnp.dot(a.astype(BF16), b.astype(BF16), preferred_element_type=F32)


def _sigmoid(x):
    return 1.0 / (1.0 + jnp.exp(-x))


def _proj_kernel(x_ref, nw_ref, wt_ref, rope_ref, *rest, tm, q_scale, n_wide, n_small):
    out_ref, small_ref, kmean_ref, vt_ref, qf_ref = rest[-5:]
    x = x_ref[...]
    ms = jnp.mean(x * x, axis=-1, keepdims=True)
    h = (x * lax.rsqrt(ms + EPS) * nw_ref[...]).astype(BF16)
    narrow = lax.dot_general(h, wt_ref[n_wide * D_MODEL:n_wide * D_MODEL + 128, :], NT,
                             preferred_element_type=F32)
    small_ref[...] = jnp.where(lax.broadcasted_iota(jnp.int32, narrow.shape, 1) < n_small, narrow, 0.0)
    heads = [slice(hh * HEAD_DIM, (hh + 1) * HEAD_DIM) for hh in range(HEADS)]
    groups = [slice(g * MOBA_BLOCK, (g + 1) * MOBA_BLOCK) for g in range(tm // MOBA_BLOCK)]

    half = ROPE_DIM // 2
    tab = rope_ref[...]
    lane = lax.broadcasted_iota(jnp.int32, tab.shape, 1)
    cos_t = jnp.where(lane < half, tab, jnp.where(lane < ROPE_DIM, pltpu.roll(tab, half, 1), 1.0))
    s1_t = jnp.where(lane < half, -pltpu.roll(tab, HEAD_DIM - half, 1), 0.0)
    s2_t = jnp.where((lane >= half) & (lane < ROPE_DIM), tab, 0.0)

    def roped(a):
        return (a * cos_t + pltpu.roll(a, HEAD_DIM - half, 1) * s1_t + pltpu.roll(a, half, 1) * s2_t)

    for c in range(out_ref.shape[1] // D_MODEL):
        cols = slice(c * D_MODEL, (c + 1) * D_MODEL)
        r0 = c * D_MODEL + (n_small if c >= n_wide else 0)
        acc = lax.dot_general(h, wt_ref[r0:r0 + D_MODEL, :], NT, preferred_element_type=F32)
        if c == 0:
            for hh, hs in enumerate(heads):
                r = roped(acc[:, hs]) * q_scale
                out_ref[:, hs] = r.astype(BF16)
                qf_ref[hh] = r
        elif c == 1:
            for hs in heads:
                r = roped(acc[:, hs])
                out_ref[:, D_MODEL + hs.start:D_MODEL + hs.stop] = r.astype(BF16)
                for g, gs in enumerate(groups):
                    kmean_ref[0, g:g + 1, hs] = jnp.sum(r[gs], axis=0, keepdims=True) * (1.0 / MOBA_BLOCK)
        elif c == 2:
            out_ref[:, cols] = acc.astype(BF16)
            for hh, hs in enumerate(heads):
                for g, gs in enumerate(groups):
                    vt_ref[hh, g, 0:HEAD_DIM, :] = acc[gs, hs].T.astype(BF16)
                    vt_ref[hh, g, HEAD_DIM:, :] = jnp.ones((ONES_ROWS, MOBA_BLOCK), BF16)
        else:
            out_ref[:, cols] = acc.astype(BF16)


def _project(x2, norm_w, w_t, rope_t, n_wide, n_small, after=(), *, tm=512):
    s = x2.shape[0]
    width = (w_t.shape[0] - n_small) // D_MODEL * D_MODEL
    once = pl.Buffered(1)
    nblk = tm // MOBA_BLOCK
    kern = functools.partial(_proj_kernel, tm=tm, q_scale=math.log2(math.e) / math.sqrt(HEAD_DIM),
                             n_wide=n_wide, n_small=n_small)
    row = lambda i: (i, 0)
    full = lambda i: (0, 0)
    return pl.pallas_call(
        kern,
        grid=(s // tm,),
        in_specs=[
            pl.BlockSpec((tm, D_MODEL), row),
            pl.BlockSpec((1, D_MODEL), full),
            pl.BlockSpec(w_t.shape, full, pipeline_mode=once),
            pl.BlockSpec((tm, HEAD_DIM), row),
        ] + [pl.BlockSpec(memory_space=pl.ANY)] * len(after),
        out_specs=[
            pl.BlockSpec((tm, width), row),
            pl.BlockSpec((tm, 128), row),
            pl.BlockSpec((1, nblk, D_MODEL), lambda i: (i, 0, 0)),
            pl.BlockSpec((HEADS, nblk, HEAD_DIM + ONES_ROWS, MOBA_BLOCK), lambda i: (0, i, 0, 0)),
            pl.BlockSpec((HEADS, tm, HEAD_DIM), lambda i: (0, i, 0)),
        ],
        out_shape=[
            jax.ShapeDtypeStruct((s, width), BF16),
            jax.ShapeDtypeStruct((s, 128), F32),
            jax.ShapeDtypeStruct((s // tm, nblk, D_MODEL), F32),
            jax.ShapeDtypeStruct((HEADS, s // MOBA_BLOCK, HEAD_DIM + ONES_ROWS, MOBA_BLOCK), BF16),
            jax.ShapeDtypeStruct((HEADS, s, HEAD_DIM), F32),
        ],
        compiler_params=pltpu.CompilerParams(
            dimension_semantics=("parallel",), vmem_limit_bytes=VMEM_LIMIT),
        name="proj",
    )(x2, norm_w, w_t, rope_t, *after)


ROUTE_QBLOCKS = 16
TILES_PER_STEP = 32
SC_WINDOW = 128
DEST_LANE_CHUNK = 2048


def _top_blocks(gate, blk_f, nb):
    picks = []
    for _ in range(MOBA_TOPK):
        m = jnp.max(gate, axis=0, keepdims=True)
        first = jnp.min(jnp.where(gate == m, blk_f, float(nb)), axis=0, keepdims=True)
        pick = (blk_f == first) & (m > -jnp.inf)
        gate = jnp.where(pick, -jnp.inf, gate)
        picks.append(pick)
    return picks


def _route_kernel(q_ref, km_ref, info_ref, cnt_ref, run_scr, *, nb, steps):
    jb = pl.program_id(1)

    @pl.when(jb == 0)
    def _():
        run_scr[...] = jnp.zeros_like(run_scr)

    for v in range(steps):
        pl.when(jb == v)(functools.partial(_route_step, q_ref, km_ref, info_ref, cnt_ref, run_scr, v, nb))


def _route_step(q_ref, km_ref, info_ref, cnt_ref, run_scr, jb, nb_all):
    L = ROUTE_QBLOCKS * MOBA_BLOCK
    nb = min(nb_all, ROUTE_QBLOCKS * (jb + 1))
    km = km_ref[0:nb, :]
    km_hi = km.astype(BF16)
    km_lo = (km - km_hi.astype(F32)).astype(BF16)
    gate2 = lax.dot_general(jnp.concatenate([km_hi, km_lo], axis=0), q_ref[...], NT,
                            preferred_element_type=F32)
    blk = lax.broadcasted_iota(jnp.int32, (nb, L), 0)
    blk_f = blk.astype(F32)
    qblk = jb * ROUTE_QBLOCKS + (lax.broadcasted_iota(jnp.int32, (nb, L), 1) >> 8)
    gate = jnp.where(blk < qblk, gate2[:nb] + gate2[nb:], -jnp.inf)
    picks = _top_blocks(gate, blk_f, nb_all)

    chosen = jnp.where(picks[0] | picks[1] | picks[2], 1.0, 0.0)
    B = MOBA_BLOCK
    before = jnp.where(lax.broadcasted_iota(jnp.int32, (B, B), 0)
                       < lax.broadcasted_iota(jnp.int32, (B, B), 1), 1.0, 0.0).astype(BF16)
    carry = run_scr[0:nb, 0:1]
    base = []
    for b in range(ROUTE_QBLOCKS):
        c_b = chosen[:, b * B:(b + 1) * B]
        within = jnp.dot(c_b.astype(BF16), before, preferred_element_type=F32)
        base.append(carry + within)
        carry = carry + within[:, B - 1:B] + c_b[:, B - 1:B]
    base = jnp.concatenate(base, axis=1)

    rows = []
    for pick in picks:
        rows.append(jnp.sum(jnp.where(pick, base, 0.0), axis=0, keepdims=True))
    for pick in picks:
        bid = jnp.sum(jnp.where(pick, blk_f, 0.0), axis=0, keepdims=True)
        valid = jnp.sum(jnp.where(pick, 1.0, 0.0), axis=0, keepdims=True)
        rows.append(jnp.where(valid > 0.0, bid, float(nb_all)))
    rows += [jnp.zeros((1, L), F32)] * (8 - len(rows))
    info_ref[...] = jnp.concatenate(rows, axis=0).astype(jnp.int32)

    run_scr[0:nb, :] = jnp.broadcast_to(carry, (nb, run_scr.shape[1]))
    cnt_ref[0] = run_scr[...]


def _route(proj, kmean):
    s = proj.shape[0]
    nb = s // MOBA_BLOCK
    L = ROUTE_QBLOCKS * MOBA_BLOCK
    steps = s // L
    return pl.pallas_call(
        functools.partial(_route_kernel, nb=nb, steps=steps),
        grid=(HEADS, steps),
        in_specs=[pl.BlockSpec((L, HEAD_DIM), lambda h, j: (j, h)),
                  pl.BlockSpec((nb, HEAD_DIM), lambda h, j: (0, h))],
        out_specs=[pl.BlockSpec((8, L), lambda h, j: (0, h * steps + j)),
                   pl.BlockSpec((1, nb, 128), lambda h, j: (h, 0, 0))],
        out_shape=[jax.ShapeDtypeStruct((8, HEADS * s), jnp.int32),
                   jax.ShapeDtypeStruct((HEADS, nb, 128), F32)],
        scratch_shapes=[pltpu.VMEM((nb, 128), F32)],
        compiler_params=pltpu.CompilerParams(
            dimension_semantics=("parallel", "arbitrary"), vmem_limit_bytes=VMEM_LIMIT),
        name="moba_route",
    )(proj, kmean)


def _dest_kernel(info_ref, cnt_ref, dest_ref, tb_ref, *, nb, s):
    h = pl.program_id(0)
    cap = 4 * s
    ntile = cap // MOBA_BLOCK
    cnt = cnt_ref[0]
    tiles = jnp.floor((cnt + float(MOBA_BLOCK - 1)) * (1.0 / MOBA_BLOCK))
    lower = jnp.where(lax.broadcasted_iota(jnp.int32, (nb, nb), 1)
                      < lax.broadcasted_iota(jnp.int32, (nb, nb), 0), 1.0, 0.0).astype(BF16)
    start = jnp.dot(lower, tiles.astype(BF16), preferred_element_type=F32)
    start1 = start[:, 0:1]
    tiles1 = tiles[:, 0:1]

    t_f = lax.broadcasted_iota(jnp.int32, (nb, ntile), 1).astype(F32)
    n_f = lax.broadcasted_iota(jnp.int32, (nb, ntile), 0).astype(F32)
    inside = (t_f >= start1) & (t_f < start1 + tiles1)
    tb = jnp.sum(jnp.where(inside, n_f, 0.0), axis=0, keepdims=True)
    used = jnp.sum(jnp.where(inside, 1.0, 0.0), axis=0, keepdims=True)
    tb = jnp.where(used > 0.0, tb, -1.0)
    tb_ref[0] = jnp.concatenate([tb, jnp.full((7, ntile), -1.0, F32)], axis=0).astype(jnp.int32)

    off1 = start1 * float(MOBA_BLOCK)
    CH = DEST_LANE_CHUNK
    blk_f = lax.broadcasted_iota(jnp.int32, (nb, CH), 0).astype(F32)
    lane = lax.broadcasted_iota(jnp.int32, (1, CH), 1)
    trash = HEADS * cap + (lane & (MOBA_BLOCK - 1))
    for ch in range(s // CH):
        sl = slice(ch * CH, (ch + 1) * CH)
        rows = []
        for r in range(MOBA_TOPK):
            rank = info_ref[r:r + 1, sl]
            bid = info_ref[MOBA_TOPK + r:MOBA_TOPK + r + 1, sl]
            off = jnp.sum(jnp.where(blk_f == bid.astype(F32), off1, 0.0), axis=0, keepdims=True)
            rows.append(jnp.where(bid < nb, h * cap + off.astype(jnp.int32) + rank, trash))
        rows += [jnp.zeros((1, CH), jnp.int32)] * (8 - len(rows))
        dest_ref[:, sl] = jnp.concatenate(rows, axis=0)


def _dest(info, cnt, s):
    nb = s // MOBA_BLOCK
    ntile = 4 * s // MOBA_BLOCK
    return pl.pallas_call(
        functools.partial(_dest_kernel, nb=nb, s=s),
        grid=(HEADS,),
        in_specs=[pl.BlockSpec((8, s), lambda h: (0, h)),
                  pl.BlockSpec((1, nb, 128), lambda h: (h, 0, 0))],
        out_specs=[pl.BlockSpec((8, s), lambda h: (0, h)),
                   pl.BlockSpec((1, 8, ntile), lambda h: (h, 0, 0))],
        out_shape=[jax.ShapeDtypeStruct((8, HEADS * s), jnp.int32),
                   jax.ShapeDtypeStruct((HEADS, 8, ntile), jnp.int32)],
        compiler_params=pltpu.CompilerParams(
            dimension_semantics=("parallel",), vmem_limit_bytes=VMEM_LIMIT),
        name="moba_dest",
    )(info, cnt)


def _sc_scatter_rows(rows, idx, n_out):
    m, c = rows.shape
    k = idx.shape[0]
    nwin = m // SC_WINDOW
    mesh = plsc.VectorSubcoreMesh(core_axis_name="core", subcore_axis_name="subcore")

    @pl.kernel(out_type=jax.ShapeDtypeStruct((n_out, c), rows.dtype), mesh=mesh)
    def kern(x_hbm, i_hbm, o_hbm):
        def body(x_vmem, i_vmem):
            pltpu.sync_copy(x_vmem, o_hbm.at[i_vmem.at[0]])

        pltpu.emit_pipeline(
            body, grid=(k * nwin,),
            in_specs=[pl.BlockSpec((SC_WINDOW, c), lambda i: (i % nwin, 0)),
                      pl.BlockSpec((1, SC_WINDOW), lambda i: (0, i))],
            out_specs=[],
            core_axis_name=("core", "subcore"),
            dimension_semantics=(pltpu.PARALLEL,),
        )(x_hbm, i_hbm)

    return kern(rows, idx.reshape(1, k * m))


def _sc_gather_rows(table, idx):
    m = idx.shape[0]
    c = table.shape[1]
    mesh = plsc.VectorSubcoreMesh(core_axis_name="core", subcore_axis_name="subcore")

    @pl.kernel(out_type=jax.ShapeDtypeStruct((m, c), table.dtype), mesh=mesh)
    def kern(x_hbm, i_hbm, o_hbm):
        def body(i_vmem, o_vmem):
            pltpu.sync_copy(x_hbm.at[i_vmem.at[0]], o_vmem)

        pltpu.emit_pipeline(
            body, grid=(m // SC_WINDOW,),
            in_specs=[pl.BlockSpec((1, SC_WINDOW), lambda i: (0, i))],
            out_specs=[pl.BlockSpec((SC_WINDOW, c), lambda i: (i, 0))],
            core_axis_name=("core", "subcore"),
            dimension_semantics=(pltpu.PARALLEL,),
        )(i_hbm, o_hbm)

    return kern(table, idx.reshape(1, m))


def _pack_partial_t(o_norm_t, lse):
    half = HEAD_DIM // 2
    u = lax.bitcast_convert_type(o_norm_t, jnp.uint32) + jnp.uint32(0x8000)
    word = (u[:half] & jnp.uint32(0xFFFF0000)) | (u[half:] >> 16)
    lse_bits = lax.bitcast_convert_type(jnp.broadcast_to(lse, word.shape), jnp.uint32)
    full = jnp.concatenate([word, lse_bits], axis=0)
    return lax.bitcast_convert_type(full, F32).T


def _unpack_partial(part):
    half = HEAD_DIM // 2
    lane = lax.broadcasted_iota(jnp.int32, part.shape, 1)
    swapped = pltpu.roll(part, half, 1)
    lse = jnp.where(lane < half, swapped, part)
    word = lax.bitcast_convert_type(jnp.where(lane < half, part, swapped), jnp.uint32)
    o = jnp.where(lane < half, lax.bitcast_convert_type(word & jnp.uint32(0xFFFF0000), F32),
                  lax.bitcast_convert_type(word << 16, F32))
    return o, lse


def _tiles_kernel(tb_ref, qs_ref, k_ref, vt_ref, after_ref, o_ref, *, ntile):
    del after_ref
    h = pl.program_id(0)
    g = pl.program_id(1)
    B = MOBA_BLOCK
    base = h * ntile + g * TILES_PER_STEP

    @pl.when(tb_ref[base] >= 0)
    def _():
        blocks = [jnp.maximum(tb_ref[base + u], 0) for u in range(TILES_PER_STEP)]
        s_t = [lax.dot_general(k_ref[pl.ds(pl.multiple_of(n * B, B), B), :],
                               qs_ref[u * B:(u + 1) * B, :].astype(BF16), NT,
                               preferred_element_type=F32) for u, n in enumerate(blocks)]
        m = [jnp.max(x, axis=0, keepdims=True) for x in s_t]
        p = [jnp.exp2(x - mm).astype(BF16) for x, mm in zip(s_t, m)]
        acc = [jnp.dot(vt_ref[0, n], pp, preferred_element_type=F32) for n, pp in zip(blocks, p)]
        for u in range(TILES_PER_STEP):
            l = acc[u][HEAD_DIM:HEAD_DIM + 1]
            o_ref[u * B:(u + 1) * B, :] = _pack_partial_t(acc[u][:HEAD_DIM] / l,
                                                          m[u] + jnp.log(l) * LOG2_E)


def _tiles(tb, qsorted, proj, v_t, after, s):
    nb = s // MOBA_BLOCK
    ntile = 4 * s // MOBA_BLOCK
    steps = ntile // TILES_PER_STEP
    rows = TILES_PER_STEP * MOBA_BLOCK
    grid_spec = pltpu.PrefetchScalarGridSpec(
        num_scalar_prefetch=1,
        grid=(HEADS, steps),
        in_specs=[pl.BlockSpec((rows, HEAD_DIM), lambda h, g, tb: (h * steps + g, 0)),
                  pl.BlockSpec((s, HEAD_DIM), lambda h, g, tb: (0, HEADS + h)),
                  pl.BlockSpec((1, nb, HEAD_DIM + ONES_ROWS, MOBA_BLOCK), lambda h, g, tb: (h, 0, 0, 0)),
                  pl.BlockSpec((16, HEAD_DIM), lambda h, g, tb: (0, 0))],
        out_specs=pl.BlockSpec((rows, HEAD_DIM), lambda h, g, tb: (h * steps + g, 0)),
    )
    return pl.pallas_call(
        functools.partial(_tiles_kernel, ntile=ntile),
        grid_spec=grid_spec,
        out_shape=jax.ShapeDtypeStruct(qsorted.shape, F32),
        compiler_params=pltpu.CompilerParams(
            dimension_semantics=("parallel", "arbitrary"), vmem_limit_bytes=VMEM_LIMIT),
        name="moba_tiles",
    )(tb, qsorted, proj, v_t, after)


def _merge_kernel(q_ref, k_ref, v_ref, part_ref, o_ref):
    j = pl.program_id(0)
    B = MOBA_BLOCK
    heads = [slice(hh * HEAD_DIM, (hh + 1) * HEAD_DIM) for hh in range(HEADS)]
    causal = (lax.broadcasted_iota(jnp.int32, (B, B), 1) <= lax.broadcasted_iota(jnp.int32, (B, B), 0))
    s = [lax.dot_general(q_ref[:, hs], k_ref[:, hs], NT, preferred_element_type=F32) for hs in heads]
    s = [jnp.where(causal, x, -jnp.inf) for x in s]
    m_own = [jnp.broadcast_to(jnp.max(x, axis=-1, keepdims=True), (B, HEAD_DIM)) for x in s]
    p = [jnp.exp2(x - jnp.concatenate([m, m], axis=1)) for x, m in zip(s, m_own)]
    ones = jnp.ones((B, HEAD_DIM), BF16)
    ol = [jnp.dot(x.astype(BF16), jnp.concatenate([v_ref[:, hs], ones], axis=1),
                  preferred_element_type=F32) for x, hs in zip(p, heads)]
    for hh, hs in enumerate(heads):
        o_own = ol[hh][:, :HEAD_DIM]
        l_own = ol[hh][:, HEAD_DIM:]
        lse_own = m_own[hh] + jnp.log(l_own) * LOG2_E
        parts = []
        for r in range(MOBA_TOPK):
            o_r, lse_r = _unpack_partial(part_ref[r, hh])
            valid = r < j
            parts.append((jnp.where(valid, o_r, 0.0), jnp.where(valid, lse_r, -jnp.inf)))
        m_all = lse_own
        for _, lse_r in parts:
            m_all = jnp.maximum(m_all, lse_r)
        w = jnp.exp2(lse_own - m_all)
        num = (w / l_own) * o_own
        den = w
        for o_r, lse_r in parts:
            w = jnp.exp2(lse_r - m_all)
            num = num + w * o_r
            den = den + w
        o_ref[:, hs] = (num / den).astype(BF16)


def _merge(proj, parts, s):
    nb = s // MOBA_BLOCK
    width = HEADS * HEAD_DIM
    return pl.pallas_call(
        _merge_kernel,
        grid=(nb,),
        in_specs=[pl.BlockSpec((MOBA_BLOCK, width), lambda j: (j, 0)),
                  pl.BlockSpec((MOBA_BLOCK, width), lambda j: (j, 1)),
                  pl.BlockSpec((MOBA_BLOCK, width), lambda j: (j, 2)),
                  pl.BlockSpec((MOBA_TOPK, HEADS, MOBA_BLOCK, HEAD_DIM), lambda j: (0, 0, j, 0))],
        out_specs=pl.BlockSpec((MOBA_BLOCK, width), lambda j: (j, 0)),
        out_shape=jax.ShapeDtypeStruct((s, width), BF16),
        compiler_params=pltpu.CompilerParams(
            dimension_semantics=("parallel",), vmem_limit_bytes=VMEM_LIMIT),
        name="moba_merge",
    )(proj, proj, proj, parts)


def _moba_sparse(proj, v_t, kmean, qf, run_during_scatter, run_during_gather):
    s = proj.shape[0]
    cap = 4 * s
    n_rows = HEADS * cap + MOBA_BLOCK
    info, cnt = _route(proj, kmean)
    dest, tb = _dest(info, cnt, s)
    dest3 = dest[:MOBA_TOPK]
    qsorted = _sc_scatter_rows(qf.reshape(HEADS * s, HEAD_DIM), dest3, n_rows)
    after = run_during_scatter()
    osorted = _tiles(tb[:, 0, :].reshape(-1), qsorted, proj, v_t, after, s)
    parts = _sc_gather_rows(osorted, dest3.reshape(-1))
    run_during_gather(osorted)
    return _merge(proj, parts.reshape(MOBA_TOPK, HEADS, s, HEAD_DIM), s)


def _delta_kernel(q_ref, k_ref, v_ref, qh_ref, kh_ref, vh_ref, z_ref, small_ref,
                  cwq_ref, cwk_ref, cwv_ref, alog_ref, dtb_ref, onw_ref, st_in_ref, *rest, hg, c0):
    o_ref, st_out_ref, state_scr, xx_scr = rest[-4:]
    c = c0 + pl.program_id(0)
    C = DN_CHUNK

    @pl.when(pl.program_id(0) == 0)
    def _():
        state_scr[...] = st_in_ref[...]

    shifts = DN_CONV - 1
    r_out = lax.broadcasted_iota(jnp.int32, (shifts * C, C), 0)
    r_in = lax.broadcasted_iota(jnp.int32, (shifts * C, C), 1)
    delay = jnp.where((r_out & (C - 1)) - r_in == (r_out >> 8) + 1, 1.0, 0.0).astype(BF16)
    assert C == 256

    def conv_silu(idx, x_ref, halo_ref, cw_ref):
        x = x_ref[...]
        delayed = jnp.dot(delay, x, preferred_element_type=F32)
        halo = jnp.where(c == 0, 0.0, halo_ref[...].astype(F32))
        xx_scr[idx, 0:HALO, :] = halo
        xx_scr[idx, HALO:HALO + HALO, :] = x_ref[0:HALO, :].astype(F32)
        y = cw_ref[shifts:shifts + 1, :] * x.astype(F32)
        for j in range(1, shifts + 1):
            d_j = delayed[(j - 1) * C:j * C]
            head8 = xx_scr[idx, HALO - j:HALO - j + 8, :]
            d_j = jnp.concatenate([head8, d_j[8:]], axis=0)
            y = y + cw_ref[shifts - j:shifts - j + 1, :] * d_j
        return y * _sigmoid(y)

    q_all = conv_silu(0, q_ref, qh_ref, cwq_ref)
    k_all = conv_silu(1, k_ref, kh_ref, cwk_ref)
    v_all = conv_silu(2, v_ref, vh_ref, cwv_ref)

    small = small_ref[...]
    beta_all = _sigmoid(small)
    xs = small + dtb_ref[...]
    softplus = jnp.maximum(xs, 0.0) + jnp.log(1.0 + jnp.exp(-jnp.abs(xs)))
    g_all = -jnp.exp(alog_ref[...]) * softplus

    row = lax.broadcasted_iota(jnp.int32, (C, C), 0)
    col = lax.broadcasted_iota(jnp.int32, (C, C), 1)
    tril = row >= col
    strict = row > col
    rxc = row ^ col
    eye = (row == col).astype(F32)
    assert C == 2 * HEAD_DIM

    heads = range(DN_HEADS_PER_STEP)
    sls = [slice(hh * HEAD_DIM, (hh + 1) * HEAD_DIM) for hh in heads]

    def hmap(f, *lists):
        return [f(*xs) for xs in zip(*lists)]

    def pieces(x, n):
        out, rest = [], x
        for _ in range(n):
            p = rest.astype(BF16)
            out.append(p)
            rest = rest - p.astype(F32)
        return jnp.concatenate(out, axis=1)

    hp = DN_HEADS_PER_STEP
    same_half = jnp.where((row >> 7) == (col >> 7), 1.0, 0.0).astype(BF16)

    def unit_rows_pair(xq, xk):
        sq = jnp.concatenate([xq * xq, xk * xk], axis=1).astype(BF16)
        inv = lax.rsqrt(jnp.dot(sq, same_half, preferred_element_type=F32) + EPS)
        return xq * inv[:, :HEAD_DIM], xk * inv[:, HEAD_DIM:]

    def selector(first_lane, n):
        src = lax.broadcasted_iota(jnp.int32, (128, hp * HEAD_DIM), 0)
        dst = lax.broadcasted_iota(jnp.int32, (128, hp * HEAD_DIM), 1) >> 7
        sel = jnp.where(src == first_lane + dst, 1.0, 0.0).astype(BF16)
        return jnp.concatenate([sel] * n, axis=0)

    qk = [unit_rows_pair(q_all[:, sl], k_all[:, sl]) for sl in sls]
    q = [x[0] * (HEAD_DIM ** -0.5) for x in qk]
    k = [x[1] for x in qk]
    v = [v_all[:, sl] for sl in sls]
    beta_rep = jnp.dot(pieces(beta_all, 1), selector(hg * hp, 1), preferred_element_type=F32)
    beta = [beta_rep[:, sl] for sl in sls]

    cum3 = jnp.dot(jnp.where(tril, 1.0, 0.0).astype(BF16), pieces(g_all, 3), preferred_element_type=F32)
    gcum_all = cum3[:, :128] + cum3[:, 128:256] + cum3[:, 256:]
    gcum_rep = jnp.dot(pieces(gcum_all, 3), selector(hg * hp + HEADS, 3), preferred_element_type=F32)
    gcum_b = [gcum_rep[:, sl] for sl in sls]
    gr = [x.T[0:1, :] for x in gcum_b]
    decay = hmap(lambda c_, r_: jnp.where(
        tril, jnp.exp(jnp.where(tril, jnp.concatenate([c_, c_], axis=1) - r_, 0.0)), 0.0), gcum_b, gr)
    e_g = hmap(jnp.exp, gcum_b)
    g_last = [x[C - 1:C, :] for x in gcum_b]

    kb = hmap(lambda a_, b_: a_ * b_, k, beta)
    vb = hmap(lambda a_, b_: a_ * b_, v, beta)
    a = hmap(lambda kb_, q_, k_: lax.dot_general(
        jnp.concatenate([kb_, q_], axis=0).astype(BF16), k_.astype(BF16), NT,
        preferred_element_type=F32), kb, q, k)
    lmat = hmap(lambda a_, d_: jnp.where(strict, a_[:C] * d_, 0.0), a, decay)
    attn = hmap(lambda a_, d_: a_[C:] * d_, a, decay)

    d1 = hmap(lambda l_: jnp.where(rxc < 8, l_, 0.0), lmat)
    d2 = hmap(lambda x: _bdot(x, x), d1)
    d4 = hmap(lambda x: _bdot(x, x), d2)
    p1 = hmap(lambda x, y: _bdot(eye - x, eye + y), d1, d2)
    tmat = hmap(lambda x, y: _bdot(x, eye + y), p1, d4)

    def odd_rows(t, sz):
        return jnp.concatenate([t[b * 2 * sz + sz:(b + 1) * 2 * sz] for b in range(C // (2 * sz))], axis=0)

    def with_odd_rows(t, odd, sz):
        pieces = []
        for b in range(C // (2 * sz)):
            pieces += [t[b * 2 * sz:b * 2 * sz + sz], odd[b * sz:(b + 1) * sz]]
        return jnp.concatenate(pieces, axis=0)

    sz = 8
    while sz < C:
        off = hmap(lambda l_: jnp.where((rxc >= sz) & (rxc < 2 * sz), l_, 0.0), lmat)
        t_odd = hmap(lambda t_: odd_rows(t_, sz), tmat)
        x = hmap(_bdot, t_odd, off)
        x = hmap(_bdot, x, tmat)
        tmat = hmap(lambda t_, o_, x_: with_odd_rows(t_, o_ - x_, sz), tmat, t_odd, x)
        sz *= 2

    uw = hmap(lambda t_, vb_, kb_, e_: _bdot(t_, jnp.concatenate([vb_, kb_ * e_], axis=1)),
              tmat, vb, kb, e_g)
    state = [state_scr[hh] for hh in heads]
    wq = hmap(lambda uw_, q_, e_, s_: _bdot(jnp.concatenate([uw_[:, HEAD_DIM:], q_ * e_], axis=0), s_),
              uw, q, e_g, state)
    v_new = hmap(lambda uw_, wq_: uw_[:, :HEAD_DIM] - wq_[:C], uw, wq)
    av = hmap(_bdot, attn, v_new)
    kt = hmap(lambda k_, gl_, gc_: (k_ * jnp.exp(gl_ - gc_)).T, k, g_last, gcum_b)
    ds = hmap(_bdot, kt, v_new)
    for hh in heads:
        new_state = state[hh] * jnp.exp(g_last[hh]) + ds[hh]
        state_scr[hh] = new_state
        st_out_ref[hh] = new_state
        o = wq[hh][C:] + av[hh]
        y = o * lax.rsqrt(jnp.mean(o * o, axis=-1, keepdims=True) + EPS) * onw_ref[...]
        z = z_ref[:, sls[hh]].astype(F32)
        o_ref[:, sls[hh]] = (y * (z * _sigmoid(z))).astype(BF16)


def _gated_deltanet(proj, small, conv_w, alog_v, dtb_v, onw, state, c0, n_chunks, yd_prev=None, after=None):
    s = proj.shape[0]
    rb = DN_CHUNK // HALO
    hp = DN_HEADS_PER_STEP
    assert hp == HEADS
    width = hp * HEAD_DIM

    def col(part):
        return pl.BlockSpec((DN_CHUNK, width), lambda c: (c0 + c, part))

    def halo(part):
        return pl.BlockSpec((HALO, width), lambda c: (jnp.maximum((c0 + c) * rb - 1, 0), part))

    def cw(part):
        return pl.BlockSpec((DN_CONV, width), lambda c: (0, part))

    vec = pl.BlockSpec((1, 128), lambda c: (0, 0))
    st_spec = pl.BlockSpec((hp, HEAD_DIM, HEAD_DIM), lambda c: (0, 0, 0))
    extras = [a for a in (yd_prev, after) if a is not None]
    n_fixed = 15
    return pl.pallas_call(
        functools.partial(_delta_kernel, hg=0, c0=c0),
        grid=(n_chunks,),
        in_specs=[col(3), col(4), col(5),
                  halo(3), halo(4), halo(5),
                  col(6),
                  pl.BlockSpec((DN_CHUNK, 128), lambda c: (c0 + c, 0)),
                  cw(0), cw(1), cw(2),
                  vec, vec, vec, st_spec] + [pl.BlockSpec(memory_space=pl.ANY)] * len(extras),
        out_specs=[pl.BlockSpec((DN_CHUNK, width), lambda c: (c0 + c, 0)), st_spec],
        out_shape=[jax.ShapeDtypeStruct((s, width), BF16),
                   jax.ShapeDtypeStruct((hp, HEAD_DIM, HEAD_DIM), F32)],
        input_output_aliases={n_fixed: 0} if yd_prev is not None else {},
        scratch_shapes=[pltpu.VMEM((hp, HEAD_DIM, HEAD_DIM), F32),
                        pltpu.VMEM((3, HALO + DN_CHUNK, width), F32)],
        compiler_params=pltpu.CompilerParams(
            dimension_semantics=("arbitrary",), vmem_limit_bytes=VMEM_LIMIT),
        name="deltanet",
    )(proj, proj, proj, proj, proj, proj, proj, small, conv_w, conv_w, conv_w, alog_v, dtb_v, onw,
      state, *extras)


def _mix_out_kernel(ya_ref, *rest):
    ng = HEADS // DN_HEADS_PER_STEP
    yd_refs = rest[:ng]
    ga_ref, gd_ref, x_ref, wa_ref, wd_ref, wo_ref, nw_ref, o_ref = rest[ng:]
    width = DN_HEADS_PER_STEP * HEAD_DIM
    pa = jnp.dot(ya_ref[...], wa_ref[...], preferred_element_type=F32)
    pd = jnp.dot(yd_refs[0][...], wd_ref[0:width, :], preferred_element_type=F32)
    for g in range(1, ng):
        pd = pd + jnp.dot(yd_refs[g][...], wd_ref[g * width:(g + 1) * width, :],
                          preferred_element_type=F32)
    merged = _sigmoid(ga_ref[...].astype(F32)) * pa + _sigmoid(gd_ref[...].astype(F32)) * pd
    mo = jnp.dot(merged.astype(BF16), wo_ref[...], preferred_element_type=F32)
    y = mo * lax.rsqrt(jnp.mean(mo * mo, axis=-1, keepdims=True) + EPS) * nw_ref[...]
    o_ref[...] = x_ref[...] + y


def _mix_out(ya, yds, proj, x2, wa, wd, wo, nw, *, tm=512):
    s = x2.shape[0]
    row = lambda i: (i, 0)
    full = lambda i: (0, 0)
    wspec = pl.BlockSpec((D_MODEL, D_MODEL), full)
    return pl.pallas_call(
        _mix_out_kernel,
        grid=(s // tm,),
        in_specs=[pl.BlockSpec((tm, D_MODEL), row)]
                 + [pl.BlockSpec((tm, yd.shape[1]), row) for yd in yds]
                 + [pl.BlockSpec((tm, D_MODEL), lambda i: (i, 7)),
                  pl.BlockSpec((tm, D_MODEL), lambda i: (i, 8)),
                  pl.BlockSpec((tm, D_MODEL), row),
                  wspec, wspec, wspec, pl.BlockSpec((1, D_MODEL), full)],
        out_specs=pl.BlockSpec((tm, D_MODEL), row),
        out_shape=jax.ShapeDtypeStruct((s, D_MODEL), F32),
        compiler_params=pltpu.CompilerParams(
            dimension_semantics=("parallel",), vmem_limit_bytes=VMEM_LIMIT),
        name="mix_out",
    )(ya, *yds, proj, proj, x2, wa, wd, wo, nw)


def _ffn_kernel(x_ref, npre_ref, wg_ref, wu_ref, wd_ref, npost_ref, o_ref):
    x = x_ref[...]
    h = (x * lax.rsqrt(jnp.mean(x * x, axis=-1, keepdims=True) + EPS) * npre_ref[...]).astype(BF16)
    acc = jnp.zeros(x.shape, F32)
    for cc in range(D_FF // FF_CHUNK):
        sl = slice(cc * FF_CHUNK, (cc + 1) * FF_CHUNK)
        g = jnp.dot(h, wg_ref[:, sl], preferred_element_type=F32)
        u = jnp.dot(h, wu_ref[:, sl], preferred_element_type=F32)
        act = (g * _sigmoid(g) * u).astype(BF16)
        acc = acc + jnp.dot(act, wd_ref[sl, :], preferred_element_type=F32)
    y = acc * lax.rsqrt(jnp.mean(acc * acc, axis=-1, keepdims=True) + EPS) * npost_ref[...]
    o_ref[...] = x + y


def _ffn(x1, npre, wg, wu, wd, npost, *, tm=512):
    s = x1.shape[0]
    row = lambda i: (i, 0)
    full = lambda i: (0, 0)
    once = pl.Buffered(1)
    return pl.pallas_call(
        _ffn_kernel,
        grid=(s // tm,),
        in_specs=[pl.BlockSpec((tm, D_MODEL), row), pl.BlockSpec((1, D_MODEL), full),
                  pl.BlockSpec((D_MODEL, D_FF), full, pipeline_mode=once),
                  pl.BlockSpec((D_MODEL, D_FF), full, pipeline_mode=once),
                  pl.BlockSpec((D_FF, D_MODEL), full, pipeline_mode=once),
                  pl.BlockSpec((1, D_MODEL), full)],
        out_specs=pl.BlockSpec((tm, D_MODEL), row),
        out_shape=jax.ShapeDtypeStruct((s, D_MODEL), F32),
        compiler_params=pltpu.CompilerParams(
            dimension_semantics=("parallel",), vmem_limit_bytes=VMEM_LIMIT),
        name="ffn",
    )(x1, npre, wg, wu, wd, npost)


def _rope_table(s):
    half = ROPE_DIM // 2
    rep = 128 // half
    inv = ROPE_THETA ** (-jnp.arange(half, dtype=F32) * 2.0 / ROPE_DIM)
    pos = (jnp.arange(s // rep)[:, None] * rep + jnp.arange(128)[None, :] // half).astype(F32)
    ang = pos * jnp.tile(inv, rep)[None, :]
    cos, sin = lax.optimization_barrier((jnp.cos(ang), jnp.sin(ang)))
    cos = cos.reshape(s, half)
    sin = sin.reshape(s, half)
    return jnp.concatenate([cos, sin, jnp.zeros((s, HEAD_DIM - ROPE_DIM), F32)], axis=-1)


def _layer(x2, l, norm_mix_pre, w_in, conv_w, a_log, dt_bias, o_norm_w, w_o_attn, w_o_delta,
           w_out, norm_mix_post, norm_ffn_pre, w_gate, w_up, w_down, norm_ffn_post):
    s = x2.shape[0]
    nb = s // MOBA_BLOCK
    wide = 7 * D_MODEL
    nsmall = 2 * HEADS
    w_t = jnp.swapaxes(w_in[l], 0, 1).astype(BF16)
    later_w = [w[l].astype(BF16) for w in (w_o_attn, w_o_delta, w_out, w_gate, w_up, w_down)]

    proj, small, kmean, v_t, qf = _project(x2, norm_mix_pre[l][None, :], w_t, _rope_table(s),
                                           wide // D_MODEL, nsmall, after=later_w)
    pad = lambda vec: jnp.pad(vec.astype(F32), (HEADS, 128 - 2 * HEADS))[None, :]
    half = (s // DN_CHUNK) // 2
    delta = functools.partial(_gated_deltanet, proj, small, conv_w[l], pad(a_log[l]), pad(dt_bias[l]),
                              o_norm_w[l][None, :])
    res = {}

    def first_half():
        res["yd"], res["state"] = delta(jnp.zeros((HEADS, HEAD_DIM, HEAD_DIM), F32), 0, half)
        return res["yd"]

    def second_half(after):
        res["yd"], _ = delta(res["state"], half, s // DN_CHUNK - half, yd_prev=res["yd"], after=after)

    ya = _moba_sparse(proj, v_t, kmean.reshape(nb, D_MODEL), qf, first_half, second_half)

    x1 = _mix_out(ya, [res["yd"]], proj, x2, *later_w[:3], norm_mix_post[l][None, :])
    return _ffn(x1, norm_ffn_pre[l][None, :], *later_w[3:], norm_ffn_post[l][None, :])


def kernel(x, norm_mix_pre, w_in, conv_w, a_log, dt_bias, o_norm_w, w_o_attn, w_o_delta, w_out,
           norm_mix_post, norm_ffn_pre, w_gate, w_up, w_down, norm_ffn_post):
    b, s, d = x.shape
    assert d == D_MODEL and s % (ROUTE_QBLOCKS * MOBA_BLOCK) == 0 and s % (2 * DN_CHUNK) == 0
    outs = []
    for bi in range(b):
        x2 = x.reshape(s, d) if b == 1 else x[bi]
        for l in range(w_in.shape[0]):
            x2 = _layer(x2, l, norm_mix_pre, w_in, conv_w, a_log, dt_bias, o_norm_w, w_o_attn,
                        w_o_delta, w_out, norm_mix_post, norm_ffn_pre, w_gate, w_up, w_down,
                        norm_ffn_post)
        outs.append(x2)
    return outs[0].reshape(1, s, d) if b == 1 else jnp.stack(outs, axis=0)
```

```python
import functools
import math

import jax
import jax.numpy as jnp
from jax import lax
from jax.experimental import pallas as pl
from jax.experimental.pallas import tpu as pltpu
from jax.experimental.pallas import tpu_sc as plsc

D_MODEL = 1024
HEADS = 8
HEAD_DIM = 128
MOBA_BLOCK = 256
MOBA_TOPK = 3
ROPE_DIM = HEAD_DIM // 4
ROPE_THETA = 500000.0
DN_CONV = 4
ONES_ROWS = 16
DN_HEADS_PER_STEP = 8
DN_CHUNK = 256
D_FF = 2816
FF_CHUNK = 256
EPS = 1e-6
LOG2_E = math.log2(math.e)
HALO = 16

F32 = jnp.float32
BF16 = jnp.bfloat16
NT = (((1,), (1,)), ((), ()))

VMEM_LIMIT = 56 * 1024 * 1024


def _bdot(a, b):
    return jnp.dot(a.astype(BF16), b.astype(BF16), preferred_element_type=F32)


def _sigmoid(x):
    return 1.0 / (1.0 + jnp.exp(-x))


def _proj_kernel(x_ref, nw_ref, wt_ref, rope_ref, *rest, tm, q_scale, n_wide, n_small):
    out_ref, small_ref, kmean_ref, vt_ref, qf_ref = rest[-5:]
    x = x_ref[...]
    ms = jnp.mean(x * x, axis=-1, keepdims=True)
    h = (x * lax.rsqrt(ms + EPS) * nw_ref[...]).astype(BF16)
    narrow = lax.dot_general(h, wt_ref[n_wide * D_MODEL:n_wide * D_MODEL + 128, :], NT,
                             preferred_element_type=F32)
    small_ref[...] = jnp.where(lax.broadcasted_iota(jnp.int32, narrow.shape, 1) < n_small, narrow, 0.0)
    heads = [slice(hh * HEAD_DIM, (hh + 1) * HEAD_DIM) for hh in range(HEADS)]
    groups = [slice(g * MOBA_BLOCK, (g + 1) * MOBA_BLOCK) for g in range(tm // MOBA_BLOCK)]

    half = ROPE_DIM // 2
    tab = rope_ref[...]
    lane = lax.broadcasted_iota(jnp.int32, tab.shape, 1)
    cos_t = jnp.where(lane < half, tab, jnp.where(lane < ROPE_DIM, pltpu.roll(tab, half, 1), 1.0))
    s1_t = jnp.where(lane < half, -pltpu.roll(tab, HEAD_DIM - half, 1), 0.0)
    s2_t = jnp.where((lane >= half) & (lane < ROPE_DIM), tab, 0.0)

    def roped(a):
        return (a * cos_t + pltpu.roll(a, HEAD_DIM - half, 1) * s1_t + pltpu.roll(a, half, 1) * s2_t)

    for c in range(out_ref.shape[1] // D_MODEL):
        cols = slice(c * D_MODEL, (c + 1) * D_MODEL)
        r0 = c * D_MODEL + (n_small if c >= n_wide else 0)
        acc = lax.dot_general(h, wt_ref[r0:r0 + D_MODEL, :], NT, preferred_element_type=F32)
        if c == 0:
            for hh, hs in enumerate(heads):
                r = roped(acc[:, hs]) * q_scale
                out_ref[:, hs] = r.astype(BF16)
                qf_ref[hh] = r
        elif c == 1:
            for hs in heads:
                r = roped(acc[:, hs])
                out_ref[:, D_MODEL + hs.start:D_MODEL + hs.stop] = r.astype(BF16)
                for g, gs in enumerate(groups):
                    kmean_ref[0, g:g + 1, hs] = jnp.sum(r[gs], axis=0, keepdims=True) * (1.0 / MOBA_BLOCK)
        elif c == 2:
            out_ref[:, cols] = acc.astype(BF16)
            for hh, hs in enumerate(heads):
                for g, gs in enumerate(groups):
                    vt_ref[hh, g, 0:HEAD_DIM, :] = acc[gs, hs].T.astype(BF16)
                    vt_ref[hh, g, HEAD_DIM:, :] = jnp.ones((ONES_ROWS, MOBA_BLOCK), BF16)
        else:
            out_ref[:, cols] = acc.astype(BF16)


def _project(x2, norm_w, w_t, rope_t, n_wide, n_small, after=(), *, tm=512):
    s = x2.shape[0]
    width = (w_t.shape[0] - n_small) // D_MODEL * D_MODEL
    once = pl.Buffered(1)
    nblk = tm // MOBA_BLOCK
    kern = functools.partial(_proj_kernel, tm=tm, q_scale=math.log2(math.e) / math.sqrt(HEAD_DIM),
                             n_wide=n_wide, n_small=n_small)
    row = lambda i: (i, 0)
    full = lambda i: (0, 0)
    return pl.pallas_call(
        kern,
        grid=(s // tm,),
        in_specs=[
            pl.BlockSpec((tm, D_MODEL), row),
            pl.BlockSpec((1, D_MODEL), full),
            pl.BlockSpec(w_t.shape, full, pipeline_mode=once),
            pl.BlockSpec((tm, HEAD_DIM), row),
        ] + [pl.BlockSpec(memory_space=pl.ANY)] * len(after),
        out_specs=[
            pl.BlockSpec((tm, width), row),
            pl.BlockSpec((tm, 128), row),
            pl.BlockSpec((1, nblk, D_MODEL), lambda i: (i, 0, 0)),
            pl.BlockSpec((HEADS, nblk, HEAD_DIM + ONES_ROWS, MOBA_BLOCK), lambda i: (0, i, 0, 0)),
            pl.BlockSpec((HEADS, tm, HEAD_DIM), lambda i: (0, i, 0)),
        ],
        out_shape=[
            jax.ShapeDtypeStruct((s, width), BF16),
            jax.ShapeDtypeStruct((s, 128), F32),
            jax.ShapeDtypeStruct((s // tm, nblk, D_MODEL), F32),
            jax.ShapeDtypeStruct((HEADS, s // MOBA_BLOCK, HEAD_DIM + ONES_ROWS, MOBA_BLOCK), BF16),
            jax.ShapeDtypeStruct((HEADS, s, HEAD_DIM), F32),
        ],
        compiler_params=pltpu.CompilerParams(
            dimension_semantics=("parallel",), vmem_limit_bytes=VMEM_LIMIT),
        name="proj",
    )(x2, norm_w, w_t, rope_t, *after)


ROUTE_QBLOCKS = 16
TILES_PER_STEP = 32
SC_WINDOW = 128
DEST_LANE_CHUNK = 2048


def _top_blocks(gate, blk_f, nb):
    picks = []
    for _ in range(MOBA_TOPK):
        m = jnp.max(gate, axis=0, keepdims=True)
        first = jnp.min(jnp.where(gate == m, blk_f, float(nb)), axis=0, keepdims=True)
        pick = (blk_f == first) & (m > -jnp.inf)
        gate = jnp.where(pick, -jnp.inf, gate)
        picks.append(pick)
    return picks


def _route_kernel(q_ref, km_ref, info_ref, cnt_ref, run_scr, *, nb, steps):
    jb = pl.program_id(1)

    @pl.when(jb == 0)
    def _():
        run_scr[...] = jnp.zeros_like(run_scr)

    for v in range(steps):
        pl.when(jb == v)(functools.partial(_route_step, q_ref, km_ref, info_ref, cnt_ref, run_scr, v, nb))


def _route_step(q_ref, km_ref, info_ref, cnt_ref, run_scr, jb, nb_all):
    L = ROUTE_QBLOCKS * MOBA_BLOCK
    nb = min(nb_all, ROUTE_QBLOCKS * (jb + 1))
    km = km_ref[0:nb, :]
    km_hi = km.astype(BF16)
    km_lo = (km - km_hi.astype(F32)).astype(BF16)
    gate2 = lax.dot_general(jnp.concatenate([km_hi, km_lo], axis=0), q_ref[...], NT,
                            preferred_element_type=F32)
    blk = lax.broadcasted_iota(jnp.int32, (nb, L), 0)
    blk_f = blk.astype(F32)
    qblk = jb * ROUTE_QBLOCKS + (lax.broadcasted_iota(jnp.int32, (nb, L), 1) >> 8)
    gate = jnp.where(blk < qblk, gate2[:nb] + gate2[nb:], -jnp.inf)
    picks = _top_blocks(gate, blk_f, nb_all)

    chosen = jnp.where(picks[0] | picks[1] | picks[2], 1.0, 0.0)
    B = MOBA_BLOCK
    before = jnp.where(lax.broadcasted_iota(jnp.int32, (B, B), 0)
                       < lax.broadcasted_iota(jnp.int32, (B, B), 1), 1.0, 0.0).astype(BF16)
    carry = run_scr[0:nb, 0:1]
    base = []
    for b in range(ROUTE_QBLOCKS):
        c_b = chosen[:, b * B:(b + 1) * B]
        within = jnp.dot(c_b.astype(BF16), before, preferred_element_type=F32)
        base.append(carry + within)
        carry = carry + within[:, B - 1:B] + c_b[:, B - 1:B]
    base = jnp.concatenate(base, axis=1)

    rows = []
    for pick in picks:
        rows.append(jnp.sum(jnp.where(pick, base, 0.0), axis=0, keepdims=True))
    for pick in picks:
        bid = jnp.sum(jnp.where(pick, blk_f, 0.0), axis=0, keepdims=True)
        valid = jnp.sum(jnp.where(pick, 1.0, 0.0), axis=0, keepdims=True)
        rows.append(jnp.where(valid > 0.0, bid, float(nb_all)))
    rows += [jnp.zeros((1, L), F32)] * (8 - len(rows))
    info_ref[...] = jnp.concatenate(rows, axis=0).astype(jnp.int32)

    run_scr[0:nb, :] = jnp.broadcast_to(carry, (nb, run_scr.shape[1]))
    cnt_ref[0] = run_scr[...]


def _route(proj, kmean):
    s = proj.shape[0]
    nb = s // MOBA_BLOCK
    L = ROUTE_QBLOCKS * MOBA_BLOCK
    steps = s // L
    return pl.pallas_call(
        functools.partial(_route_kernel, nb=nb, steps=steps),
        grid=(HEADS, steps),
        in_specs=[pl.BlockSpec((L, HEAD_DIM), lambda h, j: (j, h)),
                  pl.BlockSpec((nb, HEAD_DIM), lambda h, j: (0, h))],
        out_specs=[pl.BlockSpec((8, L), lambda h, j: (0, h * steps + j)),
                   pl.BlockSpec((1, nb, 128), lambda h, j: (h, 0, 0))],
        out_shape=[jax.ShapeDtypeStruct((8, HEADS * s), jnp.int32),
                   jax.ShapeDtypeStruct((HEADS, nb, 128), F32)],
        scratch_shapes=[pltpu.VMEM((nb, 128), F32)],
        compiler_params=pltpu.CompilerParams(
            dimension_semantics=("parallel", "arbitrary"), vmem_limit_bytes=VMEM_LIMIT),
        name="moba_route",
    )(proj, kmean)


def _dest_kernel(info_ref, cnt_ref, dest_ref, tb_ref, *, nb, s):
    h = pl.program_id(0)
    cap = 4 * s
    ntile = cap // MOBA_BLOCK
    cnt = cnt_ref[0]
    tiles = jnp.floor((cnt + float(MOBA_BLOCK - 1)) * (1.0 / MOBA_BLOCK))
    lower = jnp.where(lax.broadcasted_iota(jnp.int32, (nb, nb), 1)
                      < lax.broadcasted_iota(jnp.int32, (nb, nb), 0), 1.0, 0.0).astype(BF16)
    start = jnp.dot(lower, tiles.astype(BF16), preferred_element_type=F32)
    start1 = start[:, 0:1]
    tiles1 = tiles[:, 0:1]

    t_f = lax.broadcasted_iota(jnp.int32, (nb, ntile), 1).astype(F32)
    n_f = lax.broadcasted_iota(jnp.int32, (nb, ntile), 0).astype(F32)
    inside = (t_f >= start1) & (t_f < start1 + tiles1)
    tb = jnp.sum(jnp.where(inside, n_f, 0.0), axis=0, keepdims=True)
    used = jnp.sum(jnp.where(inside, 1.0, 0.0), axis=0, keepdims=True)
    tb = jnp.where(used > 0.0, tb, -1.0)
    tb_ref[0] = jnp.concatenate([tb, jnp.full((7, ntile), -1.0, F32)], axis=0).astype(jnp.int32)

    off1 = start1 * float(MOBA_BLOCK)
    CH = DEST_LANE_CHUNK
    blk_f = lax.broadcasted_iota(jnp.int32, (nb, CH), 0).astype(F32)
    lane = lax.broadcasted_iota(jnp.int32, (1, CH), 1)
    trash = HEADS * cap + (lane & (MOBA_BLOCK - 1))
    for ch in range(s // CH):
        sl = slice(ch * CH, (ch + 1) * CH)
        rows = []
        for r in range(MOBA_TOPK):
            rank = info_ref[r:r + 1, sl]
            bid = info_ref[MOBA_TOPK + r:MOBA_TOPK + r + 1, sl]
            off = jnp.sum(jnp.where(blk_f == bid.astype(F32), off1, 0.0), axis=0, keepdims=True)
            rows.append(jnp.where(bid < nb, h * cap + off.astype(jnp.int32) + rank, trash))
        rows += [jnp.zeros((1, CH), jnp.int32)] * (8 - len(rows))
        dest_ref[:, sl] = jnp.concatenate(rows, axis=0)


def _dest(info, cnt, s):
    nb = s // MOBA_BLOCK
    ntile = 4 * s // MOBA_BLOCK
    return pl.pallas_call(
        functools.partial(_dest_kernel, nb=nb, s=s),
        grid=(HEADS,),
        in_specs=[pl.BlockSpec((8, s), lambda h: (0, h)),
                  pl.BlockSpec((1, nb, 128), lambda h: (h, 0, 0))],
        out_specs=[pl.BlockSpec((8, s), lambda h: (0, h)),
                   pl.BlockSpec((1, 8, ntile), lambda h: (h, 0, 0))],
        out_shape=[jax.ShapeDtypeStruct((8, HEADS * s), jnp.int32),
                   jax.ShapeDtypeStruct((HEADS, 8, ntile), jnp.int32)],
        compiler_params=pltpu.CompilerParams(
            dimension_semantics=("parallel",), vmem_limit_bytes=VMEM_LIMIT),
        name="moba_dest",
    )(info, cnt)


def _sc_scatter_rows(rows, idx, n_out):
    m, c = rows.shape
    k = idx.shape[0]
    nwin = m // SC_WINDOW
    mesh = plsc.VectorSubcoreMesh(core_axis_name="core", subcore_axis_name="subcore")

    @pl.kernel(out_type=jax.ShapeDtypeStruct((n_out, c), rows.dtype), mesh=mesh)
    def kern(x_hbm, i_hbm, o_hbm):
        def body(x_vmem, i_vmem):
            pltpu.sync_copy(x_vmem, o_hbm.at[i_vmem.at[0]])

        pltpu.emit_pipeline(
            body, grid=(k * nwin,),
            in_specs=[pl.BlockSpec((SC_WINDOW, c), lambda i: (i % nwin, 0)),
                      pl.BlockSpec((1, SC_WINDOW), lambda i: (0, i))],
            out_specs=[],
            core_axis_name=("core", "subcore"),
            dimension_semantics=(pltpu.PARALLEL,),
        )(x_hbm, i_hbm)

    return kern(rows, idx.reshape(1, k * m))


def _sc_gather_rows(table, idx):
    m = idx.shape[0]
    c = table.shape[1]
    mesh = plsc.VectorSubcoreMesh(core_axis_name="core", subcore_axis_name="subcore")

    @pl.kernel(out_type=jax.ShapeDtypeStruct((m, c), table.dtype), mesh=mesh)
    def kern(x_hbm, i_hbm, o_hbm):
        def body(i_vmem, o_vmem):
            pltpu.sync_copy(x_hbm.at[i_vmem.at[0]], o_vmem)

        pltpu.emit_pipeline(
            body, grid=(m // SC_WINDOW,),
            in_specs=[pl.BlockSpec((1, SC_WINDOW), lambda i: (0, i))],
            out_specs=[pl.BlockSpec((SC_WINDOW, c), lambda i: (i, 0))],
            core_axis_name=("core", "subcore"),
            dimension_semantics=(pltpu.PARALLEL,),
        )(i_hbm, o_hbm)

    return kern(table, idx.reshape(1, m))


def _pack_partial_t(o_norm_t, lse):
    half = HEAD_DIM // 2
    u = lax.bitcast_convert_type(o_norm_t, jnp.uint32) + jnp.uint32(0x8000)
    word = (u[:half] & jnp.uint32(0xFFFF0000)) | (u[half:] >> 16)
    lse_bits = lax.bitcast_convert_type(jnp.broadcast_to(lse, word.shape), jnp.uint32)
    full = jnp.concatenate([word, lse_bits], axis=0)
    return lax.bitcast_convert_type(full, F32).T


def _unpack_partial(part):
    half = HEAD_DIM // 2
    lane = lax.broadcasted_iota(jnp.int32, part.shape, 1)
    swapped = pltpu.roll(part, half, 1)
    lse = jnp.where(lane < half, swapped, part)
    word = lax.bitcast_convert_type(jnp.where(lane < half, part, swapped), jnp.uint32)
    o = jnp.where(lane < half, lax.bitcast_convert_type(word & jnp.uint32(0xFFFF0000), F32),
                  lax.bitcast_convert_type(word << 16, F32))
    return o, lse


def _tiles_kernel(tb_ref, qs_ref, k_ref, vt_ref, after_ref, o_ref, *, ntile):
    del after_ref
    h = pl.program_id(0)
    g = pl.program_id(1)
    B = MOBA_BLOCK
    base = h * ntile + g * TILES_PER_STEP

    @pl.when(tb_ref[base] >= 0)
    def _():
        blocks = [jnp.maximum(tb_ref[base + u], 0) for u in range(TILES_PER_STEP)]
        s_t = [lax.dot_general(k_ref[pl.ds(pl.multiple_of(n * B, B), B), :],
                               qs_ref[u * B:(u + 1) * B, :].astype(BF16), NT,
                               preferred_element_type=F32) for u, n in enumerate(blocks)]
        m = [jnp.max(x, axis=0, keepdims=True) for x in s_t]
        p = [jnp.exp2(x - mm).astype(BF16) for x, mm in zip(s_t, m)]
        acc = [jnp.dot(vt_ref[0, n], pp, preferred_element_type=F32) for n, pp in zip(blocks, p)]
        for u in range(TILES_PER_STEP):
            l = acc[u][HEAD_DIM:HEAD_DIM + 1]
            o_ref[u * B:(u + 1) * B, :] = _pack_partial_t(acc[u][:HEAD_DIM] / l,
                                                          m[u] + jnp.log(l) * LOG2_E)


def _tiles(tb, qsorted, proj, v_t, after, s):
    nb = s // MOBA_BLOCK
    ntile = 4 * s // MOBA_BLOCK
    steps = ntile // TILES_PER_STEP
    rows = TILES_PER_STEP * MOBA_BLOCK
    grid_spec = pltpu.PrefetchScalarGridSpec(
        num_scalar_prefetch=1,
        grid=(HEADS, steps),
        in_specs=[pl.BlockSpec((rows, HEAD_DIM), lambda h, g, tb: (h * steps + g, 0)),
                  pl.BlockSpec((s, HEAD_DIM), lambda h, g, tb: (0, HEADS + h)),
                  pl.BlockSpec((1, nb, HEAD_DIM + ONES_ROWS, MOBA_BLOCK), lambda h, g, tb: (h, 0, 0, 0)),
                  pl.BlockSpec((16, HEAD_DIM), lambda h, g, tb: (0, 0))],
        out_specs=pl.BlockSpec((rows, HEAD_DIM), lambda h, g, tb: (h * steps + g, 0)),
    )
    return pl.pallas_call(
        functools.partial(_tiles_kernel, ntile=ntile),
        grid_spec=grid_spec,
        out_shape=jax.ShapeDtypeStruct(qsorted.shape, F32),
        compiler_params=pltpu.CompilerParams(
            dimension_semantics=("parallel", "arbitrary"), vmem_limit_bytes=VMEM_LIMIT),
        name="moba_tiles",
    )(tb, qsorted, proj, v_t, after)


def _merge_kernel(q_ref, k_ref, v_ref, part_ref, o_ref):
    j = pl.program_id(0)
    B = MOBA_BLOCK
    heads = [slice(hh * HEAD_DIM, (hh + 1) * HEAD_DIM) for hh in range(HEADS)]
    causal = (lax.broadcasted_iota(jnp.int32, (B, B), 1) <= lax.broadcasted_iota(jnp.int32, (B, B), 0))
    s = [lax.dot_general(q_ref[:, hs], k_ref[:, hs], NT, preferred_element_type=F32) for hs in heads]
    s = [jnp.where(causal, x, -jnp.inf) for x in s]
    m_own = [jnp.broadcast_to(jnp.max(x, axis=-1, keepdims=True), (B, HEAD_DIM)) for x in s]
    p = [jnp.exp2(x - jnp.concatenate([m, m], axis=1)) for x, m in zip(s, m_own)]
    ones = jnp.ones((B, HEAD_DIM), BF16)
    ol = [jnp.dot(x.astype(BF16), jnp.concatenate([v_ref[:, hs], ones], axis=1),
                  preferred_element_type=F32) for x, hs in zip(p, heads)]
    for hh, hs in enumerate(heads):
        o_own = ol[hh][:, :HEAD_DIM]
        l_own = ol[hh][:, HEAD_DIM:]
        lse_own = m_own[hh] + jnp.log(l_own) * LOG2_E
        parts = []
        for r in range(MOBA_TOPK):
            o_r, lse_r = _unpack_partial(part_ref[r, hh])
            valid = r < j
            parts.append((jnp.where(valid, o_r, 0.0), jnp.where(valid, lse_r, -jnp.inf)))
        m_all = lse_own
        for _, lse_r in parts:
            m_all = jnp.maximum(m_all, lse_r)
        w = jnp.exp2(lse_own - m_all)
        num = (w / l_own) * o_own
        den = w
        for o_r, lse_r in parts:
            w = jnp.exp2(lse_r - m_all)
            num = num + w * o_r
            den = den + w
        o_ref[:, hs] = (num / den).astype(BF16)


def _merge(proj, parts, s):
    nb = s // MOBA_BLOCK
    width = HEADS * HEAD_DIM
    return pl.pallas_call(
        _merge_kernel,
        grid=(nb,),
        in_specs=[pl.BlockSpec((MOBA_BLOCK, width), lambda j: (j, 0)),
                  pl.BlockSpec((MOBA_BLOCK, width), lambda j: (j, 1)),
                  pl.BlockSpec((MOBA_BLOCK, width), lambda j: (j, 2)),
                  pl.BlockSpec((MOBA_TOPK, HEADS, MOBA_BLOCK, HEAD_DIM), lambda j: (0, 0, j, 0))],
        out_specs=pl.BlockSpec((MOBA_BLOCK, width), lambda j: (j, 0)),
        out_shape=jax.ShapeDtypeStruct((s, width), BF16),
        compiler_params=pltpu.CompilerParams(
            dimension_semantics=("parallel",), vmem_limit_bytes=VMEM_LIMIT),
        name="moba_merge",
    )(proj, proj, proj, parts)


def _moba_sparse(proj, v_t, kmean, qf, run_during_scatter, run_during_gather):
    s = proj.shape[0]
    cap = 4 * s
    n_rows = HEADS * cap + MOBA_BLOCK
    info, cnt = _route(proj, kmean)
    dest, tb = _dest(info, cnt, s)
    dest3 = dest[:MOBA_TOPK]
    qsorted = _sc_scatter_rows(qf.reshape(HEADS * s, HEAD_DIM), dest3, n_rows)
    after = run_during_scatter()
    osorted = _tiles(tb[:, 0, :].reshape(-1), qsorted, proj, v_t, after, s)
    parts = _sc_gather_rows(osorted, dest3.reshape(-1))
    run_during_gather(osorted)
    return _merge(proj, parts.reshape(MOBA_TOPK, HEADS, s, HEAD_DIM), s)


def _delta_kernel(q_ref, k_ref, v_ref, qh_ref, kh_ref, vh_ref, z_ref, small_ref,
                  cwq_ref, cwk_ref, cwv_ref, alog_ref, dtb_ref, onw_ref, st_in_ref, *rest, hg, c0):
    o_ref, st_out_ref, state_scr, xx_scr = rest[-4:]
    c = c0 + pl.program_id(0)
    C = DN_CHUNK

    @pl.when(pl.program_id(0) == 0)
    def _():
        state_scr[...] = st_in_ref[...]

    shifts = DN_CONV - 1
    r_out = lax.broadcasted_iota(jnp.int32, (shifts * C, C), 0)
    r_in = lax.broadcasted_iota(jnp.int32, (shifts * C, C), 1)
    delay = jnp.where((r_out & (C - 1)) - r_in == (r_out >> 8) + 1, 1.0, 0.0).astype(BF16)
    assert C == 256

    def conv_silu(idx, x_ref, halo_ref, cw_ref):
        x = x_ref[...]
        delayed = jnp.dot(delay, x, preferred_element_type=F32)
        halo = jnp.where(c == 0, 0.0, halo_ref[...].astype(F32))
        xx_scr[idx, 0:HALO, :] = halo
        xx_scr[idx, HALO:HALO + HALO, :] = x_ref[0:HALO, :].astype(F32)
        y = cw_ref[shifts:shifts + 1, :] * x.astype(F32)
        for j in range(1, shifts + 1):
            d_j = delayed[(j - 1) * C:j * C]
            head8 = xx_scr[idx, HALO - j:HALO - j + 8, :]
            d_j = jnp.concatenate([head8, d_j[8:]], axis=0)
            y = y + cw_ref[shifts - j:shifts - j + 1, :] * d_j
        return y * _sigmoid(y)

    q_all = conv_silu(0, q_ref, qh_ref, cwq_ref)
    k_all = conv_silu(1, k_ref, kh_ref, cwk_ref)
    v_all = conv_silu(2, v_ref, vh_ref, cwv_ref)

    small = small_ref[...]
    beta_all = _sigmoid(small)
    xs = small + dtb_ref[...]
    softplus = jnp.maximum(xs, 0.0) + jnp.log(1.0 + jnp.exp(-jnp.abs(xs)))
    g_all = -jnp.exp(alog_ref[...]) * softplus

    row = lax.broadcasted_iota(jnp.int32, (C, C), 0)
    col = lax.broadcasted_iota(jnp.int32, (C, C), 1)
    tril = row >= col
    strict = row > col
    rxc = row ^ col
    eye = (row == col).astype(F32)
    assert C == 2 * HEAD_DIM

    heads = range(DN_HEADS_PER_STEP)
    sls = [slice(hh * HEAD_DIM, (hh + 1) * HEAD_DIM) for hh in heads]

    def hmap(f, *lists):
        return [f(*xs) for xs in zip(*lists)]

    def pieces(x, n):
        out, rest = [], x
        for _ in range(n):
            p = rest.astype(BF16)
            out.append(p)
            rest = rest - p.astype(F32)
        return jnp.concatenate(out, axis=1)

    hp = DN_HEADS_PER_STEP
    same_half = jnp.where((row >> 7) == (col >> 7), 1.0, 0.0).astype(BF16)

    def unit_rows_pair(xq, xk):
        sq = jnp.concatenate([xq * xq, xk * xk], axis=1).astype(BF16)
        inv = lax.rsqrt(jnp.dot(sq, same_half, preferred_element_type=F32) + EPS)
        return xq * inv[:, :HEAD_DIM], xk * inv[:, HEAD_DIM:]

    def selector(first_lane, n):
        src = lax.broadcasted_iota(jnp.int32, (128, hp * HEAD_DIM), 0)
        dst = lax.broadcasted_iota(jnp.int32, (128, hp * HEAD_DIM), 1) >> 7
        sel = jnp.where(src == first_lane + dst, 1.0, 0.0).astype(BF16)
        return jnp.concatenate([sel] * n, axis=0)

    qk = [unit_rows_pair(q_all[:, sl], k_all[:, sl]) for sl in sls]
    q = [x[0] * (HEAD_DIM ** -0.5) for x in qk]
    k = [x[1] for x in qk]
    v = [v_all[:, sl] for sl in sls]
    beta_rep = jnp.dot(pieces(beta_all, 1), selector(hg * hp, 1), preferred_element_type=F32)
    beta = [beta_rep[:, sl] for sl in sls]

    cum3 = jnp.dot(jnp.where(tril, 1.0, 0.0).astype(BF16), pieces(g_all, 3), preferred_element_type=F32)
    gcum_all = cum3[:, :128] + cum3[:, 128:256] + cum3[:, 256:]
    gcum_rep = jnp.dot(pieces(gcum_all, 3), selector(hg * hp + HEADS, 3), preferred_element_type=F32)
    gcum_b = [gcum_rep[:, sl] for sl in sls]
    gr = [x.T[0:1, :] for x in gcum_b]
    decay = hmap(lambda c_, r_: jnp.where(
        tril, jnp.exp(jnp.where(tril, jnp.concatenate([c_, c_], axis=1) - r_, 0.0)), 0.0), gcum_b, gr)
    e_g = hmap(jnp.exp, gcum_b)
    g_last = [x[C - 1:C, :] for x in gcum_b]

    kb = hmap(lambda a_, b_: a_ * b_, k, beta)
    vb = hmap(lambda a_, b_: a_ * b_, v, beta)
    a = hmap(lambda kb_, q_, k_: lax.dot_general(
        jnp.concatenate([kb_, q_], axis=0).astype(BF16), k_.astype(BF16), NT,
        preferred_element_type=F32), kb, q, k)
    lmat = hmap(lambda a_, d_: jnp.where(strict, a_[:C] * d_, 0.0), a, decay)
    attn = hmap(lambda a_, d_: a_[C:] * d_, a, decay)

    d1 = hmap(lambda l_: jnp.where(rxc < 8, l_, 0.0), lmat)
    d2 = hmap(lambda x: _bdot(x, x), d1)
    d4 = hmap(lambda x: _bdot(x, x), d2)
    p1 = hmap(lambda x, y: _bdot(eye - x, eye + y), d1, d2)
    tmat = hmap(lambda x, y: _bdot(x, eye + y), p1, d4)

    def odd_rows(t, sz):
        return jnp.concatenate([t[b * 2 * sz + sz:(b + 1) * 2 * sz] for b in range(C // (2 * sz))], axis=0)

    def with_odd_rows(t, odd, sz):
        pieces = []
        for b in range(C // (2 * sz)):
            pieces += [t[b * 2 * sz:b * 2 * sz + sz], odd[b * sz:(b + 1) * sz]]
        return jnp.concatenate(pieces, axis=0)

    sz = 8
    while sz < C:
        off = hmap(lambda l_: jnp.where((rxc >= sz) & (rxc < 2 * sz), l_, 0.0), lmat)
        t_odd = hmap(lambda t_: odd_rows(t_, sz), tmat)
        x = hmap(_bdot, t_odd, off)
        x = hmap(_bdot, x, tmat)
        tmat = hmap(lambda t_, o_, x_: with_odd_rows(t_, o_ - x_, sz), tmat, t_odd, x)
        sz *= 2

    uw = hmap(lambda t_, vb_, kb_, e_: _bdot(t_, jnp.concatenate([vb_, kb_ * e_], axis=1)),
              tmat, vb, kb, e_g)
    state = [state_scr[hh] for hh in heads]
    wq = hmap(lambda uw_, q_, e_, s_: _bdot(jnp.concatenate([uw_[:, HEAD_DIM:], q_ * e_], axis=0), s_),
              uw, q, e_g, state)
    v_new = hmap(lambda uw_, wq_: uw_[:, :HEAD_DIM] - wq_[:C], uw, wq)
    av = hmap(_bdot, attn, v_new)
    kt = hmap(lambda k_, gl_, gc_: (k_ * jnp.exp(gl_ - gc_)).T, k, g_last, gcum_b)
    ds = hmap(_bdot, kt, v_new)
    for hh in heads:
        new_state = state[hh] * jnp.exp(g_last[hh]) + ds[hh]
        state_scr[hh] = new_state
        st_out_ref[hh] = new_state
        o = wq[hh][C:] + av[hh]
        y = o * lax.rsqrt(jnp.mean(o * o, axis=-1, keepdims=True) + EPS) * onw_ref[...]
        z = z_ref[:, sls[hh]].astype(F32)
        o_ref[:, sls[hh]] = (y * (z * _sigmoid(z))).astype(BF16)


def _gated_deltanet(proj, small, conv_w, alog_v, dtb_v, onw, state, c0, n_chunks, yd_prev=None, after=None):
    s = proj.shape[0]
    rb = DN_CHUNK // HALO
    hp = DN_HEADS_PER_STEP
    assert hp == HEADS
    width = hp * HEAD_DIM

    def col(part):
        return pl.BlockSpec((DN_CHUNK, width), lambda c: (c0 + c, part))

    def halo(part):
        return pl.BlockSpec((HALO, width), lambda c: (jnp.maximum((c0 + c) * rb - 1, 0), part))

    def cw(part):
        return pl.BlockSpec((DN_CONV, width), lambda c: (0, part))

    vec = pl.BlockSpec((1, 128), lambda c: (0, 0))
    st_spec = pl.BlockSpec((hp, HEAD_DIM, HEAD_DIM), lambda c: (0, 0, 0))
    extras = [a for a in (yd_prev, after) if a is not None]
    n_fixed = 15
    return pl.pallas_call(
        functools.partial(_delta_kernel, hg=0, c0=c0),
        grid=(n_chunks,),
        in_specs=[col(3), col(4), col(5),
                  halo(3), halo(4), halo(5),
                  col(6),
                  pl.BlockSpec((DN_CHUNK, 128), lambda c: (c0 + c, 0)),
                  cw(0), cw(1), cw(2),
                  vec, vec, vec, st_spec] + [pl.BlockSpec(memory_space=pl.ANY)] * len(extras),
        out_specs=[pl.BlockSpec((DN_CHUNK, width), lambda c: (c0 + c, 0)), st_spec],
        out_shape=[jax.ShapeDtypeStruct((s, width), BF16),
                   jax.ShapeDtypeStruct((hp, HEAD_DIM, HEAD_DIM), F32)],
        input_output_aliases={n_fixed: 0} if yd_prev is not None else {},
        scratch_shapes=[pltpu.VMEM((hp, HEAD_DIM, HEAD_DIM), F32),
                        pltpu.VMEM((3, HALO + DN_CHUNK, width), F32)],
        compiler_params=pltpu.CompilerParams(
            dimension_semantics=("arbitrary",), vmem_limit_bytes=VMEM_LIMIT),
        name="deltanet",
    )(proj, proj, proj, proj, proj, proj, proj, small, conv_w, conv_w, conv_w, alog_v, dtb_v, onw,
      state, *extras)


def _mix_ffn_kernel(ya_ref, yd_ref, ga_ref, gd_ref, x_ref, wa_ref, wd_ref, wo_ref, nmix_ref,
                    npre_ref, wg_ref, wu_ref, wdn_ref, npost_ref, o_ref):
    pa = jnp.dot(ya_ref[...], wa_ref[...], preferred_element_type=F32)
    pd = jnp.dot(yd_ref[...], wd_ref[...], preferred_element_type=F32)
    merged = _sigmoid(ga_ref[...].astype(F32)) * pa + _sigmoid(gd_ref[...].astype(F32)) * pd
    mo = jnp.dot(merged.astype(BF16), wo_ref[...], preferred_element_type=F32)
    x = x_ref[...] + mo * lax.rsqrt(jnp.mean(mo * mo, axis=-1, keepdims=True) + EPS) * nmix_ref[...]

    h = (x * lax.rsqrt(jnp.mean(x * x, axis=-1, keepdims=True) + EPS) * npre_ref[...]).astype(BF16)
    acc = jnp.zeros(x.shape, F32)
    for cc in range(D_FF // FF_CHUNK):
        sl = slice(cc * FF_CHUNK, (cc + 1) * FF_CHUNK)
        g = jnp.dot(h, wg_ref[:, sl], preferred_element_type=F32)
        u = jnp.dot(h, wu_ref[:, sl], preferred_element_type=F32)
        act = (g * _sigmoid(g) * u).astype(BF16)
        acc = acc + jnp.dot(act, wdn_ref[sl, :], preferred_element_type=F32)
    y = acc * lax.rsqrt(jnp.mean(acc * acc, axis=-1, keepdims=True) + EPS) * npost_ref[...]
    o_ref[...] = x + y


def _mix_ffn(ya, yd, proj, x2, wa, wd, wo, nmix, npre, wg, wu, wdn, npost, *, tm=512):
    s = x2.shape[0]
    row = lambda i: (i, 0)
    full = lambda i: (0, 0)
    once = pl.Buffered(1)
    tile = pl.BlockSpec((tm, D_MODEL), row)
    vec = pl.BlockSpec((1, D_MODEL), full)
    wsq = pl.BlockSpec((D_MODEL, D_MODEL), full, pipeline_mode=once)
    return pl.pallas_call(
        _mix_ffn_kernel,
        grid=(s // tm,),
        in_specs=[tile, tile,
                  pl.BlockSpec((tm, D_MODEL), lambda i: (i, 7)),
                  pl.BlockSpec((tm, D_MODEL), lambda i: (i, 8)),
                  tile, wsq, wsq, wsq, vec, vec,
                  pl.BlockSpec((D_MODEL, D_FF), full, pipeline_mode=once),
                  pl.BlockSpec((D_MODEL, D_FF), full, pipeline_mode=once),
                  pl.BlockSpec((D_FF, D_MODEL), full, pipeline_mode=once),
                  vec],
        out_specs=tile,
        out_shape=jax.ShapeDtypeStruct((s, D_MODEL), F32),
        compiler_params=pltpu.CompilerParams(
            dimension_semantics=("parallel",), vmem_limit_bytes=VMEM_LIMIT),
        name="mix_ffn",
    )(ya, yd, proj, proj, x2, wa, wd, wo, nmix, npre, wg, wu, wdn, npost)


def _rope_table(s):
    half = ROPE_DIM // 2
    rep = 128 // half
    inv = ROPE_THETA ** (-jnp.arange(half, dtype=F32) * 2.0 / ROPE_DIM)
    pos = (jnp.arange(s // rep)[:, None] * rep + jnp.arange(128)[None, :] // half).astype(F32)
    ang = pos * jnp.tile(inv, rep)[None, :]
    cos, sin = lax.optimization_barrier((jnp.cos(ang), jnp.sin(ang)))
    cos = cos.reshape(s, half)
    sin = sin.reshape(s, half)
    return jnp.concatenate([cos, sin, jnp.zeros((s, HEAD_DIM - ROPE_DIM), F32)], axis=-1)


def _layer(x2, l, norm_mix_pre, w_in, conv_w, a_log, dt_bias, o_norm_w, w_o_attn, w_o_delta,
           w_out, norm_mix_post, norm_ffn_pre, w_gate, w_up, w_down, norm_ffn_post):
    s = x2.shape[0]
    nb = s // MOBA_BLOCK
    wide = 7 * D_MODEL
    nsmall = 2 * HEADS
    w_t = jnp.swapaxes(w_in[l], 0, 1).astype(BF16)
    later_w = [w[l].astype(BF16) for w in (w_o_attn, w_o_delta, w_out, w_gate, w_up, w_down)]

    proj, small, kmean, v_t, qf = _project(x2, norm_mix_pre[l][None, :], w_t, _rope_table(s),
                                           wide // D_MODEL, nsmall, after=later_w)
    pad = lambda vec: jnp.pad(vec.astype(F32), (HEADS, 128 - 2 * HEADS))[None, :]
    half = (s // DN_CHUNK) // 2
    delta = functools.partial(_gated_deltanet, proj, small, conv_w[l], pad(a_log[l]), pad(dt_bias[l]),
                              o_norm_w[l][None, :])
    res = {}

    def first_half():
        res["yd"], res["state"] = delta(jnp.zeros((HEADS, HEAD_DIM, HEAD_DIM), F32), 0, half)
        return res["yd"]

    def second_half(after):
        res["yd"], _ = delta(res["state"], half, s // DN_CHUNK - half, yd_prev=res["yd"], after=after)

    ya = _moba_sparse(proj, v_t, kmean.reshape(nb, D_MODEL), qf, first_half, second_half)

    return _mix_ffn(ya, res["yd"], proj, x2, *later_w[:3], norm_mix_post[l][None, :],
                    norm_ffn_pre[l][None, :], *later_w[3:], norm_ffn_post[l][None, :])


def kernel(x, norm_mix_pre, w_in, conv_w, a_log, dt_bias, o_norm_w, w_o_attn, w_o_delta, w_out,
           norm_mix_post, norm_ffn_pre, w_gate, w_up, w_down, norm_ffn_post):
    b, s, d = x.shape
    assert d == D_MODEL and s % (ROUTE_QBLOCKS * MOBA_BLOCK) == 0 and s % (2 * DN_CHUNK) == 0
    outs = []
    for bi in range(b):
        x2 = x.reshape(s, d) if b == 1 else x[bi]
        for l in range(w_in.shape[0]):
            x2 = _layer(x2, l, norm_mix_pre, w_in, conv_w, a_log, dt_bias, o_norm_w, w_o_attn,
                        w_o_delta, w_out, norm_mix_post, norm_ffn_pre, w_gate, w_up, w_down,
                        norm_ffn_post)
        outs.append(x2)
    return outs[0].reshape(1, s, d) if b == 1 else jnp.stack(outs, axis=0)
```

```python
import functools
import math

import jax
import jax.numpy as jnp
from jax import lax
from jax.experimental import pallas as pl
from jax.experimental.pallas import tpu as pltpu
from jax.experimental.pallas import tpu_sc as plsc

D_MODEL = 1024
HEADS = 8
HEAD_DIM = 128
MOBA_BLOCK = 256
MOBA_TOPK = 3
ROPE_DIM = HEAD_DIM // 4
ROPE_THETA = 500000.0
DN_CONV = 4
ONES_ROWS = 16
DN_HEADS_PER_STEP = 8
DN_CHUNK = 256
D_FF = 2816
FF_CHUNK = 256
EPS = 1e-6
LOG2_E = math.log2(math.e)
HALO = 16

F32 = jnp.float32
BF16 = jnp.bfloat16
NT = (((1,), (1,)), ((), ()))

VMEM_LIMIT = 56 * 1024 * 1024


def _bdot(a, b):
    return jnp.dot(a.astype(BF16), b.astype(BF16), preferred_element_type=F32)


def _sigmoid(x):
    return 1.0 / (1.0 + jnp.exp(-x))


def _proj_kernel(x_ref, nw_ref, wt_ref, rope_ref, *rest, tm, q_scale, n_wide, n_small):
    out_ref, small_ref, kmean_ref, vt_ref, qf_ref = rest[-5:]
    x = x_ref[...]
    ms = jnp.mean(x * x, axis=-1, keepdims=True)
    h = (x * lax.rsqrt(ms + EPS) * nw_ref[...]).astype(BF16)
    narrow = lax.dot_general(h, wt_ref[n_wide * D_MODEL:n_wide * D_MODEL + 128, :], NT,
                             preferred_element_type=F32)
    small_ref[...] = jnp.where(lax.broadcasted_iota(jnp.int32, narrow.shape, 1) < n_small, narrow, 0.0)
    heads = [slice(hh * HEAD_DIM, (hh + 1) * HEAD_DIM) for hh in range(HEADS)]
    groups = [slice(g * MOBA_BLOCK, (g + 1) * MOBA_BLOCK) for g in range(tm // MOBA_BLOCK)]

    half = ROPE_DIM // 2
    tab = rope_ref[...]
    lane = lax.broadcasted_iota(jnp.int32, tab.shape, 1)
    cos_t = jnp.where(lane < half, tab, jnp.where(lane < ROPE_DIM, pltpu.roll(tab, half, 1), 1.0))
    s1_t = jnp.where(lane < half, -pltpu.roll(tab, HEAD_DIM - half, 1), 0.0)
    s2_t = jnp.where((lane >= half) & (lane < ROPE_DIM), tab, 0.0)

    def roped(a):
        return (a * cos_t + pltpu.roll(a, HEAD_DIM - half, 1) * s1_t + pltpu.roll(a, half, 1) * s2_t)

    for c in range(out_ref.shape[1] // D_MODEL):
        cols = slice(c * D_MODEL, (c + 1) * D_MODEL)
        r0 = c * D_MODEL + (n_small if c >= n_wide else 0)
        acc = lax.dot_general(h, wt_ref[r0:r0 + D_MODEL, :], NT, preferred_element_type=F32)
        if c == 0:
            for hh, hs in enumerate(heads):
                r = roped(acc[:, hs]) * q_scale
                out_ref[:, hs] = r.astype(BF16)
                qf_ref[hh] = r
        elif c == 1:
            for hs in heads:
                r = roped(acc[:, hs])
                out_ref[:, D_MODEL + hs.start:D_MODEL + hs.stop] = r.astype(BF16)
                for g, gs in enumerate(groups):
                    kmean_ref[0, g:g + 1, hs] = jnp.sum(r[gs], axis=0, keepdims=True) * (1.0 / MOBA_BLOCK)
        elif c == 2:
            out_ref[:, cols] = acc.astype(BF16)
            for hh, hs in enumerate(heads):
                for g, gs in enumerate(groups):
                    vt_ref[hh, g, 0:HEAD_DIM, :] = acc[gs, hs].T.astype(BF16)
                    vt_ref[hh, g, HEAD_DIM:, :] = jnp.ones((ONES_ROWS, MOBA_BLOCK), BF16)
        else:
            out_ref[:, cols] = acc.astype(BF16)


def _project(x2, norm_w, w_t, rope_t, n_wide, n_small, after=(), *, tm=512):
    s = x2.shape[0]
    width = (w_t.shape[0] - n_small) // D_MODEL * D_MODEL
    once = pl.Buffered(1)
    nblk = tm // MOBA_BLOCK
    kern = functools.partial(_proj_kernel, tm=tm, q_scale=math.log2(math.e) / math.sqrt(HEAD_DIM),
                             n_wide=n_wide, n_small=n_small)
    row = lambda i: (i, 0)
    full = lambda i: (0, 0)
    return pl.pallas_call(
        kern,
        grid=(s // tm,),
        in_specs=[
            pl.BlockSpec((tm, D_MODEL), row),
            pl.BlockSpec((1, D_MODEL), full),
            pl.BlockSpec(w_t.shape, full, pipeline_mode=once),
            pl.BlockSpec((tm, HEAD_DIM), row),
        ] + [pl.BlockSpec(memory_space=pl.ANY)] * len(after),
        out_specs=[
            pl.BlockSpec((tm, width), row),
            pl.BlockSpec((tm, 128), row),
            pl.BlockSpec((1, nblk, D_MODEL), lambda i: (i, 0, 0)),
            pl.BlockSpec((HEADS, nblk, HEAD_DIM + ONES_ROWS, MOBA_BLOCK), lambda i: (0, i, 0, 0)),
            pl.BlockSpec((HEADS, tm, HEAD_DIM), lambda i: (0, i, 0)),
        ],
        out_shape=[
            jax.ShapeDtypeStruct((s, width), BF16),
            jax.ShapeDtypeStruct((s, 128), F32),
            jax.ShapeDtypeStruct((s // tm, nblk, D_MODEL), F32),
            jax.ShapeDtypeStruct((HEADS, s // MOBA_BLOCK, HEAD_DIM + ONES_ROWS, MOBA_BLOCK), BF16),
            jax.ShapeDtypeStruct((HEADS, s, HEAD_DIM), F32),
        ],
        compiler_params=pltpu.CompilerParams(
            dimension_semantics=("parallel",), vmem_limit_bytes=VMEM_LIMIT),
        name="proj",
    )(x2, norm_w, w_t, rope_t, *after)


ROUTE_QBLOCKS = 16
TILES_PER_STEP = 32
SC_WINDOW = 128
DEST_LANE_CHUNK = 2048


def _top_blocks(gate, blk_f, nb):
    picks = []
    for _ in range(MOBA_TOPK):
        m = jnp.max(gate, axis=0, keepdims=True)
        first = jnp.min(jnp.where(gate == m, blk_f, float(nb)), axis=0, keepdims=True)
        pick = (blk_f == first) & (m > -jnp.inf)
        gate = jnp.where(pick, -jnp.inf, gate)
        picks.append(pick)
    return picks


def _route_kernel(q_ref, km_ref, info_ref, cnt_ref, run_scr, *, nb, steps):
    jb = pl.program_id(1)

    @pl.when(jb == 0)
    def _():
        run_scr[...] = jnp.zeros_like(run_scr)

    for v in range(steps):
        pl.when(jb == v)(functools.partial(_route_step, q_ref, km_ref, info_ref, cnt_ref, run_scr, v, nb))


def _route_step(q_ref, km_ref, info_ref, cnt_ref, run_scr, jb, nb_all):
    L = ROUTE_QBLOCKS * MOBA_BLOCK
    nb = min(nb_all, ROUTE_QBLOCKS * (jb + 1))
    km = km_ref[0:nb, :]
    km_hi = km.astype(BF16)
    km_lo = (km - km_hi.astype(F32)).astype(BF16)
    gate2 = lax.dot_general(jnp.concatenate([km_hi, km_lo], axis=0), q_ref[...], NT,
                            preferred_element_type=F32)
    blk = lax.broadcasted_iota(jnp.int32, (nb, L), 0)
    blk_f = blk.astype(F32)
    qblk = jb * ROUTE_QBLOCKS + (lax.broadcasted_iota(jnp.int32, (nb, L), 1) >> 8)
    gate = jnp.where(blk < qblk, gate2[:nb] + gate2[nb:], -jnp.inf)
    picks = _top_blocks(gate, blk_f, nb_all)

    chosen = jnp.where(picks[0] | picks[1] | picks[2], 1.0, 0.0)
    B = MOBA_BLOCK
    before = jnp.where(lax.broadcasted_iota(jnp.int32, (B, B), 0)
                       < lax.broadcasted_iota(jnp.int32, (B, B), 1), 1.0, 0.0).astype(BF16)
    carry = run_scr[0:nb, 0:1]
    base = []
    for b in range(ROUTE_QBLOCKS):
        c_b = chosen[:, b * B:(b + 1) * B]
        within = jnp.dot(c_b.astype(BF16), before, preferred_element_type=F32)
        base.append(carry + within)
        carry = carry + within[:, B - 1:B] + c_b[:, B - 1:B]
    base = jnp.concatenate(base, axis=1)

    rows = []
    for pick in picks:
        rows.append(jnp.sum(jnp.where(pick, base, 0.0), axis=0, keepdims=True))
    for pick in picks:
        bid = jnp.sum(jnp.where(pick, blk_f, 0.0), axis=0, keepdims=True)
        valid = jnp.sum(jnp.where(pick, 1.0, 0.0), axis=0, keepdims=True)
        rows.append(jnp.where(valid > 0.0, bid, float(nb_all)))
    rows += [jnp.zeros((1, L), F32)] * (8 - len(rows))
    info_ref[...] = jnp.concatenate(rows, axis=0).astype(jnp.int32)

    run_scr[0:nb, :] = jnp.broadcast_to(carry, (nb, run_scr.shape[1]))
    cnt_ref[0] = run_scr[...]


def _route(proj, kmean):
    s = proj.shape[0]
    nb = s // MOBA_BLOCK
    L = ROUTE_QBLOCKS * MOBA_BLOCK
    steps = s // L
    return pl.pallas_call(
        functools.partial(_route_kernel, nb=nb, steps=steps),
        grid=(HEADS, steps),
        in_specs=[pl.BlockSpec((L, HEAD_DIM), lambda h, j: (j, h)),
                  pl.BlockSpec((nb, HEAD_DIM), lambda h, j: (0, h))],
        out_specs=[pl.BlockSpec((8, L), lambda h, j: (0, h * steps + j)),
                   pl.BlockSpec((1, nb, 128), lambda h, j: (h, 0, 0))],
        out_shape=[jax.ShapeDtypeStruct((8, HEADS * s), jnp.int32),
                   jax.ShapeDtypeStruct((HEADS, nb, 128), F32)],
        scratch_shapes=[pltpu.VMEM((nb, 128), F32)],
        compiler_params=pltpu.CompilerParams(
            dimension_semantics=("parallel", "arbitrary"), vmem_limit_bytes=VMEM_LIMIT),
        name="moba_route",
    )(proj, kmean)


def _dest_kernel(info_ref, cnt_ref, dest_ref, tb_ref, *, nb, s):
    h = pl.program_id(0)
    cap = 4 * s
    ntile = cap // MOBA_BLOCK
    cnt = cnt_ref[0]
    tiles = jnp.floor((cnt + float(MOBA_BLOCK - 1)) * (1.0 / MOBA_BLOCK))
    lower = jnp.where(lax.broadcasted_iota(jnp.int32, (nb, nb), 1)
                      < lax.broadcasted_iota(jnp.int32, (nb, nb), 0), 1.0, 0.0).astype(BF16)
    start = jnp.dot(lower, tiles.astype(BF16), preferred_element_type=F32)
    start1 = start[:, 0:1]
    tiles1 = tiles[:, 0:1]

    t_f = lax.broadcasted_iota(jnp.int32, (nb, ntile), 1).astype(F32)
    n_f = lax.broadcasted_iota(jnp.int32, (nb, ntile), 0).astype(F32)
    inside = (t_f >= start1) & (t_f < start1 + tiles1)
    tb = jnp.sum(jnp.where(inside, n_f, 0.0), axis=0, keepdims=True)
    used = jnp.sum(jnp.where(inside, 1.0, 0.0), axis=0, keepdims=True)
    tb = jnp.where(used > 0.0, tb, -1.0)
    tb_ref[0] = jnp.concatenate([tb, jnp.full((7, ntile), -1.0, F32)], axis=0).astype(jnp.int32)

    off1 = start1 * float(MOBA_BLOCK)
    CH = DEST_LANE_CHUNK
    blk_f = lax.broadcasted_iota(jnp.int32, (nb, CH), 0).astype(F32)
    lane = lax.broadcasted_iota(jnp.int32, (1, CH), 1)
    trash = HEADS * cap + (lane & (MOBA_BLOCK - 1))
    for ch in range(s // CH):
        sl = slice(ch * CH, (ch + 1) * CH)
        rows = []
        for r in range(MOBA_TOPK):
            rank = info_ref[r:r + 1, sl]
            bid = info_ref[MOBA_TOPK + r:MOBA_TOPK + r + 1, sl]
            off = jnp.sum(jnp.where(blk_f == bid.astype(F32), off1, 0.0), axis=0, keepdims=True)
            rows.append(jnp.where(bid < nb, h * cap + off.astype(jnp.int32) + rank, trash))
        rows += [jnp.zeros((1, CH), jnp.int32)] * (8 - len(rows))
        dest_ref[:, sl] = jnp.concatenate(rows, axis=0)


def _dest(info, cnt, s):
    nb = s // MOBA_BLOCK
    ntile = 4 * s // MOBA_BLOCK
    return pl.pallas_call(
        functools.partial(_dest_kernel, nb=nb, s=s),
        grid=(HEADS,),
        in_specs=[pl.BlockSpec((8, s), lambda h: (0, h)),
                  pl.BlockSpec((1, nb, 128), lambda h: (h, 0, 0))],
        out_specs=[pl.BlockSpec((8, s), lambda h: (0, h)),
                   pl.BlockSpec((1, 8, ntile), lambda h: (h, 0, 0))],
        out_shape=[jax.ShapeDtypeStruct((8, HEADS * s), jnp.int32),
                   jax.ShapeDtypeStruct((HEADS, 8, ntile), jnp.int32)],
        compiler_params=pltpu.CompilerParams(
            dimension_semantics=("parallel",), vmem_limit_bytes=VMEM_LIMIT),
        name="moba_dest",
    )(info, cnt)


def _sc_scatter_rows(rows, idx, n_out):
    m, c = rows.shape
    k = idx.shape[0]
    nwin = m // SC_WINDOW
    mesh = plsc.VectorSubcoreMesh(core_axis_name="core", subcore_axis_name="subcore")

    @pl.kernel(out_type=jax.ShapeDtypeStruct((n_out, c), rows.dtype), mesh=mesh)
    def kern(x_hbm, i_hbm, o_hbm):
        def body(x_vmem, i_vmem):
            pltpu.sync_copy(x_vmem, o_hbm.at[i_vmem.at[0]])

        pltpu.emit_pipeline(
            body, grid=(k * nwin,),
            in_specs=[pl.BlockSpec((SC_WINDOW, c), lambda i: (i % nwin, 0)),
                      pl.BlockSpec((1, SC_WINDOW), lambda i: (0, i))],
            out_specs=[],
            core_axis_name=("core", "subcore"),
            dimension_semantics=(pltpu.PARALLEL,),
        )(x_hbm, i_hbm)

    return kern(rows, idx.reshape(1, k * m))


def _sc_gather_rows(table, idx):
    m = idx.shape[0]
    c = table.shape[1]
    mesh = plsc.VectorSubcoreMesh(core_axis_name="core", subcore_axis_name="subcore")

    @pl.kernel(out_type=jax.ShapeDtypeStruct((m, c), table.dtype), mesh=mesh)
    def kern(x_hbm, i_hbm, o_hbm):
        def body(i_vmem, o_vmem):
            pltpu.sync_copy(x_hbm.at[i_vmem.at[0]], o_vmem)

        pltpu.emit_pipeline(
            body, grid=(m // SC_WINDOW,),
            in_specs=[pl.BlockSpec((1, SC_WINDOW), lambda i: (0, i))],
            out_specs=[pl.BlockSpec((SC_WINDOW, c), lambda i: (i, 0))],
            core_axis_name=("core", "subcore"),
            dimension_semantics=(pltpu.PARALLEL,),
        )(i_hbm, o_hbm)

    return kern(table, idx.reshape(1, m))


def _pack_partial_t(o_norm_t, lse):
    half = HEAD_DIM // 2
    u = lax.bitcast_convert_type(o_norm_t, jnp.uint32) + jnp.uint32(0x8000)
    word = (u[:half] & jnp.uint32(0xFFFF0000)) | (u[half:] >> 16)
    lse_bits = lax.bitcast_convert_type(jnp.broadcast_to(lse, word.shape), jnp.uint32)
    full = jnp.concatenate([word, lse_bits], axis=0)
    return lax.bitcast_convert_type(full, F32).T


def _unpack_partial(part):
    half = HEAD_DIM // 2
    lane = lax.broadcasted_iota(jnp.int32, part.shape, 1)
    swapped = pltpu.roll(part, half, 1)
    lse = jnp.where(lane < half, swapped, part)
    word = lax.bitcast_convert_type(jnp.where(lane < half, part, swapped), jnp.uint32)
    o = jnp.where(lane < half, lax.bitcast_convert_type(word & jnp.uint32(0xFFFF0000), F32),
                  lax.bitcast_convert_type(word << 16, F32))
    return o, lse


def _tiles_kernel(tb_ref, qs_ref, k_ref, vt_ref, after_ref, o_ref, *, ntile):
    del after_ref
    h = pl.program_id(0)
    g = pl.program_id(1)
    B = MOBA_BLOCK
    base = h * ntile + g * TILES_PER_STEP

    @pl.when(tb_ref[base] >= 0)
    def _():
        blocks = [jnp.maximum(tb_ref[base + u], 0) for u in range(TILES_PER_STEP)]
        s_t = [lax.dot_general(k_ref[pl.ds(pl.multiple_of(n * B, B), B), :],
                               qs_ref[u * B:(u + 1) * B, :].astype(BF16), NT,
                               preferred_element_type=F32) for u, n in enumerate(blocks)]
        m = [jnp.max(x, axis=0, keepdims=True) for x in s_t]
        p = [jnp.exp2(x - mm).astype(BF16) for x, mm in zip(s_t, m)]
        acc = [jnp.dot(vt_ref[0, n], pp, preferred_element_type=F32) for n, pp in zip(blocks, p)]
        for u in range(TILES_PER_STEP):
            l = acc[u][HEAD_DIM:HEAD_DIM + 1]
            o_ref[u * B:(u + 1) * B, :] = _pack_partial_t(acc[u][:HEAD_DIM] / l,
                                                          m[u] + jnp.log(l) * LOG2_E)


def _tiles(tb, qsorted, proj, v_t, after, s):
    nb = s // MOBA_BLOCK
    ntile = 4 * s // MOBA_BLOCK
    steps = ntile // TILES_PER_STEP
    rows = TILES_PER_STEP * MOBA_BLOCK
    grid_spec = pltpu.PrefetchScalarGridSpec(
        num_scalar_prefetch=1,
        grid=(HEADS, steps),
        in_specs=[pl.BlockSpec((rows, HEAD_DIM), lambda h, g, tb: (h * steps + g, 0)),
                  pl.BlockSpec((s, HEAD_DIM), lambda h, g, tb: (0, HEADS + h)),
                  pl.BlockSpec((1, nb, HEAD_DIM + ONES_ROWS, MOBA_BLOCK), lambda h, g, tb: (h, 0, 0, 0)),
                  pl.BlockSpec((16, HEAD_DIM), lambda h, g, tb: (0, 0))],
        out_specs=pl.BlockSpec((rows, HEAD_DIM), lambda h, g, tb: (h * steps + g, 0)),
    )
    return pl.pallas_call(
        functools.partial(_tiles_kernel, ntile=ntile),
        grid_spec=grid_spec,
        out_shape=jax.ShapeDtypeStruct(qsorted.shape, F32),
        compiler_params=pltpu.CompilerParams(
            dimension_semantics=("parallel", "arbitrary"), vmem_limit_bytes=VMEM_LIMIT),
        name="moba_tiles",
    )(tb, qsorted, proj, v_t, after)


def _merge_block(q_ref, k_ref, v_ref, part_ref, o_ref, j, head_ids):
    B = MOBA_BLOCK
    heads = [slice(hh * HEAD_DIM, (hh + 1) * HEAD_DIM) for hh in head_ids]
    causal = (lax.broadcasted_iota(jnp.int32, (B, B), 1) <= lax.broadcasted_iota(jnp.int32, (B, B), 0))
    s = [lax.dot_general(q_ref[:, hs], k_ref[:, hs], NT, preferred_element_type=F32) for hs in heads]
    s = [jnp.where(causal, x, -jnp.inf) for x in s]
    m_own = [jnp.broadcast_to(jnp.max(x, axis=-1, keepdims=True), (B, HEAD_DIM)) for x in s]
    p = [jnp.exp2(x - jnp.concatenate([m, m], axis=1)) for x, m in zip(s, m_own)]
    ones = jnp.ones((B, HEAD_DIM), BF16)
    ol = [jnp.dot(x.astype(BF16), jnp.concatenate([v_ref[:, hs], ones], axis=1),
                  preferred_element_type=F32) for x, hs in zip(p, heads)]
    for pos, (hh, hs) in enumerate(zip(head_ids, heads)):
        o_own = ol[pos][:, :HEAD_DIM]
        l_own = ol[pos][:, HEAD_DIM:]
        lse_own = m_own[pos] + jnp.log(l_own) * LOG2_E
        parts = []
        for r in range(MOBA_TOPK):
            o_r, lse_r = _unpack_partial(part_ref[r, hh])
            valid = r < j
            parts.append((jnp.where(valid, o_r, 0.0), jnp.where(valid, lse_r, -jnp.inf)))
        m_all = lse_own
        for _, lse_r in parts:
            m_all = jnp.maximum(m_all, lse_r)
        w = jnp.exp2(lse_own - m_all)
        num = (w / l_own) * o_own
        den = w
        for o_r, lse_r in parts:
            w = jnp.exp2(lse_r - m_all)
            num = num + w * o_r
            den = den + w
        o_ref[:, hs] = (num / den).astype(BF16)


def _moba_sparse(proj, v_t, kmean, qf, run_during_scatter, run_during_gather):
    s = proj.shape[0]
    cap = 4 * s
    n_rows = HEADS * cap + MOBA_BLOCK
    info, cnt = _route(proj, kmean)
    dest, tb = _dest(info, cnt, s)
    dest3 = dest[:MOBA_TOPK]
    qsorted = _sc_scatter_rows(qf.reshape(HEADS * s, HEAD_DIM), dest3, n_rows)
    after = run_during_scatter()
    osorted = _tiles(tb[:, 0, :].reshape(-1), qsorted, proj, v_t, after, s)
    parts = _sc_gather_rows(osorted, dest3.reshape(-1))
    run_during_gather(osorted)
    return parts.reshape(MOBA_TOPK, HEADS, s, HEAD_DIM)


def _delta_kernel(q_ref, k_ref, v_ref, qh_ref, kh_ref, vh_ref, z_ref, small_ref,
                  cwq_ref, cwk_ref, cwv_ref, alog_ref, dtb_ref, onw_ref, st_in_ref, *rest, hg, c0):
    o_ref, st_out_ref, state_scr, xx_scr = rest[-4:]
    c = c0 + pl.program_id(0)
    C = DN_CHUNK

    @pl.when(pl.program_id(0) == 0)
    def _():
        state_scr[...] = st_in_ref[...]

    shifts = DN_CONV - 1
    r_out = lax.broadcasted_iota(jnp.int32, (shifts * C, C), 0)
    r_in = lax.broadcasted_iota(jnp.int32, (shifts * C, C), 1)
    delay = jnp.where((r_out & (C - 1)) - r_in == (r_out >> 8) + 1, 1.0, 0.0).astype(BF16)
    assert C == 256

    def conv_silu(idx, x_ref, halo_ref, cw_ref):
        x = x_ref[...]
        delayed = jnp.dot(delay, x, preferred_element_type=F32)
        halo = jnp.where(c == 0, 0.0, halo_ref[...].astype(F32))
        xx_scr[idx, 0:HALO, :] = halo
        xx_scr[idx, HALO:HALO + HALO, :] = x_ref[0:HALO, :].astype(F32)
        y = cw_ref[shifts:shifts + 1, :] * x.astype(F32)
        for j in range(1, shifts + 1):
            d_j = delayed[(j - 1) * C:j * C]
            head8 = xx_scr[idx, HALO - j:HALO - j + 8, :]
            d_j = jnp.concatenate([head8, d_j[8:]], axis=0)
            y = y + cw_ref[shifts - j:shifts - j + 1, :] * d_j
        return y * _sigmoid(y)

    q_all = conv_silu(0, q_ref, qh_ref, cwq_ref)
    k_all = conv_silu(1, k_ref, kh_ref, cwk_ref)
    v_all = conv_silu(2, v_ref, vh_ref, cwv_ref)

    small = small_ref[...]
    beta_all = _sigmoid(small)
    xs = small + dtb_ref[...]
    softplus = jnp.maximum(xs, 0.0) + jnp.log(1.0 + jnp.exp(-jnp.abs(xs)))
    g_all = -jnp.exp(alog_ref[...]) * softplus

    row = lax.broadcasted_iota(jnp.int32, (C, C), 0)
    col = lax.broadcasted_iota(jnp.int32, (C, C), 1)
    tril = row >= col
    strict = row > col
    rxc = row ^ col
    eye = (row == col).astype(F32)
    assert C == 2 * HEAD_DIM

    heads = range(DN_HEADS_PER_STEP)
    sls = [slice(hh * HEAD_DIM, (hh + 1) * HEAD_DIM) for hh in heads]

    def hmap(f, *lists):
        return [f(*xs) for xs in zip(*lists)]

    def pieces(x, n):
        out, rest = [], x
        for _ in range(n):
            p = rest.astype(BF16)
            out.append(p)
            rest = rest - p.astype(F32)
        return jnp.concatenate(out, axis=1)

    hp = DN_HEADS_PER_STEP
    same_half = jnp.where((row >> 7) == (col >> 7), 1.0, 0.0).astype(BF16)

    def unit_rows_pair(xq, xk):
        sq = jnp.concatenate([xq * xq, xk * xk], axis=1).astype(BF16)
        inv = lax.rsqrt(jnp.dot(sq, same_half, preferred_element_type=F32) + EPS)
        return xq * inv[:, :HEAD_DIM], xk * inv[:, HEAD_DIM:]

    def selector(first_lane, n):
        src = lax.broadcasted_iota(jnp.int32, (128, hp * HEAD_DIM), 0)
        dst = lax.broadcasted_iota(jnp.int32, (128, hp * HEAD_DIM), 1) >> 7
        sel = jnp.where(src == first_lane + dst, 1.0, 0.0).astype(BF16)
        return jnp.concatenate([sel] * n, axis=0)

    qk = [unit_rows_pair(q_all[:, sl], k_all[:, sl]) for sl in sls]
    q = [x[0] * (HEAD_DIM ** -0.5) for x in qk]
    k = [x[1] for x in qk]
    v = [v_all[:, sl] for sl in sls]
    beta_rep = jnp.dot(pieces(beta_all, 1), selector(hg * hp, 1), preferred_element_type=F32)
    beta = [beta_rep[:, sl] for sl in sls]

    cum3 = jnp.dot(jnp.where(tril, 1.0, 0.0).astype(BF16), pieces(g_all, 3), preferred_element_type=F32)
    gcum_all = cum3[:, :128] + cum3[:, 128:256] + cum3[:, 256:]
    gcum_rep = jnp.dot(pieces(gcum_all, 3), selector(hg * hp + HEADS, 3), preferred_element_type=F32)
    gcum_b = [gcum_rep[:, sl] for sl in sls]
    gr = [x.T[0:1, :] for x in gcum_b]
    decay = hmap(lambda c_, r_: jnp.where(
        tril, jnp.exp(jnp.where(tril, jnp.concatenate([c_, c_], axis=1) - r_, 0.0)), 0.0), gcum_b, gr)
    e_g = hmap(jnp.exp, gcum_b)
    g_last = [x[C - 1:C, :] for x in gcum_b]

    kb = hmap(lambda a_, b_: a_ * b_, k, beta)
    vb = hmap(lambda a_, b_: a_ * b_, v, beta)
    a = hmap(lambda kb_, q_, k_: lax.dot_general(
        jnp.concatenate([kb_, q_], axis=0).astype(BF16), k_.astype(BF16), NT,
        preferred_element_type=F32), kb, q, k)
    lmat = hmap(lambda a_, d_: jnp.where(strict, a_[:C] * d_, 0.0), a, decay)
    attn = hmap(lambda a_, d_: a_[C:] * d_, a, decay)

    d1 = hmap(lambda l_: jnp.where(rxc < 8, l_, 0.0), lmat)
    d2 = hmap(lambda x: _bdot(x, x), d1)
    d4 = hmap(lambda x: _bdot(x, x), d2)
    p1 = hmap(lambda x, y: _bdot(eye - x, eye + y), d1, d2)
    tmat = hmap(lambda x, y: _bdot(x, eye + y), p1, d4)

    def odd_rows(t, sz):
        return jnp.concatenate([t[b * 2 * sz + sz:(b + 1) * 2 * sz] for b in range(C // (2 * sz))], axis=0)

    def with_odd_rows(t, odd, sz):
        pieces = []
        for b in range(C // (2 * sz)):
            pieces += [t[b * 2 * sz:b * 2 * sz + sz], odd[b * sz:(b + 1) * sz]]
        return jnp.concatenate(pieces, axis=0)

    sz = 8
    while sz < C:
        off = hmap(lambda l_: jnp.where((rxc >= sz) & (rxc < 2 * sz), l_, 0.0), lmat)
        t_odd = hmap(lambda t_: odd_rows(t_, sz), tmat)
        x = hmap(_bdot, t_odd, off)
        x = hmap(_bdot, x, tmat)
        tmat = hmap(lambda t_, o_, x_: with_odd_rows(t_, o_ - x_, sz), tmat, t_odd, x)
        sz *= 2

    uw = hmap(lambda t_, vb_, kb_, e_: _bdot(t_, jnp.concatenate([vb_, kb_ * e_], axis=1)),
              tmat, vb, kb, e_g)
    state = [state_scr[hh] for hh in heads]
    wq = hmap(lambda uw_, q_, e_, s_: _bdot(jnp.concatenate([uw_[:, HEAD_DIM:], q_ * e_], axis=0), s_),
              uw, q, e_g, state)
    v_new = hmap(lambda uw_, wq_: uw_[:, :HEAD_DIM] - wq_[:C], uw, wq)
    av = hmap(_bdot, attn, v_new)
    kt = hmap(lambda k_, gl_, gc_: (k_ * jnp.exp(gl_ - gc_)).T, k, g_last, gcum_b)
    ds = hmap(_bdot, kt, v_new)
    for hh in heads:
        new_state = state[hh] * jnp.exp(g_last[hh]) + ds[hh]
        state_scr[hh] = new_state
        st_out_ref[hh] = new_state
        o = wq[hh][C:] + av[hh]
        y = o * lax.rsqrt(jnp.mean(o * o, axis=-1, keepdims=True) + EPS) * onw_ref[...]
        z = z_ref[:, sls[hh]].astype(F32)
        o_ref[:, sls[hh]] = (y * (z * _sigmoid(z))).astype(BF16)


def _gated_deltanet(proj, small, conv_w, alog_v, dtb_v, onw, state, c0, n_chunks, yd_prev=None, after=None):
    s = proj.shape[0]
    rb = DN_CHUNK // HALO
    hp = DN_HEADS_PER_STEP
    assert hp == HEADS
    width = hp * HEAD_DIM

    def col(part):
        return pl.BlockSpec((DN_CHUNK, width), lambda c: (c0 + c, part))

    def halo(part):
        return pl.BlockSpec((HALO, width), lambda c: (jnp.maximum((c0 + c) * rb - 1, 0), part))

    def cw(part):
        return pl.BlockSpec((DN_CONV, width), lambda c: (0, part))

    vec = pl.BlockSpec((1, 128), lambda c: (0, 0))
    st_spec = pl.BlockSpec((hp, HEAD_DIM, HEAD_DIM), lambda c: (0, 0, 0))
    extras = [a for a in (yd_prev, after) if a is not None]
    n_fixed = 15
    return pl.pallas_call(
        functools.partial(_delta_kernel, hg=0, c0=c0),
        grid=(n_chunks,),
        in_specs=[col(3), col(4), col(5),
                  halo(3), halo(4), halo(5),
                  col(6),
                  pl.BlockSpec((DN_CHUNK, 128), lambda c: (c0 + c, 0)),
                  cw(0), cw(1), cw(2),
                  vec, vec, vec, st_spec] + [pl.BlockSpec(memory_space=pl.ANY)] * len(extras),
        out_specs=[pl.BlockSpec((DN_CHUNK, width), lambda c: (c0 + c, 0)), st_spec],
        out_shape=[jax.ShapeDtypeStruct((s, width), BF16),
                   jax.ShapeDtypeStruct((hp, HEAD_DIM, HEAD_DIM), F32)],
        input_output_aliases={n_fixed: 0} if yd_prev is not None else {},
        scratch_shapes=[pltpu.VMEM((hp, HEAD_DIM, HEAD_DIM), F32),
                        pltpu.VMEM((3, HALO + DN_CHUNK, width), F32)],
        compiler_params=pltpu.CompilerParams(
            dimension_semantics=("arbitrary",), vmem_limit_bytes=VMEM_LIMIT),
        name="deltanet",
    )(proj, proj, proj, proj, proj, proj, proj, small, conv_w, conv_w, conv_w, alog_v, dtb_v, onw,
      state, *extras)


def _mix_ffn_kernel(qm_ref, km_ref, vm_ref, part_ref, yd_ref, ga_ref, gd_ref, x_ref, wa_ref, wd_ref,
                    wo_ref, nmix_ref, npre_ref, wg_ref, wu_ref, wdn_ref, npost_ref, o_ref, ya_scr, *, nb):
    i = pl.program_id(0)

    @pl.when(i == 0)
    def _():
        ya_scr[...] = jnp.zeros_like(ya_scr)

    ya = ya_scr[...]
    _merge_block(qm_ref, km_ref, vm_ref, part_ref, ya_scr, jnp.minimum(i, nb - 1), range(HEADS))
    pa = jnp.dot(ya, wa_ref[...], preferred_element_type=F32)
    pd = jnp.dot(yd_ref[...], wd_ref[...], preferred_element_type=F32)
    merged = _sigmoid(ga_ref[...].astype(F32)) * pa + _sigmoid(gd_ref[...].astype(F32)) * pd
    mo = jnp.dot(merged.astype(BF16), wo_ref[...], preferred_element_type=F32)
    x = x_ref[...] + mo * lax.rsqrt(jnp.mean(mo * mo, axis=-1, keepdims=True) + EPS) * nmix_ref[...]

    h = (x * lax.rsqrt(jnp.mean(x * x, axis=-1, keepdims=True) + EPS) * npre_ref[...]).astype(BF16)
    acc = jnp.zeros(x.shape, F32)
    for cc in range(D_FF // FF_CHUNK):
        sl = slice(cc * FF_CHUNK, (cc + 1) * FF_CHUNK)
        g = jnp.dot(h, wg_ref[:, sl], preferred_element_type=F32)
        u = jnp.dot(h, wu_ref[:, sl], preferred_element_type=F32)
        act = (g * _sigmoid(g) * u).astype(BF16)
        acc = acc + jnp.dot(act, wdn_ref[sl, :], preferred_element_type=F32)
    y = acc * lax.rsqrt(jnp.mean(acc * acc, axis=-1, keepdims=True) + EPS) * npost_ref[...]
    o_ref[...] = x + y


def _merge_mix_ffn(parts, yd, proj, x2, wa, wd, wo, nmix, npre, wg, wu, wdn, npost):
    s = x2.shape[0]
    tm = MOBA_BLOCK
    nb = s // tm
    full = lambda i: (0, 0)
    once = pl.Buffered(1)

    def ahead(col):
        return pl.BlockSpec((tm, D_MODEL), lambda i: (jnp.minimum(i, nb - 1), col))

    def behind(col):
        return pl.BlockSpec((tm, D_MODEL), lambda i: (jnp.maximum(i - 1, 0), col))

    vec = pl.BlockSpec((1, D_MODEL), full)
    wsq = pl.BlockSpec((D_MODEL, D_MODEL), full, pipeline_mode=once)
    return pl.pallas_call(
        functools.partial(_mix_ffn_kernel, nb=nb),
        grid=(nb + 1,),
        in_specs=[ahead(0), ahead(1), ahead(2),
                  pl.BlockSpec((MOBA_TOPK, HEADS, tm, HEAD_DIM), lambda i: (0, 0, jnp.minimum(i, nb - 1), 0)),
                  behind(0), behind(7), behind(8), behind(0),
                  wsq, wsq, wsq, vec, vec,
                  pl.BlockSpec((D_MODEL, D_FF), full, pipeline_mode=once),
                  pl.BlockSpec((D_MODEL, D_FF), full, pipeline_mode=once),
                  pl.BlockSpec((D_FF, D_MODEL), full, pipeline_mode=once),
                  vec],
        out_specs=behind(0),
        out_shape=jax.ShapeDtypeStruct((s, D_MODEL), F32),
        scratch_shapes=[pltpu.VMEM((tm, D_MODEL), BF16)],
        compiler_params=pltpu.CompilerParams(
            dimension_semantics=("arbitrary",), vmem_limit_bytes=VMEM_LIMIT),
        name="merge_mix_ffn",
    )(proj, proj, proj, parts, yd, proj, proj, x2, wa, wd, wo, nmix, npre, wg, wu, wdn, npost)


def _rope_table(s):
    half = ROPE_DIM // 2
    rep = 128 // half
    inv = ROPE_THETA ** (-jnp.arange(half, dtype=F32) * 2.0 / ROPE_DIM)
    pos = (jnp.arange(s // rep)[:, None] * rep + jnp.arange(128)[None, :] // half).astype(F32)
    ang = pos * jnp.tile(inv, rep)[None, :]
    cos, sin = lax.optimization_barrier((jnp.cos(ang), jnp.sin(ang)))
    cos = cos.reshape(s, half)
    sin = sin.reshape(s, half)
    return jnp.concatenate([cos, sin, jnp.zeros((s, HEAD_DIM - ROPE_DIM), F32)], axis=-1)


def _layer(x2, l, norm_mix_pre, w_in, conv_w, a_log, dt_bias, o_norm_w, w_o_attn, w_o_delta,
           w_out, norm_mix_post, norm_ffn_pre, w_gate, w_up, w_down, norm_ffn_post):
    s = x2.shape[0]
    nb = s // MOBA_BLOCK
    wide = 7 * D_MODEL
    nsmall = 2 * HEADS
    w_t = jnp.swapaxes(w_in[l], 0, 1).astype(BF16)
    later_w = [w[l].astype(BF16) for w in (w_o_attn, w_o_delta, w_out, w_gate, w_up, w_down)]

    proj, small, kmean, v_t, qf = _project(x2, norm_mix_pre[l][None, :], w_t, _rope_table(s),
                                           wide // D_MODEL, nsmall, after=later_w)
    pad = lambda vec: jnp.pad(vec.astype(F32), (HEADS, 128 - 2 * HEADS))[None, :]
    half = (s // DN_CHUNK) // 2
    delta = functools.partial(_gated_deltanet, proj, small, conv_w[l], pad(a_log[l]), pad(dt_bias[l]),
                              o_norm_w[l][None, :])
    res = {}

    def first_half():
        res["yd"], res["state"] = delta(jnp.zeros((HEADS, HEAD_DIM, HEAD_DIM), F32), 0, half)
        return res["yd"]

    def second_half(after):
        res["yd"], _ = delta(res["state"], half, s // DN_CHUNK - half, yd_prev=res["yd"], after=after)

    parts = _moba_sparse(proj, v_t, kmean.reshape(nb, D_MODEL), qf, first_half, second_half)

    return _merge_mix_ffn(parts, res["yd"], proj, x2, *later_w[:3], norm_mix_post[l][None, :],
                    norm_ffn_pre[l][None, :], *later_w[3:], norm_ffn_post[l][None, :])


def kernel(x, norm_mix_pre, w_in, conv_w, a_log, dt_bias, o_norm_w, w_o_attn, w_o_delta, w_out,
           norm_mix_post, norm_ffn_pre, w_gate, w_up, w_down, norm_ffn_post):
    b, s, d = x.shape
    assert d == D_MODEL and s % (ROUTE_QBLOCKS * MOBA_BLOCK) == 0 and s % (2 * DN_CHUNK) == 0
    outs = []
    for bi in range(b):
        x2 = x.reshape(s, d) if b == 1 else x[bi]
        for l in range(w_in.shape[0]):
            x2 = _layer(x2, l, norm_mix_pre, w_in, conv_w, a_log, dt_bias, o_norm_w, w_o_attn,
                        w_o_delta, w_out, norm_mix_post, norm_ffn_pre, w_gate, w_up, w_down,
                        norm_ffn_post)
        outs.append(x2)
    return outs[0].reshape(1, s, d) if b == 1 else jnp.stack(outs, axis=0)
```
